```python
import jax, jax.numpy as jnp
from jax import lax
import numpy as np

D_MODEL = 1024
BATCH = 8
SEQ = 2048
DEPTH = 4

SB_HEADS = 8
SB_HEAD_DIM = 64
SB_WIDTH = SB_HEADS * SB_HEAD_DIM
MLA_HEADS = 8
MLA_NOPE_DIM = 64
MLA_ROPE_DIM = 32
MLA_V_DIM = 64
MLA_QK_DIM = MLA_NOPE_DIM + MLA_ROPE_DIM
MLA_Q_RANK = 384
MLA_KV_RANK = 256
MLA_WIDTH = MLA_HEADS * MLA_V_DIM
D_FF = 4 * D_MODEL
BLOCK_Q = 128
ROPE_THETA = 10000.0
NORM_EPS = 1e-6
N_MOD = 6
IN_WIDTHS = (SB_WIDTH, SB_WIDTH, SB_WIDTH, MLA_Q_RANK, MLA_KV_RANK, MLA_ROPE_DIM, D_MODEL, D_MODEL)
IN_DIM = sum(IN_WIDTHS)

kernel_name = "hybrid_stickbreaking_mla_sqrelu_adaln"


def _rms_norm(x, g):
    xf = x.astype(jnp.float32)
    y = xf * lax.rsqrt(jnp.mean(xf * xf, axis=-1, keepdims=True) + NORM_EPS)
    return y.astype(x.dtype) * g


def _split_cols(p, widths):
    outs, start = [], 0
    for w in widths:
        outs.append(p[..., start:start + w])
        start += w
    return outs


def _rope_tables(positions):
    inv_freq = 1.0 / (ROPE_THETA ** (jnp.arange(0, MLA_ROPE_DIM, 2, dtype=jnp.float32) / MLA_ROPE_DIM))
    ang = positions.astype(jnp.float32)[..., None] * inv_freq
    return jnp.cos(ang), jnp.sin(ang)


def _apply_rope(t, cos, sin):
    half = t.shape[-1] // 2
    t1, t2 = t[..., :half], t[..., half:]
    cs = cos[:, :, None, :].astype(t.dtype)
    sn = sin[:, :, None, :].astype(t.dtype)
    return jnp.concatenate([t1 * cs - t2 * sn, t2 * cs + t1 * sn], axis=-1)


def _stick_breaking_weights(z, mask):
    log_fail = jnp.where(mask, jax.nn.log_sigmoid(-z), 0.0)
    later = lax.cumsum(log_fail, axis=3, reverse=True) - log_fail
    return jnp.where(mask, jnp.exp(jax.nn.log_sigmoid(z) + later), 0.0)


def _softmax_weights(z, mask):
    return jax.nn.softmax(jnp.where(mask, z, -jnp.inf), axis=-1)


def _causal_block_attention(q, k, v, weight_fn, strict):
    seq = q.shape[1]
    scale = q.shape[-1] ** -0.5
    outs = []
    for t0 in range(0, seq, BLOCK_Q):
        end = t0 + BLOCK_Q
        z = jnp.einsum('bqhd,bkhd->bhqk', q[:, t0:end].astype(jnp.float32),
                       k[:, :end].astype(jnp.float32)) * scale
        t_idx = t0 + jnp.arange(BLOCK_Q)[:, None]
        s_idx = jnp.arange(end)[None, :]
        mask = (s_idx < t_idx) if strict else (s_idx <= t_idx)
        w = weight_fn(z, mask)
        outs.append(jnp.einsum('bhqk,bkhd->bqhd', w.astype(v.dtype), v[:, :end]))
    return jnp.concatenate(outs, axis=1)


def _fwd_setup_inputs(seed: int = 0) -> dict:
    key = jax.random.key(seed)
    ks = jax.random.split(key, 20)

    def nrm(k, shape, fan_in):
        return jax.random.normal(k, shape, jnp.float32) * (fan_in ** -0.5)

    def gain(k, shape):
        return 1.0 + 0.05 * jax.random.normal(k, shape, jnp.float32)

    x = jax.random.normal(ks[0], (BATCH, SEQ, D_MODEL), jnp.float32)
    c = jax.random.normal(ks[1], (BATCH, D_MODEL), jnp.float32)
    offsets = jax.random.randint(ks[2], (BATCH, 1), 0, 1024, dtype=jnp.int32)
    positions = (offsets + jnp.arange(SEQ, dtype=jnp.int32)[None, :]).astype(jnp.int32)
    return {
        "x": x,
        "c": c,
        "positions": positions,
        "w_ada": nrm(ks[3], (DEPTH, D_MODEL, N_MOD * D_MODEL), D_MODEL),
        "b_ada": 0.02 * jax.random.normal(ks[4], (DEPTH, N_MOD * D_MODEL), jnp.float32),
        "g_mix_norm": gain(ks[5], (DEPTH, D_MODEL)),
        "w_in": nrm(ks[6], (DEPTH, D_MODEL, IN_DIM), D_MODEL),
        "g_q_lat": gain(ks[7], (DEPTH, MLA_Q_RANK)),
        "w_q_up": nrm(ks[8], (DEPTH, MLA_Q_RANK, MLA_HEADS * MLA_QK_DIM), MLA_Q_RANK),
        "g_kv_lat": gain(ks[9], (DEPTH, MLA_KV_RANK)),
        "w_kv_up": nrm(ks[10], (DEPTH, MLA_KV_RANK, MLA_HEADS * (MLA_NOPE_DIM + MLA_V_DIM)), MLA_KV_RANK),
        "w_sb_out": nrm(ks[11], (DEPTH, SB_WIDTH, D_MODEL), SB_WIDTH),
        "w_mla_out": nrm(ks[12], (DEPTH, MLA_WIDTH, D_MODEL), MLA_WIDTH),
        "w_mix_out": nrm(ks[13], (DEPTH, D_MODEL, D_MODEL), D_MODEL),
        "g_mlp_norm": gain(ks[14], (DEPTH, D_MODEL)),
        "w_up": nrm(ks[15], (DEPTH, D_MODEL, D_FF), D_MODEL),
        "w_down": nrm(ks[16], (DEPTH, D_FF, D_MODEL), D_FF),
        "g_final": gain(ks[17], (D_MODEL,)),
    }


def _fwd_reference(x, c, positions, w_ada, b_ada, g_mix_norm, w_in, g_q_lat, w_q_up,
              g_kv_lat, w_kv_up, w_sb_out, w_mla_out, w_mix_out, g_mlp_norm,
              w_up, w_down, g_final):
    B, S, _ = x.shape
    cos, sin = _rope_tables(positions)
    c_act = jax.nn.silu(c)
    for l in range(DEPTH):
        mod = (c_act @ w_ada[l] + b_ada[l])[:, None, :]
        shift1, scale1, gate1, shift2, scale2, gate2 = jnp.split(mod, N_MOD, axis=-1)

        h = _rms_norm(x, g_mix_norm[l]) * (1.0 + scale1) + shift1
        p = h @ w_in[l]
        q_sb, k_sb, v_sb, q_lat, kv_lat, k_rope, gate_sb, gate_mla = _split_cols(p, IN_WIDTHS)

        o_sb = _causal_block_attention(
            q_sb.reshape(B, S, SB_HEADS, SB_HEAD_DIM),
            k_sb.reshape(B, S, SB_HEADS, SB_HEAD_DIM),
            v_sb.reshape(B, S, SB_HEADS, SB_HEAD_DIM),
            _stick_breaking_weights, strict=True)
        o_sb = o_sb.reshape(B, S, SB_WIDTH) @ w_sb_out[l]

        q = (_rms_norm(q_lat, g_q_lat[l]) @ w_q_up[l]).reshape(B, S, MLA_HEADS, MLA_QK_DIM)
        kv = (_rms_norm(kv_lat, g_kv_lat[l]) @ w_kv_up[l]).reshape(
            B, S, MLA_HEADS, MLA_NOPE_DIM + MLA_V_DIM)
        k_nope, v_mla = kv[..., :MLA_NOPE_DIM], kv[..., MLA_NOPE_DIM:]
        q_full = jnp.concatenate(
            [q[..., :MLA_NOPE_DIM], _apply_rope(q[..., MLA_NOPE_DIM:], cos, sin)], axis=-1)
        k_pe = _apply_rope(k_rope[:, :, None, :], cos, sin)
        k_full = jnp.concatenate(
            [k_nope, jnp.broadcast_to(k_pe, (B, S, MLA_HEADS, MLA_ROPE_DIM))], axis=-1)
        o_mla = _causal_block_attention(q_full, k_full, v_mla, _softmax_weights, strict=False)
        o_mla = o_mla.reshape(B, S, MLA_WIDTH) @ w_mla_out[l]

        merged = jax.nn.sigmoid(gate_sb) * o_sb + jax.nn.sigmoid(gate_mla) * o_mla
        x = x + gate1 * (merged @ w_mix_out[l])

        h = _rms_norm(x, g_mlp_norm[l]) * (1.0 + scale2) + shift2
        x = x + gate2 * (jnp.square(jax.nn.relu(h @ w_up[l])) @ w_down[l])

    return _rms_norm(x, g_final)


import jax as _jax
import jax.numpy as _jnp

TWIN_FORMAT = 'train_step'
FWD_PARAMS = ['x', 'c', 'positions', 'w_ada', 'b_ada', 'g_mix_norm', 'w_in', 'g_q_lat', 'w_q_up', 'g_kv_lat', 'w_kv_up', 'w_sb_out', 'w_mla_out', 'w_mix_out', 'g_mlp_norm', 'w_up', 'w_down', 'g_final']
TWIN_WEIGHTS = ['w_ada', 'b_ada', 'g_mix_norm', 'w_in', 'g_q_lat', 'w_q_up', 'g_kv_lat', 'w_kv_up', 'w_sb_out', 'w_mla_out', 'w_mix_out', 'g_mlp_norm', 'w_up', 'w_down', 'g_final']
TWIN_DIFF_INPUT = 'x'
TWIN_INPUTS = ['x', 'c', 'positions', 'w_ada', 'b_ada', 'g_mix_norm', 'w_in', 'g_q_lat', 'w_q_up', 'g_kv_lat', 'w_kv_up', 'w_sb_out', 'w_mla_out', 'w_mix_out', 'g_mlp_norm', 'w_up', 'w_down', 'g_final', 'loss_target', 'm_w_ada', 'm_b_ada', 'm_g_mix_norm', 'm_w_in', 'm_g_q_lat', 'm_w_q_up', 'm_g_kv_lat', 'm_w_kv_up', 'm_w_sb_out', 'm_w_mla_out', 'm_w_mix_out', 'm_g_mlp_norm', 'm_w_up', 'm_w_down', 'm_g_final', 'v_w_ada', 'v_b_ada', 'v_g_mix_norm', 'v_w_in', 'v_g_q_lat', 'v_w_q_up', 'v_g_kv_lat', 'v_w_kv_up', 'v_w_sb_out', 'v_w_mla_out', 'v_w_mix_out', 'v_g_mlp_norm', 'v_w_up', 'v_w_down', 'v_g_final']
TWIN_OUTPUTS = ['loss', 'grad_x', 'grad_w_ada', 'grad_b_ada', 'grad_g_mix_norm', 'grad_w_in', 'grad_g_q_lat', 'grad_w_q_up', 'grad_g_kv_lat', 'grad_w_kv_up', 'grad_w_sb_out', 'grad_w_mla_out', 'grad_w_mix_out', 'grad_g_mlp_norm', 'grad_w_up', 'grad_w_down', 'grad_g_final', 'delta_w_ada', 'delta_b_ada', 'delta_g_mix_norm', 'delta_w_in', 'delta_g_q_lat', 'delta_w_q_up', 'delta_g_kv_lat', 'delta_w_kv_up', 'delta_w_sb_out', 'delta_w_mla_out', 'delta_w_mix_out', 'delta_g_mlp_norm', 'delta_w_up', 'delta_w_down', 'delta_g_final', 'new_m_w_ada', 'new_m_b_ada', 'new_m_g_mix_norm', 'new_m_w_in', 'new_m_g_q_lat', 'new_m_w_q_up', 'new_m_g_kv_lat', 'new_m_w_kv_up', 'new_m_w_sb_out', 'new_m_w_mla_out', 'new_m_w_mix_out', 'new_m_g_mlp_norm', 'new_m_w_up', 'new_m_w_down', 'new_m_g_final', 'new_v_w_ada', 'new_v_b_ada', 'new_v_g_mix_norm', 'new_v_w_in', 'new_v_g_q_lat', 'new_v_w_q_up', 'new_v_g_kv_lat', 'new_v_w_kv_up', 'new_v_w_sb_out', 'new_v_w_mla_out', 'new_v_w_mix_out', 'new_v_g_mlp_norm', 'new_v_w_up', 'new_v_w_down', 'new_v_g_final']
TWIN_LEAF_KINDS = {'loss': 'loss', 'grad_x': 'grad_x', 'grad_w_ada': 'grad_w', 'grad_b_ada': 'grad_w', 'grad_g_mix_norm': 'grad_w', 'grad_w_in': 'grad_w', 'grad_g_q_lat': 'grad_w', 'grad_w_q_up': 'grad_w', 'grad_g_kv_lat': 'grad_w', 'grad_w_kv_up': 'grad_w', 'grad_w_sb_out': 'grad_w', 'grad_w_mla_out': 'grad_w', 'grad_w_mix_out': 'grad_w', 'grad_g_mlp_norm': 'grad_w', 'grad_w_up': 'grad_w', 'grad_w_down': 'grad_w', 'grad_g_final': 'grad_w', 'delta_w_ada': 'delta_w', 'delta_b_ada': 'delta_w', 'delta_g_mix_norm': 'delta_w', 'delta_w_in': 'delta_w', 'delta_g_q_lat': 'delta_w', 'delta_w_q_up': 'delta_w', 'delta_g_kv_lat': 'delta_w', 'delta_w_kv_up': 'delta_w', 'delta_w_sb_out': 'delta_w', 'delta_w_mla_out': 'delta_w', 'delta_w_mix_out': 'delta_w', 'delta_g_mlp_norm': 'delta_w', 'delta_w_up': 'delta_w', 'delta_w_down': 'delta_w', 'delta_g_final': 'delta_w', 'new_m_w_ada': 'new_m', 'new_m_b_ada': 'new_m', 'new_m_g_mix_norm': 'new_m', 'new_m_w_in': 'new_m', 'new_m_g_q_lat': 'new_m', 'new_m_w_q_up': 'new_m', 'new_m_g_kv_lat': 'new_m', 'new_m_w_kv_up': 'new_m', 'new_m_w_sb_out': 'new_m', 'new_m_w_mla_out': 'new_m', 'new_m_w_mix_out': 'new_m', 'new_m_g_mlp_norm': 'new_m', 'new_m_w_up': 'new_m', 'new_m_w_down': 'new_m', 'new_m_g_final': 'new_m', 'new_v_w_ada': 'new_v', 'new_v_b_ada': 'new_v', 'new_v_g_mix_norm': 'new_v', 'new_v_w_in': 'new_v', 'new_v_g_q_lat': 'new_v', 'new_v_w_q_up': 'new_v', 'new_v_g_kv_lat': 'new_v', 'new_v_w_kv_up': 'new_v', 'new_v_w_sb_out': 'new_v', 'new_v_w_mla_out': 'new_v', 'new_v_w_mix_out': 'new_v', 'new_v_g_mlp_norm': 'new_v', 'new_v_w_up': 'new_v', 'new_v_w_down': 'new_v', 'new_v_g_final': 'new_v'}


def _forward(args):
    return _fwd_reference(*[args[k] for k in FWD_PARAMS])


def _output_shape():
    out = _jax.eval_shape(lambda: _forward(_fwd_setup_inputs(0)))
    return out.shape, out.dtype

N_MICROBATCH = 1
ADAM_LR = 0.001
ADAM_B1 = 0.9
ADAM_B2 = 0.999
ADAM_EPS = 1e-08
ADAM_WD = 0.01
ADAM_STEP = 10
PER_EXAMPLE_BATCH_AXIS = {'x': 0, 'c': 0, 'positions': 0, 'loss_target': 0}
SHARED_INPUTS = []
_WEIGHT_DTYPES = {'w_ada': _jnp.float32, 'b_ada': _jnp.float32, 'g_mix_norm': _jnp.float32, 'w_in': _jnp.float32, 'g_q_lat': _jnp.float32, 'w_q_up': _jnp.float32, 'g_kv_lat': _jnp.float32, 'w_kv_up': _jnp.float32, 'w_sb_out': _jnp.float32, 'w_mla_out': _jnp.float32, 'w_mix_out': _jnp.float32, 'g_mlp_norm': _jnp.float32, 'w_up': _jnp.float32, 'w_down': _jnp.float32, 'g_final': _jnp.float32}
MOMENT_SCALE = {'w_ada': 1.021412e-01, 'b_ada': 1.771272e-01, 'g_mix_norm': 4.480129e-02, 'w_in': 3.466876e-02, 'g_q_lat': 6.712774e-03, 'w_q_up': 4.787973e-03, 'g_kv_lat': 7.950449e-02, 'w_kv_up': 3.772932e-02, 'w_sb_out': 5.403605e-02, 'w_mla_out': 3.692909e-02, 'w_mix_out': 6.533236e-02, 'g_mlp_norm': 1.103835e-01, 'w_up': 7.282980e-02, 'w_down': 1.691500e-01, 'g_final': 1.734021e+01}


def _to_microbatches(a, axis):
    t = _jnp.moveaxis(a, axis, 0)
    t = t.reshape((N_MICROBATCH, t.shape[0] // N_MICROBATCH) + t.shape[1:])
    return _jnp.moveaxis(t, 1, axis + 1)


def setup_inputs(seed: int = 0) -> dict:
    inp = _fwd_setup_inputs(seed)
    key = _jax.random.fold_in(_jax.random.key(seed), 7919)
    shape, _ = _output_shape()
    out = dict(inp)
    out["loss_target"] = _jax.random.normal(_jax.random.fold_in(key, 0), shape, _jnp.float32)
    for i, name in enumerate(TWIN_WEIGHTS):
        w = inp[name].astype(_jnp.float32)
        if MOMENT_SCALE is None:
            s = _jnp.sqrt(_jnp.mean(_jnp.square(w)) + 1e-30)
        else:
            s = MOMENT_SCALE[name]
        km, kv = _jax.random.split(_jax.random.fold_in(key, i + 1))
        out[name] = w
        out["m_" + name] = s * _jax.random.normal(km, w.shape, _jnp.float32)
        out["v_" + name] = (s * s) * _jax.random.uniform(kv, w.shape, _jnp.float32, 0.5, 1.5)
    if N_MICROBATCH > 1:
        for name, axis in PER_EXAMPLE_BATCH_AXIS.items():
            out[name] = _to_microbatches(out[name], axis)
    return {'x': out['x'], 'c': out['c'], 'positions': out['positions'], 'w_ada': out['w_ada'], 'b_ada': out['b_ada'], 'g_mix_norm': out['g_mix_norm'], 'w_in': out['w_in'], 'g_q_lat': out['g_q_lat'], 'w_q_up': out['w_q_up'], 'g_kv_lat': out['g_kv_lat'], 'w_kv_up': out['w_kv_up'], 'w_sb_out': out['w_sb_out'], 'w_mla_out': out['w_mla_out'], 'w_mix_out': out['w_mix_out'], 'g_mlp_norm': out['g_mlp_norm'], 'w_up': out['w_up'], 'w_down': out['w_down'], 'g_final': out['g_final'], 'loss_target': out['loss_target'], 'm_w_ada': out['m_w_ada'], 'm_b_ada': out['m_b_ada'], 'm_g_mix_norm': out['m_g_mix_norm'], 'm_w_in': out['m_w_in'], 'm_g_q_lat': out['m_g_q_lat'], 'm_w_q_up': out['m_w_q_up'], 'm_g_kv_lat': out['m_g_kv_lat'], 'm_w_kv_up': out['m_w_kv_up'], 'm_w_sb_out': out['m_w_sb_out'], 'm_w_mla_out': out['m_w_mla_out'], 'm_w_mix_out': out['m_w_mix_out'], 'm_g_mlp_norm': out['m_g_mlp_norm'], 'm_w_up': out['m_w_up'], 'm_w_down': out['m_w_down'], 'm_g_final': out['m_g_final'], 'v_w_ada': out['v_w_ada'], 'v_b_ada': out['v_b_ada'], 'v_g_mix_norm': out['v_g_mix_norm'], 'v_w_in': out['v_w_in'], 'v_g_q_lat': out['v_g_q_lat'], 'v_w_q_up': out['v_w_q_up'], 'v_g_kv_lat': out['v_g_kv_lat'], 'v_w_kv_up': out['v_w_kv_up'], 'v_w_sb_out': out['v_w_sb_out'], 'v_w_mla_out': out['v_w_mla_out'], 'v_w_mix_out': out['v_w_mix_out'], 'v_g_mlp_norm': out['v_g_mlp_norm'], 'v_w_up': out['v_w_up'], 'v_w_down': out['v_w_down'], 'v_g_final': out['v_g_final']}


def _loss(weights, diff, rest, loss_target):
    with _jax.named_scope("forward"):
        args = {**rest, TWIN_DIFF_INPUT: diff, **{k: w.astype(_WEIGHT_DTYPES[k]) for k, w in weights.items()}}
        y = _forward(args)
    with _jax.named_scope("loss_head"):
        err = _jnp.square(y.astype(_jnp.float32) - loss_target)
        return 0.5 * _jnp.sum(_jnp.mean(err, axis=-1)) if err.ndim else 0.5 * err


def _adamw(w, g, m, v):
    m = ADAM_B1 * m + (1.0 - ADAM_B1) * g
    v = ADAM_B2 * v + (1.0 - ADAM_B2) * _jnp.square(g)
    m_hat = m / (1.0 - ADAM_B1 ** ADAM_STEP)
    v_hat = v / (1.0 - ADAM_B2 ** ADAM_STEP)
    delta = -ADAM_LR * (m_hat / (_jnp.sqrt(v_hat) + ADAM_EPS) + ADAM_WD * w)
    return delta, m, v


def reference(x, c, positions, w_ada, b_ada, g_mix_norm, w_in, g_q_lat, w_q_up, g_kv_lat, w_kv_up, w_sb_out, w_mla_out, w_mix_out, g_mlp_norm, w_up, w_down, g_final, loss_target, m_w_ada, m_b_ada, m_g_mix_norm, m_w_in, m_g_q_lat, m_w_q_up, m_g_kv_lat, m_w_kv_up, m_w_sb_out, m_w_mla_out, m_w_mix_out, m_g_mlp_norm, m_w_up, m_w_down, m_g_final, v_w_ada, v_b_ada, v_g_mix_norm, v_w_in, v_g_q_lat, v_w_q_up, v_g_kv_lat, v_w_kv_up, v_w_sb_out, v_w_mla_out, v_w_mix_out, v_g_mlp_norm, v_w_up, v_w_down, v_g_final):
    given = dict(x=x, c=c, positions=positions, w_ada=w_ada, b_ada=b_ada, g_mix_norm=g_mix_norm, w_in=w_in, g_q_lat=g_q_lat, w_q_up=w_q_up, g_kv_lat=g_kv_lat, w_kv_up=w_kv_up, w_sb_out=w_sb_out, w_mla_out=w_mla_out, w_mix_out=w_mix_out, g_mlp_norm=g_mlp_norm, w_up=w_up, w_down=w_down, g_final=g_final, loss_target=loss_target, m_w_ada=m_w_ada, m_b_ada=m_b_ada, m_g_mix_norm=m_g_mix_norm, m_w_in=m_w_in, m_g_q_lat=m_g_q_lat, m_w_q_up=m_w_q_up, m_g_kv_lat=m_g_kv_lat, m_w_kv_up=m_w_kv_up, m_w_sb_out=m_w_sb_out, m_w_mla_out=m_w_mla_out, m_w_mix_out=m_w_mix_out, m_g_mlp_norm=m_g_mlp_norm, m_w_up=m_w_up, m_w_down=m_w_down, m_g_final=m_g_final, v_w_ada=v_w_ada, v_b_ada=v_b_ada, v_g_mix_norm=v_g_mix_norm, v_w_in=v_w_in, v_g_q_lat=v_g_q_lat, v_w_q_up=v_w_q_up, v_g_kv_lat=v_g_kv_lat, v_w_kv_up=v_w_kv_up, v_w_sb_out=v_w_sb_out, v_w_mla_out=v_w_mla_out, v_w_mix_out=v_w_mix_out, v_g_mlp_norm=v_g_mlp_norm, v_w_up=v_w_up, v_w_down=v_w_down, v_g_final=v_g_final)
    weights = {n: given[n] for n in TWIN_WEIGHTS}
    shared = {n: given[n] for n in SHARED_INPUTS}
    per_example = {n: given[n] for n in ['x', 'c', 'positions']}
    grad_fn = _jax.value_and_grad(_loss, argnums=(0, 1))

    def one_microbatch(ex, loss_target):
        ex = dict(ex)
        diff = ex.pop(TWIN_DIFF_INPUT)
        return grad_fn(weights, diff, {**shared, **ex}, loss_target)

    if N_MICROBATCH == 1:
        loss, (grad_w, grad_x) = one_microbatch(per_example, given["loss_target"])
    else:
        def body(carry, xs):
            loss_sum, grad_sum = carry
            l_k, (gw_k, gx_k) = one_microbatch(xs[0], xs[1])
            with _jax.named_scope("update"):
                return (loss_sum + l_k, _jax.tree.map(_jnp.add, grad_sum, gw_k)), gx_k

        init = (_jnp.zeros((), _jnp.float32), _jax.tree.map(_jnp.zeros_like, weights))
        (loss, grad_w), grad_x = _jax.lax.scan(body, init, (per_example, given["loss_target"]))
    with _jax.named_scope("update"):
        delta_w, new_m, new_v = {}, {}, {}
        for n in TWIN_WEIGHTS:
            delta_w[n], new_m[n], new_v[n] = _adamw(weights[n], grad_w[n], given["m_" + n], given["v_" + n])
    return (loss, grad_x, *[grad_w[n] for n in TWIN_WEIGHTS], *[delta_w[n] for n in TWIN_WEIGHTS],
            *[new_m[n] for n in TWIN_WEIGHTS], *[new_v[n] for n in TWIN_WEIGHTS])
```

```python
import functools

import jax
import jax.numpy as jnp
from jax import lax
from jax.experimental import pallas as pl
from jax.experimental.pallas import tpu as pltpu

F32 = jnp.float32
BF16 = jnp.bfloat16
MESH = pl.DeviceIdType.MESH

D_MODEL = 1024
N_HEADS = 8
SB_DIM = 64
SB_WIDTH = 512
Q_RANK = 384
KV_RANK = 256
ROPE_DIM = 32
NOPE_DIM = 64
QK_DIM = 96
D_FF = 4096
N_MOD = 6
EPS = 1e-6
ROPE_THETA = 10000.0
SB_SCALE = SB_DIM ** -0.5
MLA_SCALE = QK_DIM ** -0.5
ADAM_LR, ADAM_B1, ADAM_B2, ADAM_EPS, ADAM_WD, ADAM_STEP = 0.001, 0.9, 0.999, 1e-08, 0.01, 10

LANE = 128
IN_PAD = 4352
COL_GATE_SB, COL_GATE_MLA, COL_QSB, COL_KSB, COL_VSB, COL_KVLAT, COL_QLAT, COL_KROPE = (
    0, 1024, 2048, 2560, 3072, 3584, 3840, 4224)
ROPE_LANE0 = 64
VMEM_LIMIT = 48 * 1024 * 1024
NEG_BIG = -1e30


def _cp(*sem):
    return pltpu.CompilerParams(dimension_semantics=sem, vmem_limit_bytes=VMEM_LIMIT)


def _tile(n, prefs):
    for t in prefs:
        if t <= n and n % t == 0:
            return t
    return n


def _dot(a, b, dims):
    return lax.dot_general(a, b, (dims, ((), ())), preferred_element_type=F32)


def _nn(a, b):
    return _dot(a, b, ((1,), (0,)))


def _nt(a, b):
    return _dot(a, b, ((1,), (1,)))


def _tn(a, b):
    return _dot(a, b, ((0,), (0,)))


def _matmul(a, b, *, ta=False, tb=False, out_dtype=F32, name):
    (K, M) = a.shape if ta else a.shape[::-1]
    (N, Kb) = b.shape if tb else b.shape[::-1]
    assert K == Kb, (a.shape, b.shape, ta, tb)
    tm = _tile(M, (512, 384, 256, 128))
    tn = _tile(N, (1024, 2176, 768, 512, 384, 256, 128))
    tk = _tile(K, (1024, 2176, 768, 512, 384, 256, 128))
    nk = K // tk
    dims = ((0 if ta else 1,), (1 if tb else 0,))

    def body(a_ref, b_ref, o_ref, acc_ref):
        k = pl.program_id(2)

        @pl.when(k == 0)
        def _():
            acc_ref[...] = jnp.zeros_like(acc_ref)

        acc_ref[...] += _dot(a_ref[...].astype(BF16), b_ref[...].astype(BF16), dims)

        @pl.when(k == nk - 1)
        def _():
            o_ref[...] = acc_ref[...].astype(out_dtype)

    a_spec = (pl.BlockSpec((tk, tm), lambda i, j, k: (k, i)) if ta
              else pl.BlockSpec((tm, tk), lambda i, j, k: (i, k)))
    b_spec = (pl.BlockSpec((tn, tk), lambda i, j, k: (j, k)) if tb
              else pl.BlockSpec((tk, tn), lambda i, j, k: (k, j)))
    return pl.pallas_call(
        body, grid=(M // tm, N // tn, nk), in_specs=[a_spec, b_spec],
        out_specs=pl.BlockSpec((tm, tn), lambda i, j, k: (i, j)),
        out_shape=jax.ShapeDtypeStruct((M, N), out_dtype),
        scratch_shapes=[pltpu.VMEM((tm, tn), F32)],
        compiler_params=_cp("parallel", "parallel", "arbitrary"), name=name)(a, b)


def _rows(ts, w, col=0):
    return pl.BlockSpec((ts, w), lambda i: (i, col))


def _vec(w):
    return pl.BlockSpec((1, w), lambda i: (0, 0))


def _ts(S):
    return _tile(S, (256, 128))


def _rms(x):
    return lax.rsqrt(jnp.mean(x * x, axis=-1, keepdims=True) + EPS)


def _colsum(x):
    return jnp.sum(x, axis=0, keepdims=True)


def _normmod_fwd(x, g, sc, sh, name):
    S, W = x.shape
    ts = _ts(S)

    def body(x_ref, g_ref, sc_ref, sh_ref, h_ref):
        xv = x_ref[...]
        h_ref[...] = ((xv * _rms(xv)) * g_ref[...] * (1.0 + sc_ref[...]) + sh_ref[...]).astype(BF16)

    return pl.pallas_call(
        body, grid=(S // ts,), in_specs=[_rows(ts, W), _vec(W), _vec(W), _vec(W)],
        out_specs=_rows(ts, W), out_shape=jax.ShapeDtypeStruct((S, W), BF16),
        compiler_params=_cp("parallel"), name=name)(x, g, sc, sh)


def _normmod_bwd(x, dh, g, sc, dres, name):
    S, W = x.shape
    ts = _ts(S)

    def body(x_ref, dh_ref, g_ref, sc_ref, dres_ref, dx_ref, dsh_ref, dsc_ref, dg_ref):
        @pl.when(pl.program_id(0) == 0)
        def _():
            dsh_ref[...] = jnp.zeros_like(dsh_ref)
            dsc_ref[...] = jnp.zeros_like(dsc_ref)
            dg_ref[...] = jnp.zeros_like(dg_ref)

        xv, dh_v, gv = x_ref[...], dh_ref[...], g_ref[...]
        r = _rms(xv)
        y = xv * r
        dn = dh_v * (1.0 + sc_ref[...])
        dy = dn * gv
        dx_ref[...] = dres_ref[...] + r * (dy - y * jnp.mean(dy * y, axis=-1, keepdims=True))
        dsh_ref[...] += _colsum(dh_v)
        dsc_ref[...] += _colsum(dh_v * y * gv)
        dg_ref[...] += _colsum(dn * y)

    vec_out = jax.ShapeDtypeStruct((1, W), F32)
    return pl.pallas_call(
        body, grid=(S // ts,),
        in_specs=[_rows(ts, W), _rows(ts, W), _vec(W), _vec(W), _rows(ts, W)],
        out_specs=[_rows(ts, W), _vec(W), _vec(W), _vec(W)],
        out_shape=[jax.ShapeDtypeStruct((S, W), F32), vec_out, vec_out, vec_out],
        compiler_params=_cp("arbitrary"), name=name)(x, dh, g, sc, dres)


def _rmsnorm_fwd(p, width, col, g, name):
    S = p.shape[0]
    ts = _ts(S)

    def body(x_ref, g_ref, y_ref):
        xv = x_ref[...]
        y_ref[...] = ((xv * _rms(xv)) * g_ref[...]).astype(BF16)

    return pl.pallas_call(
        body, grid=(S // ts,), in_specs=[_rows(ts, width, col), _vec(width)],
        out_specs=_rows(ts, width), out_shape=jax.ShapeDtypeStruct((S, width), BF16),
        compiler_params=_cp("parallel"), name=name)(p, g)


def _rmsnorm_bwd(p, width, col, dn, g, name):
    S = p.shape[0]
    ts = _ts(S)

    def body(x_ref, dn_ref, g_ref, dx_ref, dg_ref):
        @pl.when(pl.program_id(0) == 0)
        def _():
            dg_ref[...] = jnp.zeros_like(dg_ref)

        xv, dn_v = x_ref[...], dn_ref[...]
        r = _rms(xv)
        y = xv * r
        dy = dn_v * g_ref[...]
        dx_ref[...] = r * (dy - y * jnp.mean(dy * y, axis=-1, keepdims=True))
        dg_ref[...] += _colsum(dn_v * y)

    return pl.pallas_call(
        body, grid=(S // ts,), in_specs=[_rows(ts, width, col), _rows(ts, width), _vec(width)],
        out_specs=[_rows(ts, width), _vec(width)],
        out_shape=[jax.ShapeDtypeStruct((S, width), F32), jax.ShapeDtypeStruct((1, width), F32)],
        compiler_params=_cp("arbitrary"), name=name)(p, dn, g)


def _rope_rot(t, c, s1, s2):
    return t * c + pltpu.roll(t, LANE - 16, 1) * s1 + pltpu.roll(t, 16, 1) * s2


def _rope_rot_t(d, c, s1, s2):
    return d * c + pltpu.roll(d * s1, 16, 1) + pltpu.roll(d * s2, LANE - 16, 1)


def _rope_fwd(qp, p, tabs, name):
    S = qp.shape[0]
    ts = _ts(S)
    W = N_HEADS * LANE

    def body(q_ref, kr_ref, c_ref, s1_ref, s2_ref, qr_ref, kpe_ref):
        c, s1, s2 = c_ref[...], s1_ref[...], s2_ref[...]
        for h in range(N_HEADS):
            sl = slice(h * LANE, (h + 1) * LANE)
            qr_ref[:, sl] = _rope_rot(q_ref[:, sl], c, s1, s2).astype(BF16)
        kpe_ref[...] = _rope_rot(kr_ref[...], c, s1, s2).astype(BF16)

    tab = _rows(ts, LANE)
    return pl.pallas_call(
        body, grid=(S // ts,), in_specs=[_rows(ts, W), _rows(ts, LANE, COL_KROPE // LANE), tab, tab, tab],
        out_specs=[_rows(ts, W), _rows(ts, LANE)],
        out_shape=[jax.ShapeDtypeStruct((S, W), BF16), jax.ShapeDtypeStruct((S, LANE), BF16)],
        compiler_params=_cp("parallel"), name=name)(qp, p, *tabs)


def _rope_bwd(dqr, dkpe_heads, tabs, name):
    S = dqr.shape[0]
    ts = _ts(S)
    W = N_HEADS * LANE

    def body(dq_ref, dk_ref, c_ref, s1_ref, s2_ref, dqp_ref, dkr_ref):
        c, s1, s2 = c_ref[...], s1_ref[...], s2_ref[...]
        dk = dk_ref[:, 0:LANE]
        for h in range(N_HEADS):
            sl = slice(h * LANE, (h + 1) * LANE)
            dqp_ref[:, sl] = _rope_rot_t(dq_ref[:, sl], c, s1, s2).astype(BF16)
            if h:
                dk = dk + dk_ref[:, sl]
        dkr_ref[...] = _rope_rot_t(dk, c, s1, s2)

    tab = _rows(ts, LANE)
    return pl.pallas_call(
        body, grid=(S // ts,), in_specs=[_rows(ts, W), _rows(ts, W), tab, tab, tab],
        out_specs=[_rows(ts, W), _rows(ts, LANE)],
        out_shape=[jax.ShapeDtypeStruct((S, W), BF16), jax.ShapeDtypeStruct((S, LANE), F32)],
        compiler_params=_cp("parallel"), name=name)(dqr, dkpe_heads, *tabs)


def _merge_fwd(p, o_sb, o_mla, name):
    S, W = o_sb.shape
    ts = _ts(S)

    def body(gs_ref, gm_ref, a_ref, b_ref, m_ref):
        m_ref[...] = (jax.nn.sigmoid(gs_ref[...]) * a_ref[...]
                      + jax.nn.sigmoid(gm_ref[...]) * b_ref[...]).astype(BF16)

    return pl.pallas_call(
        body, grid=(S // ts,),
        in_specs=[_rows(ts, W, COL_GATE_SB // W), _rows(ts, W, COL_GATE_MLA // W), _rows(ts, W), _rows(ts, W)],
        out_specs=_rows(ts, W), out_shape=jax.ShapeDtypeStruct((S, W), BF16),
        compiler_params=_cp("parallel"), name=name)(p, p, o_sb, o_mla)


def _merge_bwd(p, o_sb, o_mla, dm, name):
    S, W = o_sb.shape
    ts = _ts(S)

    def body(gs_ref, gm_ref, a_ref, b_ref, dm_ref, da_ref, db_ref, dgs_ref, dgm_ref):
        dmv = dm_ref[...]
        sa, sb = jax.nn.sigmoid(gs_ref[...]), jax.nn.sigmoid(gm_ref[...])
        da_ref[...] = (dmv * sa).astype(BF16)
        db_ref[...] = (dmv * sb).astype(BF16)
        dgs_ref[...] = dmv * a_ref[...] * sa * (1.0 - sa)
        dgm_ref[...] = dmv * b_ref[...] * sb * (1.0 - sb)

    row = _rows(ts, W)
    return pl.pallas_call(
        body, grid=(S // ts,),
        in_specs=[_rows(ts, W, COL_GATE_SB // W), _rows(ts, W, COL_GATE_MLA // W), row, row, row],
        out_specs=[row, row, row, row],
        out_shape=[jax.ShapeDtypeStruct((S, W), BF16), jax.ShapeDtypeStruct((S, W), BF16),
                   jax.ShapeDtypeStruct((S, W), F32), jax.ShapeDtypeStruct((S, W), F32)],
        compiler_params=_cp("parallel"), name=name)(p, p, o_sb, o_mla, dm)


def _res_fwd(x, y, gate, name):
    S, W = x.shape
    ts = _ts(S)

    def body(x_ref, y_ref, g_ref, o_ref):
        o_ref[...] = x_ref[...] + g_ref[...] * y_ref[...]

    return pl.pallas_call(
        body, grid=(S // ts,), in_specs=[_rows(ts, W), _rows(ts, W), _vec(W)], out_specs=_rows(ts, W),
        out_shape=jax.ShapeDtypeStruct((S, W), F32), compiler_params=_cp("parallel"), name=name)(x, y, gate)


def _res_bwd(dx, y, gate, name):
    S, W = dx.shape
    ts = _ts(S)

    def body(dx_ref, y_ref, g_ref, dy_ref, dg_ref):
        @pl.when(pl.program_id(0) == 0)
        def _():
            dg_ref[...] = jnp.zeros_like(dg_ref)

        dxv = dx_ref[...]
        dy_ref[...] = (g_ref[...] * dxv).astype(BF16)
        dg_ref[...] += _colsum(dxv * y_ref[...])

    return pl.pallas_call(
        body, grid=(S // ts,), in_specs=[_rows(ts, W), _rows(ts, W), _vec(W)],
        out_specs=[_rows(ts, W), _vec(W)],
        out_shape=[jax.ShapeDtypeStruct((S, W), BF16), jax.ShapeDtypeStruct((1, W), F32)],
        compiler_params=_cp("arbitrary"), name=name)(dx, y, gate)


def _sqrelu_fwd(u, name):
    S, W = u.shape
    ts = _ts(S)

    def body(u_ref, a_ref):
        r = jnp.maximum(u_ref[...], 0.0)
        a_ref[...] = (r * r).astype(BF16)

    return pl.pallas_call(
        body, grid=(S // ts,), in_specs=[_rows(ts, W)], out_specs=_rows(ts, W),
        out_shape=jax.ShapeDtypeStruct((S, W), BF16), compiler_params=_cp("parallel"), name=name)(u)


def _sqrelu_bwd(da, u, name):
    S, W = u.shape
    ts = _ts(S)

    def body(da_ref, u_ref, du_ref):
        du_ref[...] = (da_ref[...] * (2.0 * jnp.maximum(u_ref[...], 0.0))).astype(BF16)

    return pl.pallas_call(
        body, grid=(S // ts,), in_specs=[_rows(ts, W), _rows(ts, W)], out_specs=_rows(ts, W),
        out_shape=jax.ShapeDtypeStruct((S, W), BF16), compiler_params=_cp("parallel"), name=name)(da, u)


def _final_loss(x, target, g, name):
    S, W = x.shape
    ts = _ts(S)

    def body(x_ref, t_ref, g_ref, dx_ref, dg_ref, loss_ref):
        @pl.when(pl.program_id(0) == 0)
        def _():
            dg_ref[...] = jnp.zeros_like(dg_ref)
            loss_ref[...] = jnp.zeros_like(loss_ref)

        xv, gv = x_ref[...], g_ref[...]
        r = _rms(xv)
        y = xv * r
        err = y * gv - t_ref[...]
        loss_ref[...] += jnp.full((1, LANE), 0.5 * jnp.sum(jnp.mean(err * err, axis=-1)), F32)
        dout = err * (1.0 / W)
        dy = dout * gv
        dx_ref[...] = r * (dy - y * jnp.mean(dy * y, axis=-1, keepdims=True))
        dg_ref[...] += _colsum(dout * y)

    return pl.pallas_call(
        body, grid=(S // ts,), in_specs=[_rows(ts, W), _rows(ts, W), _vec(W)],
        out_specs=[_rows(ts, W), _vec(W), _vec(LANE)],
        out_shape=[jax.ShapeDtypeStruct((S, W), F32), jax.ShapeDtypeStruct((1, W), F32),
                   jax.ShapeDtypeStruct((1, LANE), F32)],
        compiler_params=_cp("arbitrary"), name=name)(x, target, g)


def _silu(c, name):
    def body(c_ref, o_ref):
        cv = c_ref[...]
        o_ref[...] = cv * jax.nn.sigmoid(cv)

    return pl.pallas_call(body, out_shape=jax.ShapeDtypeStruct(c.shape, F32), name=name)(c)


def _bias_add(a, b, name):
    def body(a_ref, b_ref, o_ref):
        o_ref[...] = a_ref[...] + b_ref[...]

    return pl.pallas_call(body, out_shape=jax.ShapeDtypeStruct(a.shape, F32), name=name)(a, b)


def _sum_blocks(xs, name):
    n = xs.shape[0]

    def body(x_ref, o_ref):
        acc = x_ref[0]
        for d in range(1, n):
            acc = acc + x_ref[d]
        o_ref[...] = acc

    return pl.pallas_call(body, out_shape=jax.ShapeDtypeStruct(xs.shape[1:], F32), name=name)(xs)


def _adamw(w, g, m, v, name):
    shape = w.shape
    cols = shape[-1]
    w2, g2, m2, v2 = (t.reshape(-1, cols) for t in (w, g, m, v))
    rows = w2.shape[0]
    tr = _tile(rows, (128,))
    c1 = 1.0 - ADAM_B1 ** ADAM_STEP
    c2 = 1.0 - ADAM_B2 ** ADAM_STEP

    def body(w_ref, g_ref, m_ref, v_ref, d_ref, nm_ref, nv_ref):
        gv = g_ref[...]
        nm = ADAM_B1 * m_ref[...] + (1.0 - ADAM_B1) * gv
        nv = ADAM_B2 * v_ref[...] + (1.0 - ADAM_B2) * (gv * gv)
        d_ref[...] = -ADAM_LR * ((nm / c1) / (jnp.sqrt(nv / c2) + ADAM_EPS) + ADAM_WD * w_ref[...])
        nm_ref[...] = nm
        nv_ref[...] = nv

    spec = pl.BlockSpec((tr, cols), lambda i: (i, 0))
    out = jax.ShapeDtypeStruct((rows, cols), F32)
    d, nm, nv = pl.pallas_call(
        body, grid=(rows // tr,), in_specs=[spec] * 4, out_specs=[spec] * 3, out_shape=[out] * 3,
        compiler_params=_cp("parallel"), name=name)(w2, g2, m2, v2)
    return d.reshape(shape), nm.reshape(shape), nv.reshape(shape)


def _split_dot(x, tri):
    hi = x.astype(BF16)
    lo = (x - hi.astype(F32)).astype(BF16)
    return _nn(hi, tri) + _nn(lo, tri)


def _sb_logs(z):
    soft = jnp.log(1.0 + jnp.exp(-jnp.abs(z)))
    return jnp.minimum(z, 0.0) - soft, -jnp.maximum(z, 0.0) - soft


def _sb_fwd(p, name):
    S = p.shape[0]
    t = _ts(S)
    qb, kb, vb = COL_QSB // LANE, COL_KSB // LANE, COL_VSB // LANE

    def body(q_ref, k_ref, v_ref, o_ref, cb_ref, acc_ref):
        i = pl.program_id(1)
        lane = lax.broadcasted_iota(jnp.int32, (t, LANE), 1)
        rows = lax.broadcasted_iota(jnp.int32, (t, t), 0)
        cols = lax.broadcasted_iota(jnp.int32, (t, t), 1)
        after = jnp.where(rows > cols, 1.0, 0.0).astype(BF16)
        diag = cols < rows
        q = q_ref[...] * SB_SCALE
        acc_ref[...] = jnp.zeros_like(acc_ref)
        cb_ref[...] = jnp.zeros_like(cb_ref)
        for h in range(2):
            hm = (lane >= SB_DIM * h) & (lane < SB_DIM * (h + 1))
            qh = jnp.where(hm, q, 0.0).astype(BF16)

            def step(j, c, masked, h=h, hm=hm, qh=qh):
                rows_j = pl.ds(pl.multiple_of(j * t, t), t)
                kj = k_ref[rows_j, :].astype(BF16)
                vj = jnp.where(hm, v_ref[rows_j, :], 0.0).astype(BF16)
                ls, lf = _sb_logs(_nt(qh, kj))
                if masked:
                    lf = jnp.where(diag, lf, 0.0)
                a = jnp.exp(ls + _split_dot(lf, after) + c)
                if masked:
                    a = jnp.where(diag, a, 0.0)
                acc_ref[...] += _nn(a.astype(BF16), vj)
                cb_ref[h] = jnp.where(lane == j, c, cb_ref[h])
                return c + jnp.sum(lf, axis=1, keepdims=True)

            c = step(i, jnp.zeros((t, 1), F32), True)
            lax.fori_loop(0, i, lambda it, c: step(i - 1 - it, c, False), c)
        o_ref[...] = acc_ref[...].astype(BF16)

    return pl.pallas_call(
        body, grid=(SB_WIDTH // LANE, S // t),
        in_specs=[pl.BlockSpec((t, LANE), lambda hp, i: (i, qb + hp)),
                  pl.BlockSpec((S, LANE), lambda hp, i: (0, kb + hp)),
                  pl.BlockSpec((S, LANE), lambda hp, i: (0, vb + hp))],
        out_specs=[pl.BlockSpec((t, LANE), lambda hp, i: (i, hp)),
                   pl.BlockSpec((2, t, LANE), lambda hp, i: (hp, i, 0))],
        out_shape=[jax.ShapeDtypeStruct((S, SB_WIDTH), BF16), jax.ShapeDtypeStruct((N_HEADS, S, LANE), F32)],
        scratch_shapes=[pltpu.VMEM((t, LANE), F32)],
        compiler_params=_cp("arbitrary", "arbitrary"), name=name)(p, p, p)


def _sb_bwd(p, do, cb, name):
    S = p.shape[0]
    t = _ts(S)
    qb, kb, vb = COL_QSB // LANE, COL_KSB // LANE, COL_VSB // LANE

    def body(q_ref, k_ref, v_ref, do_ref, cb_ref, dq_ref, dk_ref, dv_ref, acc_ref):
        i = pl.program_id(1)

        @pl.when(i == 0)
        def _():
            dk_ref[...] = jnp.zeros_like(dk_ref)
            dv_ref[...] = jnp.zeros_like(dv_ref)

        lane = lax.broadcasted_iota(jnp.int32, (t, LANE), 1)
        rows = lax.broadcasted_iota(jnp.int32, (t, t), 0)
        cols = lax.broadcasted_iota(jnp.int32, (t, t), 1)
        after = jnp.where(rows > cols, 1.0, 0.0).astype(BF16)
        before = jnp.where(rows < cols, 1.0, 0.0).astype(BF16)
        diag = cols < rows
        q = q_ref[...] * SB_SCALE
        dov = do_ref[...]
        acc_ref[...] = jnp.zeros_like(acc_ref)
        for h in range(2):
            hm = (lane >= SB_DIM * h) & (lane < SB_DIM * (h + 1))
            qh = jnp.where(hm, q, 0.0).astype(BF16)
            doh = jnp.where(hm, dov, 0.0).astype(BF16)
            cbh = cb_ref[h]

            def step(j, f, masked, hm=hm, qh=qh, doh=doh, cbh=cbh):
                rows_j = pl.ds(pl.multiple_of(j * t, t), t)
                kf = k_ref[rows_j, :]
                ls, lf = _sb_logs(_nt(qh, kf.astype(BF16)))
                if masked:
                    lf = jnp.where(diag, lf, 0.0)
                c = jnp.sum(jnp.where(lane == j, cbh, 0.0), axis=1, keepdims=True)
                a = jnp.exp(ls + _split_dot(lf, after) + c)
                if masked:
                    a = jnp.where(diag, a, 0.0)
                dl = _nt(doh, v_ref[rows_j, :].astype(BF16)) * a
                sg = jnp.exp(ls)
                dz = dl * (1.0 - sg) - sg * (_split_dot(dl, before) + f)
                if masked:
                    dz = jnp.where(diag, dz, 0.0)
                dzb = dz.astype(BF16)
                acc_ref[...] += _nn(dzb, jnp.where(hm, kf, 0.0).astype(BF16))
                dk_ref[rows_j, :] += _tn(dzb, qh)
                dv_ref[rows_j, :] += _tn(a.astype(BF16), doh)
                return f + jnp.sum(dl, axis=1, keepdims=True)

            f = lax.fori_loop(0, i, lambda j, f: step(j, f, False), jnp.zeros((t, 1), F32))
            step(i, f, True)
        dq_ref[...] = acc_ref[...] * SB_SCALE

    col = lambda hp, i: (0, hp)
    out = jax.ShapeDtypeStruct((S, SB_WIDTH), F32)
    return pl.pallas_call(
        body, grid=(SB_WIDTH // LANE, S // t),
        in_specs=[pl.BlockSpec((t, LANE), lambda hp, i: (i, qb + hp)),
                  pl.BlockSpec((S, LANE), lambda hp, i: (0, kb + hp)),
                  pl.BlockSpec((S, LANE), lambda hp, i: (0, vb + hp)),
                  pl.BlockSpec((t, LANE), lambda hp, i: (i, hp)),
                  pl.BlockSpec((2, t, LANE), lambda hp, i: (hp, i, 0))],
        out_specs=[pl.BlockSpec((t, LANE), lambda hp, i: (i, hp)),
                   pl.BlockSpec((S, LANE), col), pl.BlockSpec((S, LANE), col)],
        out_shape=[out, out, out],
        scratch_shapes=[pltpu.VMEM((t, LANE), F32)],
        compiler_params=_cp("arbitrary", "arbitrary"), name=name)(p, p, p, do, cb)


def _mla_fwd(qr, kv, kpe, name):
    S = qr.shape[0]
    t = _ts(S)

    def body(q_ref, kv_ref, kpe_ref, o_ref, lse_ref, acc_ref, m_ref, l_ref):
        i = pl.program_id(1)
        low = lax.broadcasted_iota(jnp.int32, (t, LANE), 1) < NOPE_DIM
        rows = lax.broadcasted_iota(jnp.int32, (t, t), 0)
        cols = lax.broadcasted_iota(jnp.int32, (t, t), 1)
        causal = cols <= rows
        q = q_ref[...]
        acc_ref[...] = jnp.zeros_like(acc_ref)
        m_ref[...] = jnp.full_like(m_ref, NEG_BIG)
        l_ref[...] = jnp.zeros_like(l_ref)

        def step(j, masked):
            rows_j = pl.ds(pl.multiple_of(j * t, t), t)
            kvj = kv_ref[rows_j, :]
            z = _nt(q, jnp.where(low, kvj, kpe_ref[rows_j, :])) * MLA_SCALE
            if masked:
                z = jnp.where(causal, z, NEG_BIG)
            m_old = m_ref[...]
            m_new = jnp.maximum(m_old, jnp.max(z, axis=1, keepdims=True))
            pr = jnp.exp(z - m_new)
            alpha = jnp.exp(m_old - m_new)
            l_ref[...] = alpha * l_ref[...] + jnp.sum(pr, axis=1, keepdims=True)
            acc_ref[...] = alpha * acc_ref[...] + _nn(pr.astype(BF16), kvj)
            m_ref[...] = m_new

        def loop(j, carry):
            step(j, False)
            return carry

        lax.fori_loop(0, i, loop, 0)
        step(i, True)
        o_ref[...] = jnp.where(low, 0.0, acc_ref[...] / l_ref[...]).astype(BF16)
        lse_ref[0] = jnp.broadcast_to(m_ref[...] + jnp.log(l_ref[...]), (t, LANE))

    return pl.pallas_call(
        body, grid=(N_HEADS, S // t),
        in_specs=[pl.BlockSpec((t, LANE), lambda h, i: (i, h)),
                  pl.BlockSpec((S, LANE), lambda h, i: (0, h)),
                  pl.BlockSpec((S, LANE), lambda h, i: (0, 0))],
        out_specs=[pl.BlockSpec((t, LANE), lambda h, i: (i, h)),
                   pl.BlockSpec((1, t, LANE), lambda h, i: (h, i, 0))],
        out_shape=[jax.ShapeDtypeStruct((S, N_HEADS * LANE), BF16),
                   jax.ShapeDtypeStruct((N_HEADS, S, LANE), F32)],
        scratch_shapes=[pltpu.VMEM((t, LANE), F32), pltpu.VMEM((t, 1), F32), pltpu.VMEM((t, 1), F32)],
        compiler_params=_cp("arbitrary", "arbitrary"), name=name)(qr, kv, kpe)


def _mla_bwd(qr, kv, kpe, do, o, lse, name):
    S = qr.shape[0]
    t = _ts(S)

    def body(q_ref, kv_ref, kpe_ref, do_ref, o_ref, lse_ref, dq_ref, dkv_ref, dkpe_ref, acc_ref):
        i = pl.program_id(1)

        @pl.when(i == 0)
        def _():
            dkv_ref[...] = jnp.zeros_like(dkv_ref)
            dkpe_ref[...] = jnp.zeros_like(dkpe_ref)

        low = lax.broadcasted_iota(jnp.int32, (t, LANE), 1) < NOPE_DIM
        rows = lax.broadcasted_iota(jnp.int32, (t, t), 0)
        cols = lax.broadcasted_iota(jnp.int32, (t, t), 1)
        causal = cols <= rows
        q = q_ref[...]
        dov = do_ref[...]
        dob = dov.astype(BF16)
        delta = jnp.sum(dov * o_ref[...].astype(F32), axis=1, keepdims=True)
        lse_v = lse_ref[0][:, 0:1]
        acc_ref[...] = jnp.zeros_like(acc_ref)

        def step(j, masked):
            rows_j = pl.ds(pl.multiple_of(j * t, t), t)
            kvj = kv_ref[rows_j, :]
            kcat = jnp.where(low, kvj, kpe_ref[rows_j, :])
            z = _nt(q, kcat) * MLA_SCALE
            if masked:
                z = jnp.where(causal, z, NEG_BIG)
            pr = jnp.exp(z - lse_v)
            ds = (pr * (_nt(dob, kvj) - delta)).astype(BF16)
            acc_ref[...] += _nn(ds, kcat)
            dkc = _tn(ds, q) * MLA_SCALE
            dkv_ref[rows_j, :] += jnp.where(low, dkc, _tn(pr.astype(BF16), dob))
            dkpe_ref[rows_j, :] += jnp.where(low, 0.0, dkc)

        def loop(j, carry):
            step(j, False)
            return carry

        lax.fori_loop(0, i, loop, 0)
        step(i, True)
        dq_ref[...] = acc_ref[...] * MLA_SCALE

    blk = pl.BlockSpec((t, LANE), lambda h, i: (i, h))
    col = pl.BlockSpec((S, LANE), lambda h, i: (0, h))
    out = jax.ShapeDtypeStruct((S, N_HEADS * LANE), F32)
    return pl.pallas_call(
        body, grid=(N_HEADS, S // t),
        in_specs=[blk, col, pl.BlockSpec((S, LANE), lambda h, i: (0, 0)), blk, blk,
                  pl.BlockSpec((1, t, LANE), lambda h, i: (h, i, 0))],
        out_specs=[blk, col, col], out_shape=[out, out, out],
        scratch_shapes=[pltpu.VMEM((t, LANE), F32)],
        compiler_params=_cp("arbitrary", "arbitrary"), name=name)(qr, kv, kpe, do, o, lse)


_ANY = pl.BlockSpec(memory_space=pl.ANY)


def _place():
    return lax.axis_index("x"), lax.axis_index("y"), lax.axis_index("c")


def _dma_call(body, ins, out_shapes, n_remote, n_local, name):
    n_in, n_out = len(ins), len(out_shapes)

    def wrapped(*refs):
        body(refs[:n_in], refs[n_in:n_in + n_out], *refs[n_in + n_out:])

    return pl.pallas_call(
        wrapped, out_shape=out_shapes, in_specs=[_ANY] * n_in, out_specs=[_ANY] * n_out,
        scratch_shapes=[pltpu.SemaphoreType.DMA((n_remote,)), pltpu.SemaphoreType.DMA((n_remote,)),
                        pltpu.SemaphoreType.DMA((n_local,))],
        name=name)(*ins)


def _all_gather8(blks, name):
    n_t = len(blks)

    def body(x_refs, out_refs, send_sems, recv_sems, local_sems):
        x, y, c = _place()
        me, sibling = (x, y, c), (x, y, 1 - c)
        chips = [(1 - x, y), (x, 1 - y), (1 - x, 1 - y)]

        def rows(t, px, py, pc):
            m = blks[t].shape[0]
            return out_refs[t].at[pl.ds((4 * px + 2 * py + pc) * m, m), :]

        def copy(t, k, block, to, src=None):
            return pltpu.make_async_remote_copy(
                src_ref=rows(t, *block) if src is None else src, dst_ref=rows(t, *block),
                send_sem=send_sems.at[7 * t + k], recv_sem=recv_sems.at[7 * t + k], device_id=to,
                device_id_type=MESH)

        mine = [pltpu.make_async_copy(x_refs[t], rows(t, *me), local_sems.at[t]) for t in range(n_t)]
        for cp in mine:
            cp.start()
        first = [copy(t, 0, me, sibling, src=x_refs[t]) for t in range(n_t)]
        first += [copy(t, 1 + j, me, (*chip, c), src=x_refs[t]) for t in range(n_t) for j, chip in enumerate(chips)]
        for cp in first:
            cp.start()
        passed = []
        for t in range(n_t):
            for j, chip in enumerate(chips):
                copy(t, 1 + j, (*chip, c), me).wait_recv()
                passed.append(copy(t, 4 + j, (*chip, c), sibling))
                passed[-1].start()
        for t in range(n_t):
            copy(t, 0, sibling, me).wait_recv()
            for j, chip in enumerate(chips):
                copy(t, 4 + j, (*chip, 1 - c), me).wait_recv()
        for cp in first + passed:
            cp.wait_send()
        for cp in mine:
            cp.wait()

    outs = [jax.ShapeDtypeStruct((8 * b.shape[0], b.shape[1]), b.dtype) for b in blks]
    return _dma_call(body, blks, outs, 7 * n_t, n_t, name)


def _sibling_swap_halves(gs, name):
    n_t = len(gs)

    def body(g_refs, out_refs, send_sems, recv_sems, local_sems):
        x, y, c = _place()
        copies = []
        for t in range(n_t):
            m = gs[t].shape[1] // 2
            copies += [pltpu.make_async_remote_copy(
                src_ref=g_refs[t].at[s, pl.ds((1 - c) * m, m), :], dst_ref=out_refs[t].at[s],
                send_sem=send_sems.at[4 * t + s], recv_sem=recv_sems.at[4 * t + s], device_id=(x, y, 1 - c),
                device_id_type=MESH) for s in range(4)]
        for cp in copies:
            cp.start()
        for cp in copies:
            cp.wait()

    outs = [jax.ShapeDtypeStruct((4, g.shape[1] // 2, g.shape[2]), g.dtype) for g in gs]
    return _dma_call(body, gs, outs, 4 * n_t, 1, name)


def _chip_scatter(parts, name):
    n_t = len(parts)

    def body(p_refs, out_refs, send_sems, recv_sems, local_sems):
        x, y, c = _place()
        mine = 2 * x + y
        chips = [(1 - x, y), (x, 1 - y), (1 - x, 1 - y)]
        own = [pltpu.make_async_copy(p_refs[t].at[mine], out_refs[t].at[mine], local_sems.at[t])
               for t in range(n_t)]
        for cp in own:
            cp.start()

        def copy(t, j, src_slot, dst_slot):
            px, py = chips[j]
            return pltpu.make_async_remote_copy(
                src_ref=p_refs[t].at[src_slot], dst_ref=out_refs[t].at[dst_slot],
                send_sem=send_sems.at[3 * t + j], recv_sem=recv_sems.at[3 * t + j], device_id=(px, py, c),
                device_id_type=MESH)

        copies = [copy(t, j, 2 * px + py, mine) for t in range(n_t) for j, (px, py) in enumerate(chips)]
        for cp in copies:
            cp.start()
        for t in range(n_t):
            for j, (px, py) in enumerate(chips):
                copy(t, j, mine, 2 * px + py).wait_recv()
        for cp in copies:
            cp.wait_send()
        for cp in own:
            cp.wait()

    outs = [jax.ShapeDtypeStruct(p.shape, p.dtype) for p in parts]
    return _dma_call(body, parts, outs, 3 * n_t, n_t, name)


def _sibling_gather(halves, name):
    n_t = len(halves)

    def body(h_refs, out_refs, send_sems, recv_sems, local_sems):
        x, y, c = _place()

        def rows(t, pc):
            m = halves[t].shape[0]
            return out_refs[t].at[pl.ds(pc * m, m), :]

        def copy(t, pc):
            return pltpu.make_async_remote_copy(
                src_ref=h_refs[t], dst_ref=rows(t, pc), send_sem=send_sems.at[t], recv_sem=recv_sems.at[t],
                device_id=(x, y, 1 - c), device_id_type=MESH)

        own = [pltpu.make_async_copy(h_refs[t], rows(t, c), local_sems.at[t]) for t in range(n_t)]
        sends = [copy(t, c) for t in range(n_t)]
        for cp in own + sends:
            cp.start()
        for t in range(n_t):
            copy(t, 1 - c).wait_recv()
        for cp in sends:
            cp.wait_send()
        for cp in own:
            cp.wait()

    outs = [jax.ShapeDtypeStruct((2 * h.shape[0], h.shape[1]), h.dtype) for h in halves]
    return _dma_call(body, halves, outs, n_t, n_t, name)


def _add_halves(g, recv, c, name):
    n_slot, m2, n = g.shape
    m = m2 // 2
    tr = _tile(m, (752, 256, 128, 16))

    def body(c_ref, g_ref, r_ref, o_ref):
        o_ref[...] = (g_ref[...] + r_ref[...]).astype(BF16)

    nb = m // tr
    return pl.pallas_call(
        body,
        grid_spec=pltpu.PrefetchScalarGridSpec(
            num_scalar_prefetch=1, grid=(n_slot, nb),
            in_specs=[pl.BlockSpec((1, tr, n), lambda s, i, c_ref: (s, c_ref[0] * nb + i, 0)),
                      pl.BlockSpec((1, tr, n), lambda s, i, c_ref: (s, i, 0))],
            out_specs=pl.BlockSpec((1, tr, n), lambda s, i, c_ref: (s, i, 0))),
        out_shape=jax.ShapeDtypeStruct((n_slot, m, n), BF16),
        compiler_params=_cp("parallel", "parallel"), name=name)(c, g, recv)


def _sum_slots(parts, name):
    n_slot, m, n = parts.shape
    tr = _tile(m, (752, 256, 128, 16))

    def body(p_ref, o_ref):
        acc = p_ref[0].astype(F32)
        for s in range(1, n_slot):
            acc = acc + p_ref[s].astype(F32)
        o_ref[...] = acc

    return pl.pallas_call(
        body, grid=(m // tr,), in_specs=[pl.BlockSpec((n_slot, tr, n), lambda i: (0, i, 0))],
        out_specs=pl.BlockSpec((tr, n), lambda i: (i, 0)), out_shape=jax.ShapeDtypeStruct((m, n), F32),
        compiler_params=_cp("parallel"), name=name)(parts)


_SHARDED = ("w_in", "w_q_up", "w_kv_up", "w_sb_out", "w_mla_out", "w_mix_out", "w_up", "w_down")
_ROW_SHARDED = ("w_mix_out", "w_down")


def _unshard(parts, name):
    n, r, cs = parts.shape
    if name in _ROW_SHARDED:
        return parts.reshape(n * r, cs)
    return parts.transpose(1, 0, 2).reshape(r, n * cs)


def _reshard(full, name, n=4):
    R, C = full.shape
    if name in _ROW_SHARDED:
        return full.reshape(n, R // n, C)
    return full.reshape(R, n, C // n).transpose(1, 0, 2)


def _pad_w_in(w):
    z = lambda k: jnp.zeros(w.shape[:-1] + (k,), w.dtype)
    return jnp.concatenate([
        w[..., 2208:3232], w[..., 3232:4256], w[..., 0:1536], w[..., 1920:2176], w[..., 1536:1920],
        z(ROPE_LANE0), w[..., 2176:2208], z(LANE - ROPE_LANE0 - ROPE_DIM)], axis=-1)


def _unpad_w_in(g):
    k0 = COL_KROPE + ROPE_LANE0
    return jnp.concatenate([
        g[..., COL_QSB:COL_KVLAT], g[..., COL_QLAT:COL_KROPE], g[..., COL_KVLAT:COL_QLAT],
        g[..., k0:k0 + ROPE_DIM], g[..., 0:COL_QSB]], axis=-1)


def _pad_w_q(w):
    r = w.shape[0]
    return jnp.pad(w.reshape(r, N_HEADS, QK_DIM), ((0, 0), (0, 0), (0, LANE - QK_DIM))).reshape(r, N_HEADS * LANE)


def _unpad_w_q(g):
    r = g.shape[0]
    return g.reshape(r, N_HEADS, LANE)[..., :QK_DIM].reshape(r, N_HEADS * QK_DIM)


def _pad_w_mla(w):
    n = w.shape[1]
    return jnp.pad(w.reshape(N_HEADS, NOPE_DIM, n), ((0, 0), (LANE - NOPE_DIM, 0), (0, 0))).reshape(
        N_HEADS * LANE, n)


def _unpad_w_mla(g):
    n = g.shape[1]
    return g.reshape(N_HEADS, LANE, n)[:, LANE - NOPE_DIM:, :].reshape(N_HEADS * NOPE_DIM, n)


def _rope_tables(positions):
    half = ROPE_DIM // 2
    inv_freq = 1.0 / (ROPE_THETA ** (jnp.arange(0, ROPE_DIM, 2, dtype=F32) / ROPE_DIM))
    ang = positions.astype(F32)[:, None] * inv_freq
    cos, sin = jnp.cos(ang), jnp.sin(ang)
    S = positions.shape[0]
    one = jnp.ones((S, ROPE_LANE0), F32)
    zero = lambda k: jnp.zeros((S, k), F32)
    tail = LANE - ROPE_LANE0 - ROPE_DIM
    c = jnp.concatenate([one, cos, cos, zero(tail)], axis=1)
    s1 = jnp.concatenate([zero(ROPE_LANE0), -sin, zero(half + tail)], axis=1)
    s2 = jnp.concatenate([zero(ROPE_LANE0 + half), sin, zero(tail)], axis=1)
    return c, s1, s2


def _layer_fwd(x, W, mod, tabs):
    sh1, sc1, gt1, sh2, sc2, gt2 = (mod[i] for i in range(N_MOD))
    h1 = _normmod_fwd(x, W["g_mix"], sc1, sh1, "mix_norm_fwd")
    p = _matmul(h1, W["w_in"], name="in_proj")
    osbh, cb = _sb_fwd(p, "sb_attn_fwd")
    o_sb = _matmul(osbh, W["w_sb_out"], name="sb_out")
    qn = _rmsnorm_fwd(p, Q_RANK, COL_QLAT // Q_RANK, W["g_q"], "q_lat_norm_fwd")
    kvn = _rmsnorm_fwd(p, KV_RANK, COL_KVLAT // KV_RANK, W["g_kv"], "kv_lat_norm_fwd")
    qp = _matmul(qn, W["w_q_up"], name="q_up")
    kv = _matmul(kvn, W["w_kv_up"], out_dtype=BF16, name="kv_up")
    qr, kpe = _rope_fwd(qp, p, tabs, "rope_fwd")
    omh, lse = _mla_fwd(qr, kv, kpe, "mla_attn_fwd")
    o_mla = _matmul(omh, W["w_mla_out"], name="mla_out")
    merged = _merge_fwd(p, o_sb, o_mla, "merge_fwd")
    y1 = _matmul(merged, W["w_mix_out"], name="mix_out")
    x1 = _res_fwd(x, y1, gt1, "mix_residual")
    h2 = _normmod_fwd(x1, W["g_mlp"], sc2, sh2, "mlp_norm_fwd")
    u = _matmul(h2, W["w_up"], name="mlp_up")
    a = _sqrelu_fwd(u, "sqrelu_fwd")
    y2 = _matmul(a, W["w_down"], name="mlp_down")
    x2 = _res_fwd(x1, y2, gt2, "mlp_residual")
    saved = dict(x=x, h1=h1, p=p, osbh=osbh, cb=cb, o_sb=o_sb, qn=qn, kvn=kvn, qr=qr, kv=kv, kpe=kpe, omh=omh,
                 lse=lse, o_mla=o_mla, merged=merged, y1=y1, x1=x1, h2=h2, u=u, a=a, y2=y2)
    return x2, saved


def _layer_bwd(dx2, W, mod, tabs, sv):
    sh1, sc1, gt1, sh2, sc2, gt2 = (mod[i] for i in range(N_MOD))
    dy2, dgt2 = _res_bwd(dx2, sv["y2"], gt2, "mlp_residual_bwd")
    da = _matmul(dy2, W["w_down"], tb=True, name="mlp_down_dx")
    g_down = _matmul(sv["a"], dy2, ta=True, name="mlp_down_dw")
    du = _sqrelu_bwd(da, sv["u"], "sqrelu_bwd")
    dh2 = _matmul(du, W["w_up"], tb=True, name="mlp_up_dx")
    g_up = _matmul(sv["h2"], du, ta=True, name="mlp_up_dw")
    dx1, dsh2, dsc2, dg_mlp = _normmod_bwd(sv["x1"], dh2, W["g_mlp"], sc2, dx2, "mlp_norm_bwd")
    dy1, dgt1 = _res_bwd(dx1, sv["y1"], gt1, "mix_residual_bwd")
    dm = _matmul(dy1, W["w_mix_out"], tb=True, name="mix_out_dx")
    g_mix_out = _matmul(sv["merged"], dy1, ta=True, name="mix_out_dw")
    do_sb, do_mla, dgs, dgm = _merge_bwd(sv["p"], sv["o_sb"], sv["o_mla"], dm, "merge_bwd")
    do_sbh = _matmul(do_sb, W["w_sb_out"], tb=True, name="sb_out_dx")
    g_sb_out = _matmul(sv["osbh"], do_sb, ta=True, name="sb_out_dw")
    dqs, dks, dvs = _sb_bwd(sv["p"], do_sbh, sv["cb"], "sb_attn_bwd")
    do_mh = _matmul(do_mla, W["w_mla_out"], tb=True, name="mla_out_dx")
    g_mla_out = _matmul(sv["omh"], do_mla, ta=True, name="mla_out_dw")
    dqr, dkv, dkpe = _mla_bwd(sv["qr"], sv["kv"], sv["kpe"], do_mh, sv["omh"], sv["lse"], "mla_attn_bwd")
    dqp, dkr = _rope_bwd(dqr, dkpe, tabs, "rope_bwd")
    dqn = _matmul(dqp, W["w_q_up"], tb=True, name="q_up_dx")
    g_q_up = _matmul(sv["qn"], dqp, ta=True, name="q_up_dw")
    dkvn = _matmul(dkv, W["w_kv_up"], tb=True, name="kv_up_dx")
    g_kv_up = _matmul(sv["kvn"], dkv, ta=True, name="kv_up_dw")
    dqlat, dg_q = _rmsnorm_bwd(sv["p"], Q_RANK, COL_QLAT // Q_RANK, dqn, W["g_q"], "q_lat_norm_bwd")
    dkvlat, dg_kv = _rmsnorm_bwd(sv["p"], KV_RANK, COL_KVLAT // KV_RANK, dkvn, W["g_kv"], "kv_lat_norm_bwd")
    dp = jnp.concatenate([dgs, dgm, dqs, dks, dvs, dkvlat, dqlat, dkr], axis=1)
    dh1 = _matmul(dp, W["w_in"], tb=True, name="in_proj_dx")
    g_in = _matmul(sv["h1"], dp, ta=True, name="in_proj_dw")
    dx, dsh1, dsc1, dg_mix = _normmod_bwd(sv["x"], dh1, W["g_mix"], sc1, dx1, "mix_norm_bwd")
    grads = dict(w_in=g_in, w_q_up=g_q_up, w_kv_up=g_kv_up, w_sb_out=g_sb_out, w_mla_out=g_mla_out,
                 w_mix_out=g_mix_out, w_up=g_up, w_down=g_down,
                 dmod=jnp.concatenate([dsh1, dsc1, dgt1, dsh2, dsc2, dgt2], axis=0),
                 g_mix=dg_mix, g_mlp=dg_mlp, g_q=dg_q, g_kv=dg_kv)
    return dx, grads


def kernel(x, c, positions, w_ada, b_ada, g_mix_norm, w_in, g_q_lat, w_q_up, g_kv_lat, w_kv_up, w_sb_out, w_mla_out, w_mix_out, g_mlp_norm, w_up, w_down, g_final, loss_target, m_w_ada, m_b_ada, m_g_mix_norm, m_w_in, m_g_q_lat, m_w_q_up, m_g_kv_lat, m_w_kv_up, m_w_sb_out, m_w_mla_out, m_w_mix_out, m_g_mlp_norm, m_w_up, m_w_down, m_g_final, v_w_ada, v_b_ada, v_g_mix_norm, v_w_in, v_g_q_lat, v_w_q_up, v_g_kv_lat, v_w_kv_up, v_w_sb_out, v_w_mla_out, v_w_mix_out, v_g_mlp_norm, v_w_up, v_w_down, v_g_final):
    xi, yi, ci = _place()
    chip = 2 * xi + yi
    batch = 2 * chip + ci
    L = w_ada.shape[0]
    S = x.shape[1]
    shards = dict(w_in=w_in, w_q_up=w_q_up, w_kv_up=w_kv_up, w_sb_out=w_sb_out, w_mla_out=w_mla_out,
                  w_mix_out=w_mix_out, w_up=w_up, w_down=w_down)

    def my_half(w):
        rows2d = w.astype(BF16).reshape(-1, w.shape[-1])
        half = rows2d.shape[0] // 2
        return lax.dynamic_slice_in_dim(rows2d, ci * half, half, 0)

    gathered = _all_gather8([my_half(shards[n]) for n in _SHARDED], "gather_weights")
    full = [{} for _ in range(L)]
    for n, got in zip(_SHARDED, gathered):
        by_chip = got.reshape((4,) + shards[n].shape)
        for l in range(L):
            full[l][n] = _unshard(by_chip[:, l], n)
    for l in range(L):
        full[l]["w_in"] = _pad_w_in(full[l]["w_in"])
        full[l]["w_q_up"] = _pad_w_q(full[l]["w_q_up"])
        full[l]["w_mla_out"] = _pad_w_mla(full[l]["w_mla_out"])

    c_act = _silu(c, "silu_c")
    c_all = _all_gather8([jnp.broadcast_to(c_act, (8, D_MODEL))], "gather_c")[0].reshape(8, 8, D_MODEL)[:, 0]
    c16 = jnp.concatenate([c_all, jnp.zeros_like(c_all)], axis=0)
    ada_cols = w_ada.shape[2]
    b_shard = lax.dynamic_slice_in_dim(b_ada, chip * ada_cols, ada_cols, 1)
    mod_part = jnp.stack([_matmul(c16, w_ada[l], name="ada_mod") for l in range(L)])
    mod_part = _bias_add(mod_part, jnp.broadcast_to(b_shard[:, None, :], mod_part.shape), "ada_bias")
    mod_all = _all_gather8([mod_part.reshape(L * 16, ada_cols)], "gather_mod")[0].reshape(4, 2, L, 16, ada_cols)
    mod_mine = lax.dynamic_index_in_dim(mod_all[:, 0], batch, axis=2, keepdims=False)
    mods = mod_mine.transpose(1, 0, 2).reshape(L, N_MOD, 1, D_MODEL)

    tabs = _rope_tables(positions[0])
    layer_w = [dict(full[l], g_mix=g_mix_norm[l:l + 1], g_mlp=g_mlp_norm[l:l + 1], g_q=g_q_lat[l:l + 1],
                    g_kv=g_kv_lat[l:l + 1]) for l in range(L)]

    xc, saved = x[0], []
    for l in range(L):
        xc, sv = _layer_fwd(xc, layer_w[l], mods[l], tabs)
        saved.append(sv)
    dxc, dg_final, loss_part = _final_loss(xc, loss_target[0], g_final[None, :], "final_norm_loss")
    loss = lax.psum(loss_part[0, 0], ("x", "y", "c"))
    grads = [None] * L
    for l in reversed(range(L)):
        dxc, grads[l] = _layer_bwd(dxc, layer_w[l], mods[l], tabs, saved[l])
        grads[l]["w_in"] = _unpad_w_in(grads[l]["w_in"])
        grads[l]["w_q_up"] = _unpad_w_q(grads[l]["w_q_up"])
        grads[l]["w_mla_out"] = _unpad_w_mla(grads[l]["w_mla_out"])
    grad_x = dxc

    by_dest = [jnp.concatenate([_reshard(grads[l][n], n) for l in range(L)], axis=1) for n in _SHARDED]
    from_sibling = _sibling_swap_halves(by_dest, "grads_swap_halves")
    core = jnp.reshape(ci, (1,)).astype(jnp.int32)
    chip_part = [_add_halves(g, r, core, "grads_add_halves") for g, r in zip(by_dest, from_sibling)]
    from_chips = _chip_scatter(chip_part, "grads_chip_scatter")
    my_sum = [_sum_slots(p, "grads_sum_chips") for p in from_chips]
    summed = _sibling_gather(my_sum, "grads_sibling_gather")
    gw = {n: g.reshape(shards[n].shape) for n, g in zip(_SHARDED, summed)}

    def row(v):
        return jnp.pad(v, ((0, 0), (0, D_MODEL - v.shape[1])))

    per_layer_rows = N_MOD + 4
    small = jnp.concatenate(
        [jnp.concatenate([grads[l]["dmod"], row(grads[l]["g_mix"]), row(grads[l]["g_mlp"]),
                          row(grads[l]["g_q"]), row(grads[l]["g_kv"])], axis=0) for l in range(L)]
        + [dg_final], axis=0)
    n_small = -(-small.shape[0] // 8) * 8
    small = jnp.pad(small, ((0, n_small - small.shape[0]), (0, 0)))
    small_all = _all_gather8([small], "gather_vector_grads")[0].reshape(8, n_small, D_MODEL)
    small_sum = _sum_blocks(small_all, "sum_vector_grads")
    lay = small_sum[:L * per_layer_rows].reshape(L, per_layer_rows, D_MODEL)
    g_b_ada = lay[:, :N_MOD].reshape(L, N_MOD * D_MODEL)
    g_g_mix, g_g_mlp = lay[:, N_MOD], lay[:, N_MOD + 1]
    g_g_q, g_g_kv = lay[:, N_MOD + 2, :Q_RANK], lay[:, N_MOD + 3, :KV_RANK]
    g_g_final = small_sum[L * per_layer_rows]
    dmod_all = small_all[:, :L * per_layer_rows].reshape(8, L, per_layer_rows, D_MODEL)[:, :, :N_MOD]
    dmod_all = dmod_all.reshape(8, L, N_MOD * D_MODEL)
    dmod_cols = lax.dynamic_slice_in_dim(dmod_all, chip * ada_cols, ada_cols, 2)
    dmod16 = jnp.concatenate([dmod_cols, jnp.zeros_like(dmod_cols)], axis=0)
    g_w_ada = jnp.stack([_matmul(c16, dmod16[:, l], ta=True, name="ada_dw") for l in range(L)])

    weights = dict(w_ada=w_ada, b_ada=b_ada, g_mix_norm=g_mix_norm, w_in=w_in, g_q_lat=g_q_lat, w_q_up=w_q_up,
                   g_kv_lat=g_kv_lat, w_kv_up=w_kv_up, w_sb_out=w_sb_out, w_mla_out=w_mla_out,
                   w_mix_out=w_mix_out, g_mlp_norm=g_mlp_norm, w_up=w_up, w_down=w_down, g_final=g_final)
    mom = dict(w_ada=(m_w_ada, v_w_ada), b_ada=(m_b_ada, v_b_ada), g_mix_norm=(m_g_mix_norm, v_g_mix_norm),
               w_in=(m_w_in, v_w_in), g_q_lat=(m_g_q_lat, v_g_q_lat), w_q_up=(m_w_q_up, v_w_q_up),
               g_kv_lat=(m_g_kv_lat, v_g_kv_lat), w_kv_up=(m_w_kv_up, v_w_kv_up),
               w_sb_out=(m_w_sb_out, v_w_sb_out), w_mla_out=(m_w_mla_out, v_w_mla_out),
               w_mix_out=(m_w_mix_out, v_w_mix_out), g_mlp_norm=(m_g_mlp_norm, v_g_mlp_norm),
               w_up=(m_w_up, v_w_up), w_down=(m_w_down, v_w_down), g_final=(m_g_final, v_g_final))
    gr = dict(gw, w_ada=g_w_ada, b_ada=g_b_ada, g_mix_norm=g_g_mix, g_q_lat=g_g_q, g_kv_lat=g_g_kv,
              g_mlp_norm=g_g_mlp, g_final=g_g_final)
    order = list(weights)
    deltas, new_m, new_v = [], [], []
    for n in order:
        wv, gv, (mv, vv) = weights[n], gr[n], mom[n]
        if wv.ndim == 1:
            d, nm, nv = (t[0] for t in _adamw(wv[None], gv[None], mv[None], vv[None], "adamw_" + n))
        else:
            d, nm, nv = _adamw(wv, gv, mv, vv, "adamw_" + n)
        deltas.append(d)
        new_m.append(nm)
        new_v.append(nv)
    return (loss, grad_x[None], *[gr[n] for n in order], *deltas, *new_m, *new_v)
```

```python
import functools

import jax
import jax.numpy as jnp
from jax import lax
from jax.experimental import pallas as pl
from jax.experimental.pallas import tpu as pltpu

F32 = jnp.float32
BF16 = jnp.bfloat16
MESH = pl.DeviceIdType.MESH

D_MODEL = 1024
N_HEADS = 8
SB_DIM = 64
SB_WIDTH = 512
Q_RANK = 384
KV_RANK = 256
ROPE_DIM = 32
NOPE_DIM = 64
QK_DIM = 96
D_FF = 4096
N_MOD = 6
EPS = 1e-6
ROPE_THETA = 10000.0
SB_SCALE = SB_DIM ** -0.5
MLA_SCALE = QK_DIM ** -0.5
ADAM_LR, ADAM_B1, ADAM_B2, ADAM_EPS, ADAM_WD, ADAM_STEP = 0.001, 0.9, 0.999, 1e-08, 0.01, 10

LANE = 128
IN_PAD = 4352
COL_GATE_SB, COL_GATE_MLA, COL_QSB, COL_KSB, COL_VSB, COL_KVLAT, COL_QLAT, COL_KROPE = (
    0, 1024, 2048, 2560, 3072, 3584, 3840, 4224)
ROPE_LANE0 = 64
VMEM_LIMIT = 48 * 1024 * 1024
NEG_BIG = -1e30


def _cp(*sem):
    return pltpu.CompilerParams(dimension_semantics=sem, vmem_limit_bytes=VMEM_LIMIT)


def _tile(n, prefs):
    for t in prefs:
        if t <= n and n % t == 0:
            return t
    return n


def _dot(a, b, dims):
    return lax.dot_general(a, b, (dims, ((), ())), preferred_element_type=F32)


def _nn(a, b):
    return _dot(a, b, ((1,), (0,)))


def _nt(a, b):
    return _dot(a, b, ((1,), (1,)))


def _tn(a, b):
    return _dot(a, b, ((0,), (0,)))


def _sharded_dims(shape, kind):
    n, r, cs = shape
    return (n * r, cs) if kind == "row" else (r, n * cs)


def _sharded_spec(shape, kind, t_rows, t_cols, tile_of):
    _, r, cs = shape
    if kind == "row":
        assert r % t_rows == 0, (shape, t_rows)
        per = r // t_rows

        def index(i, j, k):
            tr, tc = tile_of(i, j, k)
            return tr // per, tr % per, tc
    else:
        assert cs % t_cols == 0, (shape, t_cols)
        per = cs // t_cols

        def index(i, j, k):
            tr, tc = tile_of(i, j, k)
            return tc // per, tr, tc % per
    return pl.BlockSpec((None, t_rows, t_cols), index)


def _matmul(a, b, *, ta=False, tb=False, out_dtype=F32, b_sharded=None, out_sharded=None, name):
    (K, M) = a.shape if ta else a.shape[::-1]
    b_dims = _sharded_dims(b.shape, b_sharded) if b_sharded else b.shape
    (N, Kb) = b_dims if tb else b_dims[::-1]
    assert K == Kb, (a.shape, b.shape, ta, tb)
    tm = _tile(M, (512, 384, 256, 128))
    tn = _tile(N, (1024, 2176, 768, 512, 384, 256, 128))
    tk = _tile(K, (1024, 2176, 768, 512, 384, 256, 128))
    nk = K // tk
    dims = ((0 if ta else 1,), (1 if tb else 0,))

    def body(a_ref, b_ref, o_ref, *acc):
        prod = _dot(a_ref[...].astype(BF16), b_ref[...].astype(BF16), dims)
        if nk == 1:
            o_ref[...] = prod.astype(out_dtype)
            return
        acc_ref, = acc
        k = pl.program_id(2)

        @pl.when(k == 0)
        def _():
            acc_ref[...] = prod

        @pl.when(k > 0)
        def _():
            acc_ref[...] += prod

        @pl.when(k == nk - 1)
        def _():
            o_ref[...] = acc_ref[...].astype(out_dtype)

    a_spec = (pl.BlockSpec((tk, tm), lambda i, j, k: (k, i)) if ta
              else pl.BlockSpec((tm, tk), lambda i, j, k: (i, k)))
    if b_sharded:
        b_spec = (_sharded_spec(b.shape, b_sharded, tn, tk, lambda i, j, k: (j, k)) if tb
                  else _sharded_spec(b.shape, b_sharded, tk, tn, lambda i, j, k: (k, j)))
    else:
        b_spec = (pl.BlockSpec((tn, tk), lambda i, j, k: (j, k)) if tb
                  else pl.BlockSpec((tk, tn), lambda i, j, k: (k, j)))
    if out_sharded:
        kind, shape = out_sharded
        assert _sharded_dims(shape, kind) == (M, N), (shape, kind, M, N)
        out_spec = _sharded_spec(shape, kind, tm, tn, lambda i, j, k: (i, j))
        out_shape = jax.ShapeDtypeStruct(shape, out_dtype)
    else:
        out_spec = pl.BlockSpec((tm, tn), lambda i, j, k: (i, j))
        out_shape = jax.ShapeDtypeStruct((M, N), out_dtype)
    return pl.pallas_call(
        body, grid=(M // tm, N // tn, nk), in_specs=[a_spec, b_spec], out_specs=out_spec, out_shape=out_shape,
        scratch_shapes=[pltpu.VMEM((tm, tn), F32)] if nk > 1 else [],
        compiler_params=_cp("parallel", "parallel", "arbitrary"), name=name)(a, b)


def _rows(ts, w, col=0):
    return pl.BlockSpec((ts, w), lambda i: (i, col))


def _vec(w):
    return pl.BlockSpec((1, w), lambda i: (0, 0))


def _ts(S):
    return _tile(S, (256, 128))


def _rms(x):
    return lax.rsqrt(jnp.mean(x * x, axis=-1, keepdims=True) + EPS)


def _colsum(x):
    return jnp.sum(x, axis=0, keepdims=True)


def _normmod_fwd(x, g, sc, sh, name):
    S, W = x.shape
    ts = _ts(S)

    def body(x_ref, g_ref, sc_ref, sh_ref, h_ref):
        xv = x_ref[...]
        h_ref[...] = ((xv * _rms(xv)) * g_ref[...] * (1.0 + sc_ref[...]) + sh_ref[...]).astype(BF16)

    return pl.pallas_call(
        body, grid=(S // ts,), in_specs=[_rows(ts, W), _vec(W), _vec(W), _vec(W)],
        out_specs=_rows(ts, W), out_shape=jax.ShapeDtypeStruct((S, W), BF16),
        compiler_params=_cp("parallel"), name=name)(x, g, sc, sh)


def _normmod_bwd(x, dh, g, sc, dres, name):
    S, W = x.shape
    ts = _ts(S)

    def body(x_ref, dh_ref, g_ref, sc_ref, dres_ref, dx_ref, dsh_ref, dsc_ref, dg_ref):
        @pl.when(pl.program_id(0) == 0)
        def _():
            dsh_ref[...] = jnp.zeros_like(dsh_ref)
            dsc_ref[...] = jnp.zeros_like(dsc_ref)
            dg_ref[...] = jnp.zeros_like(dg_ref)

        xv, dh_v, gv = x_ref[...], dh_ref[...], g_ref[...]
        r = _rms(xv)
        y = xv * r
        dn = dh_v * (1.0 + sc_ref[...])
        dy = dn * gv
        dx_ref[...] = dres_ref[...] + r * (dy - y * jnp.mean(dy * y, axis=-1, keepdims=True))
        dsh_ref[...] += _colsum(dh_v)
        dsc_ref[...] += _colsum(dh_v * y * gv)
        dg_ref[...] += _colsum(dn * y)

    vec_out = jax.ShapeDtypeStruct((1, W), F32)
    return pl.pallas_call(
        body, grid=(S // ts,),
        in_specs=[_rows(ts, W), _rows(ts, W), _vec(W), _vec(W), _rows(ts, W)],
        out_specs=[_rows(ts, W), _vec(W), _vec(W), _vec(W)],
        out_shape=[jax.ShapeDtypeStruct((S, W), F32), vec_out, vec_out, vec_out],
        compiler_params=_cp("arbitrary"), name=name)(x, dh, g, sc, dres)


def _rmsnorm_fwd(p, width, col, g, name):
    S = p.shape[0]
    ts = _ts(S)

    def body(x_ref, g_ref, y_ref):
        xv = x_ref[...]
        y_ref[...] = ((xv * _rms(xv)) * g_ref[...]).astype(BF16)

    return pl.pallas_call(
        body, grid=(S // ts,), in_specs=[_rows(ts, width, col), _vec(width)],
        out_specs=_rows(ts, width), out_shape=jax.ShapeDtypeStruct((S, width), BF16),
        compiler_params=_cp("parallel"), name=name)(p, g)


def _rmsnorm_bwd(p, width, col, dn, g, name):
    S = p.shape[0]
    ts = _ts(S)

    def body(x_ref, dn_ref, g_ref, dx_ref, dg_ref):
        @pl.when(pl.program_id(0) == 0)
        def _():
            dg_ref[...] = jnp.zeros_like(dg_ref)

        xv, dn_v = x_ref[...], dn_ref[...]
        r = _rms(xv)
        y = xv * r
        dy = dn_v * g_ref[...]
        dx_ref[...] = r * (dy - y * jnp.mean(dy * y, axis=-1, keepdims=True))
        dg_ref[...] += _colsum(dn_v * y)

    return pl.pallas_call(
        body, grid=(S // ts,), in_specs=[_rows(ts, width, col), _rows(ts, width), _vec(width)],
        out_specs=[_rows(ts, width), _vec(width)],
        out_shape=[jax.ShapeDtypeStruct((S, width), F32), jax.ShapeDtypeStruct((1, width), F32)],
        compiler_params=_cp("arbitrary"), name=name)(p, dn, g)


def _rope_rot(t, c, s1, s2):
    return t * c + pltpu.roll(t, LANE - 16, 1) * s1 + pltpu.roll(t, 16, 1) * s2


def _rope_rot_t(d, c, s1, s2):
    return d * c + pltpu.roll(d * s1, 16, 1) + pltpu.roll(d * s2, LANE - 16, 1)


def _rope_fwd(qp, p, tabs, name):
    S = qp.shape[0]
    ts = _ts(S)
    W = N_HEADS * LANE

    def body(q_ref, kr_ref, c_ref, s1_ref, s2_ref, qr_ref, kpe_ref):
        c, s1, s2 = c_ref[...], s1_ref[...], s2_ref[...]
        for h in range(N_HEADS):
            sl = slice(h * LANE, (h + 1) * LANE)
            qr_ref[:, sl] = _rope_rot(q_ref[:, sl], c, s1, s2).astype(BF16)
        kpe_ref[...] = _rope_rot(kr_ref[...], c, s1, s2).astype(BF16)

    tab = _rows(ts, LANE)
    return pl.pallas_call(
        body, grid=(S // ts,), in_specs=[_rows(ts, W), _rows(ts, LANE, COL_KROPE // LANE), tab, tab, tab],
        out_specs=[_rows(ts, W), _rows(ts, LANE)],
        out_shape=[jax.ShapeDtypeStruct((S, W), BF16), jax.ShapeDtypeStruct((S, LANE), BF16)],
        compiler_params=_cp("parallel"), name=name)(qp, p, *tabs)


def _rope_bwd(dqr, dkpe_heads, tabs, name):
    S = dqr.shape[0]
    ts = _ts(S)
    W = N_HEADS * LANE

    def body(dq_ref, dk_ref, c_ref, s1_ref, s2_ref, dqp_ref, dkr_ref):
        c, s1, s2 = c_ref[...], s1_ref[...], s2_ref[...]
        dk = dk_ref[:, 0:LANE]
        for h in range(N_HEADS):
            sl = slice(h * LANE, (h + 1) * LANE)
            dqp_ref[:, sl] = _rope_rot_t(dq_ref[:, sl], c, s1, s2).astype(BF16)
            if h:
                dk = dk + dk_ref[:, sl]
        dkr_ref[...] = _rope_rot_t(dk, c, s1, s2)

    tab = _rows(ts, LANE)
    return pl.pallas_call(
        body, grid=(S // ts,), in_specs=[_rows(ts, W), _rows(ts, W), tab, tab, tab],
        out_specs=[_rows(ts, W), _rows(ts, LANE)],
        out_shape=[jax.ShapeDtypeStruct((S, W), BF16), jax.ShapeDtypeStruct((S, LANE), F32)],
        compiler_params=_cp("parallel"), name=name)(dqr, dkpe_heads, *tabs)


def _merge_fwd(p, o_sb, o_mla, name):
    S, W = o_sb.shape
    ts = _ts(S)

    def body(gs_ref, gm_ref, a_ref, b_ref, m_ref):
        m_ref[...] = (jax.nn.sigmoid(gs_ref[...]) * a_ref[...]
                      + jax.nn.sigmoid(gm_ref[...]) * b_ref[...]).astype(BF16)

    return pl.pallas_call(
        body, grid=(S // ts,),
        in_specs=[_rows(ts, W, COL_GATE_SB // W), _rows(ts, W, COL_GATE_MLA // W), _rows(ts, W), _rows(ts, W)],
        out_specs=_rows(ts, W), out_shape=jax.ShapeDtypeStruct((S, W), BF16),
        compiler_params=_cp("parallel"), name=name)(p, p, o_sb, o_mla)


def _merge_bwd(p, o_sb, o_mla, dm, name):
    S, W = o_sb.shape
    ts = _ts(S)

    def body(gs_ref, gm_ref, a_ref, b_ref, dm_ref, da_ref, db_ref, dgs_ref, dgm_ref):
        dmv = dm_ref[...]
        sa, sb = jax.nn.sigmoid(gs_ref[...]), jax.nn.sigmoid(gm_ref[...])
        da_ref[...] = (dmv * sa).astype(BF16)
        db_ref[...] = (dmv * sb).astype(BF16)
        dgs_ref[...] = dmv * a_ref[...] * sa * (1.0 - sa)
        dgm_ref[...] = dmv * b_ref[...] * sb * (1.0 - sb)

    row = _rows(ts, W)
    return pl.pallas_call(
        body, grid=(S // ts,),
        in_specs=[_rows(ts, W, COL_GATE_SB // W), _rows(ts, W, COL_GATE_MLA // W), row, row, row],
        out_specs=[row, row, row, row],
        out_shape=[jax.ShapeDtypeStruct((S, W), BF16), jax.ShapeDtypeStruct((S, W), BF16),
                   jax.ShapeDtypeStruct((S, W), F32), jax.ShapeDtypeStruct((S, W), F32)],
        compiler_params=_cp("parallel"), name=name)(p, p, o_sb, o_mla, dm)


def _res_fwd(x, y, gate, name):
    S, W = x.shape
    ts = _ts(S)

    def body(x_ref, y_ref, g_ref, o_ref):
        o_ref[...] = x_ref[...] + g_ref[...] * y_ref[...]

    return pl.pallas_call(
        body, grid=(S // ts,), in_specs=[_rows(ts, W), _rows(ts, W), _vec(W)], out_specs=_rows(ts, W),
        out_shape=jax.ShapeDtypeStruct((S, W), F32), compiler_params=_cp("parallel"), name=name)(x, y, gate)


def _res_bwd(dx, y, gate, name):
    S, W = dx.shape
    ts = _ts(S)

    def body(dx_ref, y_ref, g_ref, dy_ref, dg_ref):
        @pl.when(pl.program_id(0) == 0)
        def _():
            dg_ref[...] = jnp.zeros_like(dg_ref)

        dxv = dx_ref[...]
        dy_ref[...] = (g_ref[...] * dxv).astype(BF16)
        dg_ref[...] += _colsum(dxv * y_ref[...])

    return pl.pallas_call(
        body, grid=(S // ts,), in_specs=[_rows(ts, W), _rows(ts, W), _vec(W)],
        out_specs=[_rows(ts, W), _vec(W)],
        out_shape=[jax.ShapeDtypeStruct((S, W), BF16), jax.ShapeDtypeStruct((1, W), F32)],
        compiler_params=_cp("arbitrary"), name=name)(dx, y, gate)


def _sqrelu_fwd(u, name):
    S, W = u.shape
    ts = _ts(S)

    def body(u_ref, a_ref):
        r = jnp.maximum(u_ref[...], 0.0)
        a_ref[...] = (r * r).astype(BF16)

    return pl.pallas_call(
        body, grid=(S // ts,), in_specs=[_rows(ts, W)], out_specs=_rows(ts, W),
        out_shape=jax.ShapeDtypeStruct((S, W), BF16), compiler_params=_cp("parallel"), name=name)(u)


def _sqrelu_bwd(da, u, name):
    S, W = u.shape
    ts = _ts(S)

    def body(da_ref, u_ref, du_ref):
        du_ref[...] = (da_ref[...] * (2.0 * jnp.maximum(u_ref[...], 0.0))).astype(BF16)

    return pl.pallas_call(
        body, grid=(S // ts,), in_specs=[_rows(ts, W), _rows(ts, W)], out_specs=_rows(ts, W),
        out_shape=jax.ShapeDtypeStruct((S, W), BF16), compiler_params=_cp("parallel"), name=name)(da, u)


def _final_loss(x, target, g, name):
    S, W = x.shape
    ts = _ts(S)

    def body(x_ref, t_ref, g_ref, dx_ref, dg_ref, loss_ref):
        @pl.when(pl.program_id(0) == 0)
        def _():
            dg_ref[...] = jnp.zeros_like(dg_ref)
            loss_ref[...] = jnp.zeros_like(loss_ref)

        xv, gv = x_ref[...], g_ref[...]
        r = _rms(xv)
        y = xv * r
        err = y * gv - t_ref[...]
        loss_ref[...] += jnp.full((1, LANE), 0.5 * jnp.sum(jnp.mean(err * err, axis=-1)), F32)
        dout = err * (1.0 / W)
        dy = dout * gv
        dx_ref[...] = r * (dy - y * jnp.mean(dy * y, axis=-1, keepdims=True))
        dg_ref[...] += _colsum(dout * y)

    return pl.pallas_call(
        body, grid=(S // ts,), in_specs=[_rows(ts, W), _rows(ts, W), _vec(W)],
        out_specs=[_rows(ts, W), _vec(W), _vec(LANE)],
        out_shape=[jax.ShapeDtypeStruct((S, W), F32), jax.ShapeDtypeStruct((1, W), F32),
                   jax.ShapeDtypeStruct((1, LANE), F32)],
        compiler_params=_cp("arbitrary"), name=name)(x, target, g)


def _silu(c, name):
    def body(c_ref, o_ref):
        cv = c_ref[...]
        o_ref[...] = cv * jax.nn.sigmoid(cv)

    return pl.pallas_call(body, out_shape=jax.ShapeDtypeStruct(c.shape, F32), name=name)(c)


def _bias_add(a, b, name):
    def body(a_ref, b_ref, o_ref):
        o_ref[...] = a_ref[...] + b_ref[...]

    return pl.pallas_call(body, out_shape=jax.ShapeDtypeStruct(a.shape, F32), name=name)(a, b)


def _sum_blocks(xs, name):
    n = xs.shape[0]

    def body(x_ref, o_ref):
        acc = x_ref[0]
        for d in range(1, n):
            acc = acc + x_ref[d]
        o_ref[...] = acc

    return pl.pallas_call(body, out_shape=jax.ShapeDtypeStruct(xs.shape[1:], F32), name=name)(xs)


def _adamw(w, g, m, v, name):
    shape = w.shape
    cols = shape[-1]
    w2, g2, m2, v2 = (t.reshape(-1, cols) for t in (w, g, m, v))
    rows = w2.shape[0]
    tr = _tile(rows, (128,))
    c1 = 1.0 - ADAM_B1 ** ADAM_STEP
    c2 = 1.0 - ADAM_B2 ** ADAM_STEP

    def body(w_ref, g_ref, m_ref, v_ref, d_ref, nm_ref, nv_ref):
        gv = g_ref[...]
        nm = ADAM_B1 * m_ref[...] + (1.0 - ADAM_B1) * gv
        nv = ADAM_B2 * v_ref[...] + (1.0 - ADAM_B2) * (gv * gv)
        d_ref[...] = -ADAM_LR * ((nm / c1) / (jnp.sqrt(nv / c2) + ADAM_EPS) + ADAM_WD * w_ref[...])
        nm_ref[...] = nm
        nv_ref[...] = nv

    spec = pl.BlockSpec((tr, cols), lambda i: (i, 0))
    out = jax.ShapeDtypeStruct((rows, cols), F32)
    d, nm, nv = pl.pallas_call(
        body, grid=(rows // tr,), in_specs=[spec] * 4, out_specs=[spec] * 3, out_shape=[out] * 3,
        compiler_params=_cp("parallel"), name=name)(w2, g2, m2, v2)
    return d.reshape(shape), nm.reshape(shape), nv.reshape(shape)


def _split_dot(x, tri):
    hi = x.astype(BF16)
    lo = (x - hi.astype(F32)).astype(BF16)
    return _nn(hi, tri) + _nn(lo, tri)


def _sb_logs(z):
    soft = jnp.log(1.0 + jnp.exp(-jnp.abs(z)))
    return jnp.minimum(z, 0.0) - soft, -jnp.maximum(z, 0.0) - soft


def _sb_fwd(p, name):
    S = p.shape[0]
    t = _ts(S)
    qb, kb, vb = COL_QSB // LANE, COL_KSB // LANE, COL_VSB // LANE

    def body(q_ref, k_ref, v_ref, o_ref, cb_ref, acc_ref):
        i = pl.program_id(1)
        lane = lax.broadcasted_iota(jnp.int32, (t, LANE), 1)
        rows = lax.broadcasted_iota(jnp.int32, (t, t), 0)
        cols = lax.broadcasted_iota(jnp.int32, (t, t), 1)
        after = jnp.where(rows > cols, 1.0, 0.0).astype(BF16)
        diag = cols < rows
        q = q_ref[...] * SB_SCALE
        acc_ref[...] = jnp.zeros_like(acc_ref)
        cb_ref[...] = jnp.zeros_like(cb_ref)
        for h in range(2):
            hm = (lane >= SB_DIM * h) & (lane < SB_DIM * (h + 1))
            qh = jnp.where(hm, q, 0.0).astype(BF16)

            def step(j, c, masked, h=h, hm=hm, qh=qh):
                rows_j = pl.ds(pl.multiple_of(j * t, t), t)
                kj = k_ref[rows_j, :].astype(BF16)
                vj = jnp.where(hm, v_ref[rows_j, :], 0.0).astype(BF16)
                ls, lf = _sb_logs(_nt(qh, kj))
                if masked:
                    lf = jnp.where(diag, lf, 0.0)
                a = jnp.exp(ls + _split_dot(lf, after) + c)
                if masked:
                    a = jnp.where(diag, a, 0.0)
                acc_ref[...] += _nn(a.astype(BF16), vj)
                cb_ref[h] = jnp.where(lane == j, c, cb_ref[h])
                return c + jnp.sum(lf, axis=1, keepdims=True)

            c = step(i, jnp.zeros((t, 1), F32), True)
            lax.fori_loop(0, i, lambda it, c: step(i - 1 - it, c, False), c)
        o_ref[...] = acc_ref[...].astype(BF16)

    return pl.pallas_call(
        body, grid=(SB_WIDTH // LANE, S // t),
        in_specs=[pl.BlockSpec((t, LANE), lambda hp, i: (i, qb + hp)),
                  pl.BlockSpec((S, LANE), lambda hp, i: (0, kb + hp)),
                  pl.BlockSpec((S, LANE), lambda hp, i: (0, vb + hp))],
        out_specs=[pl.BlockSpec((t, LANE), lambda hp, i: (i, hp)),
                   pl.BlockSpec((2, t, LANE), lambda hp, i: (hp, i, 0))],
        out_shape=[jax.ShapeDtypeStruct((S, SB_WIDTH), BF16), jax.ShapeDtypeStruct((N_HEADS, S, LANE), F32)],
        scratch_shapes=[pltpu.VMEM((t, LANE), F32)],
        compiler_params=_cp("arbitrary", "arbitrary"), name=name)(p, p, p)


def _sb_bwd(p, do, cb, name):
    S = p.shape[0]
    t = _ts(S)
    qb, kb, vb = COL_QSB // LANE, COL_KSB // LANE, COL_VSB // LANE

    def body(q_ref, k_ref, v_ref, do_ref, cb_ref, dq_ref, dk_ref, dv_ref, acc_ref):
        i = pl.program_id(1)

        @pl.when(i == 0)
        def _():
            dk_ref[...] = jnp.zeros_like(dk_ref)
            dv_ref[...] = jnp.zeros_like(dv_ref)

        lane = lax.broadcasted_iota(jnp.int32, (t, LANE), 1)
        rows = lax.broadcasted_iota(jnp.int32, (t, t), 0)
        cols = lax.broadcasted_iota(jnp.int32, (t, t), 1)
        after = jnp.where(rows > cols, 1.0, 0.0).astype(BF16)
        before = jnp.where(rows < cols, 1.0, 0.0).astype(BF16)
        diag = cols < rows
        q = q_ref[...] * SB_SCALE
        dov = do_ref[...]
        acc_ref[...] = jnp.zeros_like(acc_ref)
        for h in range(2):
            hm = (lane >= SB_DIM * h) & (lane < SB_DIM * (h + 1))
            qh = jnp.where(hm, q, 0.0).astype(BF16)
            doh = jnp.where(hm, dov, 0.0).astype(BF16)
            cbh = cb_ref[h]

            def step(j, f, masked, hm=hm, qh=qh, doh=doh, cbh=cbh):
                rows_j = pl.ds(pl.multiple_of(j * t, t), t)
                kf = k_ref[rows_j, :]
                ls, lf = _sb_logs(_nt(qh, kf.astype(BF16)))
                if masked:
                    lf = jnp.where(diag, lf, 0.0)
                c = jnp.sum(jnp.where(lane == j, cbh, 0.0), axis=1, keepdims=True)
                a = jnp.exp(ls + _split_dot(lf, after) + c)
                if masked:
                    a = jnp.where(diag, a, 0.0)
                dl = _nt(doh, v_ref[rows_j, :].astype(BF16)) * a
                sg = jnp.exp(ls)
                dz = dl * (1.0 - sg) - sg * (_split_dot(dl, before) + f)
                if masked:
                    dz = jnp.where(diag, dz, 0.0)
                dzb = dz.astype(BF16)
                acc_ref[...] += _nn(dzb, jnp.where(hm, kf, 0.0).astype(BF16))
                dk_ref[rows_j, :] += _tn(dzb, qh)
                dv_ref[rows_j, :] += _tn(a.astype(BF16), doh)
                return f + jnp.sum(dl, axis=1, keepdims=True)

            f = lax.fori_loop(0, i, lambda j, f: step(j, f, False), jnp.zeros((t, 1), F32))
            step(i, f, True)
        dq_ref[...] = acc_ref[...] * SB_SCALE

    col = lambda hp, i: (0, hp)
    out = jax.ShapeDtypeStruct((S, SB_WIDTH), F32)
    return pl.pallas_call(
        body, grid=(SB_WIDTH // LANE, S // t),
        in_specs=[pl.BlockSpec((t, LANE), lambda hp, i: (i, qb + hp)),
                  pl.BlockSpec((S, LANE), lambda hp, i: (0, kb + hp)),
                  pl.BlockSpec((S, LANE), lambda hp, i: (0, vb + hp)),
                  pl.BlockSpec((t, LANE), lambda hp, i: (i, hp)),
                  pl.BlockSpec((2, t, LANE), lambda hp, i: (hp, i, 0))],
        out_specs=[pl.BlockSpec((t, LANE), lambda hp, i: (i, hp)),
                   pl.BlockSpec((S, LANE), col), pl.BlockSpec((S, LANE), col)],
        out_shape=[out, out, out],
        scratch_shapes=[pltpu.VMEM((t, LANE), F32)],
        compiler_params=_cp("arbitrary", "arbitrary"), name=name)(p, p, p, do, cb)


def _mla_fwd(qr, kv, kpe, name):
    S = qr.shape[0]
    t = _ts(S)

    def body(q_ref, kv_ref, kpe_ref, o_ref, lse_ref, acc_ref, m_ref, l_ref):
        i = pl.program_id(1)
        low = lax.broadcasted_iota(jnp.int32, (t, LANE), 1) < NOPE_DIM
        rows = lax.broadcasted_iota(jnp.int32, (t, t), 0)
        cols = lax.broadcasted_iota(jnp.int32, (t, t), 1)
        causal = cols <= rows
        q = q_ref[...]
        acc_ref[...] = jnp.zeros_like(acc_ref)
        m_ref[...] = jnp.full_like(m_ref, NEG_BIG)
        l_ref[...] = jnp.zeros_like(l_ref)

        def step(j, masked):
            rows_j = pl.ds(pl.multiple_of(j * t, t), t)
            kvj = kv_ref[rows_j, :]
            z = _nt(q, jnp.where(low, kvj, kpe_ref[rows_j, :])) * MLA_SCALE
            if masked:
                z = jnp.where(causal, z, NEG_BIG)
            m_old = m_ref[...]
            m_new = jnp.maximum(m_old, jnp.max(z, axis=1, keepdims=True))
            pr = jnp.exp(z - m_new)
            alpha = jnp.exp(m_old - m_new)
            l_ref[...] = alpha * l_ref[...] + jnp.sum(pr, axis=1, keepdims=True)
            acc_ref[...] = alpha * acc_ref[...] + _nn(pr.astype(BF16), kvj)
            m_ref[...] = m_new

        def loop(j, carry):
            step(j, False)
            return carry

        lax.fori_loop(0, i, loop, 0)
        step(i, True)
        o_ref[...] = jnp.where(low, 0.0, acc_ref[...] / l_ref[...]).astype(BF16)
        lse_ref[0] = jnp.broadcast_to(m_ref[...] + jnp.log(l_ref[...]), (t, LANE))

    return pl.pallas_call(
        body, grid=(N_HEADS, S // t),
        in_specs=[pl.BlockSpec((t, LANE), lambda h, i: (i, h)),
                  pl.BlockSpec((S, LANE), lambda h, i: (0, h)),
                  pl.BlockSpec((S, LANE), lambda h, i: (0, 0))],
        out_specs=[pl.BlockSpec((t, LANE), lambda h, i: (i, h)),
                   pl.BlockSpec((1, t, LANE), lambda h, i: (h, i, 0))],
        out_shape=[jax.ShapeDtypeStruct((S, N_HEADS * LANE), BF16),
                   jax.ShapeDtypeStruct((N_HEADS, S, LANE), F32)],
        scratch_shapes=[pltpu.VMEM((t, LANE), F32), pltpu.VMEM((t, 1), F32), pltpu.VMEM((t, 1), F32)],
        compiler_params=_cp("arbitrary", "arbitrary"), name=name)(qr, kv, kpe)


def _mla_bwd(qr, kv, kpe, do, o, lse, name):
    S = qr.shape[0]
    t = _ts(S)

    def body(q_ref, kv_ref, kpe_ref, do_ref, o_ref, lse_ref, dq_ref, dkv_ref, dkpe_ref, acc_ref):
        i = pl.program_id(1)

        @pl.when(i == 0)
        def _():
            dkv_ref[...] = jnp.zeros_like(dkv_ref)
            dkpe_ref[...] = jnp.zeros_like(dkpe_ref)

        low = lax.broadcasted_iota(jnp.int32, (t, LANE), 1) < NOPE_DIM
        rows = lax.broadcasted_iota(jnp.int32, (t, t), 0)
        cols = lax.broadcasted_iota(jnp.int32, (t, t), 1)
        causal = cols <= rows
        q = q_ref[...]
        dov = do_ref[...]
        dob = dov.astype(BF16)
        delta = jnp.sum(dov * o_ref[...].astype(F32), axis=1, keepdims=True)
        lse_v = lse_ref[0][:, 0:1]
        acc_ref[...] = jnp.zeros_like(acc_ref)

        def step(j, masked):
            rows_j = pl.ds(pl.multiple_of(j * t, t), t)
            kvj = kv_ref[rows_j, :]
            kcat = jnp.where(low, kvj, kpe_ref[rows_j, :])
            z = _nt(q, kcat) * MLA_SCALE
            if masked:
                z = jnp.where(causal, z, NEG_BIG)
            pr = jnp.exp(z - lse_v)
            ds = (pr * (_nt(dob, kvj) - delta)).astype(BF16)
            acc_ref[...] += _nn(ds, kcat)
            dkc = _tn(ds, q) * MLA_SCALE
            dkv_ref[rows_j, :] += jnp.where(low, dkc, _tn(pr.astype(BF16), dob))
            dkpe_ref[rows_j, :] += jnp.where(low, 0.0, dkc)

        def loop(j, carry):
            step(j, False)
            return carry

        lax.fori_loop(0, i, loop, 0)
        step(i, True)
        dq_ref[...] = acc_ref[...] * MLA_SCALE

    blk = pl.BlockSpec((t, LANE), lambda h, i: (i, h))
    col = pl.BlockSpec((S, LANE), lambda h, i: (0, h))
    out = jax.ShapeDtypeStruct((S, N_HEADS * LANE), F32)
    return pl.pallas_call(
        body, grid=(N_HEADS, S // t),
        in_specs=[blk, col, pl.BlockSpec((S, LANE), lambda h, i: (0, 0)), blk, blk,
                  pl.BlockSpec((1, t, LANE), lambda h, i: (h, i, 0))],
        out_specs=[blk, col, col], out_shape=[out, out, out],
        scratch_shapes=[pltpu.VMEM((t, LANE), F32)],
        compiler_params=_cp("arbitrary", "arbitrary"), name=name)(qr, kv, kpe, do, o, lse)


_ANY = pl.BlockSpec(memory_space=pl.ANY)


def _place():
    return lax.axis_index("x"), lax.axis_index("y"), lax.axis_index("c")


def _dma_call(body, ins, out_shapes, n_remote, n_local, name, aliases=None):
    n_in, n_out = len(ins), len(out_shapes)

    def wrapped(*refs):
        body(refs[:n_in], refs[n_in:n_in + n_out], *refs[n_in + n_out:])

    return pl.pallas_call(
        wrapped, out_shape=out_shapes, in_specs=[_ANY] * n_in, out_specs=[_ANY] * n_out,
        scratch_shapes=[pltpu.SemaphoreType.DMA((n_remote,)), pltpu.SemaphoreType.DMA((n_remote,)),
                        pltpu.SemaphoreType.DMA((n_local,))],
        input_output_aliases=aliases or {}, name=name)(*ins)


def _all_gather8(blks, name):
    n_t = len(blks)

    def body(x_refs, out_refs, send_sems, recv_sems, local_sems):
        x, y, c = _place()
        me, sibling = (x, y, c), (x, y, 1 - c)
        chips = [(1 - x, y), (x, 1 - y), (1 - x, 1 - y)]

        def rows(t, px, py, pc):
            m = blks[t].shape[0]
            return out_refs[t].at[pl.ds((4 * px + 2 * py + pc) * m, m), :]

        def copy(t, k, block, to, src=None):
            return pltpu.make_async_remote_copy(
                src_ref=rows(t, *block) if src is None else src, dst_ref=rows(t, *block),
                send_sem=send_sems.at[7 * t + k], recv_sem=recv_sems.at[7 * t + k], device_id=to,
                device_id_type=MESH)

        mine = [pltpu.make_async_copy(x_refs[t], rows(t, *me), local_sems.at[t]) for t in range(n_t)]
        for cp in mine:
            cp.start()
        first = [copy(t, 0, me, sibling, src=x_refs[t]) for t in range(n_t)]
        first += [copy(t, 1 + j, me, (*chip, c), src=x_refs[t]) for t in range(n_t) for j, chip in enumerate(chips)]
        for cp in first:
            cp.start()
        passed = []
        for t in range(n_t):
            for j, chip in enumerate(chips):
                copy(t, 1 + j, (*chip, c), me).wait_recv()
                passed.append(copy(t, 4 + j, (*chip, c), sibling))
                passed[-1].start()
        for t in range(n_t):
            copy(t, 0, sibling, me).wait_recv()
            for j, chip in enumerate(chips):
                copy(t, 4 + j, (*chip, 1 - c), me).wait_recv()
        for cp in first + passed:
            cp.wait_send()
        for cp in mine:
            cp.wait()

    outs = [jax.ShapeDtypeStruct((8 * b.shape[0], b.shape[1]), b.dtype) for b in blks]
    return _dma_call(body, blks, outs, 7 * n_t, n_t, name)


def _sibling_swap_halves(gs, name):
    n_t = len(gs)

    def body(g_refs, out_refs, send_sems, recv_sems, local_sems):
        x, y, c = _place()
        copies = []
        for t in range(n_t):
            m = gs[t].shape[1] // 2
            copies += [pltpu.make_async_remote_copy(
                src_ref=g_refs[t].at[s, pl.ds((1 - c) * m, m), :], dst_ref=out_refs[t].at[s],
                send_sem=send_sems.at[4 * t + s], recv_sem=recv_sems.at[4 * t + s], device_id=(x, y, 1 - c),
                device_id_type=MESH) for s in range(4)]
        for cp in copies:
            cp.start()
        for cp in copies:
            cp.wait()

    outs = [jax.ShapeDtypeStruct((4, g.shape[1] // 2, g.shape[2]), g.dtype) for g in gs]
    return _dma_call(body, gs, outs, 4 * n_t, 1, name)


def _chip_scatter(parts, name):
    n_t = len(parts)

    def body(p_refs, out_refs, send_sems, recv_sems, local_sems):
        x, y, c = _place()
        mine = 2 * x + y
        chips = [(1 - x, y), (x, 1 - y), (1 - x, 1 - y)]
        own = [pltpu.make_async_copy(p_refs[t].at[mine], out_refs[t].at[mine], local_sems.at[t])
               for t in range(n_t)]
        for cp in own:
            cp.start()

        def copy(t, j, src_slot, dst_slot):
            px, py = chips[j]
            return pltpu.make_async_remote_copy(
                src_ref=p_refs[t].at[src_slot], dst_ref=out_refs[t].at[dst_slot],
                send_sem=send_sems.at[3 * t + j], recv_sem=recv_sems.at[3 * t + j], device_id=(px, py, c),
                device_id_type=MESH)

        copies = [copy(t, j, 2 * px + py, mine) for t in range(n_t) for j, (px, py) in enumerate(chips)]
        for cp in copies:
            cp.start()
        for t in range(n_t):
            for j, (px, py) in enumerate(chips):
                copy(t, j, mine, 2 * px + py).wait_recv()
        for cp in copies:
            cp.wait_send()
        for cp in own:
            cp.wait()

    outs = [jax.ShapeDtypeStruct(p.shape, p.dtype) for p in parts]
    return _dma_call(body, parts, outs, 3 * n_t, n_t, name)


def _sibling_gather(bufs, name):
    n_t = len(bufs)

    def body(b_refs, out_refs, send_sems, recv_sems, local_sems):
        x, y, c = _place()

        def copy(t, pc):
            m = bufs[t].shape[0] // 2
            half = pl.ds(pc * m, m)
            return pltpu.make_async_remote_copy(
                src_ref=b_refs[t].at[half, :], dst_ref=out_refs[t].at[half, :], send_sem=send_sems.at[t],
                recv_sem=recv_sems.at[t], device_id=(x, y, 1 - c), device_id_type=MESH)

        sends = [copy(t, c) for t in range(n_t)]
        for cp in sends:
            cp.start()
        for t in range(n_t):
            copy(t, 1 - c).wait_recv()
        for cp in sends:
            cp.wait_send()

    outs = [jax.ShapeDtypeStruct(b.shape, b.dtype) for b in bufs]
    return _dma_call(body, bufs, outs, n_t, 1, name, aliases={t: t for t in range(n_t)})


def _add_halves(g, recv, c, name):
    n_slot, m2, n = g.shape
    m = m2 // 2
    tr = _tile(m, (512, 256, 192, 128, 16))

    def body(c_ref, g_ref, r_ref, o_ref):
        o_ref[...] = (g_ref[...] + r_ref[...]).astype(BF16)

    nb = m // tr
    return pl.pallas_call(
        body,
        grid_spec=pltpu.PrefetchScalarGridSpec(
            num_scalar_prefetch=1, grid=(n_slot, nb),
            in_specs=[pl.BlockSpec((1, tr, n), lambda s, i, c_ref: (s, c_ref[0] * nb + i, 0)),
                      pl.BlockSpec((1, tr, n), lambda s, i, c_ref: (s, i, 0))],
            out_specs=pl.BlockSpec((1, tr, n), lambda s, i, c_ref: (s, i, 0))),
        out_shape=jax.ShapeDtypeStruct((n_slot, m, n), BF16),
        compiler_params=_cp("parallel", "parallel"), name=name)(c, g, recv)


def _sum_slots(parts, c, name):
    n_slot, m, n = parts.shape
    tr = _tile(m, (512, 256, 192, 128, 16))
    nb = m // tr

    def body(c_ref, p_ref, o_ref):
        acc = p_ref[0].astype(F32)
        for s in range(1, n_slot):
            acc = acc + p_ref[s].astype(F32)
        o_ref[...] = acc

    return pl.pallas_call(
        body,
        grid_spec=pltpu.PrefetchScalarGridSpec(
            num_scalar_prefetch=1, grid=(nb,),
            in_specs=[pl.BlockSpec((n_slot, tr, n), lambda i, c_ref: (0, i, 0))],
            out_specs=pl.BlockSpec((tr, n), lambda i, c_ref: (c_ref[0] * nb + i, 0))),
        out_shape=jax.ShapeDtypeStruct((2 * m, n), F32),
        compiler_params=_cp("parallel"), name=name)(c, parts)


_SHARDED = ("w_in", "w_q_up", "w_kv_up", "w_sb_out", "w_mla_out", "w_mix_out", "w_up", "w_down")
_ROW_SHARDED = ("w_mix_out", "w_down")
_BY_CHIP = ("w_up", "w_down")


def _unshard(parts, name):
    n, r, cs = parts.shape
    if name in _ROW_SHARDED:
        return parts.reshape(n * r, cs)
    return parts.transpose(1, 0, 2).reshape(r, n * cs)


def _reshard(full, name, n=4):
    R, C = full.shape
    if name in _ROW_SHARDED:
        return full.reshape(n, R // n, C)
    return full.reshape(R, n, C // n).transpose(1, 0, 2)


def _pad_w_in(w):
    z = lambda k: jnp.zeros(w.shape[:-1] + (k,), w.dtype)
    return jnp.concatenate([
        w[..., 2208:3232], w[..., 3232:4256], w[..., 0:1536], w[..., 1920:2176], w[..., 1536:1920],
        z(ROPE_LANE0), w[..., 2176:2208], z(LANE - ROPE_LANE0 - ROPE_DIM)], axis=-1)


def _unpad_w_in(g):
    k0 = COL_KROPE + ROPE_LANE0
    return jnp.concatenate([
        g[..., COL_QSB:COL_KVLAT], g[..., COL_QLAT:COL_KROPE], g[..., COL_KVLAT:COL_QLAT],
        g[..., k0:k0 + ROPE_DIM], g[..., 0:COL_QSB]], axis=-1)


def _pad_w_q(w):
    r = w.shape[0]
    return jnp.pad(w.reshape(r, N_HEADS, QK_DIM), ((0, 0), (0, 0), (0, LANE - QK_DIM))).reshape(r, N_HEADS * LANE)


def _unpad_w_q(g):
    r = g.shape[0]
    return g.reshape(r, N_HEADS, LANE)[..., :QK_DIM].reshape(r, N_HEADS * QK_DIM)


def _pad_w_mla(w):
    n = w.shape[1]
    return jnp.pad(w.reshape(N_HEADS, NOPE_DIM, n), ((0, 0), (LANE - NOPE_DIM, 0), (0, 0))).reshape(
        N_HEADS * LANE, n)


def _unpad_w_mla(g):
    n = g.shape[1]
    return g.reshape(N_HEADS, LANE, n)[:, LANE - NOPE_DIM:, :].reshape(N_HEADS * NOPE_DIM, n)


def _rope_tables(positions):
    half = ROPE_DIM // 2
    inv_freq = 1.0 / (ROPE_THETA ** (jnp.arange(0, ROPE_DIM, 2, dtype=F32) / ROPE_DIM))
    ang = positions.astype(F32)[:, None] * inv_freq
    cos, sin = jnp.cos(ang), jnp.sin(ang)
    S = positions.shape[0]
    one = jnp.ones((S, ROPE_LANE0), F32)
    zero = lambda k: jnp.zeros((S, k), F32)
    tail = LANE - ROPE_LANE0 - ROPE_DIM
    c = jnp.concatenate([one, cos, cos, zero(tail)], axis=1)
    s1 = jnp.concatenate([zero(ROPE_LANE0), -sin, zero(half + tail)], axis=1)
    s2 = jnp.concatenate([zero(ROPE_LANE0 + half), sin, zero(tail)], axis=1)
    return c, s1, s2


def _layer_fwd(x, W, mod, tabs):
    sh1, sc1, gt1, sh2, sc2, gt2 = (mod[i] for i in range(N_MOD))
    h1 = _normmod_fwd(x, W["g_mix"], sc1, sh1, "mix_norm_fwd")
    p = _matmul(h1, W["w_in"], name="in_proj")
    osbh, cb = _sb_fwd(p, "sb_attn_fwd")
    o_sb = _matmul(osbh, W["w_sb_out"], name="sb_out")
    qn = _rmsnorm_fwd(p, Q_RANK, COL_QLAT // Q_RANK, W["g_q"], "q_lat_norm_fwd")
    kvn = _rmsnorm_fwd(p, KV_RANK, COL_KVLAT // KV_RANK, W["g_kv"], "kv_lat_norm_fwd")
    qp = _matmul(qn, W["w_q_up"], name="q_up")
    kv = _matmul(kvn, W["w_kv_up"], out_dtype=BF16, name="kv_up")
    qr, kpe = _rope_fwd(qp, p, tabs, "rope_fwd")
    omh, lse = _mla_fwd(qr, kv, kpe, "mla_attn_fwd")
    o_mla = _matmul(omh, W["w_mla_out"], name="mla_out")
    merged = _merge_fwd(p, o_sb, o_mla, "merge_fwd")
    y1 = _matmul(merged, W["w_mix_out"], name="mix_out")
    x1 = _res_fwd(x, y1, gt1, "mix_residual")
    h2 = _normmod_fwd(x1, W["g_mlp"], sc2, sh2, "mlp_norm_fwd")
    u = _matmul(h2, W["w_up"], b_sharded="col", name="mlp_up")
    a = _sqrelu_fwd(u, "sqrelu_fwd")
    y2 = _matmul(a, W["w_down"], b_sharded="row", name="mlp_down")
    x2 = _res_fwd(x1, y2, gt2, "mlp_residual")
    saved = dict(x=x, h1=h1, p=p, osbh=osbh, cb=cb, o_sb=o_sb, qn=qn, kvn=kvn, qr=qr, kv=kv, kpe=kpe, omh=omh,
                 lse=lse, o_mla=o_mla, merged=merged, y1=y1, x1=x1, h2=h2, u=u, a=a, y2=y2)
    return x2, saved


def _layer_bwd(dx2, W, mod, tabs, sv):
    sh1, sc1, gt1, sh2, sc2, gt2 = (mod[i] for i in range(N_MOD))
    dy2, dgt2 = _res_bwd(dx2, sv["y2"], gt2, "mlp_residual_bwd")
    da = _matmul(dy2, W["w_down"], tb=True, b_sharded="row", name="mlp_down_dx")
    g_down = _matmul(sv["a"], dy2, ta=True, out_sharded=("row", W["w_down"].shape), name="mlp_down_dw")
    du = _sqrelu_bwd(da, sv["u"], "sqrelu_bwd")
    dh2 = _matmul(du, W["w_up"], tb=True, b_sharded="col", name="mlp_up_dx")
    g_up = _matmul(sv["h2"], du, ta=True, out_sharded=("col", W["w_up"].shape), name="mlp_up_dw")
    dx1, dsh2, dsc2, dg_mlp = _normmod_bwd(sv["x1"], dh2, W["g_mlp"], sc2, dx2, "mlp_norm_bwd")
    dy1, dgt1 = _res_bwd(dx1, sv["y1"], gt1, "mix_residual_bwd")
    dm = _matmul(dy1, W["w_mix_out"], tb=True, name="mix_out_dx")
    g_mix_out = _matmul(sv["merged"], dy1, ta=True, name="mix_out_dw")
    do_sb, do_mla, dgs, dgm = _merge_bwd(sv["p"], sv["o_sb"], sv["o_mla"], dm, "merge_bwd")
    do_sbh = _matmul(do_sb, W["w_sb_out"], tb=True, name="sb_out_dx")
    g_sb_out = _matmul(sv["osbh"], do_sb, ta=True, name="sb_out_dw")
    dqs, dks, dvs = _sb_bwd(sv["p"], do_sbh, sv["cb"], "sb_attn_bwd")
    do_mh = _matmul(do_mla, W["w_mla_out"], tb=True, name="mla_out_dx")
    g_mla_out = _matmul(sv["omh"], do_mla, ta=True, name="mla_out_dw")
    dqr, dkv, dkpe = _mla_bwd(sv["qr"], sv["kv"], sv["kpe"], do_mh, sv["omh"], sv["lse"], "mla_attn_bwd")
    dqp, dkr = _rope_bwd(dqr, dkpe, tabs, "rope_bwd")
    dqn = _matmul(dqp, W["w_q_up"], tb=True, name="q_up_dx")
    g_q_up = _matmul(sv["qn"], dqp, ta=True, name="q_up_dw")
    dkvn = _matmul(dkv, W["w_kv_up"], tb=True, name="kv_up_dx")
    g_kv_up = _matmul(sv["kvn"], dkv, ta=True, name="kv_up_dw")
    dqlat, dg_q = _rmsnorm_bwd(sv["p"], Q_RANK, COL_QLAT // Q_RANK, dqn, W["g_q"], "q_lat_norm_bwd")
    dkvlat, dg_kv = _rmsnorm_bwd(sv["p"], KV_RANK, COL_KVLAT // KV_RANK, dkvn, W["g_kv"], "kv_lat_norm_bwd")
    dp = jnp.concatenate([dgs, dgm, dqs, dks, dvs, dkvlat, dqlat, dkr], axis=1)
    dh1 = _matmul(dp, W["w_in"], tb=True, name="in_proj_dx")
    g_in = _matmul(sv["h1"], dp, ta=True, name="in_proj_dw")
    dx, dsh1, dsc1, dg_mix = _normmod_bwd(sv["x"], dh1, W["g_mix"], sc1, dx1, "mix_norm_bwd")
    grads = dict(w_in=g_in, w_q_up=g_q_up, w_kv_up=g_kv_up, w_sb_out=g_sb_out, w_mla_out=g_mla_out,
                 w_mix_out=g_mix_out, w_up=g_up, w_down=g_down,
                 dmod=jnp.concatenate([dsh1, dsc1, dgt1, dsh2, dsc2, dgt2], axis=0),
                 g_mix=dg_mix, g_mlp=dg_mlp, g_q=dg_q, g_kv=dg_kv)
    return dx, grads


def kernel(x, c, positions, w_ada, b_ada, g_mix_norm, w_in, g_q_lat, w_q_up, g_kv_lat, w_kv_up, w_sb_out, w_mla_out, w_mix_out, g_mlp_norm, w_up, w_down, g_final, loss_target, m_w_ada, m_b_ada, m_g_mix_norm, m_w_in, m_g_q_lat, m_w_q_up, m_g_kv_lat, m_w_kv_up, m_w_sb_out, m_w_mla_out, m_w_mix_out, m_g_mlp_norm, m_w_up, m_w_down, m_g_final, v_w_ada, v_b_ada, v_g_mix_norm, v_w_in, v_g_q_lat, v_w_q_up, v_g_kv_lat, v_w_kv_up, v_w_sb_out, v_w_mla_out, v_w_mix_out, v_g_mlp_norm, v_w_up, v_w_down, v_g_final):
    xi, yi, ci = _place()
    chip = 2 * xi + yi
    batch = 2 * chip + ci
    L = w_ada.shape[0]
    S = x.shape[1]
    shards = dict(w_in=w_in, w_q_up=w_q_up, w_kv_up=w_kv_up, w_sb_out=w_sb_out, w_mla_out=w_mla_out,
                  w_mix_out=w_mix_out, w_up=w_up, w_down=w_down)

    def gather_layer(l):
        def my_half(w):
            half = w.shape[1] // 2
            return lax.dynamic_slice_in_dim(w[l].astype(BF16), ci * half, half, 0)

        got = _all_gather8([my_half(shards[n]) for n in _SHARDED], "gather_weights")
        W = {}
        for n, g in zip(_SHARDED, got):
            by_chip = g.reshape((4,) + shards[n].shape[1:])
            W[n] = by_chip if n in _BY_CHIP else _unshard(by_chip, n)
        W["w_in"] = _pad_w_in(W["w_in"])
        W["w_q_up"] = _pad_w_q(W["w_q_up"])
        W["w_mla_out"] = _pad_w_mla(W["w_mla_out"])
        return W

    full = [gather_layer(l) for l in range(L)]

    c_act = _silu(c, "silu_c")
    c_all = _all_gather8([jnp.broadcast_to(c_act, (8, D_MODEL))], "gather_c")[0].reshape(8, 8, D_MODEL)[:, 0]
    c16 = jnp.concatenate([c_all, jnp.zeros_like(c_all)], axis=0)
    ada_cols = w_ada.shape[2]
    b_shard = lax.dynamic_slice_in_dim(b_ada, chip * ada_cols, ada_cols, 1)
    mod_part = jnp.stack([_matmul(c16, w_ada[l], name="ada_mod") for l in range(L)])
    mod_part = _bias_add(mod_part, jnp.broadcast_to(b_shard[:, None, :], mod_part.shape), "ada_bias")
    mod_all = _all_gather8([mod_part.reshape(L * 16, ada_cols)], "gather_mod")[0].reshape(4, 2, L, 16, ada_cols)
    mod_mine = lax.dynamic_index_in_dim(mod_all[:, 0], batch, axis=2, keepdims=False)
    mods = mod_mine.transpose(1, 0, 2).reshape(L, N_MOD, 1, D_MODEL)

    tabs = _rope_tables(positions[0])
    layer_w = [dict(full[l], g_mix=g_mix_norm[l:l + 1], g_mlp=g_mlp_norm[l:l + 1], g_q=g_q_lat[l:l + 1],
                    g_kv=g_kv_lat[l:l + 1]) for l in range(L)]

    xc, saved = x[0], []
    for l in range(L):
        xc, sv = _layer_fwd(xc, layer_w[l], mods[l], tabs)
        saved.append(sv)
    dxc, dg_final, loss_part = _final_loss(xc, loss_target[0], g_final[None, :], "final_norm_loss")
    loss = lax.psum(loss_part[0, 0], ("x", "y", "c"))
    core = jnp.reshape(ci, (1,)).astype(jnp.int32)

    def reduce_layer(g):
        by_dest = [g[n] if n in _BY_CHIP else _reshard(g[n], n) for n in _SHARDED]
        from_sibling = _sibling_swap_halves(by_dest, "grads_swap_halves")
        chip_part = [_add_halves(d, r, core, "grads_add_halves") for d, r in zip(by_dest, from_sibling)]
        from_chips = _chip_scatter(chip_part, "grads_chip_scatter")
        my_sum = [_sum_slots(p, core, "grads_sum_chips") for p in from_chips]
        return _sibling_gather(my_sum, "grads_sibling_gather")

    grads, reduced = [None] * L, [None] * L
    for l in reversed(range(L)):
        dxc, grads[l] = _layer_bwd(dxc, layer_w[l], mods[l], tabs, saved[l])
        grads[l]["w_in"] = _unpad_w_in(grads[l]["w_in"])
        grads[l]["w_q_up"] = _unpad_w_q(grads[l]["w_q_up"])
        grads[l]["w_mla_out"] = _unpad_w_mla(grads[l]["w_mla_out"])
        reduced[l] = reduce_layer(grads[l])
    grad_x = dxc
    gw = {n: jnp.stack([reduced[l][i] for l in range(L)]) for i, n in enumerate(_SHARDED)}

    def row(v):
        return jnp.pad(v, ((0, 0), (0, D_MODEL - v.shape[1])))

    per_layer_rows = N_MOD + 4
    small = jnp.concatenate(
        [jnp.concatenate([grads[l]["dmod"], row(grads[l]["g_mix"]), row(grads[l]["g_mlp"]),
                          row(grads[l]["g_q"]), row(grads[l]["g_kv"])], axis=0) for l in range(L)]
        + [dg_final], axis=0)
    n_small = -(-small.shape[0] // 8) * 8
    small = jnp.pad(small, ((0, n_small - small.shape[0]), (0, 0)))
    small_all = _all_gather8([small], "gather_vector_grads")[0].reshape(8, n_small, D_MODEL)
    small_sum = _sum_blocks(small_all, "sum_vector_grads")
    lay = small_sum[:L * per_layer_rows].reshape(L, per_layer_rows, D_MODEL)
    g_b_ada = lay[:, :N_MOD].reshape(L, N_MOD * D_MODEL)
    g_g_mix, g_g_mlp = lay[:, N_MOD], lay[:, N_MOD + 1]
    g_g_q, g_g_kv = lay[:, N_MOD + 2, :Q_RANK], lay[:, N_MOD + 3, :KV_RANK]
    g_g_final = small_sum[L * per_layer_rows]
    dmod_all = small_all[:, :L * per_layer_rows].reshape(8, L, per_layer_rows, D_MODEL)[:, :, :N_MOD]
    dmod_all = dmod_all.reshape(8, L, N_MOD * D_MODEL)
    dmod_cols = lax.dynamic_slice_in_dim(dmod_all, chip * ada_cols, ada_cols, 2)
    dmod16 = jnp.concatenate([dmod_cols, jnp.zeros_like(dmod_cols)], axis=0)
    g_w_ada = jnp.stack([_matmul(c16, dmod16[:, l], ta=True, name="ada_dw") for l in range(L)])

    weights = dict(w_ada=w_ada, b_ada=b_ada, g_mix_norm=g_mix_norm, w_in=w_in, g_q_lat=g_q_lat, w_q_up=w_q_up,
                   g_kv_lat=g_kv_lat, w_kv_up=w_kv_up, w_sb_out=w_sb_out, w_mla_out=w_mla_out,
                   w_mix_out=w_mix_out, g_mlp_norm=g_mlp_norm, w_up=w_up, w_down=w_down, g_final=g_final)
    mom = dict(w_ada=(m_w_ada, v_w_ada), b_ada=(m_b_ada, v_b_ada), g_mix_norm=(m_g_mix_norm, v_g_mix_norm),
               w_in=(m_w_in, v_w_in), g_q_lat=(m_g_q_lat, v_g_q_lat), w_q_up=(m_w_q_up, v_w_q_up),
               g_kv_lat=(m_g_kv_lat, v_g_kv_lat), w_kv_up=(m_w_kv_up, v_w_kv_up),
               w_sb_out=(m_w_sb_out, v_w_sb_out), w_mla_out=(m_w_mla_out, v_w_mla_out),
               w_mix_out=(m_w_mix_out, v_w_mix_out), g_mlp_norm=(m_g_mlp_norm, v_g_mlp_norm),
               w_up=(m_w_up, v_w_up), w_down=(m_w_down, v_w_down), g_final=(m_g_final, v_g_final))
    gr = dict(gw, w_ada=g_w_ada, b_ada=g_b_ada, g_mix_norm=g_g_mix, g_q_lat=g_g_q, g_kv_lat=g_g_kv,
              g_mlp_norm=g_g_mlp, g_final=g_g_final)
    order = list(weights)
    deltas, new_m, new_v = [], [], []
    for n in order:
        wv, gv, (mv, vv) = weights[n], gr[n], mom[n]
        if wv.ndim == 1:
            d, nm, nv = (t[0] for t in _adamw(wv[None], gv[None], mv[None], vv[None], "adamw_" + n))
        else:
            d, nm, nv = _adamw(wv, gv, mv, vv, "adamw_" + n)
        deltas.append(d)
        new_m.append(nm)
        new_v.append(nv)
    return (loss, grad_x[None], *[gr[n] for n in order], *deltas, *new_m, *new_v)
```

```python
from typing import Any, Callable, Mapping, NamedTuple, Sequence

import jax
import jax.numpy as jnp
from jax import lax
from jax.experimental import pallas as pl
from jax.experimental.pallas import tpu as pltpu

F32 = jnp.float32
BF16 = jnp.bfloat16
MESH = pl.DeviceIdType.MESH

D_MODEL = 1024
N_HEADS = 8
SB_DIM = 64
SB_WIDTH = 512
Q_RANK = 384
KV_RANK = 256
ROPE_DIM = 32
NOPE_DIM = 64
QK_DIM = 96
D_FF = 4096
N_MOD = 6
EPS = 1e-6
ROPE_THETA = 10000.0
SB_SCALE = SB_DIM ** -0.5
MLA_SCALE = QK_DIM ** -0.5
ADAM_LR, ADAM_B1, ADAM_B2, ADAM_EPS, ADAM_WD, ADAM_STEP = 0.001, 0.9, 0.999, 1e-08, 0.01, 10

LANE = 128
IN_PAD = 4352
COL_GATE_SB, COL_GATE_MLA, COL_QSB, COL_KSB, COL_VSB, COL_KVLAT, COL_QLAT, COL_KROPE = (
    0, 1024, 2048, 2560, 3072, 3584, 3840, 4224)
ROPE_LANE0 = 64
VMEM_LIMIT = 48 * 1024 * 1024
NEG_BIG = -1e30


def _cp(*sem):
    return pltpu.CompilerParams(dimension_semantics=sem, vmem_limit_bytes=VMEM_LIMIT)


def _tile(n, prefs):
    for t in prefs:
        if t <= n and n % t == 0:
            return t
    return n


def _dot(a, b, dims):
    return lax.dot_general(a, b, (dims, ((), ())), preferred_element_type=F32)


def _nn(a, b):
    return _dot(a, b, ((1,), (0,)))


def _nt(a, b):
    return _dot(a, b, ((1,), (1,)))


def _tn(a, b):
    return _dot(a, b, ((0,), (0,)))


def _sharded_dims(shape, kind):
    n, r, cs = shape
    return (n * r, cs) if kind == "row" else (r, n * cs)


def _sharded_spec(shape, kind, t_rows, t_cols, tile_of):
    _, r, cs = shape
    if kind == "row":
        assert r % t_rows == 0, (shape, t_rows)
        per = r // t_rows

        def index(i, j, k):
            tr, tc = tile_of(i, j, k)
            return tr // per, tr % per, tc
    else:
        assert cs % t_cols == 0, (shape, t_cols)
        per = cs // t_cols

        def index(i, j, k):
            tr, tc = tile_of(i, j, k)
            return tc // per, tr, tc % per
    return pl.BlockSpec((None, t_rows, t_cols), index)


def _matmul(a, b, *, ta=False, tb=False, out_dtype=F32, b_sharded=None, out_sharded=None, name):
    (K, M) = a.shape if ta else a.shape[::-1]
    b_dims = _sharded_dims(b.shape, b_sharded) if b_sharded else b.shape
    (N, Kb) = b_dims if tb else b_dims[::-1]
    assert K == Kb, (a.shape, b.shape, ta, tb)
    tm = _tile(M, (512, 384, 256, 128))
    tn = _tile(N, (1024, 2176, 768, 512, 384, 256, 128))
    tk = _tile(K, (1024, 2176, 768, 512, 384, 256, 128))
    nk = K // tk
    dims = ((0 if ta else 1,), (1 if tb else 0,))

    def body(a_ref, b_ref, o_ref, *acc):
        prod = _dot(a_ref[...].astype(BF16), b_ref[...].astype(BF16), dims)
        if nk == 1:
            o_ref[...] = prod.astype(out_dtype)
            return
        acc_ref, = acc
        k = pl.program_id(2)

        @pl.when(k == 0)
        def _():
            acc_ref[...] = prod

        @pl.when(k > 0)
        def _():
            acc_ref[...] += prod

        @pl.when(k == nk - 1)
        def _():
            o_ref[...] = acc_ref[...].astype(out_dtype)

    a_spec = (pl.BlockSpec((tk, tm), lambda i, j, k: (k, i)) if ta
              else pl.BlockSpec((tm, tk), lambda i, j, k: (i, k)))
    if b_sharded:
        b_spec = (_sharded_spec(b.shape, b_sharded, tn, tk, lambda i, j, k: (j, k)) if tb
                  else _sharded_spec(b.shape, b_sharded, tk, tn, lambda i, j, k: (k, j)))
    else:
        b_spec = (pl.BlockSpec((tn, tk), lambda i, j, k: (j, k)) if tb
                  else pl.BlockSpec((tk, tn), lambda i, j, k: (k, j)))
    if out_sharded:
        kind, shape = out_sharded
        assert _sharded_dims(shape, kind) == (M, N), (shape, kind, M, N)
        out_spec = _sharded_spec(shape, kind, tm, tn, lambda i, j, k: (i, j))
        out_shape = jax.ShapeDtypeStruct(shape, out_dtype)
    else:
        out_spec = pl.BlockSpec((tm, tn), lambda i, j, k: (i, j))
        out_shape = jax.ShapeDtypeStruct((M, N), out_dtype)
    return pl.pallas_call(
        body, grid=(M // tm, N // tn, nk), in_specs=[a_spec, b_spec], out_specs=out_spec, out_shape=out_shape,
        scratch_shapes=[pltpu.VMEM((tm, tn), F32)] if nk > 1 else [],
        compiler_params=_cp("parallel", "parallel", "arbitrary"), name=name)(a, b)


def _rows(ts, w, col=0):
    return pl.BlockSpec((ts, w), lambda i: (i, col))


def _vec(w):
    return pl.BlockSpec((1, w), lambda i: (0, 0))


def _ts(S):
    return _tile(S, (256, 128))


def _rms(x):
    return lax.rsqrt(jnp.mean(x * x, axis=-1, keepdims=True) + EPS)


def _colsum(x):
    return jnp.sum(x, axis=0, keepdims=True)


def _normmod_fwd(x, g, sc, sh, name):
    S, W = x.shape
    ts = _ts(S)

    def body(x_ref, g_ref, sc_ref, sh_ref, h_ref):
        xv = x_ref[...]
        h_ref[...] = ((xv * _rms(xv)) * g_ref[...] * (1.0 + sc_ref[...]) + sh_ref[...]).astype(BF16)

    return pl.pallas_call(
        body, grid=(S // ts,), in_specs=[_rows(ts, W), _vec(W), _vec(W), _vec(W)],
        out_specs=_rows(ts, W), out_shape=jax.ShapeDtypeStruct((S, W), BF16),
        compiler_params=_cp("parallel"), name=name)(x, g, sc, sh)


def _normmod_bwd(x, dh, g, sc, dres, name):
    S, W = x.shape
    ts = _ts(S)

    def body(x_ref, dh_ref, g_ref, sc_ref, dres_ref, dx_ref, dsh_ref, dsc_ref, dg_ref):
        @pl.when(pl.program_id(0) == 0)
        def _():
            dsh_ref[...] = jnp.zeros_like(dsh_ref)
            dsc_ref[...] = jnp.zeros_like(dsc_ref)
            dg_ref[...] = jnp.zeros_like(dg_ref)

        xv, dh_v, gv = x_ref[...], dh_ref[...], g_ref[...]
        r = _rms(xv)
        y = xv * r
        dn = dh_v * (1.0 + sc_ref[...])
        dy = dn * gv
        dx_ref[...] = dres_ref[...] + r * (dy - y * jnp.mean(dy * y, axis=-1, keepdims=True))
        dsh_ref[...] += _colsum(dh_v)
        dsc_ref[...] += _colsum(dh_v * y * gv)
        dg_ref[...] += _colsum(dn * y)

    vec_out = jax.ShapeDtypeStruct((1, W), F32)
    return pl.pallas_call(
        body, grid=(S // ts,),
        in_specs=[_rows(ts, W), _rows(ts, W), _vec(W), _vec(W), _rows(ts, W)],
        out_specs=[_rows(ts, W), _vec(W), _vec(W), _vec(W)],
        out_shape=[jax.ShapeDtypeStruct((S, W), F32), vec_out, vec_out, vec_out],
        compiler_params=_cp("arbitrary"), name=name)(x, dh, g, sc, dres)


def _rmsnorm_fwd(p, width, col, g, name):
    S = p.shape[0]
    ts = _ts(S)

    def body(x_ref, g_ref, y_ref):
        xv = x_ref[...]
        y_ref[...] = ((xv * _rms(xv)) * g_ref[...]).astype(BF16)

    return pl.pallas_call(
        body, grid=(S // ts,), in_specs=[_rows(ts, width, col), _vec(width)],
        out_specs=_rows(ts, width), out_shape=jax.ShapeDtypeStruct((S, width), BF16),
        compiler_params=_cp("parallel"), name=name)(p, g)


def _rmsnorm_bwd(p, width, col, dn, g, name):
    S = p.shape[0]
    ts = _ts(S)

    def body(x_ref, dn_ref, g_ref, dx_ref, dg_ref):
        @pl.when(pl.program_id(0) == 0)
        def _():
            dg_ref[...] = jnp.zeros_like(dg_ref)

        xv, dn_v = x_ref[...], dn_ref[...]
        r = _rms(xv)
        y = xv * r
        dy = dn_v * g_ref[...]
        dx_ref[...] = r * (dy - y * jnp.mean(dy * y, axis=-1, keepdims=True))
        dg_ref[...] += _colsum(dn_v * y)

    return pl.pallas_call(
        body, grid=(S // ts,), in_specs=[_rows(ts, width, col), _rows(ts, width), _vec(width)],
        out_specs=[_rows(ts, width), _vec(width)],
        out_shape=[jax.ShapeDtypeStruct((S, width), F32), jax.ShapeDtypeStruct((1, width), F32)],
        compiler_params=_cp("arbitrary"), name=name)(p, dn, g)


def _rope_rot(t, c, s1, s2):
    return t * c + pltpu.roll(t, LANE - 16, 1) * s1 + pltpu.roll(t, 16, 1) * s2


def _rope_rot_t(d, c, s1, s2):
    return d * c + pltpu.roll(d * s1, 16, 1) + pltpu.roll(d * s2, LANE - 16, 1)


def _rope_fwd(qp, p, tabs, name):
    S = qp.shape[0]
    ts = _ts(S)
    W = N_HEADS * LANE

    def body(q_ref, kr_ref, c_ref, s1_ref, s2_ref, qr_ref, kpe_ref):
        c, s1, s2 = c_ref[...], s1_ref[...], s2_ref[...]
        for h in range(N_HEADS):
            sl = slice(h * LANE, (h + 1) * LANE)
            qr_ref[:, sl] = _rope_rot(q_ref[:, sl], c, s1, s2).astype(BF16)
        kpe_ref[...] = _rope_rot(kr_ref[...], c, s1, s2).astype(BF16)

    tab = _rows(ts, LANE)
    return pl.pallas_call(
        body, grid=(S // ts,), in_specs=[_rows(ts, W), _rows(ts, LANE, COL_KROPE // LANE), tab, tab, tab],
        out_specs=[_rows(ts, W), _rows(ts, LANE)],
        out_shape=[jax.ShapeDtypeStruct((S, W), BF16), jax.ShapeDtypeStruct((S, LANE), BF16)],
        compiler_params=_cp("parallel"), name=name)(qp, p, *tabs)


def _rope_bwd(dqr, dkpe_heads, tabs, name):
    S = dqr.shape[0]
    ts = _ts(S)
    W = N_HEADS * LANE

    def body(dq_ref, dk_ref, c_ref, s1_ref, s2_ref, dqp_ref, dkr_ref):
        c, s1, s2 = c_ref[...], s1_ref[...], s2_ref[...]
        dk = dk_ref[:, 0:LANE]
        for h in range(N_HEADS):
            sl = slice(h * LANE, (h + 1) * LANE)
            dqp_ref[:, sl] = _rope_rot_t(dq_ref[:, sl], c, s1, s2).astype(BF16)
            if h:
                dk = dk + dk_ref[:, sl]
        dkr_ref[...] = _rope_rot_t(dk, c, s1, s2)

    tab = _rows(ts, LANE)
    return pl.pallas_call(
        body, grid=(S // ts,), in_specs=[_rows(ts, W), _rows(ts, W), tab, tab, tab],
        out_specs=[_rows(ts, W), _rows(ts, LANE)],
        out_shape=[jax.ShapeDtypeStruct((S, W), BF16), jax.ShapeDtypeStruct((S, LANE), F32)],
        compiler_params=_cp("parallel"), name=name)(dqr, dkpe_heads, *tabs)


def _merge_fwd(p, o_sb, o_mla, name):
    S, W = o_sb.shape
    ts = _ts(S)

    def body(gs_ref, gm_ref, a_ref, b_ref, m_ref):
        m_ref[...] = (jax.nn.sigmoid(gs_ref[...]) * a_ref[...]
                      + jax.nn.sigmoid(gm_ref[...]) * b_ref[...]).astype(BF16)

    return pl.pallas_call(
        body, grid=(S // ts,),
        in_specs=[_rows(ts, W, COL_GATE_SB // W), _rows(ts, W, COL_GATE_MLA // W), _rows(ts, W), _rows(ts, W)],
        out_specs=_rows(ts, W), out_shape=jax.ShapeDtypeStruct((S, W), BF16),
        compiler_params=_cp("parallel"), name=name)(p, p, o_sb, o_mla)


def _merge_bwd(p, o_sb, o_mla, dm, name):
    S, W = o_sb.shape
    ts = _ts(S)

    def body(gs_ref, gm_ref, a_ref, b_ref, dm_ref, da_ref, db_ref, dgs_ref, dgm_ref):
        dmv = dm_ref[...]
        sa, sb = jax.nn.sigmoid(gs_ref[...]), jax.nn.sigmoid(gm_ref[...])
        da_ref[...] = (dmv * sa).astype(BF16)
        db_ref[...] = (dmv * sb).astype(BF16)
        dgs_ref[...] = dmv * a_ref[...] * sa * (1.0 - sa)
        dgm_ref[...] = dmv * b_ref[...] * sb * (1.0 - sb)

    row = _rows(ts, W)
    return pl.pallas_call(
        body, grid=(S // ts,),
        in_specs=[_rows(ts, W, COL_GATE_SB // W), _rows(ts, W, COL_GATE_MLA // W), row, row, row],
        out_specs=[row, row, row, row],
        out_shape=[jax.ShapeDtypeStruct((S, W), BF16), jax.ShapeDtypeStruct((S, W), BF16),
                   jax.ShapeDtypeStruct((S, W), F32), jax.ShapeDtypeStruct((S, W), F32)],
        compiler_params=_cp("parallel"), name=name)(p, p, o_sb, o_mla, dm)


def _res_fwd(x, y, gate, name):
    S, W = x.shape
    ts = _ts(S)

    def body(x_ref, y_ref, g_ref, o_ref):
        o_ref[...] = x_ref[...] + g_ref[...] * y_ref[...]

    return pl.pallas_call(
        body, grid=(S // ts,), in_specs=[_rows(ts, W), _rows(ts, W), _vec(W)], out_specs=_rows(ts, W),
        out_shape=jax.ShapeDtypeStruct((S, W), F32), compiler_params=_cp("parallel"), name=name)(x, y, gate)


def _res_bwd(dx, y, gate, name):
    S, W = dx.shape
    ts = _ts(S)

    def body(dx_ref, y_ref, g_ref, dy_ref, dg_ref):
        @pl.when(pl.program_id(0) == 0)
        def _():
            dg_ref[...] = jnp.zeros_like(dg_ref)

        dxv = dx_ref[...]
        dy_ref[...] = (g_ref[...] * dxv).astype(BF16)
        dg_ref[...] += _colsum(dxv * y_ref[...])

    return pl.pallas_call(
        body, grid=(S // ts,), in_specs=[_rows(ts, W), _rows(ts, W), _vec(W)],
        out_specs=[_rows(ts, W), _vec(W)],
        out_shape=[jax.ShapeDtypeStruct((S, W), BF16), jax.ShapeDtypeStruct((1, W), F32)],
        compiler_params=_cp("arbitrary"), name=name)(dx, y, gate)


def _sqrelu_fwd(u, name):
    S, W = u.shape
    ts = _ts(S)

    def body(u_ref, a_ref):
        r = jnp.maximum(u_ref[...], 0.0)
        a_ref[...] = (r * r).astype(BF16)

    return pl.pallas_call(
        body, grid=(S // ts,), in_specs=[_rows(ts, W)], out_specs=_rows(ts, W),
        out_shape=jax.ShapeDtypeStruct((S, W), BF16), compiler_params=_cp("parallel"), name=name)(u)


def _sqrelu_bwd(da, u, name):
    S, W = u.shape
    ts = _ts(S)

    def body(da_ref, u_ref, du_ref):
        du_ref[...] = (da_ref[...] * (2.0 * jnp.maximum(u_ref[...], 0.0))).astype(BF16)

    return pl.pallas_call(
        body, grid=(S // ts,), in_specs=[_rows(ts, W), _rows(ts, W)], out_specs=_rows(ts, W),
        out_shape=jax.ShapeDtypeStruct((S, W), BF16), compiler_params=_cp("parallel"), name=name)(da, u)


def _final_loss(x, target, g, name):
    S, W = x.shape
    ts = _ts(S)

    def body(x_ref, t_ref, g_ref, dx_ref, dg_ref, loss_ref):
        @pl.when(pl.program_id(0) == 0)
        def _():
            dg_ref[...] = jnp.zeros_like(dg_ref)
            loss_ref[...] = jnp.zeros_like(loss_ref)

        xv, gv = x_ref[...], g_ref[...]
        r = _rms(xv)
        y = xv * r
        err = y * gv - t_ref[...]
        loss_ref[...] += jnp.full((1, LANE), 0.5 * jnp.sum(jnp.mean(err * err, axis=-1)), F32)
        dout = err * (1.0 / W)
        dy = dout * gv
        dx_ref[...] = r * (dy - y * jnp.mean(dy * y, axis=-1, keepdims=True))
        dg_ref[...] += _colsum(dout * y)

    return pl.pallas_call(
        body, grid=(S // ts,), in_specs=[_rows(ts, W), _rows(ts, W), _vec(W)],
        out_specs=[_rows(ts, W), _vec(W), _vec(LANE)],
        out_shape=[jax.ShapeDtypeStruct((S, W), F32), jax.ShapeDtypeStruct((1, W), F32),
                   jax.ShapeDtypeStruct((1, LANE), F32)],
        compiler_params=_cp("arbitrary"), name=name)(x, target, g)


def _silu(c, name):
    def body(c_ref, o_ref):
        cv = c_ref[...]
        o_ref[...] = cv * jax.nn.sigmoid(cv)

    return pl.pallas_call(body, out_shape=jax.ShapeDtypeStruct(c.shape, F32), name=name)(c)


def _bias_add(a, b, name):
    def body(a_ref, b_ref, o_ref):
        o_ref[...] = a_ref[...] + b_ref[...]

    return pl.pallas_call(body, out_shape=jax.ShapeDtypeStruct(a.shape, F32), name=name)(a, b)


def _sum_blocks(xs, name):
    n = xs.shape[0]

    def body(x_ref, o_ref):
        acc = x_ref[0]
        for d in range(1, n):
            acc = acc + x_ref[d]
        o_ref[...] = acc

    return pl.pallas_call(body, out_shape=jax.ShapeDtypeStruct(xs.shape[1:], F32), name=name)(xs)


def _adamw(w, g, m, v, name):
    shape = w.shape
    cols = shape[-1]
    w2, g2, m2, v2 = (t.reshape(-1, cols) for t in (w, g, m, v))
    rows = w2.shape[0]
    tr = _tile(rows, (128,))
    c1 = 1.0 - ADAM_B1 ** ADAM_STEP
    c2 = 1.0 - ADAM_B2 ** ADAM_STEP

    def body(w_ref, g_ref, m_ref, v_ref, d_ref, nm_ref, nv_ref):
        gv = g_ref[...]
        nm = ADAM_B1 * m_ref[...] + (1.0 - ADAM_B1) * gv
        nv = ADAM_B2 * v_ref[...] + (1.0 - ADAM_B2) * (gv * gv)
        d_ref[...] = -ADAM_LR * ((nm / c1) / (jnp.sqrt(nv / c2) + ADAM_EPS) + ADAM_WD * w_ref[...])
        nm_ref[...] = nm
        nv_ref[...] = nv

    spec = pl.BlockSpec((tr, cols), lambda i: (i, 0))
    out = jax.ShapeDtypeStruct((rows, cols), F32)
    d, nm, nv = pl.pallas_call(
        body, grid=(rows // tr,), in_specs=[spec] * 4, out_specs=[spec] * 3, out_shape=[out] * 3,
        compiler_params=_cp("parallel"), name=name)(w2, g2, m2, v2)
    return d.reshape(shape), nm.reshape(shape), nv.reshape(shape)


def _split_dot(x, tri):
    hi = x.astype(BF16)
    lo = (x - hi.astype(F32)).astype(BF16)
    return _nn(hi, tri) + _nn(lo, tri)


def _sb_logs(z):
    soft = jnp.log(1.0 + jnp.exp(-jnp.abs(z)))
    return jnp.minimum(z, 0.0) - soft, -jnp.maximum(z, 0.0) - soft


def _attn_call(body, grid, ins, in_specs, out_specs, out_shape, scratch, ex, name):
    n_out = len(out_shape)
    body, extra = _carry(ex, body, len(ins), n_out, len(scratch), grid)
    res = pl.pallas_call(
        body, grid=grid, in_specs=in_specs + extra["in_specs"], out_specs=out_specs + extra["out_specs"],
        out_shape=out_shape + extra["out_shape"], scratch_shapes=scratch + extra["scratch"],
        input_output_aliases=extra["aliases"], compiler_params=_cp("arbitrary", "arbitrary"),
        name=name)(*ins, *extra["ins"])
    return res[:n_out], res[n_out:]


def _sb_fwd(p, name, ex=None):
    S = p.shape[0]
    t = _ts(S)
    qb, kb, vb = COL_QSB // LANE, COL_KSB // LANE, COL_VSB // LANE

    def body(q_ref, k_ref, v_ref, o_ref, cb_ref, acc_ref):
        i = pl.program_id(1)
        lane = lax.broadcasted_iota(jnp.int32, (t, LANE), 1)
        rows = lax.broadcasted_iota(jnp.int32, (t, t), 0)
        cols = lax.broadcasted_iota(jnp.int32, (t, t), 1)
        after = jnp.where(rows > cols, 1.0, 0.0).astype(BF16)
        diag = cols < rows
        q = q_ref[...] * SB_SCALE
        acc_ref[...] = jnp.zeros_like(acc_ref)
        cb_ref[...] = jnp.zeros_like(cb_ref)
        for h in range(2):
            hm = (lane >= SB_DIM * h) & (lane < SB_DIM * (h + 1))
            qh = jnp.where(hm, q, 0.0).astype(BF16)

            def step(j, c, masked, h=h, hm=hm, qh=qh):
                rows_j = pl.ds(pl.multiple_of(j * t, t), t)
                kj = k_ref[rows_j, :].astype(BF16)
                vj = jnp.where(hm, v_ref[rows_j, :], 0.0).astype(BF16)
                ls, lf = _sb_logs(_nt(qh, kj))
                if masked:
                    lf = jnp.where(diag, lf, 0.0)
                a = jnp.exp(ls + _split_dot(lf, after) + c)
                if masked:
                    a = jnp.where(diag, a, 0.0)
                acc_ref[...] += _nn(a.astype(BF16), vj)
                cb_ref[h] = jnp.where(lane == j, c, cb_ref[h])
                return c + jnp.sum(lf, axis=1, keepdims=True)

            c = step(i, jnp.zeros((t, 1), F32), True)
            lax.fori_loop(0, i, lambda it, c: step(i - 1 - it, c, False), c)
        o_ref[...] = acc_ref[...].astype(BF16)

    return _attn_call(
        body, (SB_WIDTH // LANE, S // t), [p, p, p],
        [pl.BlockSpec((t, LANE), lambda hp, i: (i, qb + hp)),
         pl.BlockSpec((S, LANE), lambda hp, i: (0, kb + hp)),
         pl.BlockSpec((S, LANE), lambda hp, i: (0, vb + hp))],
        [pl.BlockSpec((t, LANE), lambda hp, i: (i, hp)),
         pl.BlockSpec((2, t, LANE), lambda hp, i: (hp, i, 0))],
        [jax.ShapeDtypeStruct((S, SB_WIDTH), BF16), jax.ShapeDtypeStruct((N_HEADS, S, LANE), F32)],
        [pltpu.VMEM((t, LANE), F32)], ex, name)


def _sb_bwd(p, do, cb, name, ex=None):
    S = p.shape[0]
    t = _ts(S)
    qb, kb, vb = COL_QSB // LANE, COL_KSB // LANE, COL_VSB // LANE

    def body(q_ref, k_ref, v_ref, do_ref, cb_ref, dq_ref, dk_ref, dv_ref, acc_ref):
        i = pl.program_id(1)

        @pl.when(i == 0)
        def _():
            dk_ref[...] = jnp.zeros_like(dk_ref)
            dv_ref[...] = jnp.zeros_like(dv_ref)

        lane = lax.broadcasted_iota(jnp.int32, (t, LANE), 1)
        rows = lax.broadcasted_iota(jnp.int32, (t, t), 0)
        cols = lax.broadcasted_iota(jnp.int32, (t, t), 1)
        after = jnp.where(rows > cols, 1.0, 0.0).astype(BF16)
        before = jnp.where(rows < cols, 1.0, 0.0).astype(BF16)
        diag = cols < rows
        q = q_ref[...] * SB_SCALE
        dov = do_ref[...]
        acc_ref[...] = jnp.zeros_like(acc_ref)
        for h in range(2):
            hm = (lane >= SB_DIM * h) & (lane < SB_DIM * (h + 1))
            qh = jnp.where(hm, q, 0.0).astype(BF16)
            doh = jnp.where(hm, dov, 0.0).astype(BF16)
            cbh = cb_ref[h]

            def step(j, f, masked, hm=hm, qh=qh, doh=doh, cbh=cbh):
                rows_j = pl.ds(pl.multiple_of(j * t, t), t)
                kf = k_ref[rows_j, :]
                ls, lf = _sb_logs(_nt(qh, kf.astype(BF16)))
                if masked:
                    lf = jnp.where(diag, lf, 0.0)
                c = jnp.sum(jnp.where(lane == j, cbh, 0.0), axis=1, keepdims=True)
                a = jnp.exp(ls + _split_dot(lf, after) + c)
                if masked:
                    a = jnp.where(diag, a, 0.0)
                dl = _nt(doh, v_ref[rows_j, :].astype(BF16)) * a
                sg = jnp.exp(ls)
                dz = dl * (1.0 - sg) - sg * (_split_dot(dl, before) + f)
                if masked:
                    dz = jnp.where(diag, dz, 0.0)
                dzb = dz.astype(BF16)
                acc_ref[...] += _nn(dzb, jnp.where(hm, kf, 0.0).astype(BF16))
                dk_ref[rows_j, :] += _tn(dzb, qh)
                dv_ref[rows_j, :] += _tn(a.astype(BF16), doh)
                return f + jnp.sum(dl, axis=1, keepdims=True)

            f = lax.fori_loop(0, i, lambda j, f: step(j, f, False), jnp.zeros((t, 1), F32))
            step(i, f, True)
        dq_ref[...] = acc_ref[...] * SB_SCALE

    col = lambda hp, i: (0, hp)
    out = jax.ShapeDtypeStruct((S, SB_WIDTH), F32)
    return _attn_call(
        body, (SB_WIDTH // LANE, S // t), [p, p, p, do, cb],
        [pl.BlockSpec((t, LANE), lambda hp, i: (i, qb + hp)),
         pl.BlockSpec((S, LANE), lambda hp, i: (0, kb + hp)),
         pl.BlockSpec((S, LANE), lambda hp, i: (0, vb + hp)),
         pl.BlockSpec((t, LANE), lambda hp, i: (i, hp)),
         pl.BlockSpec((2, t, LANE), lambda hp, i: (hp, i, 0))],
        [pl.BlockSpec((t, LANE), lambda hp, i: (i, hp)), pl.BlockSpec((S, LANE), col), pl.BlockSpec((S, LANE), col)],
        [out, out, out], [pltpu.VMEM((t, LANE), F32)], ex, name)


def _mla_fwd(qr, kv, kpe, name, ex=None):
    S = qr.shape[0]
    t = _ts(S)

    def body(q_ref, kv_ref, kpe_ref, o_ref, lse_ref, acc_ref, m_ref, l_ref):
        i = pl.program_id(1)
        low = lax.broadcasted_iota(jnp.int32, (t, LANE), 1) < NOPE_DIM
        rows = lax.broadcasted_iota(jnp.int32, (t, t), 0)
        cols = lax.broadcasted_iota(jnp.int32, (t, t), 1)
        causal = cols <= rows
        q = q_ref[...]
        acc_ref[...] = jnp.zeros_like(acc_ref)
        m_ref[...] = jnp.full_like(m_ref, NEG_BIG)
        l_ref[...] = jnp.zeros_like(l_ref)

        def step(j, masked):
            rows_j = pl.ds(pl.multiple_of(j * t, t), t)
            kvj = kv_ref[rows_j, :]
            z = _nt(q, jnp.where(low, kvj, kpe_ref[rows_j, :])) * MLA_SCALE
            if masked:
                z = jnp.where(causal, z, NEG_BIG)
            m_old = m_ref[...]
            m_new = jnp.maximum(m_old, jnp.max(z, axis=1, keepdims=True))
            pr = jnp.exp(z - m_new)
            alpha = jnp.exp(m_old - m_new)
            l_ref[...] = alpha * l_ref[...] + jnp.sum(pr, axis=1, keepdims=True)
            acc_ref[...] = alpha * acc_ref[...] + _nn(pr.astype(BF16), kvj)
            m_ref[...] = m_new

        def loop(j, carry):
            step(j, False)
            return carry

        lax.fori_loop(0, i, loop, 0)
        step(i, True)
        o_ref[...] = jnp.where(low, 0.0, acc_ref[...] / l_ref[...]).astype(BF16)
        lse_ref[0] = jnp.broadcast_to(m_ref[...] + jnp.log(l_ref[...]), (t, LANE))

    return _attn_call(
        body, (N_HEADS, S // t), [qr, kv, kpe],
        [pl.BlockSpec((t, LANE), lambda h, i: (i, h)),
         pl.BlockSpec((S, LANE), lambda h, i: (0, h)),
         pl.BlockSpec((S, LANE), lambda h, i: (0, 0))],
        [pl.BlockSpec((t, LANE), lambda h, i: (i, h)), pl.BlockSpec((1, t, LANE), lambda h, i: (h, i, 0))],
        [jax.ShapeDtypeStruct((S, N_HEADS * LANE), BF16), jax.ShapeDtypeStruct((N_HEADS, S, LANE), F32)],
        [pltpu.VMEM((t, LANE), F32), pltpu.VMEM((t, 1), F32), pltpu.VMEM((t, 1), F32)], ex, name)


def _mla_bwd(qr, kv, kpe, do, o, lse, name, ex=None):
    S = qr.shape[0]
    t = _ts(S)

    def body(q_ref, kv_ref, kpe_ref, do_ref, o_ref, lse_ref, dq_ref, dkv_ref, dkpe_ref, acc_ref):
        i = pl.program_id(1)

        @pl.when(i == 0)
        def _():
            dkv_ref[...] = jnp.zeros_like(dkv_ref)
            dkpe_ref[...] = jnp.zeros_like(dkpe_ref)

        low = lax.broadcasted_iota(jnp.int32, (t, LANE), 1) < NOPE_DIM
        rows = lax.broadcasted_iota(jnp.int32, (t, t), 0)
        cols = lax.broadcasted_iota(jnp.int32, (t, t), 1)
        causal = cols <= rows
        q = q_ref[...]
        dov = do_ref[...]
        dob = dov.astype(BF16)
        delta = jnp.sum(dov * o_ref[...].astype(F32), axis=1, keepdims=True)
        lse_v = lse_ref[0][:, 0:1]
        acc_ref[...] = jnp.zeros_like(acc_ref)

        def step(j, masked):
            rows_j = pl.ds(pl.multiple_of(j * t, t), t)
            kvj = kv_ref[rows_j, :]
            kcat = jnp.where(low, kvj, kpe_ref[rows_j, :])
            z = _nt(q, kcat) * MLA_SCALE
            if masked:
                z = jnp.where(causal, z, NEG_BIG)
            pr = jnp.exp(z - lse_v)
            ds = (pr * (_nt(dob, kvj) - delta)).astype(BF16)
            acc_ref[...] += _nn(ds, kcat)
            dkc = _tn(ds, q) * MLA_SCALE
            dkv_ref[rows_j, :] += jnp.where(low, dkc, _tn(pr.astype(BF16), dob))
            dkpe_ref[rows_j, :] += jnp.where(low, 0.0, dkc)

        def loop(j, carry):
            step(j, False)
            return carry

        lax.fori_loop(0, i, loop, 0)
        step(i, True)
        dq_ref[...] = acc_ref[...] * MLA_SCALE

    blk = pl.BlockSpec((t, LANE), lambda h, i: (i, h))
    col = pl.BlockSpec((S, LANE), lambda h, i: (0, h))
    out = jax.ShapeDtypeStruct((S, N_HEADS * LANE), F32)
    return _attn_call(
        body, (N_HEADS, S // t), [qr, kv, kpe, do, o, lse],
        [blk, col, pl.BlockSpec((S, LANE), lambda h, i: (0, 0)), blk, blk,
         pl.BlockSpec((1, t, LANE), lambda h, i: (h, i, 0))],
        [blk, col, col], [out, out, out], [pltpu.VMEM((t, LANE), F32)], ex, name)


_ANY = pl.BlockSpec(memory_space=pl.ANY)


def _place():
    return lax.axis_index("x"), lax.axis_index("y"), lax.axis_index("c")


class _Exchange(NamedTuple):
    ins: Sequence[Any]
    outs: Sequence[Any]
    aliases: Mapping[int, int]
    n_remote: int
    n_local: int
    start: Callable
    finish: Callable


def _exchange_scratch(ex):
    return [pltpu.SemaphoreType.DMA((ex.n_remote,)), pltpu.SemaphoreType.DMA((ex.n_remote,)),
            pltpu.SemaphoreType.DMA((ex.n_local,))]


def _run_exchange(ex, name):
    n_in, n_out = len(ex.ins), len(ex.outs)

    def body(*refs):
        args = (refs[:n_in], refs[n_in:n_in + n_out], *refs[n_in + n_out:])
        ex.start(*args)
        ex.finish(*args)

    return pl.pallas_call(
        body, out_shape=list(ex.outs), in_specs=[_ANY] * n_in, out_specs=[_ANY] * n_out,
        scratch_shapes=_exchange_scratch(ex), input_output_aliases=dict(ex.aliases), name=name)(*ex.ins)


def _carry(ex, body, n_in, n_out, n_scratch, grid):
    if ex is None:
        return body, dict(ins=[], in_specs=[], out_specs=[], out_shape=[], scratch=[], aliases={})
    e_in, e_out = len(ex.ins), len(ex.outs)

    def carried(*refs):
        own_in, refs = refs[:n_in], refs[n_in:]
        ex_in, refs = refs[:e_in], refs[e_in:]
        own_out, refs = refs[:n_out], refs[n_out:]
        ex_out, refs = refs[:e_out], refs[e_out:]
        own_scratch, sems = refs[:n_scratch], refs[n_scratch:]
        at = [pl.program_id(d) for d in range(2)]

        @pl.when((at[0] == 0) & (at[1] == 0))
        def _():
            ex.start(ex_in, ex_out, *sems)

        body(*own_in, *own_out, *own_scratch)

        @pl.when((at[0] == grid[0] - 1) & (at[1] == grid[1] - 1))
        def _():
            ex.finish(ex_in, ex_out, *sems)

    return carried, dict(
        ins=list(ex.ins), in_specs=[_ANY] * e_in, out_specs=[_ANY] * e_out, out_shape=list(ex.outs),
        scratch=_exchange_scratch(ex), aliases={n_in + i: n_out + o for i, o in ex.aliases.items()})


def _gather_exchange(arrs, phase="all"):
    n_t = len(arrs)
    ms = [a.shape[0] // (8 if phase == "b" else 1) for a in arrs]

    def plan(in_refs, out_refs, send_sems, recv_sems, local_sems):
        x, y, c = _place()
        me, sibling = (x, y, c), (x, y, 1 - c)
        chips = [(1 - x, y), (x, 1 - y), (1 - x, 1 - y)]

        def rows(ref, t, px, py, pc):
            return ref.at[pl.ds((4 * px + 2 * py + pc) * ms[t], ms[t]), :]

        def copy(t, k, block, to, src):
            return pltpu.make_async_remote_copy(
                src_ref=src, dst_ref=rows(out_refs[t], t, *block), send_sem=send_sems.at[7 * t + k],
                recv_sem=recv_sems.at[7 * t + k], device_id=to, device_id_type=MESH)

        mine, first, first_in, passed, passed_in = [], [], [], [], []
        for t in range(n_t):
            if phase != "b":
                mine.append(pltpu.make_async_copy(in_refs[t], rows(out_refs[t], t, *me), local_sems.at[t]))
                first.append(copy(t, 0, me, sibling, in_refs[t]))
                first_in.append(copy(t, 0, sibling, me, in_refs[t]))
                for j, chip in enumerate(chips):
                    first.append(copy(t, 1 + j, me, (*chip, c), in_refs[t]))
                    first_in.append(copy(t, 1 + j, (*chip, c), me, in_refs[t]))
            if phase != "a":
                held = in_refs[t] if phase == "b" else out_refs[t]
                for j, chip in enumerate(chips):
                    passed.append(copy(t, 4 + j, (*chip, c), sibling, rows(held, t, *chip, c)))
                    passed_in.append(copy(t, 4 + j, (*chip, 1 - c), me, rows(held, t, *chip, c)))
        return mine, first, first_in, passed, passed_in

    def start(*refs):
        mine, first, _, passed, _ = plan(*refs)
        for cp in mine + first + (passed if phase == "b" else []):
            cp.start()

    def finish(*refs):
        mine, first, first_in, passed, passed_in = plan(*refs)
        for cp in first_in:
            cp.wait_recv()
        if phase == "all":
            for cp in passed:
                cp.start()
        for cp in passed_in:
            cp.wait_recv()
        for cp in first + passed:
            cp.wait_send()
        for cp in mine:
            cp.wait()

    if phase == "b":
        outs = [jax.ShapeDtypeStruct(a.shape, a.dtype) for a in arrs]
        aliases = {t: t for t in range(n_t)}
    else:
        outs = [jax.ShapeDtypeStruct((8 * a.shape[0], a.shape[1]), a.dtype) for a in arrs]
        aliases = {}
    return _Exchange(list(arrs), outs, aliases, 7 * n_t, n_t, start, finish)


def _all_gather8(blks, name):
    return _run_exchange(_gather_exchange(blks), name)


def _swap_halves_exchange(gs):
    n_t = len(gs)

    def plan(g_refs, out_refs, send_sems, recv_sems, local_sems):
        x, y, c = _place()
        copies = []
        for t in range(n_t):
            m = gs[t].shape[1] // 2
            copies += [pltpu.make_async_remote_copy(
                src_ref=g_refs[t].at[s, pl.ds((1 - c) * m, m), :], dst_ref=out_refs[t].at[s],
                send_sem=send_sems.at[4 * t + s], recv_sem=recv_sems.at[4 * t + s], device_id=(x, y, 1 - c),
                device_id_type=MESH) for s in range(4)]
        return copies

    def start(*refs):
        for cp in plan(*refs):
            cp.start()

    def finish(*refs):
        for cp in plan(*refs):
            cp.wait()

    outs = [jax.ShapeDtypeStruct((4, g.shape[1] // 2, g.shape[2]), g.dtype) for g in gs]
    return _Exchange(list(gs), outs, {}, 4 * n_t, 1, start, finish)


def _chip_scatter_exchange(parts):
    n_t = len(parts)

    def plan(p_refs, out_refs, send_sems, recv_sems, local_sems):
        x, y, c = _place()
        mine = 2 * x + y
        chips = [(1 - x, y), (x, 1 - y), (1 - x, 1 - y)]

        def copy(t, j, src_slot, dst_slot):
            px, py = chips[j]
            return pltpu.make_async_remote_copy(
                src_ref=p_refs[t].at[src_slot], dst_ref=out_refs[t].at[dst_slot],
                send_sem=send_sems.at[3 * t + j], recv_sem=recv_sems.at[3 * t + j], device_id=(px, py, c),
                device_id_type=MESH)

        own = [pltpu.make_async_copy(p_refs[t].at[mine], out_refs[t].at[mine], local_sems.at[t])
               for t in range(n_t)]
        sends = [copy(t, j, 2 * px + py, mine) for t in range(n_t) for j, (px, py) in enumerate(chips)]
        arrivals = [copy(t, j, mine, 2 * px + py) for t in range(n_t) for j, (px, py) in enumerate(chips)]
        return own, sends, arrivals

    def start(*refs):
        own, sends, _ = plan(*refs)
        for cp in own + sends:
            cp.start()

    def finish(*refs):
        own, sends, arrivals = plan(*refs)
        for cp in arrivals:
            cp.wait_recv()
        for cp in sends:
            cp.wait_send()
        for cp in own:
            cp.wait()

    outs = [jax.ShapeDtypeStruct(p.shape, p.dtype) for p in parts]
    return _Exchange(list(parts), outs, {}, 3 * n_t, n_t, start, finish)


def _sibling_gather_exchange(bufs):
    n_t = len(bufs)

    def plan(b_refs, out_refs, send_sems, recv_sems, local_sems):
        x, y, c = _place()

        def copy(t, pc):
            m = bufs[t].shape[0] // 2
            half = pl.ds(pc * m, m)
            return pltpu.make_async_remote_copy(
                src_ref=b_refs[t].at[half, :], dst_ref=out_refs[t].at[half, :], send_sem=send_sems.at[t],
                recv_sem=recv_sems.at[t], device_id=(x, y, 1 - c), device_id_type=MESH)

        return [copy(t, c) for t in range(n_t)], [copy(t, 1 - c) for t in range(n_t)]

    def start(*refs):
        for cp in plan(*refs)[0]:
            cp.start()

    def finish(*refs):
        sends, arrivals = plan(*refs)
        for cp in arrivals:
            cp.wait_recv()
        for cp in sends:
            cp.wait_send()

    outs = [jax.ShapeDtypeStruct(b.shape, b.dtype) for b in bufs]
    return _Exchange(list(bufs), outs, {t: t for t in range(n_t)}, n_t, 1, start, finish)


def _add_halves(g, recv, c, name):
    n_slot, m2, n = g.shape
    m = m2 // 2
    tr = _tile(m, (512, 256, 192, 128, 16))

    def body(c_ref, g_ref, r_ref, o_ref):
        o_ref[...] = (g_ref[...] + r_ref[...]).astype(BF16)

    nb = m // tr
    return pl.pallas_call(
        body,
        grid_spec=pltpu.PrefetchScalarGridSpec(
            num_scalar_prefetch=1, grid=(n_slot, nb),
            in_specs=[pl.BlockSpec((1, tr, n), lambda s, i, c_ref: (s, c_ref[0] * nb + i, 0)),
                      pl.BlockSpec((1, tr, n), lambda s, i, c_ref: (s, i, 0))],
            out_specs=pl.BlockSpec((1, tr, n), lambda s, i, c_ref: (s, i, 0))),
        out_shape=jax.ShapeDtypeStruct((n_slot, m, n), BF16),
        compiler_params=_cp("parallel", "parallel"), name=name)(c, g, recv)


def _sum_slots(parts, c, name):
    n_slot, m, n = parts.shape
    tr = _tile(m, (512, 256, 192, 128, 16))
    nb = m // tr

    def body(c_ref, p_ref, o_ref):
        acc = p_ref[0].astype(F32)
        for s in range(1, n_slot):
            acc = acc + p_ref[s].astype(F32)
        o_ref[...] = acc

    return pl.pallas_call(
        body,
        grid_spec=pltpu.PrefetchScalarGridSpec(
            num_scalar_prefetch=1, grid=(nb,),
            in_specs=[pl.BlockSpec((n_slot, tr, n), lambda i, c_ref: (0, i, 0))],
            out_specs=pl.BlockSpec((tr, n), lambda i, c_ref: (c_ref[0] * nb + i, 0))),
        out_shape=jax.ShapeDtypeStruct((2 * m, n), F32),
        compiler_params=_cp("parallel"), name=name)(c, parts)


_SHARDED = ("w_in", "w_q_up", "w_kv_up", "w_sb_out", "w_mla_out", "w_mix_out", "w_up", "w_down")
_ROW_SHARDED = ("w_mix_out", "w_down")
_BY_CHIP = ("w_up", "w_down")


def _unshard(parts, name):
    n, r, cs = parts.shape
    if name in _ROW_SHARDED:
        return parts.reshape(n * r, cs)
    return parts.transpose(1, 0, 2).reshape(r, n * cs)


def _reshard(full, name, n=4):
    R, C = full.shape
    if name in _ROW_SHARDED:
        return full.reshape(n, R // n, C)
    return full.reshape(R, n, C // n).transpose(1, 0, 2)


def _pad_w_in(w):
    z = lambda k: jnp.zeros(w.shape[:-1] + (k,), w.dtype)
    return jnp.concatenate([
        w[..., 2208:3232], w[..., 3232:4256], w[..., 0:1536], w[..., 1920:2176], w[..., 1536:1920],
        z(ROPE_LANE0), w[..., 2176:2208], z(LANE - ROPE_LANE0 - ROPE_DIM)], axis=-1)


def _unpad_w_in(g):
    k0 = COL_KROPE + ROPE_LANE0
    return jnp.concatenate([
        g[..., COL_QSB:COL_KVLAT], g[..., COL_QLAT:COL_KROPE], g[..., COL_KVLAT:COL_QLAT],
        g[..., k0:k0 + ROPE_DIM], g[..., 0:COL_QSB]], axis=-1)


def _pad_w_q(w):
    r = w.shape[0]
    return jnp.pad(w.reshape(r, N_HEADS, QK_DIM), ((0, 0), (0, 0), (0, LANE - QK_DIM))).reshape(r, N_HEADS * LANE)


def _unpad_w_q(g):
    r = g.shape[0]
    return g.reshape(r, N_HEADS, LANE)[..., :QK_DIM].reshape(r, N_HEADS * QK_DIM)


def _pad_w_mla(w):
    n = w.shape[1]
    return jnp.pad(w.reshape(N_HEADS, NOPE_DIM, n), ((0, 0), (LANE - NOPE_DIM, 0), (0, 0))).reshape(
        N_HEADS * LANE, n)


def _unpad_w_mla(g):
    n = g.shape[1]
    return g.reshape(N_HEADS, LANE, n)[:, LANE - NOPE_DIM:, :].reshape(N_HEADS * NOPE_DIM, n)


def _rope_tables(positions):
    half = ROPE_DIM // 2
    inv_freq = 1.0 / (ROPE_THETA ** (jnp.arange(0, ROPE_DIM, 2, dtype=F32) / ROPE_DIM))
    ang = positions.astype(F32)[:, None] * inv_freq
    cos, sin = jnp.cos(ang), jnp.sin(ang)
    S = positions.shape[0]
    one = jnp.ones((S, ROPE_LANE0), F32)
    zero = lambda k: jnp.zeros((S, k), F32)
    tail = LANE - ROPE_LANE0 - ROPE_DIM
    c = jnp.concatenate([one, cos, cos, zero(tail)], axis=1)
    s1 = jnp.concatenate([zero(ROPE_LANE0), -sin, zero(half + tail)], axis=1)
    s2 = jnp.concatenate([zero(ROPE_LANE0 + half), sin, zero(tail)], axis=1)
    return c, s1, s2


def _layer_fwd(x, W, mod, tabs, next_blocks=None):
    sh1, sc1, gt1, sh2, sc2, gt2 = (mod[i] for i in range(N_MOD))
    h1 = _normmod_fwd(x, W["g_mix"], sc1, sh1, "mix_norm_fwd")
    p = _matmul(h1, W["w_in"], name="in_proj")
    (osbh, cb), arriving = _sb_fwd(
        p, "sb_attn_fwd", _gather_exchange(next_blocks, "a") if next_blocks else None)
    o_sb = _matmul(osbh, W["w_sb_out"], name="sb_out")
    qn = _rmsnorm_fwd(p, Q_RANK, COL_QLAT // Q_RANK, W["g_q"], "q_lat_norm_fwd")
    kvn = _rmsnorm_fwd(p, KV_RANK, COL_KVLAT // KV_RANK, W["g_kv"], "kv_lat_norm_fwd")
    qp = _matmul(qn, W["w_q_up"], name="q_up")
    kv = _matmul(kvn, W["w_kv_up"], out_dtype=BF16, name="kv_up")
    qr, kpe = _rope_fwd(qp, p, tabs, "rope_fwd")
    (omh, lse), gathered = _mla_fwd(
        qr, kv, kpe, "mla_attn_fwd", _gather_exchange(arriving, "b") if next_blocks else None)
    o_mla = _matmul(omh, W["w_mla_out"], name="mla_out")
    merged = _merge_fwd(p, o_sb, o_mla, "merge_fwd")
    y1 = _matmul(merged, W["w_mix_out"], name="mix_out")
    x1 = _res_fwd(x, y1, gt1, "mix_residual")
    h2 = _normmod_fwd(x1, W["g_mlp"], sc2, sh2, "mlp_norm_fwd")
    u = _matmul(h2, W["w_up"], b_sharded="col", name="mlp_up")
    a = _sqrelu_fwd(u, "sqrelu_fwd")
    y2 = _matmul(a, W["w_down"], b_sharded="row", name="mlp_down")
    x2 = _res_fwd(x1, y2, gt2, "mlp_residual")
    saved = dict(x=x, h1=h1, p=p, osbh=osbh, cb=cb, o_sb=o_sb, qn=qn, kvn=kvn, qr=qr, kv=kv, kpe=kpe, omh=omh,
                 lse=lse, o_mla=o_mla, merged=merged, y1=y1, x1=x1, h2=h2, u=u, a=a, y2=y2)
    return x2, saved, gathered


def _layer_bwd(dx2, W, mod, tabs, sv, core, pending=None):
    sh1, sc1, gt1, sh2, sc2, gt2 = (mod[i] for i in range(N_MOD))
    dy2, dgt2 = _res_bwd(dx2, sv["y2"], gt2, "mlp_residual_bwd")
    da = _matmul(dy2, W["w_down"], tb=True, b_sharded="row", name="mlp_down_dx")
    g_down = _matmul(sv["a"], dy2, ta=True, out_sharded=("row", W["w_down"].shape), name="mlp_down_dw")
    du = _sqrelu_bwd(da, sv["u"], "sqrelu_bwd")
    dh2 = _matmul(du, W["w_up"], tb=True, b_sharded="col", name="mlp_up_dx")
    g_up = _matmul(sv["h2"], du, ta=True, out_sharded=("col", W["w_up"].shape), name="mlp_up_dw")
    dx1, dsh2, dsc2, dg_mlp = _normmod_bwd(sv["x1"], dh2, W["g_mlp"], sc2, dx2, "mlp_norm_bwd")
    dy1, dgt1 = _res_bwd(dx1, sv["y1"], gt1, "mix_residual_bwd")
    dm = _matmul(dy1, W["w_mix_out"], tb=True, name="mix_out_dx")
    g_mix_out = _matmul(sv["merged"], dy1, ta=True, name="mix_out_dw")
    do_sb, do_mla, dgs, dgm = _merge_bwd(sv["p"], sv["o_sb"], sv["o_mla"], dm, "merge_bwd")
    do_sbh = _matmul(do_sb, W["w_sb_out"], tb=True, name="sb_out_dx")
    g_sb_out = _matmul(sv["osbh"], do_sb, ta=True, name="sb_out_dw")
    (dqs, dks, dvs), from_chips = _sb_bwd(
        sv["p"], do_sbh, sv["cb"], "sb_attn_bwd", _chip_scatter_exchange(pending) if pending else None)
    my_sum = [_sum_slots(part, core, "grads_sum_chips") for part in from_chips]
    do_mh = _matmul(do_mla, W["w_mla_out"], tb=True, name="mla_out_dx")
    g_mla_out = _matmul(sv["omh"], do_mla, ta=True, name="mla_out_dw")
    (dqr, dkv, dkpe), reduced_above = _mla_bwd(
        sv["qr"], sv["kv"], sv["kpe"], do_mh, sv["omh"], sv["lse"], "mla_attn_bwd",
        _sibling_gather_exchange(my_sum) if pending else None)
    dqp, dkr = _rope_bwd(dqr, dkpe, tabs, "rope_bwd")
    dqn = _matmul(dqp, W["w_q_up"], tb=True, name="q_up_dx")
    g_q_up = _matmul(sv["qn"], dqp, ta=True, name="q_up_dw")
    dkvn = _matmul(dkv, W["w_kv_up"], tb=True, name="kv_up_dx")
    g_kv_up = _matmul(sv["kvn"], dkv, ta=True, name="kv_up_dw")
    dqlat, dg_q = _rmsnorm_bwd(sv["p"], Q_RANK, COL_QLAT // Q_RANK, dqn, W["g_q"], "q_lat_norm_bwd")
    dkvlat, dg_kv = _rmsnorm_bwd(sv["p"], KV_RANK, COL_KVLAT // KV_RANK, dkvn, W["g_kv"], "kv_lat_norm_bwd")
    dp = jnp.concatenate([dgs, dgm, dqs, dks, dvs, dkvlat, dqlat, dkr], axis=1)
    dh1 = _matmul(dp, W["w_in"], tb=True, name="in_proj_dx")
    g_in = _matmul(sv["h1"], dp, ta=True, name="in_proj_dw")
    dx, dsh1, dsc1, dg_mix = _normmod_bwd(sv["x"], dh1, W["g_mix"], sc1, dx1, "mix_norm_bwd")
    grads = dict(w_in=g_in, w_q_up=g_q_up, w_kv_up=g_kv_up, w_sb_out=g_sb_out, w_mla_out=g_mla_out,
                 w_mix_out=g_mix_out, w_up=g_up, w_down=g_down,
                 dmod=jnp.concatenate([dsh1, dsc1, dgt1, dsh2, dsc2, dgt2], axis=0),
                 g_mix=dg_mix, g_mlp=dg_mlp, g_q=dg_q, g_kv=dg_kv)
    return dx, grads, reduced_above


def kernel(x, c, positions, w_ada, b_ada, g_mix_norm, w_in, g_q_lat, w_q_up, g_kv_lat, w_kv_up, w_sb_out, w_mla_out, w_mix_out, g_mlp_norm, w_up, w_down, g_final, loss_target, m_w_ada, m_b_ada, m_g_mix_norm, m_w_in, m_g_q_lat, m_w_q_up, m_g_kv_lat, m_w_kv_up, m_w_sb_out, m_w_mla_out, m_w_mix_out, m_g_mlp_norm, m_w_up, m_w_down, m_g_final, v_w_ada, v_b_ada, v_g_mix_norm, v_w_in, v_g_q_lat, v_w_q_up, v_g_kv_lat, v_w_kv_up, v_w_sb_out, v_w_mla_out, v_w_mix_out, v_g_mlp_norm, v_w_up, v_w_down, v_g_final):
    xi, yi, ci = _place()
    chip = 2 * xi + yi
    batch = 2 * chip + ci
    L = w_ada.shape[0]
    S = x.shape[1]
    shards = dict(w_in=w_in, w_q_up=w_q_up, w_kv_up=w_kv_up, w_sb_out=w_sb_out, w_mla_out=w_mla_out,
                  w_mix_out=w_mix_out, w_up=w_up, w_down=w_down)

    def my_halves(l):
        def half_of(w):
            half = w.shape[1] // 2
            return lax.dynamic_slice_in_dim(w[l].astype(BF16), ci * half, half, 0)

        return [half_of(shards[n]) for n in _SHARDED]

    def layer_weights(l, gathered):
        W = {}
        for n, g in zip(_SHARDED, gathered):
            by_chip = g.reshape((4,) + shards[n].shape[1:])
            W[n] = by_chip if n in _BY_CHIP else _unshard(by_chip, n)
        W["w_in"] = _pad_w_in(W["w_in"])
        W["w_q_up"] = _pad_w_q(W["w_q_up"])
        W["w_mla_out"] = _pad_w_mla(W["w_mla_out"])
        return dict(W, g_mix=g_mix_norm[l:l + 1], g_mlp=g_mlp_norm[l:l + 1], g_q=g_q_lat[l:l + 1],
                    g_kv=g_kv_lat[l:l + 1])

    gathered0 = _all_gather8(my_halves(0), "gather_weights")

    c_act = _silu(c, "silu_c")
    c_all = _all_gather8([jnp.broadcast_to(c_act, (8, D_MODEL))], "gather_c")[0].reshape(8, 8, D_MODEL)[:, 0]
    c16 = jnp.concatenate([c_all, jnp.zeros_like(c_all)], axis=0)
    ada_cols = w_ada.shape[2]
    b_shard = lax.dynamic_slice_in_dim(b_ada, chip * ada_cols, ada_cols, 1)
    mod_part = jnp.stack([_matmul(c16, w_ada[l], name="ada_mod") for l in range(L)])
    mod_part = _bias_add(mod_part, jnp.broadcast_to(b_shard[:, None, :], mod_part.shape), "ada_bias")
    mod_all = _all_gather8([mod_part.reshape(L * 16, ada_cols)], "gather_mod")[0].reshape(4, 2, L, 16, ada_cols)
    mod_mine = lax.dynamic_index_in_dim(mod_all[:, 0], batch, axis=2, keepdims=False)
    mods = mod_mine.transpose(1, 0, 2).reshape(L, N_MOD, 1, D_MODEL)

    tabs = _rope_tables(positions[0])

    xc, saved, layer_w = x[0], [], [layer_weights(0, gathered0)]
    for l in range(L):
        xc, sv, gathered = _layer_fwd(xc, layer_w[l], mods[l], tabs, my_halves(l + 1) if l + 1 < L else None)
        saved.append(sv)
        if l + 1 < L:
            layer_w.append(layer_weights(l + 1, gathered))
    dxc, dg_final, loss_part = _final_loss(xc, loss_target[0], g_final[None, :], "final_norm_loss")
    loss = lax.psum(loss_part[0, 0], ("x", "y", "c"))
    core = jnp.reshape(ci, (1,)).astype(jnp.int32)

    def sum_on_chip(g):
        by_dest = [g[n] if n in _BY_CHIP else _reshard(g[n], n) for n in _SHARDED]
        from_sibling = _run_exchange(_swap_halves_exchange(by_dest), "grads_swap_halves")
        return [_add_halves(d, r, core, "grads_add_halves") for d, r in zip(by_dest, from_sibling)]

    grads, reduced, pending = [None] * L, [None] * L, None
    for l in reversed(range(L)):
        dxc, grads[l], reduced_above = _layer_bwd(dxc, layer_w[l], mods[l], tabs, saved[l], core, pending)
        if pending:
            reduced[l + 1] = reduced_above
        grads[l]["w_in"] = _unpad_w_in(grads[l]["w_in"])
        grads[l]["w_q_up"] = _unpad_w_q(grads[l]["w_q_up"])
        grads[l]["w_mla_out"] = _unpad_w_mla(grads[l]["w_mla_out"])
        pending = sum_on_chip(grads[l])
    from_chips = _run_exchange(_chip_scatter_exchange(pending), "grads_chip_scatter")
    my_sum = [_sum_slots(part, core, "grads_sum_chips") for part in from_chips]
    reduced[0] = _run_exchange(_sibling_gather_exchange(my_sum), "grads_sibling_gather")
    grad_x = dxc
    gw = {n: jnp.stack([reduced[l][i] for l in range(L)]) for i, n in enumerate(_SHARDED)}

    def row(v):
        return jnp.pad(v, ((0, 0), (0, D_MODEL - v.shape[1])))

    per_layer_rows = N_MOD + 4
    small = jnp.concatenate(
        [jnp.concatenate([grads[l]["dmod"], row(grads[l]["g_mix"]), row(grads[l]["g_mlp"]),
                          row(grads[l]["g_q"]), row(grads[l]["g_kv"])], axis=0) for l in range(L)]
        + [dg_final], axis=0)
    n_small = -(-small.shape[0] // 8) * 8
    small = jnp.pad(small, ((0, n_small - small.shape[0]), (0, 0)))
    small_all = _all_gather8([small], "gather_vector_grads")[0].reshape(8, n_small, D_MODEL)
    small_sum = _sum_blocks(small_all, "sum_vector_grads")
    lay = small_sum[:L * per_layer_rows].reshape(L, per_layer_rows, D_MODEL)
    g_b_ada = lay[:, :N_MOD].reshape(L, N_MOD * D_MODEL)
    g_g_mix, g_g_mlp = lay[:, N_MOD], lay[:, N_MOD + 1]
    g_g_q, g_g_kv = lay[:, N_MOD + 2, :Q_RANK], lay[:, N_MOD + 3, :KV_RANK]
    g_g_final = small_sum[L * per_layer_rows]
    dmod_all = small_all[:, :L * per_layer_rows].reshape(8, L, per_layer_rows, D_MODEL)[:, :, :N_MOD]
    dmod_all = dmod_all.reshape(8, L, N_MOD * D_MODEL)
    dmod_cols = lax.dynamic_slice_in_dim(dmod_all, chip * ada_cols, ada_cols, 2)
    dmod16 = jnp.concatenate([dmod_cols, jnp.zeros_like(dmod_cols)], axis=0)
    g_w_ada = jnp.stack([_matmul(c16, dmod16[:, l], ta=True, name="ada_dw") for l in range(L)])

    weights = dict(w_ada=w_ada, b_ada=b_ada, g_mix_norm=g_mix_norm, w_in=w_in, g_q_lat=g_q_lat, w_q_up=w_q_up,
                   g_kv_lat=g_kv_lat, w_kv_up=w_kv_up, w_sb_out=w_sb_out, w_mla_out=w_mla_out,
                   w_mix_out=w_mix_out, g_mlp_norm=g_mlp_norm, w_up=w_up, w_down=w_down, g_final=g_final)
    mom = dict(w_ada=(m_w_ada, v_w_ada), b_ada=(m_b_ada, v_b_ada), g_mix_norm=(m_g_mix_norm, v_g_mix_norm),
               w_in=(m_w_in, v_w_in), g_q_lat=(m_g_q_lat, v_g_q_lat), w_q_up=(m_w_q_up, v_w_q_up),
               g_kv_lat=(m_g_kv_lat, v_g_kv_lat), w_kv_up=(m_w_kv_up, v_w_kv_up),
               w_sb_out=(m_w_sb_out, v_w_sb_out), w_mla_out=(m_w_mla_out, v_w_mla_out),
               w_mix_out=(m_w_mix_out, v_w_mix_out), g_mlp_norm=(m_g_mlp_norm, v_g_mlp_norm),
               w_up=(m_w_up, v_w_up), w_down=(m_w_down, v_w_down), g_final=(m_g_final, v_g_final))
    gr = dict(gw, w_ada=g_w_ada, b_ada=g_b_ada, g_mix_norm=g_g_mix, g_q_lat=g_g_q, g_kv_lat=g_g_kv,
              g_mlp_norm=g_g_mlp, g_final=g_g_final)
    order = list(weights)
    deltas, new_m, new_v = [], [], []
    for n in order:
        wv, gv, (mv, vv) = weights[n], gr[n], mom[n]
        if wv.ndim == 1:
            d, nm, nv = (t[0] for t in _adamw(wv[None], gv[None], mv[None], vv[None], "adamw_" + n))
        else:
            d, nm, nv = _adamw(wv, gv, mv, vv, "adamw_" + n)
        deltas.append(d)
        new_m.append(nm)
        new_v.append(nv)
    return (loss, grad_x[None], *[gr[n] for n in order], *deltas, *new_m, *new_v)
```

```python
from typing import Any, Callable, Mapping, NamedTuple, Sequence

import jax
import jax.numpy as jnp
from jax import lax
from jax.experimental import pallas as pl
from jax.experimental.pallas import tpu as pltpu

F32 = jnp.float32
BF16 = jnp.bfloat16
MESH = pl.DeviceIdType.MESH

D_MODEL = 1024
N_HEADS = 8
SB_DIM = 64
SB_WIDTH = 512
Q_RANK = 384
KV_RANK = 256
ROPE_DIM = 32
NOPE_DIM = 64
QK_DIM = 96
D_FF = 4096
N_MOD = 6
EPS = 1e-6
ROPE_THETA = 10000.0
SB_SCALE = SB_DIM ** -0.5
MLA_SCALE = QK_DIM ** -0.5
ADAM_LR, ADAM_B1, ADAM_B2, ADAM_EPS, ADAM_WD, ADAM_STEP = 0.001, 0.9, 0.999, 1e-08, 0.01, 10

LANE = 128
IN_PAD = 4352
COL_GATE_SB, COL_GATE_MLA, COL_QSB, COL_KSB, COL_VSB, COL_KVLAT, COL_QLAT, COL_KROPE = (
    0, 1024, 2048, 2560, 3072, 3584, 3840, 4224)
ROPE_LANE0 = 64
VMEM_LIMIT = 48 * 1024 * 1024
NEG_BIG = -1e30


def _cp(*sem):
    return pltpu.CompilerParams(dimension_semantics=sem, vmem_limit_bytes=VMEM_LIMIT)


def _tile(n, prefs):
    for t in prefs:
        if t <= n and n % t == 0:
            return t
    return n


def _dot(a, b, dims):
    return lax.dot_general(a, b, (dims, ((), ())), preferred_element_type=F32)


def _nn(a, b):
    return _dot(a, b, ((1,), (0,)))


def _nt(a, b):
    return _dot(a, b, ((1,), (1,)))


def _tn(a, b):
    return _dot(a, b, ((0,), (0,)))


def _sharded_dims(shape, kind):
    n, r, cs = shape
    return (n * r, cs) if kind == "row" else (r, n * cs)


def _sharded_spec(shape, kind, t_rows, t_cols, tile_of):
    _, r, cs = shape
    if kind == "row":
        assert r % t_rows == 0, (shape, t_rows)
        per = r // t_rows

        def index(i, j, k):
            tr, tc = tile_of(i, j, k)
            return tr // per, tr % per, tc
    else:
        assert cs % t_cols == 0, (shape, t_cols)
        per = cs // t_cols

        def index(i, j, k):
            tr, tc = tile_of(i, j, k)
            return tc // per, tr, tc % per
    return pl.BlockSpec((None, t_rows, t_cols), index)


def _matmul(a, b, *, ta=False, tb=False, out_dtype=F32, b_sharded=None, out_sharded=None, name):
    (K, M) = a.shape if ta else a.shape[::-1]
    b_dims = _sharded_dims(b.shape, b_sharded) if b_sharded else b.shape
    (N, Kb) = b_dims if tb else b_dims[::-1]
    assert K == Kb, (a.shape, b.shape, ta, tb)
    tm = _tile(M, (512, 384, 256, 128))
    tn = _tile(N, (1024, 2176, 768, 512, 384, 256, 128))
    tk = _tile(K, (1024, 2176, 768, 512, 384, 256, 128))
    nk = K // tk
    dims = ((0 if ta else 1,), (1 if tb else 0,))

    def body(a_ref, b_ref, o_ref, *acc):
        prod = _dot(a_ref[...].astype(BF16), b_ref[...].astype(BF16), dims)
        if nk == 1:
            o_ref[...] = prod.astype(out_dtype)
            return
        acc_ref, = acc
        k = pl.program_id(2)

        @pl.when(k == 0)
        def _():
            acc_ref[...] = prod

        @pl.when(k > 0)
        def _():
            acc_ref[...] += prod

        @pl.when(k == nk - 1)
        def _():
            o_ref[...] = acc_ref[...].astype(out_dtype)

    a_spec = (pl.BlockSpec((tk, tm), lambda i, j, k: (k, i)) if ta
              else pl.BlockSpec((tm, tk), lambda i, j, k: (i, k)))
    if b_sharded:
        b_spec = (_sharded_spec(b.shape, b_sharded, tn, tk, lambda i, j, k: (j, k)) if tb
                  else _sharded_spec(b.shape, b_sharded, tk, tn, lambda i, j, k: (k, j)))
    else:
        b_spec = (pl.BlockSpec((tn, tk), lambda i, j, k: (j, k)) if tb
                  else pl.BlockSpec((tk, tn), lambda i, j, k: (k, j)))
    if out_sharded:
        kind, shape = out_sharded
        assert _sharded_dims(shape, kind) == (M, N), (shape, kind, M, N)
        out_spec = _sharded_spec(shape, kind, tm, tn, lambda i, j, k: (i, j))
        out_shape = jax.ShapeDtypeStruct(shape, out_dtype)
    else:
        out_spec = pl.BlockSpec((tm, tn), lambda i, j, k: (i, j))
        out_shape = jax.ShapeDtypeStruct((M, N), out_dtype)
    return pl.pallas_call(
        body, grid=(M // tm, N // tn, nk), in_specs=[a_spec, b_spec], out_specs=out_spec, out_shape=out_shape,
        scratch_shapes=[pltpu.VMEM((tm, tn), F32)] if nk > 1 else [],
        compiler_params=_cp("parallel", "parallel", "arbitrary"), name=name)(a, b)


def _rows(ts, w, col=0):
    return pl.BlockSpec((ts, w), lambda i: (i, col))


def _vec(w):
    return pl.BlockSpec((1, w), lambda i: (0, 0))


def _ts(S):
    return _tile(S, (256, 128))


def _rms(x):
    return lax.rsqrt(jnp.mean(x * x, axis=-1, keepdims=True) + EPS)


def _colsum(x):
    return jnp.sum(x, axis=0, keepdims=True)


def _normmod_fwd(x, g, sc, sh, name):
    S, W = x.shape
    ts = _ts(S)

    def body(x_ref, g_ref, sc_ref, sh_ref, h_ref):
        xv = x_ref[...]
        h_ref[...] = ((xv * _rms(xv)) * g_ref[...] * (1.0 + sc_ref[...]) + sh_ref[...]).astype(BF16)

    return pl.pallas_call(
        body, grid=(S // ts,), in_specs=[_rows(ts, W), _vec(W), _vec(W), _vec(W)],
        out_specs=_rows(ts, W), out_shape=jax.ShapeDtypeStruct((S, W), BF16),
        compiler_params=_cp("parallel"), name=name)(x, g, sc, sh)


def _normmod_bwd(x, dh, g, sc, dres, name):
    S, W = x.shape
    ts = _ts(S)

    def body(x_ref, dh_ref, g_ref, sc_ref, dres_ref, dx_ref, dsh_ref, dsc_ref, dg_ref):
        @pl.when(pl.program_id(0) == 0)
        def _():
            dsh_ref[...] = jnp.zeros_like(dsh_ref)
            dsc_ref[...] = jnp.zeros_like(dsc_ref)
            dg_ref[...] = jnp.zeros_like(dg_ref)

        xv, dh_v, gv = x_ref[...], dh_ref[...], g_ref[...]
        r = _rms(xv)
        y = xv * r
        dn = dh_v * (1.0 + sc_ref[...])
        dy = dn * gv
        dx_ref[...] = dres_ref[...] + r * (dy - y * jnp.mean(dy * y, axis=-1, keepdims=True))
        dsh_ref[...] += _colsum(dh_v)
        dsc_ref[...] += _colsum(dh_v * y * gv)
        dg_ref[...] += _colsum(dn * y)

    vec_out = jax.ShapeDtypeStruct((1, W), F32)
    return pl.pallas_call(
        body, grid=(S // ts,),
        in_specs=[_rows(ts, W), _rows(ts, W), _vec(W), _vec(W), _rows(ts, W)],
        out_specs=[_rows(ts, W), _vec(W), _vec(W), _vec(W)],
        out_shape=[jax.ShapeDtypeStruct((S, W), F32), vec_out, vec_out, vec_out],
        compiler_params=_cp("arbitrary"), name=name)(x, dh, g, sc, dres)


def _rmsnorm_fwd(p, width, col, g, name):
    S = p.shape[0]
    ts = _ts(S)

    def body(x_ref, g_ref, y_ref):
        xv = x_ref[...]
        y_ref[...] = ((xv * _rms(xv)) * g_ref[...]).astype(BF16)

    return pl.pallas_call(
        body, grid=(S // ts,), in_specs=[_rows(ts, width, col), _vec(width)],
        out_specs=_rows(ts, width), out_shape=jax.ShapeDtypeStruct((S, width), BF16),
        compiler_params=_cp("parallel"), name=name)(p, g)


def _rmsnorm_bwd(p, width, col, dn, g, name):
    S = p.shape[0]
    ts = _ts(S)

    def body(x_ref, dn_ref, g_ref, dx_ref, dg_ref):
        @pl.when(pl.program_id(0) == 0)
        def _():
            dg_ref[...] = jnp.zeros_like(dg_ref)

        xv, dn_v = x_ref[...], dn_ref[...]
        r = _rms(xv)
        y = xv * r
        dy = dn_v * g_ref[...]
        dx_ref[...] = r * (dy - y * jnp.mean(dy * y, axis=-1, keepdims=True))
        dg_ref[...] += _colsum(dn_v * y)

    return pl.pallas_call(
        body, grid=(S // ts,), in_specs=[_rows(ts, width, col), _rows(ts, width), _vec(width)],
        out_specs=[_rows(ts, width), _vec(width)],
        out_shape=[jax.ShapeDtypeStruct((S, width), F32), jax.ShapeDtypeStruct((1, width), F32)],
        compiler_params=_cp("arbitrary"), name=name)(p, dn, g)


def _rope_rot(t, c, s1, s2):
    return t * c + pltpu.roll(t, LANE - 16, 1) * s1 + pltpu.roll(t, 16, 1) * s2


def _rope_rot_t(d, c, s1, s2):
    return d * c + pltpu.roll(d * s1, 16, 1) + pltpu.roll(d * s2, LANE - 16, 1)


def _rope_fwd(qp, p, tabs, name):
    S = qp.shape[0]
    ts = _ts(S)
    W = N_HEADS * LANE

    def body(q_ref, kr_ref, c_ref, s1_ref, s2_ref, qr_ref, kpe_ref):
        c, s1, s2 = c_ref[...], s1_ref[...], s2_ref[...]
        for h in range(N_HEADS):
            sl = slice(h * LANE, (h + 1) * LANE)
            qr_ref[:, sl] = _rope_rot(q_ref[:, sl], c, s1, s2).astype(BF16)
        kpe_ref[...] = _rope_rot(kr_ref[...], c, s1, s2).astype(BF16)

    tab = _rows(ts, LANE)
    return pl.pallas_call(
        body, grid=(S // ts,), in_specs=[_rows(ts, W), _rows(ts, LANE, COL_KROPE // LANE), tab, tab, tab],
        out_specs=[_rows(ts, W), _rows(ts, LANE)],
        out_shape=[jax.ShapeDtypeStruct((S, W), BF16), jax.ShapeDtypeStruct((S, LANE), BF16)],
        compiler_params=_cp("parallel"), name=name)(qp, p, *tabs)


def _rope_bwd(dqr, dkpe_heads, tabs, name):
    S = dqr.shape[0]
    ts = _ts(S)
    W = N_HEADS * LANE

    def body(dq_ref, dk_ref, c_ref, s1_ref, s2_ref, dqp_ref, dkr_ref):
        c, s1, s2 = c_ref[...], s1_ref[...], s2_ref[...]
        dk = dk_ref[:, 0:LANE]
        for h in range(N_HEADS):
            sl = slice(h * LANE, (h + 1) * LANE)
            dqp_ref[:, sl] = _rope_rot_t(dq_ref[:, sl], c, s1, s2).astype(BF16)
            if h:
                dk = dk + dk_ref[:, sl]
        dkr_ref[...] = _rope_rot_t(dk, c, s1, s2)

    tab = _rows(ts, LANE)
    return pl.pallas_call(
        body, grid=(S // ts,), in_specs=[_rows(ts, W), _rows(ts, W), tab, tab, tab],
        out_specs=[_rows(ts, W), _rows(ts, LANE)],
        out_shape=[jax.ShapeDtypeStruct((S, W), BF16), jax.ShapeDtypeStruct((S, LANE), F32)],
        compiler_params=_cp("parallel"), name=name)(dqr, dkpe_heads, *tabs)


def _merge_fwd(p, o_sb, o_mla, name):
    S, W = o_sb.shape
    ts = _ts(S)

    def body(gs_ref, gm_ref, a_ref, b_ref, m_ref):
        m_ref[...] = (jax.nn.sigmoid(gs_ref[...]) * a_ref[...]
                      + jax.nn.sigmoid(gm_ref[...]) * b_ref[...]).astype(BF16)

    return pl.pallas_call(
        body, grid=(S // ts,),
        in_specs=[_rows(ts, W, COL_GATE_SB // W), _rows(ts, W, COL_GATE_MLA // W), _rows(ts, W), _rows(ts, W)],
        out_specs=_rows(ts, W), out_shape=jax.ShapeDtypeStruct((S, W), BF16),
        compiler_params=_cp("parallel"), name=name)(p, p, o_sb, o_mla)


def _merge_bwd(p, o_sb, o_mla, dm, name):
    S, W = o_sb.shape
    ts = _ts(S)

    def body(gs_ref, gm_ref, a_ref, b_ref, dm_ref, da_ref, db_ref, dgs_ref, dgm_ref):
        dmv = dm_ref[...]
        sa, sb = jax.nn.sigmoid(gs_ref[...]), jax.nn.sigmoid(gm_ref[...])
        da_ref[...] = (dmv * sa).astype(BF16)
        db_ref[...] = (dmv * sb).astype(BF16)
        dgs_ref[...] = dmv * a_ref[...] * sa * (1.0 - sa)
        dgm_ref[...] = dmv * b_ref[...] * sb * (1.0 - sb)

    row = _rows(ts, W)
    return pl.pallas_call(
        body, grid=(S // ts,),
        in_specs=[_rows(ts, W, COL_GATE_SB // W), _rows(ts, W, COL_GATE_MLA // W), row, row, row],
        out_specs=[row, row, row, row],
        out_shape=[jax.ShapeDtypeStruct((S, W), BF16), jax.ShapeDtypeStruct((S, W), BF16),
                   jax.ShapeDtypeStruct((S, W), F32), jax.ShapeDtypeStruct((S, W), F32)],
        compiler_params=_cp("parallel"), name=name)(p, p, o_sb, o_mla, dm)


def _res_fwd(x, y, gate, name):
    S, W = x.shape
    ts = _ts(S)

    def body(x_ref, y_ref, g_ref, o_ref):
        o_ref[...] = x_ref[...] + g_ref[...] * y_ref[...]

    return pl.pallas_call(
        body, grid=(S // ts,), in_specs=[_rows(ts, W), _rows(ts, W), _vec(W)], out_specs=_rows(ts, W),
        out_shape=jax.ShapeDtypeStruct((S, W), F32), compiler_params=_cp("parallel"), name=name)(x, y, gate)


def _res_bwd(dx, y, gate, name):
    S, W = dx.shape
    ts = _ts(S)

    def body(dx_ref, y_ref, g_ref, dy_ref, dg_ref):
        @pl.when(pl.program_id(0) == 0)
        def _():
            dg_ref[...] = jnp.zeros_like(dg_ref)

        dxv = dx_ref[...]
        dy_ref[...] = (g_ref[...] * dxv).astype(BF16)
        dg_ref[...] += _colsum(dxv * y_ref[...])

    return pl.pallas_call(
        body, grid=(S // ts,), in_specs=[_rows(ts, W), _rows(ts, W), _vec(W)],
        out_specs=[_rows(ts, W), _vec(W)],
        out_shape=[jax.ShapeDtypeStruct((S, W), BF16), jax.ShapeDtypeStruct((1, W), F32)],
        compiler_params=_cp("arbitrary"), name=name)(dx, y, gate)


def _sqrelu_fwd(u, name):
    S, W = u.shape
    ts = _ts(S)

    def body(u_ref, a_ref):
        r = jnp.maximum(u_ref[...], 0.0)
        a_ref[...] = (r * r).astype(BF16)

    return pl.pallas_call(
        body, grid=(S // ts,), in_specs=[_rows(ts, W)], out_specs=_rows(ts, W),
        out_shape=jax.ShapeDtypeStruct((S, W), BF16), compiler_params=_cp("parallel"), name=name)(u)


def _sqrelu_bwd(da, u, name):
    S, W = u.shape
    ts = _ts(S)

    def body(da_ref, u_ref, du_ref):
        du_ref[...] = (da_ref[...] * (2.0 * jnp.maximum(u_ref[...], 0.0))).astype(BF16)

    return pl.pallas_call(
        body, grid=(S // ts,), in_specs=[_rows(ts, W), _rows(ts, W)], out_specs=_rows(ts, W),
        out_shape=jax.ShapeDtypeStruct((S, W), BF16), compiler_params=_cp("parallel"), name=name)(da, u)


def _final_loss(x, target, g, name):
    S, W = x.shape
    ts = _ts(S)

    def body(x_ref, t_ref, g_ref, dx_ref, dg_ref, loss_ref):
        @pl.when(pl.program_id(0) == 0)
        def _():
            dg_ref[...] = jnp.zeros_like(dg_ref)
            loss_ref[...] = jnp.zeros_like(loss_ref)

        xv, gv = x_ref[...], g_ref[...]
        r = _rms(xv)
        y = xv * r
        err = y * gv - t_ref[...]
        loss_ref[...] += jnp.full((1, LANE), 0.5 * jnp.sum(jnp.mean(err * err, axis=-1)), F32)
        dout = err * (1.0 / W)
        dy = dout * gv
        dx_ref[...] = r * (dy - y * jnp.mean(dy * y, axis=-1, keepdims=True))
        dg_ref[...] += _colsum(dout * y)

    return pl.pallas_call(
        body, grid=(S // ts,), in_specs=[_rows(ts, W), _rows(ts, W), _vec(W)],
        out_specs=[_rows(ts, W), _vec(W), _vec(LANE)],
        out_shape=[jax.ShapeDtypeStruct((S, W), F32), jax.ShapeDtypeStruct((1, W), F32),
                   jax.ShapeDtypeStruct((1, LANE), F32)],
        compiler_params=_cp("arbitrary"), name=name)(x, target, g)


def _silu(c, name):
    def body(c_ref, o_ref):
        cv = c_ref[...]
        o_ref[...] = cv * jax.nn.sigmoid(cv)

    return pl.pallas_call(body, out_shape=jax.ShapeDtypeStruct(c.shape, F32), name=name)(c)


def _bias_add(a, b, name):
    def body(a_ref, b_ref, o_ref):
        o_ref[...] = a_ref[...] + b_ref[...]

    return pl.pallas_call(body, out_shape=jax.ShapeDtypeStruct(a.shape, F32), name=name)(a, b)


def _sum_blocks(xs, name):
    n = xs.shape[0]

    def body(x_ref, o_ref):
        acc = x_ref[0]
        for d in range(1, n):
            acc = acc + x_ref[d]
        o_ref[...] = acc

    return pl.pallas_call(body, out_shape=jax.ShapeDtypeStruct(xs.shape[1:], F32), name=name)(xs)


def _adamw(w, g, m, v, name):
    shape = w.shape
    cols = shape[-1]
    w2, g2, m2, v2 = (t.reshape(-1, cols) for t in (w, g, m, v))
    rows = w2.shape[0]
    tr = _tile(rows, (128,))
    c1 = 1.0 - ADAM_B1 ** ADAM_STEP
    c2 = 1.0 - ADAM_B2 ** ADAM_STEP

    def body(w_ref, g_ref, m_ref, v_ref, d_ref, nm_ref, nv_ref):
        gv = g_ref[...]
        nm = ADAM_B1 * m_ref[...] + (1.0 - ADAM_B1) * gv
        nv = ADAM_B2 * v_ref[...] + (1.0 - ADAM_B2) * (gv * gv)
        d_ref[...] = -ADAM_LR * ((nm / c1) / (jnp.sqrt(nv / c2) + ADAM_EPS) + ADAM_WD * w_ref[...])
        nm_ref[...] = nm
        nv_ref[...] = nv

    spec = pl.BlockSpec((tr, cols), lambda i: (i, 0))
    out = jax.ShapeDtypeStruct((rows, cols), F32)
    d, nm, nv = pl.pallas_call(
        body, grid=(rows // tr,), in_specs=[spec] * 4, out_specs=[spec] * 3, out_shape=[out] * 3,
        compiler_params=_cp("parallel"), name=name)(w2, g2, m2, v2)
    return d.reshape(shape), nm.reshape(shape), nv.reshape(shape)


def _split_dot(x, tri):
    hi = x.astype(BF16)
    lo = (x - hi.astype(F32)).astype(BF16)
    return _nn(hi, tri) + _nn(lo, tri)


def _sb_logs(z):
    soft = jnp.log(1.0 + jnp.exp(-jnp.abs(z)))
    return jnp.minimum(z, 0.0) - soft, -jnp.maximum(z, 0.0) - soft


def _attn_call(body, grid, ins, in_specs, out_specs, out_shape, scratch, ex, name):
    n_out = len(out_shape)
    body, extra = _carry(ex, body, len(ins), n_out, len(scratch), grid)
    res = pl.pallas_call(
        body, grid=grid, in_specs=in_specs + extra["in_specs"], out_specs=out_specs + extra["out_specs"],
        out_shape=out_shape + extra["out_shape"], scratch_shapes=scratch + extra["scratch"],
        input_output_aliases=extra["aliases"], compiler_params=_cp("arbitrary", "arbitrary"),
        name=name)(*ins, *extra["ins"])
    return res[:n_out], res[n_out:]


def _sb_fwd(p, name, ex=None):
    S = p.shape[0]
    t = _ts(S)
    qb, kb, vb = COL_QSB // LANE, COL_KSB // LANE, COL_VSB // LANE

    def body(q_ref, k_ref, v_ref, o_ref, cb_ref, acc_ref):
        i = pl.program_id(1)
        lane = lax.broadcasted_iota(jnp.int32, (t, LANE), 1)
        rows = lax.broadcasted_iota(jnp.int32, (t, t), 0)
        cols = lax.broadcasted_iota(jnp.int32, (t, t), 1)
        after = jnp.where(rows > cols, 1.0, 0.0).astype(BF16)
        diag = cols < rows
        q = q_ref[...] * SB_SCALE
        acc_ref[...] = jnp.zeros_like(acc_ref)
        cb_ref[...] = jnp.zeros_like(cb_ref)
        hms = [(lane >= SB_DIM * h) & (lane < SB_DIM * (h + 1)) for h in range(2)]
        qhs = [jnp.where(hm, q, 0.0).astype(BF16) for hm in hms]

        def step(j, cs, masked):
            rows_j = pl.ds(pl.multiple_of(j * t, t), t)
            kj = k_ref[rows_j, :].astype(BF16)
            vf = v_ref[rows_j, :]
            out, pv = [], None
            for h in range(2):
                ls, lf = _sb_logs(_nt(qhs[h], kj))
                if masked:
                    lf = jnp.where(diag, lf, 0.0)
                a = jnp.exp(ls + _split_dot(lf, after) + cs[h])
                if masked:
                    a = jnp.where(diag, a, 0.0)
                term = _nn(a.astype(BF16), jnp.where(hms[h], vf, 0.0).astype(BF16))
                pv = term if pv is None else pv + term
                cb_ref[h] = jnp.where(lane == j, cs[h], cb_ref[h])
                out.append(cs[h] + jnp.sum(lf, axis=1, keepdims=True))
            acc_ref[...] += pv
            return tuple(out)

        zero = jnp.zeros((t, 1), F32)
        cs = step(i, (zero, zero), True)
        lax.fori_loop(0, i, lambda it, cs: step(i - 1 - it, cs, False), cs)
        o_ref[...] = acc_ref[...].astype(BF16)

    return _attn_call(
        body, (SB_WIDTH // LANE, S // t), [p, p, p],
        [pl.BlockSpec((t, LANE), lambda hp, i: (i, qb + hp)),
         pl.BlockSpec((S, LANE), lambda hp, i: (0, kb + hp)),
         pl.BlockSpec((S, LANE), lambda hp, i: (0, vb + hp))],
        [pl.BlockSpec((t, LANE), lambda hp, i: (i, hp)),
         pl.BlockSpec((2, t, LANE), lambda hp, i: (hp, i, 0))],
        [jax.ShapeDtypeStruct((S, SB_WIDTH), BF16), jax.ShapeDtypeStruct((N_HEADS, S, LANE), F32)],
        [pltpu.VMEM((t, LANE), F32)], ex, name)


def _sb_bwd(p, do, cb, name, ex=None):
    S = p.shape[0]
    t = _ts(S)
    qb, kb, vb = COL_QSB // LANE, COL_KSB // LANE, COL_VSB // LANE

    def body(q_ref, k_ref, v_ref, do_ref, cb_ref, dq_ref, dk_ref, dv_ref, acc_ref):
        i = pl.program_id(1)

        @pl.when(i == 0)
        def _():
            dk_ref[...] = jnp.zeros_like(dk_ref)
            dv_ref[...] = jnp.zeros_like(dv_ref)

        lane = lax.broadcasted_iota(jnp.int32, (t, LANE), 1)
        rows = lax.broadcasted_iota(jnp.int32, (t, t), 0)
        cols = lax.broadcasted_iota(jnp.int32, (t, t), 1)
        after = jnp.where(rows > cols, 1.0, 0.0).astype(BF16)
        before = jnp.where(rows < cols, 1.0, 0.0).astype(BF16)
        diag = cols < rows
        q = q_ref[...] * SB_SCALE
        dov = do_ref[...]
        acc_ref[...] = jnp.zeros_like(acc_ref)
        hms = [(lane >= SB_DIM * h) & (lane < SB_DIM * (h + 1)) for h in range(2)]
        qhs = [jnp.where(hm, q, 0.0).astype(BF16) for hm in hms]
        dohs = [jnp.where(hm, dov, 0.0).astype(BF16) for hm in hms]

        def step(j, fs, masked):
            rows_j = pl.ds(pl.multiple_of(j * t, t), t)
            kf = k_ref[rows_j, :]
            kj = kf.astype(BF16)
            vj = v_ref[rows_j, :].astype(BF16)
            out, dq_t, dk_t, dv_t = [], None, None, None
            for h in range(2):
                ls, lf = _sb_logs(_nt(qhs[h], kj))
                if masked:
                    lf = jnp.where(diag, lf, 0.0)
                c = jnp.sum(jnp.where(lane == j, cb_ref[h], 0.0), axis=1, keepdims=True)
                a = jnp.exp(ls + _split_dot(lf, after) + c)
                if masked:
                    a = jnp.where(diag, a, 0.0)
                dl = _nt(dohs[h], vj) * a
                sg = jnp.exp(ls)
                dz = dl * (1.0 - sg) - sg * (_split_dot(dl, before) + fs[h])
                if masked:
                    dz = jnp.where(diag, dz, 0.0)
                dzb = dz.astype(BF16)
                terms = (_nn(dzb, jnp.where(hms[h], kf, 0.0).astype(BF16)), _tn(dzb, qhs[h]),
                         _tn(a.astype(BF16), dohs[h]))
                dq_t, dk_t, dv_t = terms if dq_t is None else (dq_t + terms[0], dk_t + terms[1], dv_t + terms[2])
                out.append(fs[h] + jnp.sum(dl, axis=1, keepdims=True))
            acc_ref[...] += dq_t
            dk_ref[rows_j, :] += dk_t
            dv_ref[rows_j, :] += dv_t
            return tuple(out)

        zero = jnp.zeros((t, 1), F32)
        fs = lax.fori_loop(0, i, lambda j, fs: step(j, fs, False), (zero, zero))
        step(i, fs, True)
        dq_ref[...] = acc_ref[...] * SB_SCALE

    col = lambda hp, i: (0, hp)
    out = jax.ShapeDtypeStruct((S, SB_WIDTH), F32)
    return _attn_call(
        body, (SB_WIDTH // LANE, S // t), [p, p, p, do, cb],
        [pl.BlockSpec((t, LANE), lambda hp, i: (i, qb + hp)),
         pl.BlockSpec((S, LANE), lambda hp, i: (0, kb + hp)),
         pl.BlockSpec((S, LANE), lambda hp, i: (0, vb + hp)),
         pl.BlockSpec((t, LANE), lambda hp, i: (i, hp)),
         pl.BlockSpec((2, t, LANE), lambda hp, i: (hp, i, 0))],
        [pl.BlockSpec((t, LANE), lambda hp, i: (i, hp)), pl.BlockSpec((S, LANE), col), pl.BlockSpec((S, LANE), col)],
        [out, out, out], [pltpu.VMEM((t, LANE), F32)], ex, name)


def _mla_fwd(qr, kv, kpe, name, ex=None):
    S = qr.shape[0]
    t = _ts(S)

    def body(q_ref, kv_ref, kpe_ref, o_ref, lse_ref, acc_ref, m_ref):
        i = pl.program_id(1)
        low = lax.broadcasted_iota(jnp.int32, (t, LANE), 1) < NOPE_DIM
        rows = lax.broadcasted_iota(jnp.int32, (t, t), 0)
        cols = lax.broadcasted_iota(jnp.int32, (t, t), 1)
        causal = cols <= rows
        one = jnp.ones((t, LANE), BF16)
        heads = [slice(h * LANE, (h + 1) * LANE) for h in range(2)]
        qs = [q_ref[:, sl] for sl in heads]
        acc_ref[...] = jnp.zeros_like(acc_ref)
        m_ref[...] = jnp.full_like(m_ref, NEG_BIG)

        def step(j, masked):
            rows_j = pl.ds(pl.multiple_of(j * t, t), t)
            kpe_j = kpe_ref[rows_j, :]
            for h, sl in enumerate(heads):
                kvj = kv_ref[rows_j, sl]
                z = _nt(qs[h], jnp.where(low, kvj, kpe_j)) * MLA_SCALE
                if masked:
                    z = jnp.where(causal, z, NEG_BIG)
                m_old = m_ref[h]
                m_new = jnp.maximum(m_old, jnp.max(z, axis=1, keepdims=True))
                pr = jnp.exp(z - m_new)
                acc_ref[:, sl] = jnp.exp(m_old - m_new) * acc_ref[:, sl] + _nn(
                    pr.astype(BF16), jnp.where(low, one, kvj))
                m_ref[h] = m_new

        def loop(j, carry):
            step(j, False)
            return carry

        lax.fori_loop(0, i, loop, 0)
        step(i, True)
        for h, sl in enumerate(heads):
            acc = acc_ref[:, sl]
            den = acc[:, 0:1]
            o_ref[:, sl] = jnp.where(low, 0.0, acc / den).astype(BF16)
            lse_ref[h] = jnp.broadcast_to(m_ref[h] + jnp.log(den), (t, LANE))

    pair = 2 * LANE
    return _attn_call(
        body, (N_HEADS // 2, S // t), [qr, kv, kpe],
        [pl.BlockSpec((t, pair), lambda hp, i: (i, hp)),
         pl.BlockSpec((S, pair), lambda hp, i: (0, hp)),
         pl.BlockSpec((S, LANE), lambda hp, i: (0, 0))],
        [pl.BlockSpec((t, pair), lambda hp, i: (i, hp)), pl.BlockSpec((2, t, LANE), lambda hp, i: (hp, i, 0))],
        [jax.ShapeDtypeStruct((S, N_HEADS * LANE), BF16), jax.ShapeDtypeStruct((N_HEADS, S, LANE), F32)],
        [pltpu.VMEM((t, pair), F32), pltpu.VMEM((2, t, 1), F32)], ex, name)


def _mla_bwd(qr, kv, kpe, do, o, lse, name, ex=None):
    S = qr.shape[0]
    t = _ts(S)

    def body(q_ref, kv_ref, kpe_ref, do_ref, o_ref, lse_ref, dq_ref, dkv_ref, dkpe_ref, acc_ref):
        i = pl.program_id(1)

        @pl.when(i == 0)
        def _():
            dkv_ref[...] = jnp.zeros_like(dkv_ref)
            dkpe_ref[...] = jnp.zeros_like(dkpe_ref)

        low = lax.broadcasted_iota(jnp.int32, (t, LANE), 1) < NOPE_DIM
        rows = lax.broadcasted_iota(jnp.int32, (t, t), 0)
        cols = lax.broadcasted_iota(jnp.int32, (t, t), 1)
        causal = cols <= rows
        heads = [slice(h * LANE, (h + 1) * LANE) for h in range(2)]
        qs = [q_ref[:, sl] for sl in heads]
        dovs = [do_ref[:, sl] for sl in heads]
        dobs = [d.astype(BF16) for d in dovs]
        deltas = [jnp.sum(dovs[h] * o_ref[:, sl].astype(F32), axis=1, keepdims=True) for h, sl in enumerate(heads)]
        lses = [lse_ref[h][:, 0:1] for h in range(2)]
        acc_ref[...] = jnp.zeros_like(acc_ref)

        def step(j, masked):
            rows_j = pl.ds(pl.multiple_of(j * t, t), t)
            kpe_j = kpe_ref[rows_j, :]
            for h, sl in enumerate(heads):
                kvj = kv_ref[rows_j, sl]
                kcat = jnp.where(low, kvj, kpe_j)
                z = _nt(qs[h], kcat) * MLA_SCALE
                if masked:
                    z = jnp.where(causal, z, NEG_BIG)
                pr = jnp.exp(z - lses[h])
                ds = (pr * (_nt(dobs[h], kvj) - deltas[h])).astype(BF16)
                acc_ref[:, sl] += _nn(ds, kcat)
                dkc = _tn(ds, qs[h]) * MLA_SCALE
                dkv_ref[rows_j, sl] += jnp.where(low, dkc, _tn(pr.astype(BF16), dobs[h]))
                dkpe_ref[rows_j, sl] += jnp.where(low, 0.0, dkc)

        def loop(j, carry):
            step(j, False)
            return carry

        lax.fori_loop(0, i, loop, 0)
        step(i, True)
        dq_ref[...] = acc_ref[...] * MLA_SCALE

    pair = 2 * LANE
    blk = pl.BlockSpec((t, pair), lambda hp, i: (i, hp))
    col = pl.BlockSpec((S, pair), lambda hp, i: (0, hp))
    out = jax.ShapeDtypeStruct((S, N_HEADS * LANE), F32)
    return _attn_call(
        body, (N_HEADS // 2, S // t), [qr, kv, kpe, do, o, lse],
        [blk, col, pl.BlockSpec((S, LANE), lambda hp, i: (0, 0)), blk, blk,
         pl.BlockSpec((2, t, LANE), lambda hp, i: (hp, i, 0))],
        [blk, col, col], [out, out, out], [pltpu.VMEM((t, pair), F32)], ex, name)


_ANY = pl.BlockSpec(memory_space=pl.ANY)


def _place():
    return lax.axis_index("x"), lax.axis_index("y"), lax.axis_index("c")


class _Exchange(NamedTuple):
    ins: Sequence[Any]
    outs: Sequence[Any]
    aliases: Mapping[int, int]
    n_remote: int
    n_local: int
    start: Callable
    finish: Callable


def _exchange_scratch(ex):
    return [pltpu.SemaphoreType.DMA((ex.n_remote,)), pltpu.SemaphoreType.DMA((ex.n_remote,)),
            pltpu.SemaphoreType.DMA((ex.n_local,))]


def _run_exchange(ex, name):
    n_in, n_out = len(ex.ins), len(ex.outs)

    def body(*refs):
        args = (refs[:n_in], refs[n_in:n_in + n_out], *refs[n_in + n_out:])
        ex.start(*args)
        ex.finish(*args)

    return pl.pallas_call(
        body, out_shape=list(ex.outs), in_specs=[_ANY] * n_in, out_specs=[_ANY] * n_out,
        scratch_shapes=_exchange_scratch(ex), input_output_aliases=dict(ex.aliases), name=name)(*ex.ins)


def _carry(ex, body, n_in, n_out, n_scratch, grid):
    if ex is None:
        return body, dict(ins=[], in_specs=[], out_specs=[], out_shape=[], scratch=[], aliases={})
    e_in, e_out = len(ex.ins), len(ex.outs)

    def carried(*refs):
        own_in, refs = refs[:n_in], refs[n_in:]
        ex_in, refs = refs[:e_in], refs[e_in:]
        own_out, refs = refs[:n_out], refs[n_out:]
        ex_out, refs = refs[:e_out], refs[e_out:]
        own_scratch, sems = refs[:n_scratch], refs[n_scratch:]
        at = [pl.program_id(d) for d in range(2)]

        @pl.when((at[0] == 0) & (at[1] == 0))
        def _():
            ex.start(ex_in, ex_out, *sems)

        body(*own_in, *own_out, *own_scratch)

        @pl.when((at[0] == grid[0] - 1) & (at[1] == grid[1] - 1))
        def _():
            ex.finish(ex_in, ex_out, *sems)

    return carried, dict(
        ins=list(ex.ins), in_specs=[_ANY] * e_in, out_specs=[_ANY] * e_out, out_shape=list(ex.outs),
        scratch=_exchange_scratch(ex), aliases={n_in + i: n_out + o for i, o in ex.aliases.items()})


def _gather_exchange(arrs, phase="all"):
    n_t = len(arrs)
    ms = [a.shape[0] // (8 if phase == "b" else 1) for a in arrs]

    def plan(in_refs, out_refs, send_sems, recv_sems, local_sems):
        x, y, c = _place()
        me, sibling = (x, y, c), (x, y, 1 - c)
        chips = [(1 - x, y), (x, 1 - y), (1 - x, 1 - y)]

        def rows(ref, t, px, py, pc):
            return ref.at[pl.ds((4 * px + 2 * py + pc) * ms[t], ms[t]), :]

        def copy(t, k, block, to, src):
            return pltpu.make_async_remote_copy(
                src_ref=src, dst_ref=rows(out_refs[t], t, *block), send_sem=send_sems.at[7 * t + k],
                recv_sem=recv_sems.at[7 * t + k], device_id=to, device_id_type=MESH)

        mine, first, first_in, passed, passed_in = [], [], [], [], []
        for t in range(n_t):
            if phase != "b":
                mine.append(pltpu.make_async_copy(in_refs[t], rows(out_refs[t], t, *me), local_sems.at[t]))
                first.append(copy(t, 0, me, sibling, in_refs[t]))
                first_in.append(copy(t, 0, sibling, me, in_refs[t]))
                for j, chip in enumerate(chips):
                    first.append(copy(t, 1 + j, me, (*chip, c), in_refs[t]))
                    first_in.append(copy(t, 1 + j, (*chip, c), me, in_refs[t]))
            if phase != "a":
                held = in_refs[t] if phase == "b" else out_refs[t]
                for j, chip in enumerate(chips):
                    passed.append(copy(t, 4 + j, (*chip, c), sibling, rows(held, t, *chip, c)))
                    passed_in.append(copy(t, 4 + j, (*chip, 1 - c), me, rows(held, t, *chip, c)))
        return mine, first, first_in, passed, passed_in

    def start(*refs):
        mine, first, _, passed, _ = plan(*refs)
        for cp in mine + first + (passed if phase == "b" else []):
            cp.start()

    def finish(*refs):
        mine, first, first_in, passed, passed_in = plan(*refs)
        for cp in first_in:
            cp.wait_recv()
        if phase == "all":
            for cp in passed:
                cp.start()
        for cp in passed_in:
            cp.wait_recv()
        for cp in first + passed:
            cp.wait_send()
        for cp in mine:
            cp.wait()

    if phase == "b":
        outs = [jax.ShapeDtypeStruct(a.shape, a.dtype) for a in arrs]
        aliases = {t: t for t in range(n_t)}
    else:
        outs = [jax.ShapeDtypeStruct((8 * a.shape[0], a.shape[1]), a.dtype) for a in arrs]
        aliases = {}
    return _Exchange(list(arrs), outs, aliases, 7 * n_t, n_t, start, finish)


def _all_gather8(blks, name):
    return _run_exchange(_gather_exchange(blks), name)


def _swap_halves_exchange(gs):
    n_t = len(gs)

    def plan(g_refs, out_refs, send_sems, recv_sems, local_sems):
        x, y, c = _place()
        copies = []
        for t in range(n_t):
            m = gs[t].shape[1] // 2
            copies += [pltpu.make_async_remote_copy(
                src_ref=g_refs[t].at[s, pl.ds((1 - c) * m, m), :], dst_ref=out_refs[t].at[s],
                send_sem=send_sems.at[4 * t + s], recv_sem=recv_sems.at[4 * t + s], device_id=(x, y, 1 - c),
                device_id_type=MESH) for s in range(4)]
        return copies

    def start(*refs):
        for cp in plan(*refs):
            cp.start()

    def finish(*refs):
        for cp in plan(*refs):
            cp.wait()

    outs = [jax.ShapeDtypeStruct((4, g.shape[1] // 2, g.shape[2]), g.dtype) for g in gs]
    return _Exchange(list(gs), outs, {}, 4 * n_t, 1, start, finish)


def _chip_scatter_exchange(parts):
    n_t = len(parts)

    def plan(p_refs, out_refs, send_sems, recv_sems, local_sems):
        x, y, c = _place()
        mine = 2 * x + y
        chips = [(1 - x, y), (x, 1 - y), (1 - x, 1 - y)]

        def copy(t, j, src_slot, dst_slot):
            px, py = chips[j]
            return pltpu.make_async_remote_copy(
                src_ref=p_refs[t].at[src_slot], dst_ref=out_refs[t].at[dst_slot],
                send_sem=send_sems.at[3 * t + j], recv_sem=recv_sems.at[3 * t + j], device_id=(px, py, c),
                device_id_type=MESH)

        own = [pltpu.make_async_copy(p_refs[t].at[mine], out_refs[t].at[mine], local_sems.at[t])
               for t in range(n_t)]
        sends = [copy(t, j, 2 * px + py, mine) for t in range(n_t) for j, (px, py) in enumerate(chips)]
        arrivals = [copy(t, j, mine, 2 * px + py) for t in range(n_t) for j, (px, py) in enumerate(chips)]
        return own, sends, arrivals

    def start(*refs):
        own, sends, _ = plan(*refs)
        for cp in own + sends:
            cp.start()

    def finish(*refs):
        own, sends, arrivals = plan(*refs)
        for cp in arrivals:
            cp.wait_recv()
        for cp in sends:
            cp.wait_send()
        for cp in own:
            cp.wait()

    outs = [jax.ShapeDtypeStruct(p.shape, p.dtype) for p in parts]
    return _Exchange(list(parts), outs, {}, 3 * n_t, n_t, start, finish)


def _sibling_gather_exchange(bufs):
    n_t = len(bufs)

    def plan(b_refs, out_refs, send_sems, recv_sems, local_sems):
        x, y, c = _place()

        def copy(t, pc):
            m = bufs[t].shape[0] // 2
            half = pl.ds(pc * m, m)
            return pltpu.make_async_remote_copy(
                src_ref=b_refs[t].at[half, :], dst_ref=out_refs[t].at[half, :], send_sem=send_sems.at[t],
                recv_sem=recv_sems.at[t], device_id=(x, y, 1 - c), device_id_type=MESH)

        return [copy(t, c) for t in range(n_t)], [copy(t, 1 - c) for t in range(n_t)]

    def start(*refs):
        for cp in plan(*refs)[0]:
            cp.start()

    def finish(*refs):
        sends, arrivals = plan(*refs)
        for cp in arrivals:
            cp.wait_recv()
        for cp in sends:
            cp.wait_send()

    outs = [jax.ShapeDtypeStruct(b.shape, b.dtype) for b in bufs]
    return _Exchange(list(bufs), outs, {t: t for t in range(n_t)}, n_t, 1, start, finish)


def _add_halves(g, recv, c, name):
    n_slot, m2, n = g.shape
    m = m2 // 2
    tr = _tile(m, (512, 256, 192, 128, 16))

    def body(c_ref, g_ref, r_ref, o_ref):
        o_ref[...] = (g_ref[...] + r_ref[...]).astype(BF16)

    nb = m // tr
    return pl.pallas_call(
        body,
        grid_spec=pltpu.PrefetchScalarGridSpec(
            num_scalar_prefetch=1, grid=(n_slot, nb),
            in_specs=[pl.BlockSpec((1, tr, n), lambda s, i, c_ref: (s, c_ref[0] * nb + i, 0)),
                      pl.BlockSpec((1, tr, n), lambda s, i, c_ref: (s, i, 0))],
            out_specs=pl.BlockSpec((1, tr, n), lambda s, i, c_ref: (s, i, 0))),
        out_shape=jax.ShapeDtypeStruct((n_slot, m, n), BF16),
        compiler_params=_cp("parallel", "parallel"), name=name)(c, g, recv)


def _sum_slots(parts, c, name):
    n_slot, m, n = parts.shape
    tr = _tile(m, (512, 256, 192, 128, 16))
    nb = m // tr

    def body(c_ref, p_ref, o_ref):
        acc = p_ref[0].astype(F32)
        for s in range(1, n_slot):
            acc = acc + p_ref[s].astype(F32)
        o_ref[...] = acc

    return pl.pallas_call(
        body,
        grid_spec=pltpu.PrefetchScalarGridSpec(
            num_scalar_prefetch=1, grid=(nb,),
            in_specs=[pl.BlockSpec((n_slot, tr, n), lambda i, c_ref: (0, i, 0))],
            out_specs=pl.BlockSpec((tr, n), lambda i, c_ref: (c_ref[0] * nb + i, 0))),
        out_shape=jax.ShapeDtypeStruct((2 * m, n), F32),
        compiler_params=_cp("parallel"), name=name)(c, parts)


_SHARDED = ("w_in", "w_q_up", "w_kv_up", "w_sb_out", "w_mla_out", "w_mix_out", "w_up", "w_down")
_ROW_SHARDED = ("w_mix_out", "w_down")
_BY_CHIP = ("w_up", "w_down")


def _unshard(parts, name):
    n, r, cs = parts.shape
    if name in _ROW_SHARDED:
        return parts.reshape(n * r, cs)
    return parts.transpose(1, 0, 2).reshape(r, n * cs)


def _reshard(full, name, n=4):
    R, C = full.shape
    if name in _ROW_SHARDED:
        return full.reshape(n, R // n, C)
    return full.reshape(R, n, C // n).transpose(1, 0, 2)


def _pad_w_in(w):
    z = lambda k: jnp.zeros(w.shape[:-1] + (k,), w.dtype)
    return jnp.concatenate([
        w[..., 2208:3232], w[..., 3232:4256], w[..., 0:1536], w[..., 1920:2176], w[..., 1536:1920],
        z(ROPE_LANE0), w[..., 2176:2208], z(LANE - ROPE_LANE0 - ROPE_DIM)], axis=-1)


def _unpad_w_in(g):
    k0 = COL_KROPE + ROPE_LANE0
    return jnp.concatenate([
        g[..., COL_QSB:COL_KVLAT], g[..., COL_QLAT:COL_KROPE], g[..., COL_KVLAT:COL_QLAT],
        g[..., k0:k0 + ROPE_DIM], g[..., 0:COL_QSB]], axis=-1)


def _pad_w_q(w):
    r = w.shape[0]
    return jnp.pad(w.reshape(r, N_HEADS, QK_DIM), ((0, 0), (0, 0), (0, LANE - QK_DIM))).reshape(r, N_HEADS * LANE)


def _unpad_w_q(g):
    r = g.shape[0]
    return g.reshape(r, N_HEADS, LANE)[..., :QK_DIM].reshape(r, N_HEADS * QK_DIM)


def _pad_w_mla(w):
    n = w.shape[1]
    return jnp.pad(w.reshape(N_HEADS, NOPE_DIM, n), ((0, 0), (LANE - NOPE_DIM, 0), (0, 0))).reshape(
        N_HEADS * LANE, n)


def _unpad_w_mla(g):
    n = g.shape[1]
    return g.reshape(N_HEADS, LANE, n)[:, LANE - NOPE_DIM:, :].reshape(N_HEADS * NOPE_DIM, n)


def _rope_tables(positions):
    half = ROPE_DIM // 2
    inv_freq = 1.0 / (ROPE_THETA ** (jnp.arange(0, ROPE_DIM, 2, dtype=F32) / ROPE_DIM))
    ang = positions.astype(F32)[:, None] * inv_freq
    cos, sin = jnp.cos(ang), jnp.sin(ang)
    S = positions.shape[0]
    one = jnp.ones((S, ROPE_LANE0), F32)
    zero = lambda k: jnp.zeros((S, k), F32)
    tail = LANE - ROPE_LANE0 - ROPE_DIM
    c = jnp.concatenate([one, cos, cos, zero(tail)], axis=1)
    s1 = jnp.concatenate([zero(ROPE_LANE0), -sin, zero(half + tail)], axis=1)
    s2 = jnp.concatenate([zero(ROPE_LANE0 + half), sin, zero(tail)], axis=1)
    return c, s1, s2


def _layer_fwd(x, W, mod, tabs, next_blocks=None):
    sh1, sc1, gt1, sh2, sc2, gt2 = (mod[i] for i in range(N_MOD))
    h1 = _normmod_fwd(x, W["g_mix"], sc1, sh1, "mix_norm_fwd")
    p = _matmul(h1, W["w_in"], name="in_proj")
    (osbh, cb), arriving = _sb_fwd(
        p, "sb_attn_fwd", _gather_exchange(next_blocks, "a") if next_blocks else None)
    o_sb = _matmul(osbh, W["w_sb_out"], name="sb_out")
    qn = _rmsnorm_fwd(p, Q_RANK, COL_QLAT // Q_RANK, W["g_q"], "q_lat_norm_fwd")
    kvn = _rmsnorm_fwd(p, KV_RANK, COL_KVLAT // KV_RANK, W["g_kv"], "kv_lat_norm_fwd")
    qp = _matmul(qn, W["w_q_up"], name="q_up")
    kv = _matmul(kvn, W["w_kv_up"], out_dtype=BF16, name="kv_up")
    qr, kpe = _rope_fwd(qp, p, tabs, "rope_fwd")
    (omh, lse), gathered = _mla_fwd(
        qr, kv, kpe, "mla_attn_fwd", _gather_exchange(arriving, "b") if next_blocks else None)
    o_mla = _matmul(omh, W["w_mla_out"], name="mla_out")
    merged = _merge_fwd(p, o_sb, o_mla, "merge_fwd")
    y1 = _matmul(merged, W["w_mix_out"], name="mix_out")
    x1 = _res_fwd(x, y1, gt1, "mix_residual")
    h2 = _normmod_fwd(x1, W["g_mlp"], sc2, sh2, "mlp_norm_fwd")
    u = _matmul(h2, W["w_up"], b_sharded="col", name="mlp_up")
    a = _sqrelu_fwd(u, "sqrelu_fwd")
    y2 = _matmul(a, W["w_down"], b_sharded="row", name="mlp_down")
    x2 = _res_fwd(x1, y2, gt2, "mlp_residual")
    saved = dict(x=x, h1=h1, p=p, osbh=osbh, cb=cb, o_sb=o_sb, qn=qn, kvn=kvn, qr=qr, kv=kv, kpe=kpe, omh=omh,
                 lse=lse, o_mla=o_mla, merged=merged, y1=y1, x1=x1, h2=h2, u=u, a=a, y2=y2)
    return x2, saved, gathered


def _layer_bwd(dx2, W, mod, tabs, sv, core, pending=None):
    sh1, sc1, gt1, sh2, sc2, gt2 = (mod[i] for i in range(N_MOD))
    dy2, dgt2 = _res_bwd(dx2, sv["y2"], gt2, "mlp_residual_bwd")
    da = _matmul(dy2, W["w_down"], tb=True, b_sharded="row", name="mlp_down_dx")
    g_down = _matmul(sv["a"], dy2, ta=True, out_sharded=("row", W["w_down"].shape), name="mlp_down_dw")
    du = _sqrelu_bwd(da, sv["u"], "sqrelu_bwd")
    dh2 = _matmul(du, W["w_up"], tb=True, b_sharded="col", name="mlp_up_dx")
    g_up = _matmul(sv["h2"], du, ta=True, out_sharded=("col", W["w_up"].shape), name="mlp_up_dw")
    dx1, dsh2, dsc2, dg_mlp = _normmod_bwd(sv["x1"], dh2, W["g_mlp"], sc2, dx2, "mlp_norm_bwd")
    dy1, dgt1 = _res_bwd(dx1, sv["y1"], gt1, "mix_residual_bwd")
    dm = _matmul(dy1, W["w_mix_out"], tb=True, name="mix_out_dx")
    g_mix_out = _matmul(sv["merged"], dy1, ta=True, name="mix_out_dw")
    do_sb, do_mla, dgs, dgm = _merge_bwd(sv["p"], sv["o_sb"], sv["o_mla"], dm, "merge_bwd")
    do_sbh = _matmul(do_sb, W["w_sb_out"], tb=True, name="sb_out_dx")
    g_sb_out = _matmul(sv["osbh"], do_sb, ta=True, name="sb_out_dw")
    (dqs, dks, dvs), from_chips = _sb_bwd(
        sv["p"], do_sbh, sv["cb"], "sb_attn_bwd", _chip_scatter_exchange(pending) if pending else None)
    my_sum = [_sum_slots(part, core, "grads_sum_chips") for part in from_chips]
    do_mh = _matmul(do_mla, W["w_mla_out"], tb=True, name="mla_out_dx")
    g_mla_out = _matmul(sv["omh"], do_mla, ta=True, name="mla_out_dw")
    (dqr, dkv, dkpe), reduced_above = _mla_bwd(
        sv["qr"], sv["kv"], sv["kpe"], do_mh, sv["omh"], sv["lse"], "mla_attn_bwd",
        _sibling_gather_exchange(my_sum) if pending else None)
    dqp, dkr = _rope_bwd(dqr, dkpe, tabs, "rope_bwd")
    dqn = _matmul(dqp, W["w_q_up"], tb=True, name="q_up_dx")
    g_q_up = _matmul(sv["qn"], dqp, ta=True, name="q_up_dw")
    dkvn = _matmul(dkv, W["w_kv_up"], tb=True, name="kv_up_dx")
    g_kv_up = _matmul(sv["kvn"], dkv, ta=True, name="kv_up_dw")
    dqlat, dg_q = _rmsnorm_bwd(sv["p"], Q_RANK, COL_QLAT // Q_RANK, dqn, W["g_q"], "q_lat_norm_bwd")
    dkvlat, dg_kv = _rmsnorm_bwd(sv["p"], KV_RANK, COL_KVLAT // KV_RANK, dkvn, W["g_kv"], "kv_lat_norm_bwd")
    dp = jnp.concatenate([dgs, dgm, dqs, dks, dvs, dkvlat, dqlat, dkr], axis=1)
    dh1 = _matmul(dp, W["w_in"], tb=True, name="in_proj_dx")
    g_in = _matmul(sv["h1"], dp, ta=True, name="in_proj_dw")
    dx, dsh1, dsc1, dg_mix = _normmod_bwd(sv["x"], dh1, W["g_mix"], sc1, dx1, "mix_norm_bwd")
    grads = dict(w_in=g_in, w_q_up=g_q_up, w_kv_up=g_kv_up, w_sb_out=g_sb_out, w_mla_out=g_mla_out,
                 w_mix_out=g_mix_out, w_up=g_up, w_down=g_down,
                 dmod=jnp.concatenate([dsh1, dsc1, dgt1, dsh2, dsc2, dgt2], axis=0),
                 g_mix=dg_mix, g_mlp=dg_mlp, g_q=dg_q, g_kv=dg_kv)
    return dx, grads, reduced_above


def kernel(x, c, positions, w_ada, b_ada, g_mix_norm, w_in, g_q_lat, w_q_up, g_kv_lat, w_kv_up, w_sb_out, w_mla_out, w_mix_out, g_mlp_norm, w_up, w_down, g_final, loss_target, m_w_ada, m_b_ada, m_g_mix_norm, m_w_in, m_g_q_lat, m_w_q_up, m_g_kv_lat, m_w_kv_up, m_w_sb_out, m_w_mla_out, m_w_mix_out, m_g_mlp_norm, m_w_up, m_w_down, m_g_final, v_w_ada, v_b_ada, v_g_mix_norm, v_w_in, v_g_q_lat, v_w_q_up, v_g_kv_lat, v_w_kv_up, v_w_sb_out, v_w_mla_out, v_w_mix_out, v_g_mlp_norm, v_w_up, v_w_down, v_g_final):
    xi, yi, ci = _place()
    chip = 2 * xi + yi
    batch = 2 * chip + ci
    L = w_ada.shape[0]
    S = x.shape[1]
    shards = dict(w_in=w_in, w_q_up=w_q_up, w_kv_up=w_kv_up, w_sb_out=w_sb_out, w_mla_out=w_mla_out,
                  w_mix_out=w_mix_out, w_up=w_up, w_down=w_down)

    def my_halves(l):
        def half_of(w):
            half = w.shape[1] // 2
            return lax.dynamic_slice_in_dim(w[l].astype(BF16), ci * half, half, 0)

        return [half_of(shards[n]) for n in _SHARDED]

    def layer_weights(l, gathered):
        W = {}
        for n, g in zip(_SHARDED, gathered):
            by_chip = g.reshape((4,) + shards[n].shape[1:])
            W[n] = by_chip if n in _BY_CHIP else _unshard(by_chip, n)
        W["w_in"] = _pad_w_in(W["w_in"])
        W["w_q_up"] = _pad_w_q(W["w_q_up"])
        W["w_mla_out"] = _pad_w_mla(W["w_mla_out"])
        return dict(W, g_mix=g_mix_norm[l:l + 1], g_mlp=g_mlp_norm[l:l + 1], g_q=g_q_lat[l:l + 1],
                    g_kv=g_kv_lat[l:l + 1])

    gathered0 = _all_gather8(my_halves(0), "gather_weights")

    c_act = _silu(c, "silu_c")
    c_all = _all_gather8([jnp.broadcast_to(c_act, (8, D_MODEL))], "gather_c")[0].reshape(8, 8, D_MODEL)[:, 0]
    c16 = jnp.concatenate([c_all, jnp.zeros_like(c_all)], axis=0)
    ada_cols = w_ada.shape[2]
    b_shard = lax.dynamic_slice_in_dim(b_ada, chip * ada_cols, ada_cols, 1)
    mod_part = jnp.stack([_matmul(c16, w_ada[l], name="ada_mod") for l in range(L)])
    mod_part = _bias_add(mod_part, jnp.broadcast_to(b_shard[:, None, :], mod_part.shape), "ada_bias")
    mod_all = _all_gather8([mod_part.reshape(L * 16, ada_cols)], "gather_mod")[0].reshape(4, 2, L, 16, ada_cols)
    mod_mine = lax.dynamic_index_in_dim(mod_all[:, 0], batch, axis=2, keepdims=False)
    mods = mod_mine.transpose(1, 0, 2).reshape(L, N_MOD, 1, D_MODEL)

    tabs = _rope_tables(positions[0])

    xc, saved, layer_w = x[0], [], [layer_weights(0, gathered0)]
    for l in range(L):
        xc, sv, gathered = _layer_fwd(xc, layer_w[l], mods[l], tabs, my_halves(l + 1) if l + 1 < L else None)
        saved.append(sv)
        if l + 1 < L:
            layer_w.append(layer_weights(l + 1, gathered))
    dxc, dg_final, loss_part = _final_loss(xc, loss_target[0], g_final[None, :], "final_norm_loss")
    loss = lax.psum(loss_part[0, 0], ("x", "y", "c"))
    core = jnp.reshape(ci, (1,)).astype(jnp.int32)

    def sum_on_chip(g):
        by_dest = [g[n] if n in _BY_CHIP else _reshard(g[n], n) for n in _SHARDED]
        from_sibling = _run_exchange(_swap_halves_exchange(by_dest), "grads_swap_halves")
        return [_add_halves(d, r, core, "grads_add_halves") for d, r in zip(by_dest, from_sibling)]

    grads, reduced, pending = [None] * L, [None] * L, None
    for l in reversed(range(L)):
        dxc, grads[l], reduced_above = _layer_bwd(dxc, layer_w[l], mods[l], tabs, saved[l], core, pending)
        if pending:
            reduced[l + 1] = reduced_above
        grads[l]["w_in"] = _unpad_w_in(grads[l]["w_in"])
        grads[l]["w_q_up"] = _unpad_w_q(grads[l]["w_q_up"])
        grads[l]["w_mla_out"] = _unpad_w_mla(grads[l]["w_mla_out"])
        pending = sum_on_chip(grads[l])
    from_chips = _run_exchange(_chip_scatter_exchange(pending), "grads_chip_scatter")
    my_sum = [_sum_slots(part, core, "grads_sum_chips") for part in from_chips]
    reduced[0] = _run_exchange(_sibling_gather_exchange(my_sum), "grads_sibling_gather")
    grad_x = dxc
    gw = {n: jnp.stack([reduced[l][i] for l in range(L)]) for i, n in enumerate(_SHARDED)}

    def row(v):
        return jnp.pad(v, ((0, 0), (0, D_MODEL - v.shape[1])))

    per_layer_rows = N_MOD + 4
    small = jnp.concatenate(
        [jnp.concatenate([grads[l]["dmod"], row(grads[l]["g_mix"]), row(grads[l]["g_mlp"]),
                          row(grads[l]["g_q"]), row(grads[l]["g_kv"])], axis=0) for l in range(L)]
        + [dg_final], axis=0)
    n_small = -(-small.shape[0] // 8) * 8
    small = jnp.pad(small, ((0, n_small - small.shape[0]), (0, 0)))
    small_all = _all_gather8([small], "gather_vector_grads")[0].reshape(8, n_small, D_MODEL)
    small_sum = _sum_blocks(small_all, "sum_vector_grads")
    lay = small_sum[:L * per_layer_rows].reshape(L, per_layer_rows, D_MODEL)
    g_b_ada = lay[:, :N_MOD].reshape(L, N_MOD * D_MODEL)
    g_g_mix, g_g_mlp = lay[:, N_MOD], lay[:, N_MOD + 1]
    g_g_q, g_g_kv = lay[:, N_MOD + 2, :Q_RANK], lay[:, N_MOD + 3, :KV_RANK]
    g_g_final = small_sum[L * per_layer_rows]
    dmod_all = small_all[:, :L * per_layer_rows].reshape(8, L, per_layer_rows, D_MODEL)[:, :, :N_MOD]
    dmod_all = dmod_all.reshape(8, L, N_MOD * D_MODEL)
    dmod_cols = lax.dynamic_slice_in_dim(dmod_all, chip * ada_cols, ada_cols, 2)
    dmod16 = jnp.concatenate([dmod_cols, jnp.zeros_like(dmod_cols)], axis=0)
    g_w_ada = jnp.stack([_matmul(c16, dmod16[:, l], ta=True, name="ada_dw") for l in range(L)])

    weights = dict(w_ada=w_ada, b_ada=b_ada, g_mix_norm=g_mix_norm, w_in=w_in, g_q_lat=g_q_lat, w_q_up=w_q_up,
                   g_kv_lat=g_kv_lat, w_kv_up=w_kv_up, w_sb_out=w_sb_out, w_mla_out=w_mla_out,
                   w_mix_out=w_mix_out, g_mlp_norm=g_mlp_norm, w_up=w_up, w_down=w_down, g_final=g_final)
    mom = dict(w_ada=(m_w_ada, v_w_ada), b_ada=(m_b_ada, v_b_ada), g_mix_norm=(m_g_mix_norm, v_g_mix_norm),
               w_in=(m_w_in, v_w_in), g_q_lat=(m_g_q_lat, v_g_q_lat), w_q_up=(m_w_q_up, v_w_q_up),
               g_kv_lat=(m_g_kv_lat, v_g_kv_lat), w_kv_up=(m_w_kv_up, v_w_kv_up),
               w_sb_out=(m_w_sb_out, v_w_sb_out), w_mla_out=(m_w_mla_out, v_w_mla_out),
               w_mix_out=(m_w_mix_out, v_w_mix_out), g_mlp_norm=(m_g_mlp_norm, v_g_mlp_norm),
               w_up=(m_w_up, v_w_up), w_down=(m_w_down, v_w_down), g_final=(m_g_final, v_g_final))
    gr = dict(gw, w_ada=g_w_ada, b_ada=g_b_ada, g_mix_norm=g_g_mix, g_q_lat=g_g_q, g_kv_lat=g_g_kv,
              g_mlp_norm=g_g_mlp, g_final=g_g_final)
    order = list(weights)
    deltas, new_m, new_v = [], [], []
    for n in order:
        wv, gv, (mv, vv) = weights[n], gr[n], mom[n]
        if wv.ndim == 1:
            d, nm, nv = (t[0] for t in _adamw(wv[None], gv[None], mv[None], vv[None], "adamw_" + n))
        else:
            d, nm, nv = _adamw(wv, gv, mv, vv, "adamw_" + n)
        deltas.append(d)
        new_m.append(nm)
        new_v.append(nv)
    return (loss, grad_x[None], *[gr[n] for n in order], *deltas, *new_m, *new_v)
```

```python
from typing import Any, Callable, Mapping, NamedTuple, Sequence

import jax
import jax.numpy as jnp
from jax import lax
from jax.experimental import pallas as pl
from jax.experimental.pallas import tpu as pltpu

F32 = jnp.float32
BF16 = jnp.bfloat16
MESH = pl.DeviceIdType.MESH

D_MODEL = 1024
N_HEADS = 8
SB_DIM = 64
SB_WIDTH = 512
Q_RANK = 384
KV_RANK = 256
ROPE_DIM = 32
NOPE_DIM = 64
QK_DIM = 96
D_FF = 4096
N_MOD = 6
EPS = 1e-6
ROPE_THETA = 10000.0
SB_SCALE = SB_DIM ** -0.5
MLA_SCALE = QK_DIM ** -0.5
ADAM_LR, ADAM_B1, ADAM_B2, ADAM_EPS, ADAM_WD, ADAM_STEP = 0.001, 0.9, 0.999, 1e-08, 0.01, 10

LANE = 128
IN_PAD = 4352
COL_GATE_SB, COL_GATE_MLA, COL_QSB, COL_KSB, COL_VSB, COL_KVLAT, COL_QLAT, COL_KROPE = (
    0, 1024, 2048, 2560, 3072, 3584, 3840, 4224)
ROPE_LANE0 = 64
VMEM_LIMIT = 48 * 1024 * 1024
NEG_BIG = -1e30


def _cp(*sem):
    return pltpu.CompilerParams(dimension_semantics=sem, vmem_limit_bytes=VMEM_LIMIT)


def _tile(n, prefs):
    for t in prefs:
        if t <= n and n % t == 0:
            return t
    return n


def _dot(a, b, dims):
    return lax.dot_general(a, b, (dims, ((), ())), preferred_element_type=F32)


def _nn(a, b):
    return _dot(a, b, ((1,), (0,)))


def _nt(a, b):
    return _dot(a, b, ((1,), (1,)))


def _tn(a, b):
    return _dot(a, b, ((0,), (0,)))


def _sharded_dims(shape, kind):
    n, r, cs = shape
    return (n * r, cs) if kind == "row" else (r, n * cs)


def _sharded_spec(shape, kind, t_rows, t_cols, tile_of):
    _, r, cs = shape
    if kind == "row":
        assert r % t_rows == 0, (shape, t_rows)
        per = r // t_rows

        def index(i, j, k):
            tr, tc = tile_of(i, j, k)
            return tr // per, tr % per, tc
    else:
        assert cs % t_cols == 0, (shape, t_cols)
        per = cs // t_cols

        def index(i, j, k):
            tr, tc = tile_of(i, j, k)
            return tc // per, tr, tc % per
    return pl.BlockSpec((None, t_rows, t_cols), index)


def _matmul(a, b, *, ta=False, tb=False, out_dtype=F32, b_sharded=None, out_sharded=None, epilogue=None,
            extra=(), exs=None, name):
    (K, M) = a.shape if ta else a.shape[::-1]
    b_dims = _sharded_dims(b.shape, b_sharded) if b_sharded else b.shape
    (N, Kb) = b_dims if tb else b_dims[::-1]
    assert K == Kb, (a.shape, b.shape, ta, tb)
    tm = _tile(M, (512, 384, 256, 128))
    tn = _tile(N, (1024, 2176, 768, 512, 384, 256, 128))
    tk = _tile(K, (1024, 2176, 768, 512, 384, 256, 128))
    nk = K // tk
    dims = ((0 if ta else 1,), (1 if tb else 0,))

    out_dtypes = out_dtype if isinstance(out_dtype, tuple) else (out_dtype,)
    n_extra, n_o = len(extra), len(out_dtypes)

    def body(a_ref, b_ref, *rest):
        extra_refs, o_refs, acc = rest[:n_extra], rest[n_extra:n_extra + n_o], rest[n_extra + n_o:]
        prod = _dot(a_ref[...].astype(BF16), b_ref[...].astype(BF16), dims)

        def write(total):
            vals = epilogue(total, *[r[...] for r in extra_refs]) if epilogue else (total,)
            for o_ref, val, dt in zip(o_refs, vals, out_dtypes):
                o_ref[...] = val.astype(dt)

        if nk == 1:
            write(prod)
            return
        acc_ref, = acc
        k = pl.program_id(2)

        @pl.when(k == 0)
        def _():
            acc_ref[...] = prod

        @pl.when(k > 0)
        def _():
            acc_ref[...] += prod

        @pl.when(k == nk - 1)
        def _():
            write(acc_ref[...])

    a_spec = (pl.BlockSpec((tk, tm), lambda i, j, k: (k, i)) if ta
              else pl.BlockSpec((tm, tk), lambda i, j, k: (i, k)))
    if b_sharded:
        b_spec = (_sharded_spec(b.shape, b_sharded, tn, tk, lambda i, j, k: (j, k)) if tb
                  else _sharded_spec(b.shape, b_sharded, tk, tn, lambda i, j, k: (k, j)))
    else:
        b_spec = (pl.BlockSpec((tn, tk), lambda i, j, k: (j, k)) if tb
                  else pl.BlockSpec((tk, tn), lambda i, j, k: (k, j)))
    tile = pl.BlockSpec((tm, tn), lambda i, j, k: (i, j))
    if out_sharded:
        kind, shape = out_sharded
        assert _sharded_dims(shape, kind) == (M, N) and n_o == 1, (shape, kind, M, N)
        out_specs = [_sharded_spec(shape, kind, tm, tn, lambda i, j, k: (i, j))]
        out_shape = [jax.ShapeDtypeStruct(shape, out_dtypes[0])]
    else:
        out_specs = [tile] * n_o
        out_shape = [jax.ShapeDtypeStruct((M, N), dt) for dt in out_dtypes]
    grid = (M // tm, N // tn, nk)
    scratch = [pltpu.VMEM((tm, tn), F32)] if nk > 1 else []
    ins = [a, b, *extra]
    body, more, split = _carry(exs, body, len(ins), n_o, len(scratch), grid)
    own, carried = split(pl.pallas_call(
        body, grid=grid, in_specs=[a_spec, b_spec] + [tile] * n_extra + more["in_specs"],
        out_specs=out_specs + more["out_specs"], out_shape=out_shape + more["out_shape"],
        scratch_shapes=scratch + more["scratch"], input_output_aliases=more["aliases"],
        compiler_params=_cp(*(("arbitrary",) * 3 if exs else ("parallel", "parallel", "arbitrary"))),
        name=name)(*ins, *more["ins"]))
    result = own[0] if n_o == 1 else tuple(own)
    return (result, carried) if exs else result


def _rows(ts, w, col=0):
    return pl.BlockSpec((ts, w), lambda i: (i, col))


def _vec(w):
    return pl.BlockSpec((1, w), lambda i: (0, 0))


def _ts(S):
    return _tile(S, (256, 128))


def _rms(x):
    return lax.rsqrt(jnp.mean(x * x, axis=-1, keepdims=True) + EPS)


def _colsum(x):
    return jnp.sum(x, axis=0, keepdims=True)


def _normmod_fwd(x, g, sc, sh, name):
    S, W = x.shape
    ts = _ts(S)

    def body(x_ref, g_ref, sc_ref, sh_ref, h_ref):
        xv = x_ref[...]
        h_ref[...] = ((xv * _rms(xv)) * g_ref[...] * (1.0 + sc_ref[...]) + sh_ref[...]).astype(BF16)

    return pl.pallas_call(
        body, grid=(S // ts,), in_specs=[_rows(ts, W), _vec(W), _vec(W), _vec(W)],
        out_specs=_rows(ts, W), out_shape=jax.ShapeDtypeStruct((S, W), BF16),
        compiler_params=_cp("parallel"), name=name)(x, g, sc, sh)


def _normmod_bwd(x, dh, g, sc, dres, name):
    S, W = x.shape
    ts = _ts(S)

    def body(x_ref, dh_ref, g_ref, sc_ref, dres_ref, dx_ref, dsh_ref, dsc_ref, dg_ref):
        @pl.when(pl.program_id(0) == 0)
        def _():
            dsh_ref[...] = jnp.zeros_like(dsh_ref)
            dsc_ref[...] = jnp.zeros_like(dsc_ref)
            dg_ref[...] = jnp.zeros_like(dg_ref)

        xv, dh_v, gv = x_ref[...], dh_ref[...], g_ref[...]
        r = _rms(xv)
        y = xv * r
        dn = dh_v * (1.0 + sc_ref[...])
        dy = dn * gv
        dx_ref[...] = dres_ref[...] + r * (dy - y * jnp.mean(dy * y, axis=-1, keepdims=True))
        dsh_ref[...] += _colsum(dh_v)
        dsc_ref[...] += _colsum(dh_v * y * gv)
        dg_ref[...] += _colsum(dn * y)

    vec_out = jax.ShapeDtypeStruct((1, W), F32)
    return pl.pallas_call(
        body, grid=(S // ts,),
        in_specs=[_rows(ts, W), _rows(ts, W), _vec(W), _vec(W), _rows(ts, W)],
        out_specs=[_rows(ts, W), _vec(W), _vec(W), _vec(W)],
        out_shape=[jax.ShapeDtypeStruct((S, W), F32), vec_out, vec_out, vec_out],
        compiler_params=_cp("arbitrary"), name=name)(x, dh, g, sc, dres)


def _rmsnorm_fwd(p, width, col, g, name):
    S = p.shape[0]
    ts = _ts(S)

    def body(x_ref, g_ref, y_ref):
        xv = x_ref[...]
        y_ref[...] = ((xv * _rms(xv)) * g_ref[...]).astype(BF16)

    return pl.pallas_call(
        body, grid=(S // ts,), in_specs=[_rows(ts, width, col), _vec(width)],
        out_specs=_rows(ts, width), out_shape=jax.ShapeDtypeStruct((S, width), BF16),
        compiler_params=_cp("parallel"), name=name)(p, g)


def _rmsnorm_bwd(p, width, col, dn, g, name):
    S = p.shape[0]
    ts = _ts(S)

    def body(x_ref, dn_ref, g_ref, dx_ref, dg_ref):
        @pl.when(pl.program_id(0) == 0)
        def _():
            dg_ref[...] = jnp.zeros_like(dg_ref)

        xv, dn_v = x_ref[...], dn_ref[...]
        r = _rms(xv)
        y = xv * r
        dy = dn_v * g_ref[...]
        dx_ref[...] = r * (dy - y * jnp.mean(dy * y, axis=-1, keepdims=True))
        dg_ref[...] += _colsum(dn_v * y)

    return pl.pallas_call(
        body, grid=(S // ts,), in_specs=[_rows(ts, width, col), _rows(ts, width), _vec(width)],
        out_specs=[_rows(ts, width), _vec(width)],
        out_shape=[jax.ShapeDtypeStruct((S, width), F32), jax.ShapeDtypeStruct((1, width), F32)],
        compiler_params=_cp("arbitrary"), name=name)(p, dn, g)


def _rope_rot(t, c, s1, s2):
    return t * c + pltpu.roll(t, LANE - 16, 1) * s1 + pltpu.roll(t, 16, 1) * s2


def _rope_rot_t(d, c, s1, s2):
    return d * c + pltpu.roll(d * s1, 16, 1) + pltpu.roll(d * s2, LANE - 16, 1)


def _rope_fwd(qp, p, tabs, name):
    S = qp.shape[0]
    ts = _ts(S)
    W = N_HEADS * LANE

    def body(q_ref, kr_ref, c_ref, s1_ref, s2_ref, qr_ref, kpe_ref):
        c, s1, s2 = c_ref[...], s1_ref[...], s2_ref[...]
        for h in range(N_HEADS):
            sl = slice(h * LANE, (h + 1) * LANE)
            qr_ref[:, sl] = _rope_rot(q_ref[:, sl], c, s1, s2).astype(BF16)
        kpe_ref[...] = _rope_rot(kr_ref[...], c, s1, s2).astype(BF16)

    tab = _rows(ts, LANE)
    return pl.pallas_call(
        body, grid=(S // ts,), in_specs=[_rows(ts, W), _rows(ts, LANE, COL_KROPE // LANE), tab, tab, tab],
        out_specs=[_rows(ts, W), _rows(ts, LANE)],
        out_shape=[jax.ShapeDtypeStruct((S, W), BF16), jax.ShapeDtypeStruct((S, LANE), BF16)],
        compiler_params=_cp("parallel"), name=name)(qp, p, *tabs)


def _rope_bwd(dqr, dkpe_heads, tabs, name):
    S = dqr.shape[0]
    ts = _ts(S)
    W = N_HEADS * LANE

    def body(dq_ref, dk_ref, c_ref, s1_ref, s2_ref, dqp_ref, dkr_ref):
        c, s1, s2 = c_ref[...], s1_ref[...], s2_ref[...]
        dk = dk_ref[:, 0:LANE]
        for h in range(N_HEADS):
            sl = slice(h * LANE, (h + 1) * LANE)
            dqp_ref[:, sl] = _rope_rot_t(dq_ref[:, sl], c, s1, s2).astype(BF16)
            if h:
                dk = dk + dk_ref[:, sl]
        dkr_ref[...] = _rope_rot_t(dk, c, s1, s2)

    tab = _rows(ts, LANE)
    return pl.pallas_call(
        body, grid=(S // ts,), in_specs=[_rows(ts, W), _rows(ts, W), tab, tab, tab],
        out_specs=[_rows(ts, W), _rows(ts, LANE)],
        out_shape=[jax.ShapeDtypeStruct((S, W), BF16), jax.ShapeDtypeStruct((S, LANE), F32)],
        compiler_params=_cp("parallel"), name=name)(dqr, dkpe_heads, *tabs)


def _merge_fwd(p, o_sb, o_mla, name):
    S, W = o_sb.shape
    ts = _ts(S)

    def body(gs_ref, gm_ref, a_ref, b_ref, m_ref):
        m_ref[...] = (jax.nn.sigmoid(gs_ref[...]) * a_ref[...]
                      + jax.nn.sigmoid(gm_ref[...]) * b_ref[...]).astype(BF16)

    return pl.pallas_call(
        body, grid=(S // ts,),
        in_specs=[_rows(ts, W, COL_GATE_SB // W), _rows(ts, W, COL_GATE_MLA // W), _rows(ts, W), _rows(ts, W)],
        out_specs=_rows(ts, W), out_shape=jax.ShapeDtypeStruct((S, W), BF16),
        compiler_params=_cp("parallel"), name=name)(p, p, o_sb, o_mla)


def _merge_bwd(p, o_sb, o_mla, dm, name):
    S, W = o_sb.shape
    ts = _ts(S)

    def body(gs_ref, gm_ref, a_ref, b_ref, dm_ref, da_ref, db_ref, dgs_ref, dgm_ref):
        dmv = dm_ref[...]
        sa, sb = jax.nn.sigmoid(gs_ref[...]), jax.nn.sigmoid(gm_ref[...])
        da_ref[...] = (dmv * sa).astype(BF16)
        db_ref[...] = (dmv * sb).astype(BF16)
        dgs_ref[...] = dmv * a_ref[...] * sa * (1.0 - sa)
        dgm_ref[...] = dmv * b_ref[...] * sb * (1.0 - sb)

    row = _rows(ts, W)
    return pl.pallas_call(
        body, grid=(S // ts,),
        in_specs=[_rows(ts, W, COL_GATE_SB // W), _rows(ts, W, COL_GATE_MLA // W), row, row, row],
        out_specs=[row, row, row, row],
        out_shape=[jax.ShapeDtypeStruct((S, W), BF16), jax.ShapeDtypeStruct((S, W), BF16),
                   jax.ShapeDtypeStruct((S, W), F32), jax.ShapeDtypeStruct((S, W), F32)],
        compiler_params=_cp("parallel"), name=name)(p, p, o_sb, o_mla, dm)


def _res_fwd(x, y, gate, name):
    S, W = x.shape
    ts = _ts(S)

    def body(x_ref, y_ref, g_ref, o_ref):
        o_ref[...] = x_ref[...] + g_ref[...] * y_ref[...]

    return pl.pallas_call(
        body, grid=(S // ts,), in_specs=[_rows(ts, W), _rows(ts, W), _vec(W)], out_specs=_rows(ts, W),
        out_shape=jax.ShapeDtypeStruct((S, W), F32), compiler_params=_cp("parallel"), name=name)(x, y, gate)


def _res_bwd(dx, y, gate, name):
    S, W = dx.shape
    ts = _ts(S)

    def body(dx_ref, y_ref, g_ref, dy_ref, dg_ref):
        @pl.when(pl.program_id(0) == 0)
        def _():
            dg_ref[...] = jnp.zeros_like(dg_ref)

        dxv = dx_ref[...]
        dy_ref[...] = (g_ref[...] * dxv).astype(BF16)
        dg_ref[...] += _colsum(dxv * y_ref[...])

    return pl.pallas_call(
        body, grid=(S // ts,), in_specs=[_rows(ts, W), _rows(ts, W), _vec(W)],
        out_specs=[_rows(ts, W), _vec(W)],
        out_shape=[jax.ShapeDtypeStruct((S, W), BF16), jax.ShapeDtypeStruct((1, W), F32)],
        compiler_params=_cp("arbitrary"), name=name)(dx, y, gate)


def _final_loss(x, target, g, name):
    S, W = x.shape
    ts = _ts(S)

    def body(x_ref, t_ref, g_ref, dx_ref, dg_ref, loss_ref):
        @pl.when(pl.program_id(0) == 0)
        def _():
            dg_ref[...] = jnp.zeros_like(dg_ref)
            loss_ref[...] = jnp.zeros_like(loss_ref)

        xv, gv = x_ref[...], g_ref[...]
        r = _rms(xv)
        y = xv * r
        err = y * gv - t_ref[...]
        loss_ref[...] += jnp.full((1, LANE), 0.5 * jnp.sum(jnp.mean(err * err, axis=-1)), F32)
        dout = err * (1.0 / W)
        dy = dout * gv
        dx_ref[...] = r * (dy - y * jnp.mean(dy * y, axis=-1, keepdims=True))
        dg_ref[...] += _colsum(dout * y)

    return pl.pallas_call(
        body, grid=(S // ts,), in_specs=[_rows(ts, W), _rows(ts, W), _vec(W)],
        out_specs=[_rows(ts, W), _vec(W), _vec(LANE)],
        out_shape=[jax.ShapeDtypeStruct((S, W), F32), jax.ShapeDtypeStruct((1, W), F32),
                   jax.ShapeDtypeStruct((1, LANE), F32)],
        compiler_params=_cp("arbitrary"), name=name)(x, target, g)


def _silu(c, name):
    def body(c_ref, o_ref):
        cv = c_ref[...]
        o_ref[...] = cv * jax.nn.sigmoid(cv)

    return pl.pallas_call(body, out_shape=jax.ShapeDtypeStruct(c.shape, F32), name=name)(c)


def _bias_add(a, b, name):
    def body(a_ref, b_ref, o_ref):
        o_ref[...] = a_ref[...] + b_ref[...]

    return pl.pallas_call(body, out_shape=jax.ShapeDtypeStruct(a.shape, F32), name=name)(a, b)


def _sum_blocks(xs, name):
    n = xs.shape[0]

    def body(x_ref, o_ref):
        acc = x_ref[0]
        for d in range(1, n):
            acc = acc + x_ref[d]
        o_ref[...] = acc

    return pl.pallas_call(body, out_shape=jax.ShapeDtypeStruct(xs.shape[1:], F32), name=name)(xs)


def _adamw(w, g, m, v, name):
    shape = w.shape
    cols = shape[-1]
    w2, g2, m2, v2 = (t.reshape(-1, cols) for t in (w, g, m, v))
    rows = w2.shape[0]
    tr = _tile(rows, (128,))
    c1 = 1.0 - ADAM_B1 ** ADAM_STEP
    c2 = 1.0 - ADAM_B2 ** ADAM_STEP

    def body(w_ref, g_ref, m_ref, v_ref, d_ref, nm_ref, nv_ref):
        gv = g_ref[...]
        nm = ADAM_B1 * m_ref[...] + (1.0 - ADAM_B1) * gv
        nv = ADAM_B2 * v_ref[...] + (1.0 - ADAM_B2) * (gv * gv)
        d_ref[...] = -ADAM_LR * ((nm / c1) / (jnp.sqrt(nv / c2) + ADAM_EPS) + ADAM_WD * w_ref[...])
        nm_ref[...] = nm
        nv_ref[...] = nv

    spec = pl.BlockSpec((tr, cols), lambda i: (i, 0))
    out = jax.ShapeDtypeStruct((rows, cols), F32)
    d, nm, nv = pl.pallas_call(
        body, grid=(rows // tr,), in_specs=[spec] * 4, out_specs=[spec] * 3, out_shape=[out] * 3,
        compiler_params=_cp("parallel"), name=name)(w2, g2, m2, v2)
    return d.reshape(shape), nm.reshape(shape), nv.reshape(shape)


def _split_dot(x, tri):
    hi = x.astype(BF16)
    lo = (x - hi.astype(F32)).astype(BF16)
    return _nn(hi, tri) + _nn(lo, tri)


def _sb_logs(z):
    soft = jnp.log(1.0 + jnp.exp(-jnp.abs(z)))
    return jnp.minimum(z, 0.0) - soft, -jnp.maximum(z, 0.0) - soft


def _attn_call(body, grid, ins, in_specs, out_specs, out_shape, scratch, exs, name):
    body, extra, split = _carry(exs, body, len(ins), len(out_shape), len(scratch), grid)
    return split(pl.pallas_call(
        body, grid=grid, in_specs=in_specs + extra["in_specs"], out_specs=out_specs + extra["out_specs"],
        out_shape=out_shape + extra["out_shape"], scratch_shapes=scratch + extra["scratch"],
        input_output_aliases=extra["aliases"], compiler_params=_cp("arbitrary", "arbitrary"),
        name=name)(*ins, *extra["ins"]))


def _sb_fwd(p, name, exs=None):
    S = p.shape[0]
    t = _ts(S)
    qb, kb, vb = COL_QSB // LANE, COL_KSB // LANE, COL_VSB // LANE

    def body(q_ref, k_ref, v_ref, o_ref, cb_ref, acc_ref):
        i = pl.program_id(1)
        lane = lax.broadcasted_iota(jnp.int32, (t, LANE), 1)
        rows = lax.broadcasted_iota(jnp.int32, (t, t), 0)
        cols = lax.broadcasted_iota(jnp.int32, (t, t), 1)
        after = jnp.where(rows > cols, 1.0, 0.0).astype(BF16)
        diag = cols < rows
        q = q_ref[...] * SB_SCALE
        acc_ref[...] = jnp.zeros_like(acc_ref)
        cb_ref[...] = jnp.zeros_like(cb_ref)
        hms = [(lane >= SB_DIM * h) & (lane < SB_DIM * (h + 1)) for h in range(2)]
        qhs = [jnp.where(hm, q, 0.0).astype(BF16) for hm in hms]

        def step(j, cs, masked):
            rows_j = pl.ds(pl.multiple_of(j * t, t), t)
            kj = k_ref[rows_j, :].astype(BF16)
            vf = v_ref[rows_j, :]
            out, pv = [], None
            for h in range(2):
                ls, lf = _sb_logs(_nt(qhs[h], kj))
                if masked:
                    lf = jnp.where(diag, lf, 0.0)
                a = jnp.exp(ls + _split_dot(lf, after) + cs[h])
                if masked:
                    a = jnp.where(diag, a, 0.0)
                term = _nn(a.astype(BF16), jnp.where(hms[h], vf, 0.0).astype(BF16))
                pv = term if pv is None else pv + term
                cb_ref[h] = jnp.where(lane == j, cs[h], cb_ref[h])
                out.append(cs[h] + jnp.sum(lf, axis=1, keepdims=True))
            acc_ref[...] += pv
            return tuple(out)

        zero = jnp.zeros((t, 1), F32)
        cs = step(i, (zero, zero), True)
        lax.fori_loop(0, i, lambda it, cs: step(i - 1 - it, cs, False), cs)
        o_ref[...] = acc_ref[...].astype(BF16)

    return _attn_call(
        body, (SB_WIDTH // LANE, S // t), [p, p, p],
        [pl.BlockSpec((t, LANE), lambda hp, i: (i, qb + hp)),
         pl.BlockSpec((S, LANE), lambda hp, i: (0, kb + hp)),
         pl.BlockSpec((S, LANE), lambda hp, i: (0, vb + hp))],
        [pl.BlockSpec((t, LANE), lambda hp, i: (i, hp)),
         pl.BlockSpec((2, t, LANE), lambda hp, i: (hp, i, 0))],
        [jax.ShapeDtypeStruct((S, SB_WIDTH), BF16), jax.ShapeDtypeStruct((N_HEADS, S, LANE), F32)],
        [pltpu.VMEM((t, LANE), F32)], exs, name)


def _sb_bwd(p, do, cb, name, exs=None):
    S = p.shape[0]
    t = _ts(S)
    qb, kb, vb = COL_QSB // LANE, COL_KSB // LANE, COL_VSB // LANE

    def body(q_ref, k_ref, v_ref, do_ref, cb_ref, dq_ref, dk_ref, dv_ref, acc_ref):
        i = pl.program_id(1)

        @pl.when(i == 0)
        def _():
            dk_ref[...] = jnp.zeros_like(dk_ref)
            dv_ref[...] = jnp.zeros_like(dv_ref)

        lane = lax.broadcasted_iota(jnp.int32, (t, LANE), 1)
        rows = lax.broadcasted_iota(jnp.int32, (t, t), 0)
        cols = lax.broadcasted_iota(jnp.int32, (t, t), 1)
        after = jnp.where(rows > cols, 1.0, 0.0).astype(BF16)
        before = jnp.where(rows < cols, 1.0, 0.0).astype(BF16)
        diag = cols < rows
        q = q_ref[...] * SB_SCALE
        dov = do_ref[...]
        acc_ref[...] = jnp.zeros_like(acc_ref)
        hms = [(lane >= SB_DIM * h) & (lane < SB_DIM * (h + 1)) for h in range(2)]
        qhs = [jnp.where(hm, q, 0.0).astype(BF16) for hm in hms]
        dohs = [jnp.where(hm, dov, 0.0).astype(BF16) for hm in hms]

        def step(j, fs, masked):
            rows_j = pl.ds(pl.multiple_of(j * t, t), t)
            kf = k_ref[rows_j, :]
            kj = kf.astype(BF16)
            vj = v_ref[rows_j, :].astype(BF16)
            out, dq_t, dk_t, dv_t = [], None, None, None
            for h in range(2):
                ls, lf = _sb_logs(_nt(qhs[h], kj))
                if masked:
                    lf = jnp.where(diag, lf, 0.0)
                c = jnp.sum(jnp.where(lane == j, cb_ref[h], 0.0), axis=1, keepdims=True)
                a = jnp.exp(ls + _split_dot(lf, after) + c)
                if masked:
                    a = jnp.where(diag, a, 0.0)
                dl = _nt(dohs[h], vj) * a
                sg = jnp.exp(ls)
                dz = dl * (1.0 - sg) - sg * (_split_dot(dl, before) + fs[h])
                if masked:
                    dz = jnp.where(diag, dz, 0.0)
                dzb = dz.astype(BF16)
                terms = (_nn(dzb, jnp.where(hms[h], kf, 0.0).astype(BF16)), _tn(dzb, qhs[h]),
                         _tn(a.astype(BF16), dohs[h]))
                dq_t, dk_t, dv_t = terms if dq_t is None else (dq_t + terms[0], dk_t + terms[1], dv_t + terms[2])
                out.append(fs[h] + jnp.sum(dl, axis=1, keepdims=True))
            acc_ref[...] += dq_t
            dk_ref[rows_j, :] += dk_t
            dv_ref[rows_j, :] += dv_t
            return tuple(out)

        zero = jnp.zeros((t, 1), F32)
        fs = lax.fori_loop(0, i, lambda j, fs: step(j, fs, False), (zero, zero))
        step(i, fs, True)
        dq_ref[...] = acc_ref[...] * SB_SCALE

    col = lambda hp, i: (0, hp)
    out = jax.ShapeDtypeStruct((S, SB_WIDTH), F32)
    return _attn_call(
        body, (SB_WIDTH // LANE, S // t), [p, p, p, do, cb],
        [pl.BlockSpec((t, LANE), lambda hp, i: (i, qb + hp)),
         pl.BlockSpec((S, LANE), lambda hp, i: (0, kb + hp)),
         pl.BlockSpec((S, LANE), lambda hp, i: (0, vb + hp)),
         pl.BlockSpec((t, LANE), lambda hp, i: (i, hp)),
         pl.BlockSpec((2, t, LANE), lambda hp, i: (hp, i, 0))],
        [pl.BlockSpec((t, LANE), lambda hp, i: (i, hp)), pl.BlockSpec((S, LANE), col), pl.BlockSpec((S, LANE), col)],
        [out, out, out], [pltpu.VMEM((t, LANE), F32)], exs, name)


def _mla_fwd(qr, kv, kpe, name, exs=None):
    S = qr.shape[0]
    t = _ts(S)

    def body(q_ref, kv_ref, kpe_ref, o_ref, lse_ref, acc_ref, m_ref):
        i = pl.program_id(1)
        low = lax.broadcasted_iota(jnp.int32, (t, LANE), 1) < NOPE_DIM
        rows = lax.broadcasted_iota(jnp.int32, (t, t), 0)
        cols = lax.broadcasted_iota(jnp.int32, (t, t), 1)
        causal = cols <= rows
        one = jnp.ones((t, LANE), BF16)
        heads = [slice(h * LANE, (h + 1) * LANE) for h in range(2)]
        qs = [q_ref[:, sl] for sl in heads]
        acc_ref[...] = jnp.zeros_like(acc_ref)
        m_ref[...] = jnp.full_like(m_ref, NEG_BIG)

        def step(j, masked):
            rows_j = pl.ds(pl.multiple_of(j * t, t), t)
            kpe_j = kpe_ref[rows_j, :]
            for h, sl in enumerate(heads):
                kvj = kv_ref[rows_j, sl]
                z = _nt(qs[h], jnp.where(low, kvj, kpe_j)) * MLA_SCALE
                if masked:
                    z = jnp.where(causal, z, NEG_BIG)
                m_old = m_ref[h]
                m_new = jnp.maximum(m_old, jnp.max(z, axis=1, keepdims=True))
                pr = jnp.exp(z - m_new)
                acc_ref[:, sl] = jnp.exp(m_old - m_new) * acc_ref[:, sl] + _nn(
                    pr.astype(BF16), jnp.where(low, one, kvj))
                m_ref[h] = m_new

        def loop(j, carry):
            step(j, False)
            return carry

        lax.fori_loop(0, i, loop, 0)
        step(i, True)
        for h, sl in enumerate(heads):
            acc = acc_ref[:, sl]
            den = acc[:, 0:1]
            o_ref[:, sl] = jnp.where(low, 0.0, acc / den).astype(BF16)
            lse_ref[h] = jnp.broadcast_to(m_ref[h] + jnp.log(den), (t, LANE))

    pair = 2 * LANE
    return _attn_call(
        body, (N_HEADS // 2, S // t), [qr, kv, kpe],
        [pl.BlockSpec((t, pair), lambda hp, i: (i, hp)),
         pl.BlockSpec((S, pair), lambda hp, i: (0, hp)),
         pl.BlockSpec((S, LANE), lambda hp, i: (0, 0))],
        [pl.BlockSpec((t, pair), lambda hp, i: (i, hp)), pl.BlockSpec((2, t, LANE), lambda hp, i: (hp, i, 0))],
        [jax.ShapeDtypeStruct((S, N_HEADS * LANE), BF16), jax.ShapeDtypeStruct((N_HEADS, S, LANE), F32)],
        [pltpu.VMEM((t, pair), F32), pltpu.VMEM((2, t, 1), F32)], exs, name)


def _mla_bwd(qr, kv, kpe, do, o, lse, name, exs=None):
    S = qr.shape[0]
    t = _ts(S)

    def body(q_ref, kv_ref, kpe_ref, do_ref, o_ref, lse_ref, dq_ref, dkv_ref, dkpe_ref, acc_ref):
        i = pl.program_id(1)

        @pl.when(i == 0)
        def _():
            dkv_ref[...] = jnp.zeros_like(dkv_ref)
            dkpe_ref[...] = jnp.zeros_like(dkpe_ref)

        low = lax.broadcasted_iota(jnp.int32, (t, LANE), 1) < NOPE_DIM
        rows = lax.broadcasted_iota(jnp.int32, (t, t), 0)
        cols = lax.broadcasted_iota(jnp.int32, (t, t), 1)
        causal = cols <= rows
        heads = [slice(h * LANE, (h + 1) * LANE) for h in range(2)]
        qs = [q_ref[:, sl] for sl in heads]
        dovs = [do_ref[:, sl] for sl in heads]
        dobs = [d.astype(BF16) for d in dovs]
        deltas = [jnp.sum(dovs[h] * o_ref[:, sl].astype(F32), axis=1, keepdims=True) for h, sl in enumerate(heads)]
        lses = [lse_ref[h][:, 0:1] for h in range(2)]
        acc_ref[...] = jnp.zeros_like(acc_ref)

        def step(j, masked):
            rows_j = pl.ds(pl.multiple_of(j * t, t), t)
            kpe_j = kpe_ref[rows_j, :]
            for h, sl in enumerate(heads):
                kvj = kv_ref[rows_j, sl]
                kcat = jnp.where(low, kvj, kpe_j)
                z = _nt(qs[h], kcat) * MLA_SCALE
                if masked:
                    z = jnp.where(causal, z, NEG_BIG)
                pr = jnp.exp(z - lses[h])
                ds = (pr * (_nt(dobs[h], kvj) - deltas[h])).astype(BF16)
                acc_ref[:, sl] += _nn(ds, kcat)
                dkc = _tn(ds, qs[h]) * MLA_SCALE
                dkv_ref[rows_j, sl] += jnp.where(low, dkc, _tn(pr.astype(BF16), dobs[h]))
                dkpe_ref[rows_j, sl] += jnp.where(low, 0.0, dkc)

        def loop(j, carry):
            step(j, False)
            return carry

        lax.fori_loop(0, i, loop, 0)
        step(i, True)
        dq_ref[...] = acc_ref[...] * MLA_SCALE

    pair = 2 * LANE
    blk = pl.BlockSpec((t, pair), lambda hp, i: (i, hp))
    col = pl.BlockSpec((S, pair), lambda hp, i: (0, hp))
    out = jax.ShapeDtypeStruct((S, N_HEADS * LANE), F32)
    return _attn_call(
        body, (N_HEADS // 2, S // t), [qr, kv, kpe, do, o, lse],
        [blk, col, pl.BlockSpec((S, LANE), lambda hp, i: (0, 0)), blk, blk,
         pl.BlockSpec((2, t, LANE), lambda hp, i: (hp, i, 0))],
        [blk, col, col], [out, out, out], [pltpu.VMEM((t, pair), F32)], exs, name)


_ANY = pl.BlockSpec(memory_space=pl.ANY)


def _place():
    return lax.axis_index("x"), lax.axis_index("y"), lax.axis_index("c")


class _Exchange(NamedTuple):
    ins: Sequence[Any]
    outs: Sequence[Any]
    aliases: Mapping[int, int]
    n_remote: int
    n_local: int
    start: Callable
    finish: Callable


def _exchange_scratch(ex):
    return [pltpu.SemaphoreType.DMA((ex.n_remote,)), pltpu.SemaphoreType.DMA((ex.n_remote,)),
            pltpu.SemaphoreType.DMA((ex.n_local,))]


def _run_exchange(ex, name):
    n_in, n_out = len(ex.ins), len(ex.outs)

    def body(*refs):
        args = (refs[:n_in], refs[n_in:n_in + n_out], *refs[n_in + n_out:])
        ex.start(*args)
        ex.finish(*args)

    return pl.pallas_call(
        body, out_shape=list(ex.outs), in_specs=[_ANY] * n_in, out_specs=[_ANY] * n_out,
        scratch_shapes=_exchange_scratch(ex), input_output_aliases=dict(ex.aliases), name=name)(*ex.ins)


def _carry(exs, body, n_in, n_out, n_scratch, grid):
    exs = [ex for ex in (exs or []) if ex is not None]
    e_ins, e_outs = [len(ex.ins) for ex in exs], [len(ex.outs) for ex in exs]

    def take(refs, counts):
        groups = []
        for n in counts:
            groups.append(refs[:n])
            refs = refs[n:]
        return groups, refs

    def carried(*refs):
        own_in, refs = refs[:n_in], refs[n_in:]
        ex_in, refs = take(refs, e_ins)
        own_out, refs = refs[:n_out], refs[n_out:]
        ex_out, refs = take(refs, e_outs)
        own_scratch, refs = refs[:n_scratch], refs[n_scratch:]
        sems, _ = take(refs, [3] * len(exs))
        at = [pl.program_id(d) for d in range(len(grid))]
        first, last = at[0] == 0, at[0] == grid[0] - 1
        for d in range(1, len(grid)):
            first, last = first & (at[d] == 0), last & (at[d] == grid[d] - 1)

        @pl.when(first)
        def _():
            for e, ex in enumerate(exs):
                ex.start(ex_in[e], ex_out[e], *sems[e])

        body(*own_in, *own_out, *own_scratch)

        @pl.when(last)
        def _():
            for e, ex in enumerate(exs):
                ex.finish(ex_in[e], ex_out[e], *sems[e])

    aliases, i0, o0 = {}, n_in, n_out
    for ex in exs:
        aliases.update({i0 + i: o0 + o for i, o in ex.aliases.items()})
        i0, o0 = i0 + len(ex.ins), o0 + len(ex.outs)

    def split(res):
        groups, _ = take(list(res[n_out:]), e_outs)
        return list(res[:n_out]), groups

    extra = dict(
        ins=[a for ex in exs for a in ex.ins], in_specs=[_ANY] * sum(e_ins), out_specs=[_ANY] * sum(e_outs),
        out_shape=[o for ex in exs for o in ex.outs], scratch=[s for ex in exs for s in _exchange_scratch(ex)],
        aliases=aliases)
    return (carried if exs else body), extra, split


def _gather_exchange(arrs, phase="all"):
    n_t = len(arrs)
    ms = [a.shape[0] // (8 if phase == "b" else 1) for a in arrs]

    def plan(in_refs, out_refs, send_sems, recv_sems, local_sems):
        x, y, c = _place()
        me, sibling = (x, y, c), (x, y, 1 - c)
        chips = [(1 - x, y), (x, 1 - y), (1 - x, 1 - y)]

        def rows(ref, t, px, py, pc):
            return ref.at[pl.ds((4 * px + 2 * py + pc) * ms[t], ms[t]), :]

        def copy(t, k, block, to, src):
            return pltpu.make_async_remote_copy(
                src_ref=src, dst_ref=rows(out_refs[t], t, *block), send_sem=send_sems.at[7 * t + k],
                recv_sem=recv_sems.at[7 * t + k], device_id=to, device_id_type=MESH)

        mine, first, first_in, passed, passed_in = [], [], [], [], []
        for t in range(n_t):
            if phase != "b":
                mine.append(pltpu.make_async_copy(in_refs[t], rows(out_refs[t], t, *me), local_sems.at[t]))
                first.append(copy(t, 0, me, sibling, in_refs[t]))
                first_in.append(copy(t, 0, sibling, me, in_refs[t]))
                for j, chip in enumerate(chips):
                    first.append(copy(t, 1 + j, me, (*chip, c), in_refs[t]))
                    first_in.append(copy(t, 1 + j, (*chip, c), me, in_refs[t]))
            if phase != "a":
                held = in_refs[t] if phase == "b" else out_refs[t]
                for j, chip in enumerate(chips):
                    passed.append(copy(t, 4 + j, (*chip, c), sibling, rows(held, t, *chip, c)))
                    passed_in.append(copy(t, 4 + j, (*chip, 1 - c), me, rows(held, t, *chip, c)))
        return mine, first, first_in, passed, passed_in

    def start(*refs):
        mine, first, _, passed, _ = plan(*refs)
        for cp in mine + first + (passed if phase == "b" else []):
            cp.start()

    def finish(*refs):
        mine, first, first_in, passed, passed_in = plan(*refs)
        for cp in first_in:
            cp.wait_recv()
        if phase == "all":
            for cp in passed:
                cp.start()
        for cp in passed_in:
            cp.wait_recv()
        for cp in first + passed:
            cp.wait_send()
        for cp in mine:
            cp.wait()

    if phase == "b":
        outs = [jax.ShapeDtypeStruct(a.shape, a.dtype) for a in arrs]
        aliases = {t: t for t in range(n_t)}
    else:
        outs = [jax.ShapeDtypeStruct((8 * a.shape[0], a.shape[1]), a.dtype) for a in arrs]
        aliases = {}
    return _Exchange(list(arrs), outs, aliases, 7 * n_t, n_t, start, finish)


def _all_gather8(blks, name):
    return _run_exchange(_gather_exchange(blks), name)


def _swap_halves_exchange(gs):
    n_t = len(gs)

    def plan(g_refs, out_refs, send_sems, recv_sems, local_sems):
        x, y, c = _place()
        copies = []
        for t in range(n_t):
            m = gs[t].shape[1] // 2
            copies += [pltpu.make_async_remote_copy(
                src_ref=g_refs[t].at[s, pl.ds((1 - c) * m, m), :], dst_ref=out_refs[t].at[s],
                send_sem=send_sems.at[4 * t + s], recv_sem=recv_sems.at[4 * t + s], device_id=(x, y, 1 - c),
                device_id_type=MESH) for s in range(4)]
        return copies

    def start(*refs):
        for cp in plan(*refs):
            cp.start()

    def finish(*refs):
        for cp in plan(*refs):
            cp.wait()

    outs = [jax.ShapeDtypeStruct((4, g.shape[1] // 2, g.shape[2]), g.dtype) for g in gs]
    return _Exchange(list(gs), outs, {}, 4 * n_t, 1, start, finish)


def _chip_scatter_exchange(parts):
    n_t = len(parts)

    def plan(p_refs, out_refs, send_sems, recv_sems, local_sems):
        x, y, c = _place()
        mine = 2 * x + y
        chips = [(1 - x, y), (x, 1 - y), (1 - x, 1 - y)]

        def copy(t, j, src_slot, dst_slot):
            px, py = chips[j]
            return pltpu.make_async_remote_copy(
                src_ref=p_refs[t].at[src_slot], dst_ref=out_refs[t].at[dst_slot],
                send_sem=send_sems.at[3 * t + j], recv_sem=recv_sems.at[3 * t + j], device_id=(px, py, c),
                device_id_type=MESH)

        own = [pltpu.make_async_copy(p_refs[t].at[mine], out_refs[t].at[mine], local_sems.at[t])
               for t in range(n_t)]
        sends = [copy(t, j, 2 * px + py, mine) for t in range(n_t) for j, (px, py) in enumerate(chips)]
        arrivals = [copy(t, j, mine, 2 * px + py) for t in range(n_t) for j, (px, py) in enumerate(chips)]
        return own, sends, arrivals

    def start(*refs):
        own, sends, _ = plan(*refs)
        for cp in own + sends:
            cp.start()

    def finish(*refs):
        own, sends, arrivals = plan(*refs)
        for cp in arrivals:
            cp.wait_recv()
        for cp in sends:
            cp.wait_send()
        for cp in own:
            cp.wait()

    outs = [jax.ShapeDtypeStruct(p.shape, p.dtype) for p in parts]
    return _Exchange(list(parts), outs, {}, 3 * n_t, n_t, start, finish)


def _sibling_gather_exchange(bufs):
    n_t = len(bufs)

    def plan(b_refs, out_refs, send_sems, recv_sems, local_sems):
        x, y, c = _place()

        def copy(t, pc):
            m = bufs[t].shape[0] // 2
            half = pl.ds(pc * m, m)
            return pltpu.make_async_remote_copy(
                src_ref=b_refs[t].at[half, :], dst_ref=out_refs[t].at[half, :], send_sem=send_sems.at[t],
                recv_sem=recv_sems.at[t], device_id=(x, y, 1 - c), device_id_type=MESH)

        return [copy(t, c) for t in range(n_t)], [copy(t, 1 - c) for t in range(n_t)]

    def start(*refs):
        for cp in plan(*refs)[0]:
            cp.start()

    def finish(*refs):
        sends, arrivals = plan(*refs)
        for cp in arrivals:
            cp.wait_recv()
        for cp in sends:
            cp.wait_send()

    outs = [jax.ShapeDtypeStruct(b.shape, b.dtype) for b in bufs]
    return _Exchange(list(bufs), outs, {t: t for t in range(n_t)}, n_t, 1, start, finish)


def _add_halves(g, recv, c, name):
    n_slot, m2, n = g.shape
    m = m2 // 2
    tr = _tile(m, (512, 256, 192, 128, 16))

    def body(c_ref, g_ref, r_ref, o_ref):
        o_ref[...] = (g_ref[...] + r_ref[...]).astype(BF16)

    nb = m // tr
    return pl.pallas_call(
        body,
        grid_spec=pltpu.PrefetchScalarGridSpec(
            num_scalar_prefetch=1, grid=(n_slot, nb),
            in_specs=[pl.BlockSpec((1, tr, n), lambda s, i, c_ref: (s, c_ref[0] * nb + i, 0)),
                      pl.BlockSpec((1, tr, n), lambda s, i, c_ref: (s, i, 0))],
            out_specs=pl.BlockSpec((1, tr, n), lambda s, i, c_ref: (s, i, 0))),
        out_shape=jax.ShapeDtypeStruct((n_slot, m, n), BF16),
        compiler_params=_cp("parallel", "parallel"), name=name)(c, g, recv)


def _sum_slots(parts, c, name):
    n_slot, m, n = parts.shape
    tr = _tile(m, (512, 256, 192, 128, 16))
    nb = m // tr

    def body(c_ref, p_ref, o_ref):
        acc = p_ref[0].astype(F32)
        for s in range(1, n_slot):
            acc = acc + p_ref[s].astype(F32)
        o_ref[...] = acc

    return pl.pallas_call(
        body,
        grid_spec=pltpu.PrefetchScalarGridSpec(
            num_scalar_prefetch=1, grid=(nb,),
            in_specs=[pl.BlockSpec((n_slot, tr, n), lambda i, c_ref: (0, i, 0))],
            out_specs=pl.BlockSpec((tr, n), lambda i, c_ref: (c_ref[0] * nb + i, 0))),
        out_shape=jax.ShapeDtypeStruct((2 * m, n), F32),
        compiler_params=_cp("parallel"), name=name)(c, parts)


_SHARDED = ("w_in", "w_q_up", "w_kv_up", "w_sb_out", "w_mla_out", "w_mix_out", "w_up", "w_down")
_ROW_SHARDED = ("w_mix_out", "w_down")
_BY_CHIP = ("w_up", "w_down")


def _unshard(parts, name):
    n, r, cs = parts.shape
    if name in _ROW_SHARDED:
        return parts.reshape(n * r, cs)
    return parts.transpose(1, 0, 2).reshape(r, n * cs)


def _reshard(full, name, n=4):
    R, C = full.shape
    if name in _ROW_SHARDED:
        return full.reshape(n, R // n, C)
    return full.reshape(R, n, C // n).transpose(1, 0, 2)


def _pad_w_in(w):
    z = lambda k: jnp.zeros(w.shape[:-1] + (k,), w.dtype)
    return jnp.concatenate([
        w[..., 2208:3232], w[..., 3232:4256], w[..., 0:1536], w[..., 1920:2176], w[..., 1536:1920],
        z(ROPE_LANE0), w[..., 2176:2208], z(LANE - ROPE_LANE0 - ROPE_DIM)], axis=-1)


def _unpad_w_in(g):
    k0 = COL_KROPE + ROPE_LANE0
    return jnp.concatenate([
        g[..., COL_QSB:COL_KVLAT], g[..., COL_QLAT:COL_KROPE], g[..., COL_KVLAT:COL_QLAT],
        g[..., k0:k0 + ROPE_DIM], g[..., 0:COL_QSB]], axis=-1)


def _pad_w_q(w):
    r = w.shape[0]
    return jnp.pad(w.reshape(r, N_HEADS, QK_DIM), ((0, 0), (0, 0), (0, LANE - QK_DIM))).reshape(r, N_HEADS * LANE)


def _unpad_w_q(g):
    r = g.shape[0]
    return g.reshape(r, N_HEADS, LANE)[..., :QK_DIM].reshape(r, N_HEADS * QK_DIM)


def _pad_w_mla(w):
    n = w.shape[1]
    return jnp.pad(w.reshape(N_HEADS, NOPE_DIM, n), ((0, 0), (LANE - NOPE_DIM, 0), (0, 0))).reshape(
        N_HEADS * LANE, n)


def _unpad_w_mla(g):
    n = g.shape[1]
    return g.reshape(N_HEADS, LANE, n)[:, LANE - NOPE_DIM:, :].reshape(N_HEADS * NOPE_DIM, n)


def _rope_tables(positions):
    half = ROPE_DIM // 2
    inv_freq = 1.0 / (ROPE_THETA ** (jnp.arange(0, ROPE_DIM, 2, dtype=F32) / ROPE_DIM))
    ang = positions.astype(F32)[:, None] * inv_freq
    cos, sin = jnp.cos(ang), jnp.sin(ang)
    S = positions.shape[0]
    one = jnp.ones((S, ROPE_LANE0), F32)
    zero = lambda k: jnp.zeros((S, k), F32)
    tail = LANE - ROPE_LANE0 - ROPE_DIM
    c = jnp.concatenate([one, cos, cos, zero(tail)], axis=1)
    s1 = jnp.concatenate([zero(ROPE_LANE0), -sin, zero(half + tail)], axis=1)
    s2 = jnp.concatenate([zero(ROPE_LANE0 + half), sin, zero(tail)], axis=1)
    return c, s1, s2


def _layer_fwd(x, W, mod, tabs, next_blocks=None):
    sh1, sc1, gt1, sh2, sc2, gt2 = (mod[i] for i in range(N_MOD))
    n_small = len(_SHARDED) - len(_BY_CHIP)
    small, big = (next_blocks[:n_small], next_blocks[n_small:]) if next_blocks else (None, None)
    h1 = _normmod_fwd(x, W["g_mix"], sc1, sh1, "mix_norm_fwd")
    p = _matmul(h1, W["w_in"], name="in_proj")
    (osbh, cb), carried = _sb_fwd(p, "sb_attn_fwd", [_gather_exchange(small, "a")] if small else None)
    o_sb = _matmul(osbh, W["w_sb_out"], name="sb_out")
    qn = _rmsnorm_fwd(p, Q_RANK, COL_QLAT // Q_RANK, W["g_q"], "q_lat_norm_fwd")
    kvn = _rmsnorm_fwd(p, KV_RANK, COL_KVLAT // KV_RANK, W["g_kv"], "kv_lat_norm_fwd")
    qp = _matmul(qn, W["w_q_up"], name="q_up")
    kv = _matmul(kvn, W["w_kv_up"], out_dtype=BF16, name="kv_up")
    qr, kpe = _rope_fwd(qp, p, tabs, "rope_fwd")
    (omh, lse), carried = _mla_fwd(
        qr, kv, kpe, "mla_attn_fwd",
        [_gather_exchange(carried[0], "b"), _gather_exchange(big, "a")] if small else None)
    o_mla = _matmul(omh, W["w_mla_out"], name="mla_out")
    merged = _merge_fwd(p, o_sb, o_mla, "merge_fwd")
    y1 = _matmul(merged, W["w_mix_out"], name="mix_out")
    x1 = _res_fwd(x, y1, gt1, "mix_residual")
    h2 = _normmod_fwd(x1, W["g_mlp"], sc2, sh2, "mlp_norm_fwd")

    def sqrelu(t):
        r = jnp.maximum(t, 0.0)
        return t, r * r

    if small:
        (u, a), big_done = _matmul(h2, W["w_up"], b_sharded="col", out_dtype=(F32, BF16), epilogue=sqrelu,
                                   exs=[_gather_exchange(carried[1], "b")], name="mlp_up")
        gathered = list(carried[0]) + list(big_done[0])
    else:
        u, a = _matmul(h2, W["w_up"], b_sharded="col", out_dtype=(F32, BF16), epilogue=sqrelu, name="mlp_up")
        gathered = []
    y2 = _matmul(a, W["w_down"], b_sharded="row", name="mlp_down")
    x2 = _res_fwd(x1, y2, gt2, "mlp_residual")
    saved = dict(x=x, h1=h1, p=p, osbh=osbh, cb=cb, o_sb=o_sb, qn=qn, kvn=kvn, qr=qr, kv=kv, kpe=kpe, omh=omh,
                 lse=lse, o_mla=o_mla, merged=merged, y1=y1, x1=x1, h2=h2, u=u, a=a, y2=y2)
    return x2, saved, gathered


def _layer_bwd(dx2, W, mod, tabs, sv, core, above=None):
    sh1, sc1, gt1, sh2, sc2, gt2 = (mod[i] for i in range(N_MOD))
    dy2, dgt2 = _res_bwd(dx2, sv["y2"], gt2, "mlp_residual_bwd")

    def sqrelu_bwd(da, u):
        return (da * (2.0 * jnp.maximum(u, 0.0)),)

    du = _matmul(dy2, W["w_down"], tb=True, b_sharded="row", out_dtype=BF16, epilogue=sqrelu_bwd, extra=(sv["u"],),
                 exs=[_swap_halves_exchange(above)] if above else None, name="mlp_down_dx")
    pending = None
    if above:
        du, (from_sibling,) = du
        pending = [_add_halves(d, r, core, "grads_add_halves") for d, r in zip(above, from_sibling)]
    g_down = _matmul(sv["a"], dy2, ta=True, out_sharded=("row", W["w_down"].shape), name="mlp_down_dw")
    dh2 = _matmul(du, W["w_up"], tb=True, b_sharded="col", name="mlp_up_dx")
    g_up = _matmul(sv["h2"], du, ta=True, out_sharded=("col", W["w_up"].shape), name="mlp_up_dw")
    dx1, dsh2, dsc2, dg_mlp = _normmod_bwd(sv["x1"], dh2, W["g_mlp"], sc2, dx2, "mlp_norm_bwd")
    dy1, dgt1 = _res_bwd(dx1, sv["y1"], gt1, "mix_residual_bwd")
    dm = _matmul(dy1, W["w_mix_out"], tb=True, name="mix_out_dx")
    g_mix_out = _matmul(sv["merged"], dy1, ta=True, name="mix_out_dw")
    do_sb, do_mla, dgs, dgm = _merge_bwd(sv["p"], sv["o_sb"], sv["o_mla"], dm, "merge_bwd")
    do_sbh = _matmul(do_sb, W["w_sb_out"], tb=True, name="sb_out_dx")
    g_sb_out = _matmul(sv["osbh"], do_sb, ta=True, name="sb_out_dw")
    (dqs, dks, dvs), carried = _sb_bwd(
        sv["p"], do_sbh, sv["cb"], "sb_attn_bwd", [_chip_scatter_exchange(pending)] if above else None)
    my_sum = [_sum_slots(part, core, "grads_sum_chips") for part in carried[0]] if above else []
    do_mh = _matmul(do_mla, W["w_mla_out"], tb=True, name="mla_out_dx")
    g_mla_out = _matmul(sv["omh"], do_mla, ta=True, name="mla_out_dw")
    (dqr, dkv, dkpe), carried = _mla_bwd(
        sv["qr"], sv["kv"], sv["kpe"], do_mh, sv["omh"], sv["lse"], "mla_attn_bwd",
        [_sibling_gather_exchange(my_sum)] if above else None)
    reduced_above = list(carried[0]) if above else []
    dqp, dkr = _rope_bwd(dqr, dkpe, tabs, "rope_bwd")
    dqn = _matmul(dqp, W["w_q_up"], tb=True, name="q_up_dx")
    g_q_up = _matmul(sv["qn"], dqp, ta=True, name="q_up_dw")
    dkvn = _matmul(dkv, W["w_kv_up"], tb=True, name="kv_up_dx")
    g_kv_up = _matmul(sv["kvn"], dkv, ta=True, name="kv_up_dw")
    dqlat, dg_q = _rmsnorm_bwd(sv["p"], Q_RANK, COL_QLAT // Q_RANK, dqn, W["g_q"], "q_lat_norm_bwd")
    dkvlat, dg_kv = _rmsnorm_bwd(sv["p"], KV_RANK, COL_KVLAT // KV_RANK, dkvn, W["g_kv"], "kv_lat_norm_bwd")
    dp = jnp.concatenate([dgs, dgm, dqs, dks, dvs, dkvlat, dqlat, dkr], axis=1)
    dh1 = _matmul(dp, W["w_in"], tb=True, name="in_proj_dx")
    g_in = _matmul(sv["h1"], dp, ta=True, name="in_proj_dw")
    dx, dsh1, dsc1, dg_mix = _normmod_bwd(sv["x"], dh1, W["g_mix"], sc1, dx1, "mix_norm_bwd")
    grads = dict(w_in=g_in, w_q_up=g_q_up, w_kv_up=g_kv_up, w_sb_out=g_sb_out, w_mla_out=g_mla_out,
                 w_mix_out=g_mix_out, w_up=g_up, w_down=g_down,
                 dmod=jnp.concatenate([dsh1, dsc1, dgt1, dsh2, dsc2, dgt2], axis=0),
                 g_mix=dg_mix, g_mlp=dg_mlp, g_q=dg_q, g_kv=dg_kv)
    return dx, grads, reduced_above


def kernel(x, c, positions, w_ada, b_ada, g_mix_norm, w_in, g_q_lat, w_q_up, g_kv_lat, w_kv_up, w_sb_out, w_mla_out, w_mix_out, g_mlp_norm, w_up, w_down, g_final, loss_target, m_w_ada, m_b_ada, m_g_mix_norm, m_w_in, m_g_q_lat, m_w_q_up, m_g_kv_lat, m_w_kv_up, m_w_sb_out, m_w_mla_out, m_w_mix_out, m_g_mlp_norm, m_w_up, m_w_down, m_g_final, v_w_ada, v_b_ada, v_g_mix_norm, v_w_in, v_g_q_lat, v_w_q_up, v_g_kv_lat, v_w_kv_up, v_w_sb_out, v_w_mla_out, v_w_mix_out, v_g_mlp_norm, v_w_up, v_w_down, v_g_final):
    xi, yi, ci = _place()
    chip = 2 * xi + yi
    batch = 2 * chip + ci
    L = w_ada.shape[0]
    S = x.shape[1]
    shards = dict(w_in=w_in, w_q_up=w_q_up, w_kv_up=w_kv_up, w_sb_out=w_sb_out, w_mla_out=w_mla_out,
                  w_mix_out=w_mix_out, w_up=w_up, w_down=w_down)

    def my_halves(l):
        def half_of(w):
            half = w.shape[1] // 2
            return lax.dynamic_slice_in_dim(w[l].astype(BF16), ci * half, half, 0)

        return [half_of(shards[n]) for n in _SHARDED]

    def layer_weights(l, gathered):
        W = {}
        for n, g in zip(_SHARDED, gathered):
            by_chip = g.reshape((4,) + shards[n].shape[1:])
            W[n] = by_chip if n in _BY_CHIP else _unshard(by_chip, n)
        W["w_in"] = _pad_w_in(W["w_in"])
        W["w_q_up"] = _pad_w_q(W["w_q_up"])
        W["w_mla_out"] = _pad_w_mla(W["w_mla_out"])
        return dict(W, g_mix=g_mix_norm[l:l + 1], g_mlp=g_mlp_norm[l:l + 1], g_q=g_q_lat[l:l + 1],
                    g_kv=g_kv_lat[l:l + 1])

    gathered0 = _all_gather8(my_halves(0), "gather_weights")

    c_act = _silu(c, "silu_c")
    c_all = _all_gather8([jnp.broadcast_to(c_act, (8, D_MODEL))], "gather_c")[0].reshape(8, 8, D_MODEL)[:, 0]
    c16 = jnp.concatenate([c_all, jnp.zeros_like(c_all)], axis=0)
    ada_cols = w_ada.shape[2]
    b_shard = lax.dynamic_slice_in_dim(b_ada, chip * ada_cols, ada_cols, 1)
    mod_part = jnp.stack([_matmul(c16, w_ada[l], name="ada_mod") for l in range(L)])
    mod_part = _bias_add(mod_part, jnp.broadcast_to(b_shard[:, None, :], mod_part.shape), "ada_bias")
    mod_all = _all_gather8([mod_part.reshape(L * 16, ada_cols)], "gather_mod")[0].reshape(4, 2, L, 16, ada_cols)
    mod_mine = lax.dynamic_index_in_dim(mod_all[:, 0], batch, axis=2, keepdims=False)
    mods = mod_mine.transpose(1, 0, 2).reshape(L, N_MOD, 1, D_MODEL)

    tabs = _rope_tables(positions[0])

    xc, saved, layer_w = x[0], [], [layer_weights(0, gathered0)]
    for l in range(L):
        xc, sv, gathered = _layer_fwd(xc, layer_w[l], mods[l], tabs, my_halves(l + 1) if l + 1 < L else None)
        saved.append(sv)
        if l + 1 < L:
            layer_w.append(layer_weights(l + 1, gathered))
    dxc, dg_final, loss_part = _final_loss(xc, loss_target[0], g_final[None, :], "final_norm_loss")
    loss = lax.psum(loss_part[0, 0], ("x", "y", "c"))
    core = jnp.reshape(ci, (1,)).astype(jnp.int32)

    grads, reduced, above = [None] * L, [None] * L, None
    for l in reversed(range(L)):
        dxc, grads[l], reduced_above = _layer_bwd(dxc, layer_w[l], mods[l], tabs, saved[l], core, above)
        if above:
            reduced[l + 1] = reduced_above
        grads[l]["w_in"] = _unpad_w_in(grads[l]["w_in"])
        grads[l]["w_q_up"] = _unpad_w_q(grads[l]["w_q_up"])
        grads[l]["w_mla_out"] = _unpad_w_mla(grads[l]["w_mla_out"])
        above = [grads[l][n] if n in _BY_CHIP else _reshard(grads[l][n], n) for n in _SHARDED]
    from_sibling = _run_exchange(_swap_halves_exchange(above), "grads_swap_halves")
    pending = [_add_halves(d, r, core, "grads_add_halves") for d, r in zip(above, from_sibling)]
    from_chips = _run_exchange(_chip_scatter_exchange(pending), "grads_chip_scatter")
    my_sum = [_sum_slots(part, core, "grads_sum_chips") for part in from_chips]
    reduced[0] = _run_exchange(_sibling_gather_exchange(my_sum), "grads_sibling_gather")
    grad_x = dxc
    gw = {n: jnp.stack([reduced[l][i] for l in range(L)]) for i, n in enumerate(_SHARDED)}

    def row(v):
        return jnp.pad(v, ((0, 0), (0, D_MODEL - v.shape[1])))

    per_layer_rows = N_MOD + 4
    small = jnp.concatenate(
        [jnp.concatenate([grads[l]["dmod"], row(grads[l]["g_mix"]), row(grads[l]["g_mlp"]),
                          row(grads[l]["g_q"]), row(grads[l]["g_kv"])], axis=0) for l in range(L)]
        + [dg_final], axis=0)
    n_small = -(-small.shape[0] // 8) * 8
    small = jnp.pad(small, ((0, n_small - small.shape[0]), (0, 0)))
    small_all = _all_gather8([small], "gather_vector_grads")[0].reshape(8, n_small, D_MODEL)
    small_sum = _sum_blocks(small_all, "sum_vector_grads")
    lay = small_sum[:L * per_layer_rows].reshape(L, per_layer_rows, D_MODEL)
    g_b_ada = lay[:, :N_MOD].reshape(L, N_MOD * D_MODEL)
    g_g_mix, g_g_mlp = lay[:, N_MOD], lay[:, N_MOD + 1]
    g_g_q, g_g_kv = lay[:, N_MOD + 2, :Q_RANK], lay[:, N_MOD + 3, :KV_RANK]
    g_g_final = small_sum[L * per_layer_rows]
    dmod_all = small_all[:, :L * per_layer_rows].reshape(8, L, per_layer_rows, D_MODEL)[:, :, :N_MOD]
    dmod_all = dmod_all.reshape(8, L, N_MOD * D_MODEL)
    dmod_cols = lax.dynamic_slice_in_dim(dmod_all, chip * ada_cols, ada_cols, 2)
    dmod16 = jnp.concatenate([dmod_cols, jnp.zeros_like(dmod_cols)], axis=0)
    g_w_ada = jnp.stack([_matmul(c16, dmod16[:, l], ta=True, name="ada_dw") for l in range(L)])

    weights = dict(w_ada=w_ada, b_ada=b_ada, g_mix_norm=g_mix_norm, w_in=w_in, g_q_lat=g_q_lat, w_q_up=w_q_up,
                   g_kv_lat=g_kv_lat, w_kv_up=w_kv_up, w_sb_out=w_sb_out, w_mla_out=w_mla_out,
                   w_mix_out=w_mix_out, g_mlp_norm=g_mlp_norm, w_up=w_up, w_down=w_down, g_final=g_final)
    mom = dict(w_ada=(m_w_ada, v_w_ada), b_ada=(m_b_ada, v_b_ada), g_mix_norm=(m_g_mix_norm, v_g_mix_norm),
               w_in=(m_w_in, v_w_in), g_q_lat=(m_g_q_lat, v_g_q_lat), w_q_up=(m_w_q_up, v_w_q_up),
               g_kv_lat=(m_g_kv_lat, v_g_kv_lat), w_kv_up=(m_w_kv_up, v_w_kv_up),
               w_sb_out=(m_w_sb_out, v_w_sb_out), w_mla_out=(m_w_mla_out, v_w_mla_out),
               w_mix_out=(m_w_mix_out, v_w_mix_out), g_mlp_norm=(m_g_mlp_norm, v_g_mlp_norm),
               w_up=(m_w_up, v_w_up), w_down=(m_w_down, v_w_down), g_final=(m_g_final, v_g_final))
    gr = dict(gw, w_ada=g_w_ada, b_ada=g_b_ada, g_mix_norm=g_g_mix, g_q_lat=g_g_q, g_kv_lat=g_g_kv,
              g_mlp_norm=g_g_mlp, g_final=g_g_final)
    order = list(weights)
    deltas, new_m, new_v = [], [], []
    for n in order:
        wv, gv, (mv, vv) = weights[n], gr[n], mom[n]
        if wv.ndim == 1:
            d, nm, nv = (t[0] for t in _adamw(wv[None], gv[None], mv[None], vv[None], "adamw_" + n))
        else:
            d, nm, nv = _adamw(wv, gv, mv, vv, "adamw_" + n)
        deltas.append(d)
        new_m.append(nm)
        new_v.append(nv)
    return (loss, grad_x[None], *[gr[n] for n in order], *deltas, *new_m, *new_v)
```

```python
from typing import Any, Callable, Mapping, NamedTuple, Sequence

import jax
import jax.numpy as jnp
from jax import lax
from jax.experimental import pallas as pl
from jax.experimental.pallas import tpu as pltpu

F32 = jnp.float32
BF16 = jnp.bfloat16
MESH = pl.DeviceIdType.MESH

D_MODEL = 1024
N_HEADS = 8
SB_DIM = 64
SB_WIDTH = 512
Q_RANK = 384
KV_RANK = 256
ROPE_DIM = 32
NOPE_DIM = 64
QK_DIM = 96
D_FF = 4096
N_MOD = 6
EPS = 1e-6
ROPE_THETA = 10000.0
SB_SCALE = SB_DIM ** -0.5
MLA_SCALE = QK_DIM ** -0.5
ADAM_LR, ADAM_B1, ADAM_B2, ADAM_EPS, ADAM_WD, ADAM_STEP = 0.001, 0.9, 0.999, 1e-08, 0.01, 10

LANE = 128
IN_PAD = 4352
COL_GATE_SB, COL_GATE_MLA, COL_QSB, COL_KSB, COL_VSB, COL_KVLAT, COL_QLAT, COL_KROPE = (
    0, 1024, 2048, 2560, 3072, 3584, 3840, 4224)
ROPE_LANE0 = 64
VMEM_LIMIT = 48 * 1024 * 1024
NEG_BIG = -1e30


def _cp(*sem):
    return pltpu.CompilerParams(dimension_semantics=sem, vmem_limit_bytes=VMEM_LIMIT)


def _tile(n, prefs):
    for t in prefs:
        if t <= n and n % t == 0:
            return t
    return n


def _dot(a, b, dims):
    return lax.dot_general(a, b, (dims, ((), ())), preferred_element_type=F32)


def _nn(a, b):
    return _dot(a, b, ((1,), (0,)))


def _nt(a, b):
    return _dot(a, b, ((1,), (1,)))


def _tn(a, b):
    return _dot(a, b, ((0,), (0,)))


def _sharded_dims(shape, kind):
    n, r, cs = shape
    return (n * r, cs) if kind == "row" else (r, n * cs)


def _sharded_spec(shape, kind, t_rows, t_cols, tile_of):
    _, r, cs = shape
    if kind == "row":
        assert r % t_rows == 0, (shape, t_rows)
        per = r // t_rows

        def index(i, j, k):
            tr, tc = tile_of(i, j, k)
            return tr // per, tr % per, tc
    else:
        assert cs % t_cols == 0, (shape, t_cols)
        per = cs // t_cols

        def index(i, j, k):
            tr, tc = tile_of(i, j, k)
            return tc // per, tr, tc % per
    return pl.BlockSpec((None, t_rows, t_cols), index)


def _matmul(a, b, *, ta=False, tb=False, out_dtype=F32, b_sharded=None, out_sharded=None, epilogue=None,
            extra=(), exs=None, name):
    (K, M) = a.shape if ta else a.shape[::-1]
    b_dims = _sharded_dims(b.shape, b_sharded) if b_sharded else b.shape
    (N, Kb) = b_dims if tb else b_dims[::-1]
    assert K == Kb, (a.shape, b.shape, ta, tb)
    tm = _tile(M, (512, 384, 256, 128))
    tn = _tile(N, (1024, 2176, 768, 512, 384, 256, 128))
    tk = _tile(K, (1024, 2176, 768, 512, 384, 256, 128))
    nk = K // tk
    dims = ((0 if ta else 1,), (1 if tb else 0,))

    out_dtypes = out_dtype if isinstance(out_dtype, tuple) else (out_dtype,)
    n_extra, n_o = len(extra), len(out_dtypes)

    def body(a_ref, b_ref, *rest):
        extra_refs, o_refs, acc = rest[:n_extra], rest[n_extra:n_extra + n_o], rest[n_extra + n_o:]
        prod = _dot(a_ref[...].astype(BF16), b_ref[...].astype(BF16), dims)

        def write(total):
            vals = epilogue(total, *[r[...] for r in extra_refs]) if epilogue else (total,)
            for o_ref, val, dt in zip(o_refs, vals, out_dtypes):
                o_ref[...] = val.astype(dt)

        if nk == 1:
            write(prod)
            return
        acc_ref, = acc
        k = pl.program_id(2)

        @pl.when(k == 0)
        def _():
            acc_ref[...] = prod

        @pl.when(k > 0)
        def _():
            acc_ref[...] += prod

        @pl.when(k == nk - 1)
        def _():
            write(acc_ref[...])

    a_spec = (pl.BlockSpec((tk, tm), lambda i, j, k: (k, i)) if ta
              else pl.BlockSpec((tm, tk), lambda i, j, k: (i, k)))
    if b_sharded:
        b_spec = (_sharded_spec(b.shape, b_sharded, tn, tk, lambda i, j, k: (j, k)) if tb
                  else _sharded_spec(b.shape, b_sharded, tk, tn, lambda i, j, k: (k, j)))
    else:
        b_spec = (pl.BlockSpec((tn, tk), lambda i, j, k: (j, k)) if tb
                  else pl.BlockSpec((tk, tn), lambda i, j, k: (k, j)))
    tile = pl.BlockSpec((tm, tn), lambda i, j, k: (i, j))
    if out_sharded:
        kind, shape = out_sharded
        assert _sharded_dims(shape, kind) == (M, N) and n_o == 1, (shape, kind, M, N)
        out_specs = [_sharded_spec(shape, kind, tm, tn, lambda i, j, k: (i, j))]
        out_shape = [jax.ShapeDtypeStruct(shape, out_dtypes[0])]
    else:
        out_specs = [tile] * n_o
        out_shape = [jax.ShapeDtypeStruct((M, N), dt) for dt in out_dtypes]
    grid = (M // tm, N // tn, nk)
    scratch = [pltpu.VMEM((tm, tn), F32)] if nk > 1 else []
    ins = [a, b, *extra]
    body, more, split = _carry(exs, body, len(ins), n_o, len(scratch), grid)
    own, carried = split(pl.pallas_call(
        body, grid=grid, in_specs=[a_spec, b_spec] + [tile] * n_extra + more["in_specs"],
        out_specs=out_specs + more["out_specs"], out_shape=out_shape + more["out_shape"],
        scratch_shapes=scratch + more["scratch"], input_output_aliases=more["aliases"],
        compiler_params=_cp(*(("arbitrary",) * 3 if exs else ("parallel", "parallel", "arbitrary"))),
        name=name)(*ins, *more["ins"]))
    result = own[0] if n_o == 1 else tuple(own)
    return (result, carried) if exs else result


def _rows(ts, w, col=0):
    return pl.BlockSpec((ts, w), lambda i: (i, col))


def _vec(w):
    return pl.BlockSpec((1, w), lambda i: (0, 0))


def _ts(S):
    return _tile(S, (256, 128))


def _attn_tile(S):
    return _tile(S, (512, 256, 128))


def _rms(x):
    return lax.rsqrt(jnp.mean(x * x, axis=-1, keepdims=True) + EPS)


def _colsum(x):
    return jnp.sum(x, axis=0, keepdims=True)


def _normmod_fwd(x, g, sc, sh, name):
    S, W = x.shape
    ts = _ts(S)

    def body(x_ref, g_ref, sc_ref, sh_ref, h_ref):
        xv = x_ref[...]
        h_ref[...] = ((xv * _rms(xv)) * g_ref[...] * (1.0 + sc_ref[...]) + sh_ref[...]).astype(BF16)

    return pl.pallas_call(
        body, grid=(S // ts,), in_specs=[_rows(ts, W), _vec(W), _vec(W), _vec(W)],
        out_specs=_rows(ts, W), out_shape=jax.ShapeDtypeStruct((S, W), BF16),
        compiler_params=_cp("parallel"), name=name)(x, g, sc, sh)


def _normmod_bwd(x, dh, g, sc, dres, name):
    S, W = x.shape
    ts = _ts(S)

    def body(x_ref, dh_ref, g_ref, sc_ref, dres_ref, dx_ref, dsh_ref, dsc_ref, dg_ref):
        @pl.when(pl.program_id(0) == 0)
        def _():
            dsh_ref[...] = jnp.zeros_like(dsh_ref)
            dsc_ref[...] = jnp.zeros_like(dsc_ref)
            dg_ref[...] = jnp.zeros_like(dg_ref)

        xv, dh_v, gv = x_ref[...], dh_ref[...], g_ref[...]
        r = _rms(xv)
        y = xv * r
        dn = dh_v * (1.0 + sc_ref[...])
        dy = dn * gv
        dx_ref[...] = dres_ref[...] + r * (dy - y * jnp.mean(dy * y, axis=-1, keepdims=True))
        dsh_ref[...] += _colsum(dh_v)
        dsc_ref[...] += _colsum(dh_v * y * gv)
        dg_ref[...] += _colsum(dn * y)

    vec_out = jax.ShapeDtypeStruct((1, W), F32)
    return pl.pallas_call(
        body, grid=(S // ts,),
        in_specs=[_rows(ts, W), _rows(ts, W), _vec(W), _vec(W), _rows(ts, W)],
        out_specs=[_rows(ts, W), _vec(W), _vec(W), _vec(W)],
        out_shape=[jax.ShapeDtypeStruct((S, W), F32), vec_out, vec_out, vec_out],
        compiler_params=_cp("arbitrary"), name=name)(x, dh, g, sc, dres)


def _rmsnorm_fwd(p, width, col, g, name):
    S = p.shape[0]
    ts = _ts(S)

    def body(x_ref, g_ref, y_ref):
        xv = x_ref[...]
        y_ref[...] = ((xv * _rms(xv)) * g_ref[...]).astype(BF16)

    return pl.pallas_call(
        body, grid=(S // ts,), in_specs=[_rows(ts, width, col), _vec(width)],
        out_specs=_rows(ts, width), out_shape=jax.ShapeDtypeStruct((S, width), BF16),
        compiler_params=_cp("parallel"), name=name)(p, g)


def _rmsnorm_bwd(p, width, col, dn, g, name):
    S = p.shape[0]
    ts = _ts(S)

    def body(x_ref, dn_ref, g_ref, dx_ref, dg_ref):
        @pl.when(pl.program_id(0) == 0)
        def _():
            dg_ref[...] = jnp.zeros_like(dg_ref)

        xv, dn_v = x_ref[...], dn_ref[...]
        r = _rms(xv)
        y = xv * r
        dy = dn_v * g_ref[...]
        dx_ref[...] = r * (dy - y * jnp.mean(dy * y, axis=-1, keepdims=True))
        dg_ref[...] += _colsum(dn_v * y)

    return pl.pallas_call(
        body, grid=(S // ts,), in_specs=[_rows(ts, width, col), _rows(ts, width), _vec(width)],
        out_specs=[_rows(ts, width), _vec(width)],
        out_shape=[jax.ShapeDtypeStruct((S, width), F32), jax.ShapeDtypeStruct((1, width), F32)],
        compiler_params=_cp("arbitrary"), name=name)(p, dn, g)


def _rope_rot(t, c, s1, s2):
    return t * c + pltpu.roll(t, LANE - 16, 1) * s1 + pltpu.roll(t, 16, 1) * s2


def _rope_rot_t(d, c, s1, s2):
    return d * c + pltpu.roll(d * s1, 16, 1) + pltpu.roll(d * s2, LANE - 16, 1)


def _rope_fwd(qp, p, tabs, name):
    S = qp.shape[0]
    ts = _ts(S)
    W = N_HEADS * LANE

    def body(q_ref, kr_ref, c_ref, s1_ref, s2_ref, qr_ref, kpe_ref):
        c, s1, s2 = c_ref[...], s1_ref[...], s2_ref[...]
        for h in range(N_HEADS):
            sl = slice(h * LANE, (h + 1) * LANE)
            qr_ref[:, sl] = _rope_rot(q_ref[:, sl], c, s1, s2).astype(BF16)
        kpe_ref[...] = _rope_rot(kr_ref[...], c, s1, s2).astype(BF16)

    tab = _rows(ts, LANE)
    return pl.pallas_call(
        body, grid=(S // ts,), in_specs=[_rows(ts, W), _rows(ts, LANE, COL_KROPE // LANE), tab, tab, tab],
        out_specs=[_rows(ts, W), _rows(ts, LANE)],
        out_shape=[jax.ShapeDtypeStruct((S, W), BF16), jax.ShapeDtypeStruct((S, LANE), BF16)],
        compiler_params=_cp("parallel"), name=name)(qp, p, *tabs)


def _rope_bwd(dqr, dkpe_heads, tabs, name):
    S = dqr.shape[0]
    ts = _ts(S)
    W = N_HEADS * LANE

    def body(dq_ref, dk_ref, c_ref, s1_ref, s2_ref, dqp_ref, dkr_ref):
        c, s1, s2 = c_ref[...], s1_ref[...], s2_ref[...]
        dk = dk_ref[:, 0:LANE]
        for h in range(N_HEADS):
            sl = slice(h * LANE, (h + 1) * LANE)
            dqp_ref[:, sl] = _rope_rot_t(dq_ref[:, sl], c, s1, s2).astype(BF16)
            if h:
                dk = dk + dk_ref[:, sl]
        dkr_ref[...] = _rope_rot_t(dk, c, s1, s2)

    tab = _rows(ts, LANE)
    return pl.pallas_call(
        body, grid=(S // ts,), in_specs=[_rows(ts, W), _rows(ts, W), tab, tab, tab],
        out_specs=[_rows(ts, W), _rows(ts, LANE)],
        out_shape=[jax.ShapeDtypeStruct((S, W), BF16), jax.ShapeDtypeStruct((S, LANE), F32)],
        compiler_params=_cp("parallel"), name=name)(dqr, dkpe_heads, *tabs)


def _merge_fwd(p, o_sb, o_mla, name):
    S, W = o_sb.shape
    ts = _ts(S)

    def body(gs_ref, gm_ref, a_ref, b_ref, m_ref):
        m_ref[...] = (jax.nn.sigmoid(gs_ref[...]) * a_ref[...]
                      + jax.nn.sigmoid(gm_ref[...]) * b_ref[...]).astype(BF16)

    return pl.pallas_call(
        body, grid=(S // ts,),
        in_specs=[_rows(ts, W, COL_GATE_SB // W), _rows(ts, W, COL_GATE_MLA // W), _rows(ts, W), _rows(ts, W)],
        out_specs=_rows(ts, W), out_shape=jax.ShapeDtypeStruct((S, W), BF16),
        compiler_params=_cp("parallel"), name=name)(p, p, o_sb, o_mla)


def _merge_bwd(p, o_sb, o_mla, dm, name):
    S, W = o_sb.shape
    ts = _ts(S)

    def body(gs_ref, gm_ref, a_ref, b_ref, dm_ref, da_ref, db_ref, dgs_ref, dgm_ref):
        dmv = dm_ref[...]
        sa, sb = jax.nn.sigmoid(gs_ref[...]), jax.nn.sigmoid(gm_ref[...])
        da_ref[...] = (dmv * sa).astype(BF16)
        db_ref[...] = (dmv * sb).astype(BF16)
        dgs_ref[...] = dmv * a_ref[...] * sa * (1.0 - sa)
        dgm_ref[...] = dmv * b_ref[...] * sb * (1.0 - sb)

    row = _rows(ts, W)
    return pl.pallas_call(
        body, grid=(S // ts,),
        in_specs=[_rows(ts, W, COL_GATE_SB // W), _rows(ts, W, COL_GATE_MLA // W), row, row, row],
        out_specs=[row, row, row, row],
        out_shape=[jax.ShapeDtypeStruct((S, W), BF16), jax.ShapeDtypeStruct((S, W), BF16),
                   jax.ShapeDtypeStruct((S, W), F32), jax.ShapeDtypeStruct((S, W), F32)],
        compiler_params=_cp("parallel"), name=name)(p, p, o_sb, o_mla, dm)


def _res_fwd(x, y, gate, name):
    S, W = x.shape
    ts = _ts(S)

    def body(x_ref, y_ref, g_ref, o_ref):
        o_ref[...] = x_ref[...] + g_ref[...] * y_ref[...]

    return pl.pallas_call(
        body, grid=(S // ts,), in_specs=[_rows(ts, W), _rows(ts, W), _vec(W)], out_specs=_rows(ts, W),
        out_shape=jax.ShapeDtypeStruct((S, W), F32), compiler_params=_cp("parallel"), name=name)(x, y, gate)


def _res_bwd(dx, y, gate, name):
    S, W = dx.shape
    ts = _ts(S)

    def body(dx_ref, y_ref, g_ref, dy_ref, dg_ref):
        @pl.when(pl.program_id(0) == 0)
        def _():
            dg_ref[...] = jnp.zeros_like(dg_ref)

        dxv = dx_ref[...]
        dy_ref[...] = (g_ref[...] * dxv).astype(BF16)
        dg_ref[...] += _colsum(dxv * y_ref[...])

    return pl.pallas_call(
        body, grid=(S // ts,), in_specs=[_rows(ts, W), _rows(ts, W), _vec(W)],
        out_specs=[_rows(ts, W), _vec(W)],
        out_shape=[jax.ShapeDtypeStruct((S, W), BF16), jax.ShapeDtypeStruct((1, W), F32)],
        compiler_params=_cp("arbitrary"), name=name)(dx, y, gate)


def _final_loss(x, target, g, name):
    S, W = x.shape
    ts = _ts(S)

    def body(x_ref, t_ref, g_ref, dx_ref, dg_ref, loss_ref):
        @pl.when(pl.program_id(0) == 0)
        def _():
            dg_ref[...] = jnp.zeros_like(dg_ref)
            loss_ref[...] = jnp.zeros_like(loss_ref)

        xv, gv = x_ref[...], g_ref[...]
        r = _rms(xv)
        y = xv * r
        err = y * gv - t_ref[...]
        loss_ref[...] += jnp.full((1, LANE), 0.5 * jnp.sum(jnp.mean(err * err, axis=-1)), F32)
        dout = err * (1.0 / W)
        dy = dout * gv
        dx_ref[...] = r * (dy - y * jnp.mean(dy * y, axis=-1, keepdims=True))
        dg_ref[...] += _colsum(dout * y)

    return pl.pallas_call(
        body, grid=(S // ts,), in_specs=[_rows(ts, W), _rows(ts, W), _vec(W)],
        out_specs=[_rows(ts, W), _vec(W), _vec(LANE)],
        out_shape=[jax.ShapeDtypeStruct((S, W), F32), jax.ShapeDtypeStruct((1, W), F32),
                   jax.ShapeDtypeStruct((1, LANE), F32)],
        compiler_params=_cp("arbitrary"), name=name)(x, target, g)


def _silu(c, name):
    def body(c_ref, o_ref):
        cv = c_ref[...]
        o_ref[...] = cv * jax.nn.sigmoid(cv)

    return pl.pallas_call(body, out_shape=jax.ShapeDtypeStruct(c.shape, F32), name=name)(c)


def _bias_add(a, b, name):
    def body(a_ref, b_ref, o_ref):
        o_ref[...] = a_ref[...] + b_ref[...]

    return pl.pallas_call(body, out_shape=jax.ShapeDtypeStruct(a.shape, F32), name=name)(a, b)


def _sum_blocks(xs, name):
    n = xs.shape[0]

    def body(x_ref, o_ref):
        acc = x_ref[0]
        for d in range(1, n):
            acc = acc + x_ref[d]
        o_ref[...] = acc

    return pl.pallas_call(body, out_shape=jax.ShapeDtypeStruct(xs.shape[1:], F32), name=name)(xs)


def _adamw(w, g, m, v, name):
    shape = w.shape
    cols = shape[-1]
    w2, g2, m2, v2 = (t.reshape(-1, cols) for t in (w, g, m, v))
    rows = w2.shape[0]
    tr = _tile(rows, (128,))
    c1 = 1.0 - ADAM_B1 ** ADAM_STEP
    c2 = 1.0 - ADAM_B2 ** ADAM_STEP

    def body(w_ref, g_ref, m_ref, v_ref, d_ref, nm_ref, nv_ref):
        gv = g_ref[...]
        nm = ADAM_B1 * m_ref[...] + (1.0 - ADAM_B1) * gv
        nv = ADAM_B2 * v_ref[...] + (1.0 - ADAM_B2) * (gv * gv)
        d_ref[...] = -ADAM_LR * ((nm / c1) / (jnp.sqrt(nv / c2) + ADAM_EPS) + ADAM_WD * w_ref[...])
        nm_ref[...] = nm
        nv_ref[...] = nv

    spec = pl.BlockSpec((tr, cols), lambda i: (i, 0))
    out = jax.ShapeDtypeStruct((rows, cols), F32)
    d, nm, nv = pl.pallas_call(
        body, grid=(rows // tr,), in_specs=[spec] * 4, out_specs=[spec] * 3, out_shape=[out] * 3,
        compiler_params=_cp("parallel"), name=name)(w2, g2, m2, v2)
    return d.reshape(shape), nm.reshape(shape), nv.reshape(shape)


def _split_dot(x, tri):
    hi = x.astype(BF16)
    lo = (x - hi.astype(F32)).astype(BF16)
    return _nn(hi, tri) + _nn(lo, tri)


def _sb_logs(z):
    soft = jnp.log(1.0 + jnp.exp(-jnp.abs(z)))
    return jnp.minimum(z, 0.0) - soft, -jnp.maximum(z, 0.0) - soft


def _attn_call(body, grid, ins, in_specs, out_specs, out_shape, scratch, exs, name):
    body, extra, split = _carry(exs, body, len(ins), len(out_shape), len(scratch), grid)
    return split(pl.pallas_call(
        body, grid=grid, in_specs=in_specs + extra["in_specs"], out_specs=out_specs + extra["out_specs"],
        out_shape=out_shape + extra["out_shape"], scratch_shapes=scratch + extra["scratch"],
        input_output_aliases=extra["aliases"], compiler_params=_cp("arbitrary", "arbitrary"),
        name=name)(*ins, *extra["ins"]))


def _sb_fwd(p, name, exs=None):
    S = p.shape[0]
    t = _attn_tile(S)
    qb, kb, vb = COL_QSB // LANE, COL_KSB // LANE, COL_VSB // LANE

    def body(q_ref, k_ref, v_ref, o_ref, cb_ref, acc_ref):
        i = pl.program_id(1)
        lane = lax.broadcasted_iota(jnp.int32, (t, LANE), 1)
        rows = lax.broadcasted_iota(jnp.int32, (t, t), 0)
        cols = lax.broadcasted_iota(jnp.int32, (t, t), 1)
        after = jnp.where(rows > cols, 1.0, 0.0).astype(BF16)
        diag = cols < rows
        q = q_ref[...] * SB_SCALE
        acc_ref[...] = jnp.zeros_like(acc_ref)
        cb_ref[...] = jnp.zeros_like(cb_ref)
        hms = [(lane >= SB_DIM * h) & (lane < SB_DIM * (h + 1)) for h in range(2)]
        qhs = [jnp.where(hm, q, 0.0).astype(BF16) for hm in hms]

        def step(j, cs, masked):
            rows_j = pl.ds(pl.multiple_of(j * t, t), t)
            kj = k_ref[rows_j, :].astype(BF16)
            vf = v_ref[rows_j, :]
            out, pv = [], None
            for h in range(2):
                ls, lf = _sb_logs(_nt(qhs[h], kj))
                if masked:
                    lf = jnp.where(diag, lf, 0.0)
                a = jnp.exp(ls + _split_dot(lf, after) + cs[h])
                if masked:
                    a = jnp.where(diag, a, 0.0)
                term = _nn(a.astype(BF16), jnp.where(hms[h], vf, 0.0).astype(BF16))
                pv = term if pv is None else pv + term
                cb_ref[h] = jnp.where(lane == j, cs[h], cb_ref[h])
                out.append(cs[h] + jnp.sum(lf, axis=1, keepdims=True))
            acc_ref[...] += pv
            return tuple(out)

        zero = jnp.zeros((t, 1), F32)
        cs = step(i, (zero, zero), True)
        lax.fori_loop(0, i, lambda it, cs: step(i - 1 - it, cs, False), cs)
        o_ref[...] = acc_ref[...].astype(BF16)

    return _attn_call(
        body, (SB_WIDTH // LANE, S // t), [p, p, p],
        [pl.BlockSpec((t, LANE), lambda hp, i: (i, qb + hp)),
         pl.BlockSpec((S, LANE), lambda hp, i: (0, kb + hp)),
         pl.BlockSpec((S, LANE), lambda hp, i: (0, vb + hp))],
        [pl.BlockSpec((t, LANE), lambda hp, i: (i, hp)),
         pl.BlockSpec((2, t, LANE), lambda hp, i: (hp, i, 0))],
        [jax.ShapeDtypeStruct((S, SB_WIDTH), BF16), jax.ShapeDtypeStruct((N_HEADS, S, LANE), F32)],
        [pltpu.VMEM((t, LANE), F32)], exs, name)


def _sb_bwd(p, do, cb, name, exs=None):
    S = p.shape[0]
    t = _attn_tile(S)
    qb, kb, vb = COL_QSB // LANE, COL_KSB // LANE, COL_VSB // LANE

    def body(q_ref, k_ref, v_ref, do_ref, cb_ref, dq_ref, dk_ref, dv_ref, acc_ref):
        i = pl.program_id(1)

        @pl.when(i == 0)
        def _():
            dk_ref[...] = jnp.zeros_like(dk_ref)
            dv_ref[...] = jnp.zeros_like(dv_ref)

        lane = lax.broadcasted_iota(jnp.int32, (t, LANE), 1)
        rows = lax.broadcasted_iota(jnp.int32, (t, t), 0)
        cols = lax.broadcasted_iota(jnp.int32, (t, t), 1)
        after = jnp.where(rows > cols, 1.0, 0.0).astype(BF16)
        before = jnp.where(rows < cols, 1.0, 0.0).astype(BF16)
        diag = cols < rows
        q = q_ref[...] * SB_SCALE
        dov = do_ref[...]
        acc_ref[...] = jnp.zeros_like(acc_ref)
        hms = [(lane >= SB_DIM * h) & (lane < SB_DIM * (h + 1)) for h in range(2)]
        qhs = [jnp.where(hm, q, 0.0).astype(BF16) for hm in hms]
        dohs = [jnp.where(hm, dov, 0.0).astype(BF16) for hm in hms]

        def step(j, fs, masked):
            rows_j = pl.ds(pl.multiple_of(j * t, t), t)
            kf = k_ref[rows_j, :]
            kj = kf.astype(BF16)
            vj = v_ref[rows_j, :].astype(BF16)
            out, dq_t, dk_t, dv_t = [], None, None, None
            for h in range(2):
                ls, lf = _sb_logs(_nt(qhs[h], kj))
                if masked:
                    lf = jnp.where(diag, lf, 0.0)
                c = jnp.sum(jnp.where(lane == j, cb_ref[h], 0.0), axis=1, keepdims=True)
                a = jnp.exp(ls + _split_dot(lf, after) + c)
                if masked:
                    a = jnp.where(diag, a, 0.0)
                dl = _nt(dohs[h], vj) * a
                sg = jnp.exp(ls)
                dz = dl * (1.0 - sg) - sg * (_split_dot(dl, before) + fs[h])
                if masked:
                    dz = jnp.where(diag, dz, 0.0)
                dzb = dz.astype(BF16)
                terms = (_nn(dzb, jnp.where(hms[h], kf, 0.0).astype(BF16)), _tn(dzb, qhs[h]),
                         _tn(a.astype(BF16), dohs[h]))
                dq_t, dk_t, dv_t = terms if dq_t is None else (dq_t + terms[0], dk_t + terms[1], dv_t + terms[2])
                out.append(fs[h] + jnp.sum(dl, axis=1, keepdims=True))
            acc_ref[...] += dq_t
            dk_ref[rows_j, :] += dk_t
            dv_ref[rows_j, :] += dv_t
            return tuple(out)

        zero = jnp.zeros((t, 1), F32)
        fs = lax.fori_loop(0, i, lambda j, fs: step(j, fs, False), (zero, zero))
        step(i, fs, True)
        dq_ref[...] = acc_ref[...] * SB_SCALE

    col = lambda hp, i: (0, hp)
    out = jax.ShapeDtypeStruct((S, SB_WIDTH), F32)
    return _attn_call(
        body, (SB_WIDTH // LANE, S // t), [p, p, p, do, cb],
        [pl.BlockSpec((t, LANE), lambda hp, i: (i, qb + hp)),
         pl.BlockSpec((S, LANE), lambda hp, i: (0, kb + hp)),
         pl.BlockSpec((S, LANE), lambda hp, i: (0, vb + hp)),
         pl.BlockSpec((t, LANE), lambda hp, i: (i, hp)),
         pl.BlockSpec((2, t, LANE), lambda hp, i: (hp, i, 0))],
        [pl.BlockSpec((t, LANE), lambda hp, i: (i, hp)), pl.BlockSpec((S, LANE), col), pl.BlockSpec((S, LANE), col)],
        [out, out, out], [pltpu.VMEM((t, LANE), F32)], exs, name)


def _mla_fwd(qr, kv, kpe, name, exs=None):
    S = qr.shape[0]
    t = _attn_tile(S)

    def body(q_ref, kv_ref, kpe_ref, o_ref, lse_ref, acc_ref, m_ref):
        i = pl.program_id(1)
        low = lax.broadcasted_iota(jnp.int32, (t, LANE), 1) < NOPE_DIM
        rows = lax.broadcasted_iota(jnp.int32, (t, t), 0)
        cols = lax.broadcasted_iota(jnp.int32, (t, t), 1)
        causal = cols <= rows
        one = jnp.ones((t, LANE), BF16)
        heads = [slice(h * LANE, (h + 1) * LANE) for h in range(2)]
        qs = [q_ref[:, sl] for sl in heads]
        acc_ref[...] = jnp.zeros_like(acc_ref)
        m_ref[...] = jnp.full_like(m_ref, NEG_BIG)

        def step(j, masked):
            rows_j = pl.ds(pl.multiple_of(j * t, t), t)
            kpe_j = kpe_ref[rows_j, :]
            for h, sl in enumerate(heads):
                kvj = kv_ref[rows_j, sl]
                z = _nt(qs[h], jnp.where(low, kvj, kpe_j)) * MLA_SCALE
                if masked:
                    z = jnp.where(causal, z, NEG_BIG)
                m_old = m_ref[h]
                m_new = jnp.maximum(m_old, jnp.max(z, axis=1, keepdims=True))
                pr = jnp.exp(z - m_new)
                acc_ref[:, sl] = jnp.exp(m_old - m_new) * acc_ref[:, sl] + _nn(
                    pr.astype(BF16), jnp.where(low, one, kvj))
                m_ref[h] = m_new

        def loop(j, carry):
            step(j, False)
            return carry

        lax.fori_loop(0, i, loop, 0)
        step(i, True)
        for h, sl in enumerate(heads):
            acc = acc_ref[:, sl]
            den = acc[:, 0:1]
            o_ref[:, sl] = jnp.where(low, 0.0, acc / den).astype(BF16)
            lse_ref[h] = jnp.broadcast_to(m_ref[h] + jnp.log(den), (t, LANE))

    pair = 2 * LANE
    return _attn_call(
        body, (N_HEADS // 2, S // t), [qr, kv, kpe],
        [pl.BlockSpec((t, pair), lambda hp, i: (i, hp)),
         pl.BlockSpec((S, pair), lambda hp, i: (0, hp)),
         pl.BlockSpec((S, LANE), lambda hp, i: (0, 0))],
        [pl.BlockSpec((t, pair), lambda hp, i: (i, hp)), pl.BlockSpec((2, t, LANE), lambda hp, i: (hp, i, 0))],
        [jax.ShapeDtypeStruct((S, N_HEADS * LANE), BF16), jax.ShapeDtypeStruct((N_HEADS, S, LANE), F32)],
        [pltpu.VMEM((t, pair), F32), pltpu.VMEM((2, t, 1), F32)], exs, name)


def _mla_bwd(qr, kv, kpe, do, o, lse, name, exs=None):
    S = qr.shape[0]
    t = _attn_tile(S)

    def body(q_ref, kv_ref, kpe_ref, do_ref, o_ref, lse_ref, dq_ref, dkv_ref, dkpe_ref, acc_ref):
        i = pl.program_id(1)

        @pl.when(i == 0)
        def _():
            dkv_ref[...] = jnp.zeros_like(dkv_ref)
            dkpe_ref[...] = jnp.zeros_like(dkpe_ref)

        low = lax.broadcasted_iota(jnp.int32, (t, LANE), 1) < NOPE_DIM
        rows = lax.broadcasted_iota(jnp.int32, (t, t), 0)
        cols = lax.broadcasted_iota(jnp.int32, (t, t), 1)
        causal = cols <= rows
        heads = [slice(h * LANE, (h + 1) * LANE) for h in range(2)]
        qs = [q_ref[:, sl] for sl in heads]
        dovs = [do_ref[:, sl] for sl in heads]
        dobs = [d.astype(BF16) for d in dovs]
        deltas = [jnp.sum(dovs[h] * o_ref[:, sl].astype(F32), axis=1, keepdims=True) for h, sl in enumerate(heads)]
        lses = [lse_ref[h][:, 0:1] for h in range(2)]
        acc_ref[...] = jnp.zeros_like(acc_ref)

        def step(j, masked):
            rows_j = pl.ds(pl.multiple_of(j * t, t), t)
            kpe_j = kpe_ref[rows_j, :]
            for h, sl in enumerate(heads):
                kvj = kv_ref[rows_j, sl]
                kcat = jnp.where(low, kvj, kpe_j)
                z = _nt(qs[h], kcat) * MLA_SCALE
                if masked:
                    z = jnp.where(causal, z, NEG_BIG)
                pr = jnp.exp(z - lses[h])
                ds = (pr * (_nt(dobs[h], kvj) - deltas[h])).astype(BF16)
                acc_ref[:, sl] += _nn(ds, kcat)
                dkc = _tn(ds, qs[h]) * MLA_SCALE
                dkv_ref[rows_j, sl] += jnp.where(low, dkc, _tn(pr.astype(BF16), dobs[h]))
                dkpe_ref[rows_j, sl] += jnp.where(low, 0.0, dkc)

        def loop(j, carry):
            step(j, False)
            return carry

        lax.fori_loop(0, i, loop, 0)
        step(i, True)
        dq_ref[...] = acc_ref[...] * MLA_SCALE

    pair = 2 * LANE
    blk = pl.BlockSpec((t, pair), lambda hp, i: (i, hp))
    col = pl.BlockSpec((S, pair), lambda hp, i: (0, hp))
    out = jax.ShapeDtypeStruct((S, N_HEADS * LANE), F32)
    return _attn_call(
        body, (N_HEADS // 2, S // t), [qr, kv, kpe, do, o, lse],
        [blk, col, pl.BlockSpec((S, LANE), lambda hp, i: (0, 0)), blk, blk,
         pl.BlockSpec((2, t, LANE), lambda hp, i: (hp, i, 0))],
        [blk, col, col], [out, out, out], [pltpu.VMEM((t, pair), F32)], exs, name)


_ANY = pl.BlockSpec(memory_space=pl.ANY)


def _place():
    return lax.axis_index("x"), lax.axis_index("y"), lax.axis_index("c")


class _Exchange(NamedTuple):
    ins: Sequence[Any]
    outs: Sequence[Any]
    aliases: Mapping[int, int]
    n_remote: int
    n_local: int
    start: Callable
    finish: Callable


def _exchange_scratch(ex):
    return [pltpu.SemaphoreType.DMA((ex.n_remote,)), pltpu.SemaphoreType.DMA((ex.n_remote,)),
            pltpu.SemaphoreType.DMA((ex.n_local,))]


def _run_exchange(ex, name):
    n_in, n_out = len(ex.ins), len(ex.outs)

    def body(*refs):
        args = (refs[:n_in], refs[n_in:n_in + n_out], *refs[n_in + n_out:])
        ex.start(*args)
        ex.finish(*args)

    return pl.pallas_call(
        body, out_shape=list(ex.outs), in_specs=[_ANY] * n_in, out_specs=[_ANY] * n_out,
        scratch_shapes=_exchange_scratch(ex), input_output_aliases=dict(ex.aliases), name=name)(*ex.ins)


def _carry(exs, body, n_in, n_out, n_scratch, grid):
    exs = [ex for ex in (exs or []) if ex is not None]
    e_ins, e_outs = [len(ex.ins) for ex in exs], [len(ex.outs) for ex in exs]

    def take(refs, counts):
        groups = []
        for n in counts:
            groups.append(refs[:n])
            refs = refs[n:]
        return groups, refs

    def carried(*refs):
        own_in, refs = refs[:n_in], refs[n_in:]
        ex_in, refs = take(refs, e_ins)
        own_out, refs = refs[:n_out], refs[n_out:]
        ex_out, refs = take(refs, e_outs)
        own_scratch, refs = refs[:n_scratch], refs[n_scratch:]
        sems, _ = take(refs, [3] * len(exs))
        at = [pl.program_id(d) for d in range(len(grid))]
        first, last = at[0] == 0, at[0] == grid[0] - 1
        for d in range(1, len(grid)):
            first, last = first & (at[d] == 0), last & (at[d] == grid[d] - 1)

        @pl.when(first)
        def _():
            for e, ex in enumerate(exs):
                ex.start(ex_in[e], ex_out[e], *sems[e])

        body(*own_in, *own_out, *own_scratch)

        @pl.when(last)
        def _():
            for e, ex in enumerate(exs):
                ex.finish(ex_in[e], ex_out[e], *sems[e])

    aliases, i0, o0 = {}, n_in, n_out
    for ex in exs:
        aliases.update({i0 + i: o0 + o for i, o in ex.aliases.items()})
        i0, o0 = i0 + len(ex.ins), o0 + len(ex.outs)

    def split(res):
        groups, _ = take(list(res[n_out:]), e_outs)
        return list(res[:n_out]), groups

    extra = dict(
        ins=[a for ex in exs for a in ex.ins], in_specs=[_ANY] * sum(e_ins), out_specs=[_ANY] * sum(e_outs),
        out_shape=[o for ex in exs for o in ex.outs], scratch=[s for ex in exs for s in _exchange_scratch(ex)],
        aliases=aliases)
    return (carried if exs else body), extra, split


def _gather_exchange(arrs, phase="all"):
    n_t = len(arrs)
    ms = [a.shape[0] // (8 if phase == "b" else 1) for a in arrs]

    def plan(in_refs, out_refs, send_sems, recv_sems, local_sems):
        x, y, c = _place()
        me, sibling = (x, y, c), (x, y, 1 - c)
        chips = [(1 - x, y), (x, 1 - y), (1 - x, 1 - y)]

        def rows(ref, t, px, py, pc):
            return ref.at[pl.ds((4 * px + 2 * py + pc) * ms[t], ms[t]), :]

        def copy(t, k, block, to, src):
            return pltpu.make_async_remote_copy(
                src_ref=src, dst_ref=rows(out_refs[t], t, *block), send_sem=send_sems.at[7 * t + k],
                recv_sem=recv_sems.at[7 * t + k], device_id=to, device_id_type=MESH)

        mine, first, first_in, passed, passed_in = [], [], [], [], []
        for t in range(n_t):
            if phase != "b":
                mine.append(pltpu.make_async_copy(in_refs[t], rows(out_refs[t], t, *me), local_sems.at[t]))
                first.append(copy(t, 0, me, sibling, in_refs[t]))
                first_in.append(copy(t, 0, sibling, me, in_refs[t]))
                for j, chip in enumerate(chips):
                    first.append(copy(t, 1 + j, me, (*chip, c), in_refs[t]))
                    first_in.append(copy(t, 1 + j, (*chip, c), me, in_refs[t]))
            if phase != "a":
                held = in_refs[t] if phase == "b" else out_refs[t]
                for j, chip in enumerate(chips):
                    passed.append(copy(t, 4 + j, (*chip, c), sibling, rows(held, t, *chip, c)))
                    passed_in.append(copy(t, 4 + j, (*chip, 1 - c), me, rows(held, t, *chip, c)))
        return mine, first, first_in, passed, passed_in

    def start(*refs):
        mine, first, _, passed, _ = plan(*refs)
        for cp in mine + first + (passed if phase == "b" else []):
            cp.start()

    def finish(*refs):
        mine, first, first_in, passed, passed_in = plan(*refs)
        for cp in first_in:
            cp.wait_recv()
        if phase == "all":
            for cp in passed:
                cp.start()
        for cp in passed_in:
            cp.wait_recv()
        for cp in first + passed:
            cp.wait_send()
        for cp in mine:
            cp.wait()

    if phase == "b":
        outs = [jax.ShapeDtypeStruct(a.shape, a.dtype) for a in arrs]
        aliases = {t: t for t in range(n_t)}
    else:
        outs = [jax.ShapeDtypeStruct((8 * a.shape[0], a.shape[1]), a.dtype) for a in arrs]
        aliases = {}
    return _Exchange(list(arrs), outs, aliases, 7 * n_t, n_t, start, finish)


def _all_gather8(blks, name):
    return _run_exchange(_gather_exchange(blks), name)


def _swap_halves_exchange(gs):
    n_t = len(gs)

    def plan(g_refs, out_refs, send_sems, recv_sems, local_sems):
        x, y, c = _place()
        copies = []
        for t in range(n_t):
            m = gs[t].shape[1] // 2
            copies += [pltpu.make_async_remote_copy(
                src_ref=g_refs[t].at[s, pl.ds((1 - c) * m, m), :], dst_ref=out_refs[t].at[s],
                send_sem=send_sems.at[4 * t + s], recv_sem=recv_sems.at[4 * t + s], device_id=(x, y, 1 - c),
                device_id_type=MESH) for s in range(4)]
        return copies

    def start(*refs):
        for cp in plan(*refs):
            cp.start()

    def finish(*refs):
        for cp in plan(*refs):
            cp.wait()

    outs = [jax.ShapeDtypeStruct((4, g.shape[1] // 2, g.shape[2]), g.dtype) for g in gs]
    return _Exchange(list(gs), outs, {}, 4 * n_t, 1, start, finish)


def _chip_scatter_exchange(parts):
    n_t = len(parts)

    def plan(p_refs, out_refs, send_sems, recv_sems, local_sems):
        x, y, c = _place()
        mine = 2 * x + y
        chips = [(1 - x, y), (x, 1 - y), (1 - x, 1 - y)]

        def copy(t, j, src_slot, dst_slot):
            px, py = chips[j]
            return pltpu.make_async_remote_copy(
                src_ref=p_refs[t].at[src_slot], dst_ref=out_refs[t].at[dst_slot],
                send_sem=send_sems.at[3 * t + j], recv_sem=recv_sems.at[3 * t + j], device_id=(px, py, c),
                device_id_type=MESH)

        own = [pltpu.make_async_copy(p_refs[t].at[mine], out_refs[t].at[mine], local_sems.at[t])
               for t in range(n_t)]
        sends = [copy(t, j, 2 * px + py, mine) for t in range(n_t) for j, (px, py) in enumerate(chips)]
        arrivals = [copy(t, j, mine, 2 * px + py) for t in range(n_t) for j, (px, py) in enumerate(chips)]
        return own, sends, arrivals

    def start(*refs):
        own, sends, _ = plan(*refs)
        for cp in own + sends:
            cp.start()

    def finish(*refs):
        own, sends, arrivals = plan(*refs)
        for cp in arrivals:
            cp.wait_recv()
        for cp in sends:
            cp.wait_send()
        for cp in own:
            cp.wait()

    outs = [jax.ShapeDtypeStruct(p.shape, p.dtype) for p in parts]
    return _Exchange(list(parts), outs, {}, 3 * n_t, n_t, start, finish)


def _sibling_gather_exchange(bufs):
    n_t = len(bufs)

    def plan(b_refs, out_refs, send_sems, recv_sems, local_sems):
        x, y, c = _place()

        def copy(t, pc):
            m = bufs[t].shape[0] // 2
            half = pl.ds(pc * m, m)
            return pltpu.make_async_remote_copy(
                src_ref=b_refs[t].at[half, :], dst_ref=out_refs[t].at[half, :], send_sem=send_sems.at[t],
                recv_sem=recv_sems.at[t], device_id=(x, y, 1 - c), device_id_type=MESH)

        return [copy(t, c) for t in range(n_t)], [copy(t, 1 - c) for t in range(n_t)]

    def start(*refs):
        for cp in plan(*refs)[0]:
            cp.start()

    def finish(*refs):
        sends, arrivals = plan(*refs)
        for cp in arrivals:
            cp.wait_recv()
        for cp in sends:
            cp.wait_send()

    outs = [jax.ShapeDtypeStruct(b.shape, b.dtype) for b in bufs]
    return _Exchange(list(bufs), outs, {t: t for t in range(n_t)}, n_t, 1, start, finish)


def _add_halves(g, recv, c, name):
    n_slot, m2, n = g.shape
    m = m2 // 2
    tr = _tile(m, (512, 256, 192, 128, 16))

    def body(c_ref, g_ref, r_ref, o_ref):
        o_ref[...] = (g_ref[...] + r_ref[...]).astype(BF16)

    nb = m // tr
    return pl.pallas_call(
        body,
        grid_spec=pltpu.PrefetchScalarGridSpec(
            num_scalar_prefetch=1, grid=(n_slot, nb),
            in_specs=[pl.BlockSpec((1, tr, n), lambda s, i, c_ref: (s, c_ref[0] * nb + i, 0)),
                      pl.BlockSpec((1, tr, n), lambda s, i, c_ref: (s, i, 0))],
            out_specs=pl.BlockSpec((1, tr, n), lambda s, i, c_ref: (s, i, 0))),
        out_shape=jax.ShapeDtypeStruct((n_slot, m, n), BF16),
        compiler_params=_cp("parallel", "parallel"), name=name)(c, g, recv)


def _sum_slots(parts, c, name):
    n_slot, m, n = parts.shape
    tr = _tile(m, (512, 256, 192, 128, 16))
    nb = m // tr

    def body(c_ref, p_ref, o_ref):
        acc = p_ref[0].astype(F32)
        for s in range(1, n_slot):
            acc = acc + p_ref[s].astype(F32)
        o_ref[...] = acc

    return pl.pallas_call(
        body,
        grid_spec=pltpu.PrefetchScalarGridSpec(
            num_scalar_prefetch=1, grid=(nb,),
            in_specs=[pl.BlockSpec((n_slot, tr, n), lambda i, c_ref: (0, i, 0))],
            out_specs=pl.BlockSpec((tr, n), lambda i, c_ref: (c_ref[0] * nb + i, 0))),
        out_shape=jax.ShapeDtypeStruct((2 * m, n), F32),
        compiler_params=_cp("parallel"), name=name)(c, parts)


_SHARDED = ("w_in", "w_q_up", "w_kv_up", "w_sb_out", "w_mla_out", "w_mix_out", "w_up", "w_down")
_ROW_SHARDED = ("w_mix_out", "w_down")
_BY_CHIP = ("w_up", "w_down")


def _unshard(parts, name):
    n, r, cs = parts.shape
    if name in _ROW_SHARDED:
        return parts.reshape(n * r, cs)
    return parts.transpose(1, 0, 2).reshape(r, n * cs)


def _reshard(full, name, n=4):
    R, C = full.shape
    if name in _ROW_SHARDED:
        return full.reshape(n, R // n, C)
    return full.reshape(R, n, C // n).transpose(1, 0, 2)


def _pad_w_in(w):
    z = lambda k: jnp.zeros(w.shape[:-1] + (k,), w.dtype)
    return jnp.concatenate([
        w[..., 2208:3232], w[..., 3232:4256], w[..., 0:1536], w[..., 1920:2176], w[..., 1536:1920],
        z(ROPE_LANE0), w[..., 2176:2208], z(LANE - ROPE_LANE0 - ROPE_DIM)], axis=-1)


def _unpad_w_in(g):
    k0 = COL_KROPE + ROPE_LANE0
    return jnp.concatenate([
        g[..., COL_QSB:COL_KVLAT], g[..., COL_QLAT:COL_KROPE], g[..., COL_KVLAT:COL_QLAT],
        g[..., k0:k0 + ROPE_DIM], g[..., 0:COL_QSB]], axis=-1)


def _pad_w_q(w):
    r = w.shape[0]
    return jnp.pad(w.reshape(r, N_HEADS, QK_DIM), ((0, 0), (0, 0), (0, LANE - QK_DIM))).reshape(r, N_HEADS * LANE)


def _unpad_w_q(g):
    r = g.shape[0]
    return g.reshape(r, N_HEADS, LANE)[..., :QK_DIM].reshape(r, N_HEADS * QK_DIM)


def _pad_w_mla(w):
    n = w.shape[1]
    return jnp.pad(w.reshape(N_HEADS, NOPE_DIM, n), ((0, 0), (LANE - NOPE_DIM, 0), (0, 0))).reshape(
        N_HEADS * LANE, n)


def _unpad_w_mla(g):
    n = g.shape[1]
    return g.reshape(N_HEADS, LANE, n)[:, LANE - NOPE_DIM:, :].reshape(N_HEADS * NOPE_DIM, n)


def _rope_tables(positions):
    half = ROPE_DIM // 2
    inv_freq = 1.0 / (ROPE_THETA ** (jnp.arange(0, ROPE_DIM, 2, dtype=F32) / ROPE_DIM))
    ang = positions.astype(F32)[:, None] * inv_freq
    cos, sin = jnp.cos(ang), jnp.sin(ang)
    S = positions.shape[0]
    one = jnp.ones((S, ROPE_LANE0), F32)
    zero = lambda k: jnp.zeros((S, k), F32)
    tail = LANE - ROPE_LANE0 - ROPE_DIM
    c = jnp.concatenate([one, cos, cos, zero(tail)], axis=1)
    s1 = jnp.concatenate([zero(ROPE_LANE0), -sin, zero(half + tail)], axis=1)
    s2 = jnp.concatenate([zero(ROPE_LANE0 + half), sin, zero(tail)], axis=1)
    return c, s1, s2


def _layer_fwd(x, W, mod, tabs, next_blocks=None):
    sh1, sc1, gt1, sh2, sc2, gt2 = (mod[i] for i in range(N_MOD))
    n_small = len(_SHARDED) - len(_BY_CHIP)
    small, big = (next_blocks[:n_small], next_blocks[n_small:]) if next_blocks else (None, None)
    h1 = _normmod_fwd(x, W["g_mix"], sc1, sh1, "mix_norm_fwd")
    p = _matmul(h1, W["w_in"], name="in_proj")
    (osbh, cb), carried = _sb_fwd(p, "sb_attn_fwd", [_gather_exchange(small, "a")] if small else None)
    o_sb = _matmul(osbh, W["w_sb_out"], name="sb_out")
    qn = _rmsnorm_fwd(p, Q_RANK, COL_QLAT // Q_RANK, W["g_q"], "q_lat_norm_fwd")
    kvn = _rmsnorm_fwd(p, KV_RANK, COL_KVLAT // KV_RANK, W["g_kv"], "kv_lat_norm_fwd")
    qp = _matmul(qn, W["w_q_up"], name="q_up")
    kv = _matmul(kvn, W["w_kv_up"], out_dtype=BF16, name="kv_up")
    qr, kpe = _rope_fwd(qp, p, tabs, "rope_fwd")
    (omh, lse), carried = _mla_fwd(
        qr, kv, kpe, "mla_attn_fwd",
        [_gather_exchange(carried[0], "b"), _gather_exchange(big, "a")] if small else None)
    o_mla = _matmul(omh, W["w_mla_out"], name="mla_out")
    merged = _merge_fwd(p, o_sb, o_mla, "merge_fwd")
    y1 = _matmul(merged, W["w_mix_out"], name="mix_out")
    x1 = _res_fwd(x, y1, gt1, "mix_residual")
    h2 = _normmod_fwd(x1, W["g_mlp"], sc2, sh2, "mlp_norm_fwd")

    def sqrelu(t):
        r = jnp.maximum(t, 0.0)
        return t, r * r

    if small:
        (u, a), big_done = _matmul(h2, W["w_up"], b_sharded="col", out_dtype=(F32, BF16), epilogue=sqrelu,
                                   exs=[_gather_exchange(carried[1], "b")], name="mlp_up")
        gathered = list(carried[0]) + list(big_done[0])
    else:
        u, a = _matmul(h2, W["w_up"], b_sharded="col", out_dtype=(F32, BF16), epilogue=sqrelu, name="mlp_up")
        gathered = []
    y2 = _matmul(a, W["w_down"], b_sharded="row", name="mlp_down")
    x2 = _res_fwd(x1, y2, gt2, "mlp_residual")
    saved = dict(x=x, h1=h1, p=p, osbh=osbh, cb=cb, o_sb=o_sb, qn=qn, kvn=kvn, qr=qr, kv=kv, kpe=kpe, omh=omh,
                 lse=lse, o_mla=o_mla, merged=merged, y1=y1, x1=x1, h2=h2, u=u, a=a, y2=y2)
    return x2, saved, gathered


def _layer_bwd(dx2, W, mod, tabs, sv, core, above=None):
    sh1, sc1, gt1, sh2, sc2, gt2 = (mod[i] for i in range(N_MOD))
    dy2, dgt2 = _res_bwd(dx2, sv["y2"], gt2, "mlp_residual_bwd")

    def sqrelu_bwd(da, u):
        return (da * (2.0 * jnp.maximum(u, 0.0)),)

    du = _matmul(dy2, W["w_down"], tb=True, b_sharded="row", out_dtype=BF16, epilogue=sqrelu_bwd, extra=(sv["u"],),
                 exs=[_swap_halves_exchange(above)] if above else None, name="mlp_down_dx")
    pending = None
    if above:
        du, (from_sibling,) = du
        pending = [_add_halves(d, r, core, "grads_add_halves") for d, r in zip(above, from_sibling)]
    g_down = _matmul(sv["a"], dy2, ta=True, out_sharded=("row", W["w_down"].shape), name="mlp_down_dw")
    dh2 = _matmul(du, W["w_up"], tb=True, b_sharded="col", name="mlp_up_dx")
    g_up = _matmul(sv["h2"], du, ta=True, out_sharded=("col", W["w_up"].shape), name="mlp_up_dw")
    dx1, dsh2, dsc2, dg_mlp = _normmod_bwd(sv["x1"], dh2, W["g_mlp"], sc2, dx2, "mlp_norm_bwd")
    dy1, dgt1 = _res_bwd(dx1, sv["y1"], gt1, "mix_residual_bwd")
    dm = _matmul(dy1, W["w_mix_out"], tb=True, name="mix_out_dx")
    g_mix_out = _matmul(sv["merged"], dy1, ta=True, name="mix_out_dw")
    do_sb, do_mla, dgs, dgm = _merge_bwd(sv["p"], sv["o_sb"], sv["o_mla"], dm, "merge_bwd")
    do_sbh = _matmul(do_sb, W["w_sb_out"], tb=True, name="sb_out_dx")
    g_sb_out = _matmul(sv["osbh"], do_sb, ta=True, name="sb_out_dw")
    (dqs, dks, dvs), carried = _sb_bwd(
        sv["p"], do_sbh, sv["cb"], "sb_attn_bwd", [_chip_scatter_exchange(pending)] if above else None)
    my_sum = [_sum_slots(part, core, "grads_sum_chips") for part in carried[0]] if above else []
    do_mh = _matmul(do_mla, W["w_mla_out"], tb=True, name="mla_out_dx")
    g_mla_out = _matmul(sv["omh"], do_mla, ta=True, name="mla_out_dw")
    (dqr, dkv, dkpe), carried = _mla_bwd(
        sv["qr"], sv["kv"], sv["kpe"], do_mh, sv["omh"], sv["lse"], "mla_attn_bwd",
        [_sibling_gather_exchange(my_sum)] if above else None)
    reduced_above = list(carried[0]) if above else []
    dqp, dkr = _rope_bwd(dqr, dkpe, tabs, "rope_bwd")
    dqn = _matmul(dqp, W["w_q_up"], tb=True, name="q_up_dx")
    g_q_up = _matmul(sv["qn"], dqp, ta=True, name="q_up_dw")
    dkvn = _matmul(dkv, W["w_kv_up"], tb=True, name="kv_up_dx")
    g_kv_up = _matmul(sv["kvn"], dkv, ta=True, name="kv_up_dw")
    dqlat, dg_q = _rmsnorm_bwd(sv["p"], Q_RANK, COL_QLAT // Q_RANK, dqn, W["g_q"], "q_lat_norm_bwd")
    dkvlat, dg_kv = _rmsnorm_bwd(sv["p"], KV_RANK, COL_KVLAT // KV_RANK, dkvn, W["g_kv"], "kv_lat_norm_bwd")
    dp = jnp.concatenate([dgs, dgm, dqs, dks, dvs, dkvlat, dqlat, dkr], axis=1)
    dh1 = _matmul(dp, W["w_in"], tb=True, name="in_proj_dx")
    g_in = _matmul(sv["h1"], dp, ta=True, name="in_proj_dw")
    dx, dsh1, dsc1, dg_mix = _normmod_bwd(sv["x"], dh1, W["g_mix"], sc1, dx1, "mix_norm_bwd")
    grads = dict(w_in=g_in, w_q_up=g_q_up, w_kv_up=g_kv_up, w_sb_out=g_sb_out, w_mla_out=g_mla_out,
                 w_mix_out=g_mix_out, w_up=g_up, w_down=g_down,
                 dmod=jnp.concatenate([dsh1, dsc1, dgt1, dsh2, dsc2, dgt2], axis=0),
                 g_mix=dg_mix, g_mlp=dg_mlp, g_q=dg_q, g_kv=dg_kv)
    return dx, grads, reduced_above


def kernel(x, c, positions, w_ada, b_ada, g_mix_norm, w_in, g_q_lat, w_q_up, g_kv_lat, w_kv_up, w_sb_out, w_mla_out, w_mix_out, g_mlp_norm, w_up, w_down, g_final, loss_target, m_w_ada, m_b_ada, m_g_mix_norm, m_w_in, m_g_q_lat, m_w_q_up, m_g_kv_lat, m_w_kv_up, m_w_sb_out, m_w_mla_out, m_w_mix_out, m_g_mlp_norm, m_w_up, m_w_down, m_g_final, v_w_ada, v_b_ada, v_g_mix_norm, v_w_in, v_g_q_lat, v_w_q_up, v_g_kv_lat, v_w_kv_up, v_w_sb_out, v_w_mla_out, v_w_mix_out, v_g_mlp_norm, v_w_up, v_w_down, v_g_final):
    xi, yi, ci = _place()
    chip = 2 * xi + yi
    batch = 2 * chip + ci
    L = w_ada.shape[0]
    S = x.shape[1]
    shards = dict(w_in=w_in, w_q_up=w_q_up, w_kv_up=w_kv_up, w_sb_out=w_sb_out, w_mla_out=w_mla_out,
                  w_mix_out=w_mix_out, w_up=w_up, w_down=w_down)

    def my_halves(l):
        def half_of(w):
            half = w.shape[1] // 2
            return lax.dynamic_slice_in_dim(w[l].astype(BF16), ci * half, half, 0)

        return [half_of(shards[n]) for n in _SHARDED]

    def layer_weights(l, gathered):
        W = {}
        for n, g in zip(_SHARDED, gathered):
            by_chip = g.reshape((4,) + shards[n].shape[1:])
            W[n] = by_chip if n in _BY_CHIP else _unshard(by_chip, n)
        W["w_in"] = _pad_w_in(W["w_in"])
        W["w_q_up"] = _pad_w_q(W["w_q_up"])
        W["w_mla_out"] = _pad_w_mla(W["w_mla_out"])
        return dict(W, g_mix=g_mix_norm[l:l + 1], g_mlp=g_mlp_norm[l:l + 1], g_q=g_q_lat[l:l + 1],
                    g_kv=g_kv_lat[l:l + 1])

    gathered0 = _all_gather8(my_halves(0), "gather_weights")

    c_act = _silu(c, "silu_c")
    c_all = _all_gather8([jnp.broadcast_to(c_act, (8, D_MODEL))], "gather_c")[0].reshape(8, 8, D_MODEL)[:, 0]
    c16 = jnp.concatenate([c_all, jnp.zeros_like(c_all)], axis=0)
    ada_cols = w_ada.shape[2]
    b_shard = lax.dynamic_slice_in_dim(b_ada, chip * ada_cols, ada_cols, 1)
    mod_part = jnp.stack([_matmul(c16, w_ada[l], name="ada_mod") for l in range(L)])
    mod_part = _bias_add(mod_part, jnp.broadcast_to(b_shard[:, None, :], mod_part.shape), "ada_bias")
    mod_all = _all_gather8([mod_part.reshape(L * 16, ada_cols)], "gather_mod")[0].reshape(4, 2, L, 16, ada_cols)
    mod_mine = lax.dynamic_index_in_dim(mod_all[:, 0], batch, axis=2, keepdims=False)
    mods = mod_mine.transpose(1, 0, 2).reshape(L, N_MOD, 1, D_MODEL)

    tabs = _rope_tables(positions[0])

    xc, saved, layer_w = x[0], [], [layer_weights(0, gathered0)]
    for l in range(L):
        xc, sv, gathered = _layer_fwd(xc, layer_w[l], mods[l], tabs, my_halves(l + 1) if l + 1 < L else None)
        saved.append(sv)
        if l + 1 < L:
            layer_w.append(layer_weights(l + 1, gathered))
    dxc, dg_final, loss_part = _final_loss(xc, loss_target[0], g_final[None, :], "final_norm_loss")
    loss = lax.psum(loss_part[0, 0], ("x", "y", "c"))
    core = jnp.reshape(ci, (1,)).astype(jnp.int32)

    grads, reduced, above = [None] * L, [None] * L, None
    for l in reversed(range(L)):
        dxc, grads[l], reduced_above = _layer_bwd(dxc, layer_w[l], mods[l], tabs, saved[l], core, above)
        if above:
            reduced[l + 1] = reduced_above
        grads[l]["w_in"] = _unpad_w_in(grads[l]["w_in"])
        grads[l]["w_q_up"] = _unpad_w_q(grads[l]["w_q_up"])
        grads[l]["w_mla_out"] = _unpad_w_mla(grads[l]["w_mla_out"])
        above = [grads[l][n] if n in _BY_CHIP else _reshard(grads[l][n], n) for n in _SHARDED]
    from_sibling = _run_exchange(_swap_halves_exchange(above), "grads_swap_halves")
    pending = [_add_halves(d, r, core, "grads_add_halves") for d, r in zip(above, from_sibling)]
    from_chips = _run_exchange(_chip_scatter_exchange(pending), "grads_chip_scatter")
    my_sum = [_sum_slots(part, core, "grads_sum_chips") for part in from_chips]
    reduced[0] = _run_exchange(_sibling_gather_exchange(my_sum), "grads_sibling_gather")
    grad_x = dxc
    gw = {n: jnp.stack([reduced[l][i] for l in range(L)]) for i, n in enumerate(_SHARDED)}

    def row(v):
        return jnp.pad(v, ((0, 0), (0, D_MODEL - v.shape[1])))

    per_layer_rows = N_MOD + 4
    small = jnp.concatenate(
        [jnp.concatenate([grads[l]["dmod"], row(grads[l]["g_mix"]), row(grads[l]["g_mlp"]),
                          row(grads[l]["g_q"]), row(grads[l]["g_kv"])], axis=0) for l in range(L)]
        + [dg_final], axis=0)
    n_small = -(-small.shape[0] // 8) * 8
    small = jnp.pad(small, ((0, n_small - small.shape[0]), (0, 0)))
    small_all = _all_gather8([small], "gather_vector_grads")[0].reshape(8, n_small, D_MODEL)
    small_sum = _sum_blocks(small_all, "sum_vector_grads")
    lay = small_sum[:L * per_layer_rows].reshape(L, per_layer_rows, D_MODEL)
    g_b_ada = lay[:, :N_MOD].reshape(L, N_MOD * D_MODEL)
    g_g_mix, g_g_mlp = lay[:, N_MOD], lay[:, N_MOD + 1]
    g_g_q, g_g_kv = lay[:, N_MOD + 2, :Q_RANK], lay[:, N_MOD + 3, :KV_RANK]
    g_g_final = small_sum[L * per_layer_rows]
    dmod_all = small_all[:, :L * per_layer_rows].reshape(8, L, per_layer_rows, D_MODEL)[:, :, :N_MOD]
    dmod_all = dmod_all.reshape(8, L, N_MOD * D_MODEL)
    dmod_cols = lax.dynamic_slice_in_dim(dmod_all, chip * ada_cols, ada_cols, 2)
    dmod16 = jnp.concatenate([dmod_cols, jnp.zeros_like(dmod_cols)], axis=0)
    g_w_ada = jnp.stack([_matmul(c16, dmod16[:, l], ta=True, name="ada_dw") for l in range(L)])

    weights = dict(w_ada=w_ada, b_ada=b_ada, g_mix_norm=g_mix_norm, w_in=w_in, g_q_lat=g_q_lat, w_q_up=w_q_up,
                   g_kv_lat=g_kv_lat, w_kv_up=w_kv_up, w_sb_out=w_sb_out, w_mla_out=w_mla_out,
                   w_mix_out=w_mix_out, g_mlp_norm=g_mlp_norm, w_up=w_up, w_down=w_down, g_final=g_final)
    mom = dict(w_ada=(m_w_ada, v_w_ada), b_ada=(m_b_ada, v_b_ada), g_mix_norm=(m_g_mix_norm, v_g_mix_norm),
               w_in=(m_w_in, v_w_in), g_q_lat=(m_g_q_lat, v_g_q_lat), w_q_up=(m_w_q_up, v_w_q_up),
               g_kv_lat=(m_g_kv_lat, v_g_kv_lat), w_kv_up=(m_w_kv_up, v_w_kv_up),
               w_sb_out=(m_w_sb_out, v_w_sb_out), w_mla_out=(m_w_mla_out, v_w_mla_out),
               w_mix_out=(m_w_mix_out, v_w_mix_out), g_mlp_norm=(m_g_mlp_norm, v_g_mlp_norm),
               w_up=(m_w_up, v_w_up), w_down=(m_w_down, v_w_down), g_final=(m_g_final, v_g_final))
    gr = dict(gw, w_ada=g_w_ada, b_ada=g_b_ada, g_mix_norm=g_g_mix, g_q_lat=g_g_q, g_kv_lat=g_g_kv,
              g_mlp_norm=g_g_mlp, g_final=g_g_final)
    order = list(weights)
    deltas, new_m, new_v = [], [], []
    for n in order:
        wv, gv, (mv, vv) = weights[n], gr[n], mom[n]
        if wv.ndim == 1:
            d, nm, nv = (t[0] for t in _adamw(wv[None], gv[None], mv[None], vv[None], "adamw_" + n))
        else:
            d, nm, nv = _adamw(wv, gv, mv, vv, "adamw_" + n)
        deltas.append(d)
        new_m.append(nm)
        new_v.append(nv)
    return (loss, grad_x[None], *[gr[n] for n in order], *deltas, *new_m, *new_v)
```

```python
from typing import Any, Callable, Mapping, NamedTuple, Sequence

import jax
import jax.numpy as jnp
from jax import lax
from jax.experimental import pallas as pl
from jax.experimental.pallas import tpu as pltpu

F32 = jnp.float32
BF16 = jnp.bfloat16
MESH = pl.DeviceIdType.MESH

D_MODEL = 1024
N_HEADS = 8
SB_DIM = 64
SB_WIDTH = 512
Q_RANK = 384
KV_RANK = 256
ROPE_DIM = 32
NOPE_DIM = 64
QK_DIM = 96
D_FF = 4096
N_MOD = 6
EPS = 1e-6
ROPE_THETA = 10000.0
SB_SCALE = SB_DIM ** -0.5
MLA_SCALE = QK_DIM ** -0.5
ADAM_LR, ADAM_B1, ADAM_B2, ADAM_EPS, ADAM_WD, ADAM_STEP = 0.001, 0.9, 0.999, 1e-08, 0.01, 10

LANE = 128
IN_PAD = 4352
COL_GATE_SB, COL_GATE_MLA, COL_QSB, COL_KSB, COL_VSB, COL_KVLAT, COL_QLAT, COL_KROPE = (
    0, 1024, 2048, 2560, 3072, 3584, 3840, 4224)
ROPE_LANE0 = 64
VMEM_LIMIT = 48 * 1024 * 1024
NEG_BIG = -1e30


def _cp(*sem):
    return pltpu.CompilerParams(dimension_semantics=sem, vmem_limit_bytes=VMEM_LIMIT)


def _tile(n, prefs):
    for t in prefs:
        if t <= n and n % t == 0:
            return t
    return n


def _dot(a, b, dims):
    return lax.dot_general(a, b, (dims, ((), ())), preferred_element_type=F32)


def _nn(a, b):
    return _dot(a, b, ((1,), (0,)))


def _nt(a, b):
    return _dot(a, b, ((1,), (1,)))


def _tn(a, b):
    return _dot(a, b, ((0,), (0,)))


def _sharded_dims(shape, kind):
    n, r, cs = shape
    return (n * r, cs) if kind == "row" else (r, n * cs)


def _sharded_spec(shape, kind, t_rows, t_cols, tile_of):
    _, r, cs = shape
    if kind == "row":
        assert r % t_rows == 0, (shape, t_rows)
        per = r // t_rows

        def index(i, j, k):
            tr, tc = tile_of(i, j, k)
            return tr // per, tr % per, tc
    else:
        assert cs % t_cols == 0, (shape, t_cols)
        per = cs // t_cols

        def index(i, j, k):
            tr, tc = tile_of(i, j, k)
            return tc // per, tr, tc % per
    return pl.BlockSpec((None, t_rows, t_cols), index)


def _matmul(a, b, *, ta=False, tb=False, out_dtype=F32, b_sharded=None, out_sharded=None, epilogue=None,
            extra=(), exs=None, name):
    (K, M) = a.shape if ta else a.shape[::-1]
    b_dims = _sharded_dims(b.shape, b_sharded) if b_sharded else b.shape
    (N, Kb) = b_dims if tb else b_dims[::-1]
    assert K == Kb, (a.shape, b.shape, ta, tb)
    tm = _tile(M, (512, 384, 256, 128))
    tn = _tile(N, (1024, 2176, 768, 512, 384, 256, 128))
    tk = _tile(K, (1024, 2176, 768, 512, 384, 256, 128))
    nk = K // tk
    dims = ((0 if ta else 1,), (1 if tb else 0,))

    out_dtypes = out_dtype if isinstance(out_dtype, tuple) else (out_dtype,)
    n_extra, n_o = len(extra), len(out_dtypes)

    def body(a_ref, b_ref, *rest):
        extra_refs, o_refs, acc = rest[:n_extra], rest[n_extra:n_extra + n_o], rest[n_extra + n_o:]
        prod = _dot(a_ref[...].astype(BF16), b_ref[...].astype(BF16), dims)

        def write(total):
            vals = epilogue(total, *[r[...] for r in extra_refs]) if epilogue else (total,)
            for o_ref, val, dt in zip(o_refs, vals, out_dtypes):
                o_ref[...] = val.astype(dt)

        if nk == 1:
            write(prod)
            return
        acc_ref, = acc
        k = pl.program_id(2)

        @pl.when(k == 0)
        def _():
            acc_ref[...] = prod

        @pl.when(k > 0)
        def _():
            acc_ref[...] += prod

        @pl.when(k == nk - 1)
        def _():
            write(acc_ref[...])

    a_spec = (pl.BlockSpec((tk, tm), lambda i, j, k: (k, i)) if ta
              else pl.BlockSpec((tm, tk), lambda i, j, k: (i, k)))
    if b_sharded:
        b_spec = (_sharded_spec(b.shape, b_sharded, tn, tk, lambda i, j, k: (j, k)) if tb
                  else _sharded_spec(b.shape, b_sharded, tk, tn, lambda i, j, k: (k, j)))
    else:
        b_spec = (pl.BlockSpec((tn, tk), lambda i, j, k: (j, k)) if tb
                  else pl.BlockSpec((tk, tn), lambda i, j, k: (k, j)))
    tile = pl.BlockSpec((tm, tn), lambda i, j, k: (i, j))
    if out_sharded:
        kind, shape = out_sharded
        assert _sharded_dims(shape, kind) == (M, N) and n_o == 1, (shape, kind, M, N)
        out_specs = [_sharded_spec(shape, kind, tm, tn, lambda i, j, k: (i, j))]
        out_shape = [jax.ShapeDtypeStruct(shape, out_dtypes[0])]
    else:
        out_specs = [tile] * n_o
        out_shape = [jax.ShapeDtypeStruct((M, N), dt) for dt in out_dtypes]
    grid = (M // tm, N // tn, nk)
    scratch = [pltpu.VMEM((tm, tn), F32)] if nk > 1 else []
    ins = [a, b, *extra]
    body, more, split = _carry(exs, body, len(ins), n_o, len(scratch), grid)
    own, carried = split(pl.pallas_call(
        body, grid=grid, in_specs=[a_spec, b_spec] + [tile] * n_extra + more["in_specs"],
        out_specs=out_specs + more["out_specs"], out_shape=out_shape + more["out_shape"],
        scratch_shapes=scratch + more["scratch"], input_output_aliases=more["aliases"],
        compiler_params=_cp(*(("arbitrary",) * 3 if exs else ("parallel", "parallel", "arbitrary"))),
        name=name)(*ins, *more["ins"]))
    result = own[0] if n_o == 1 else tuple(own)
    return (result, carried) if exs else result


def _rows(ts, w, col=0):
    return pl.BlockSpec((ts, w), lambda i: (i, col))


def _vec(w):
    return pl.BlockSpec((1, w), lambda i: (0, 0))


def _ts(S):
    return _tile(S, (256, 128))


def _attn_tile(S):
    return _tile(S, (512, 256, 128))


def _rms(x):
    return lax.rsqrt(jnp.mean(x * x, axis=-1, keepdims=True) + EPS)


def _colsum(x):
    return jnp.sum(x, axis=0, keepdims=True)


def _normmod_fwd(x, g, sc, sh, name):
    S, W = x.shape
    ts = _ts(S)

    def body(x_ref, g_ref, sc_ref, sh_ref, h_ref):
        xv = x_ref[...]
        h_ref[...] = ((xv * _rms(xv)) * g_ref[...] * (1.0 + sc_ref[...]) + sh_ref[...]).astype(BF16)

    return pl.pallas_call(
        body, grid=(S // ts,), in_specs=[_rows(ts, W), _vec(W), _vec(W), _vec(W)],
        out_specs=_rows(ts, W), out_shape=jax.ShapeDtypeStruct((S, W), BF16),
        compiler_params=_cp("parallel"), name=name)(x, g, sc, sh)


def _normmod_bwd(x, dh, g, sc, dres, name):
    S, W = x.shape
    ts = _ts(S)

    def body(x_ref, dh_ref, g_ref, sc_ref, dres_ref, dx_ref, dsh_ref, dsc_ref, dg_ref):
        @pl.when(pl.program_id(0) == 0)
        def _():
            dsh_ref[...] = jnp.zeros_like(dsh_ref)
            dsc_ref[...] = jnp.zeros_like(dsc_ref)
            dg_ref[...] = jnp.zeros_like(dg_ref)

        xv, dh_v, gv = x_ref[...], dh_ref[...], g_ref[...]
        r = _rms(xv)
        y = xv * r
        dn = dh_v * (1.0 + sc_ref[...])
        dy = dn * gv
        dx_ref[...] = dres_ref[...] + r * (dy - y * jnp.mean(dy * y, axis=-1, keepdims=True))
        dsh_ref[...] += _colsum(dh_v)
        dsc_ref[...] += _colsum(dh_v * y * gv)
        dg_ref[...] += _colsum(dn * y)

    vec_out = jax.ShapeDtypeStruct((1, W), F32)
    return pl.pallas_call(
        body, grid=(S // ts,),
        in_specs=[_rows(ts, W), _rows(ts, W), _vec(W), _vec(W), _rows(ts, W)],
        out_specs=[_rows(ts, W), _vec(W), _vec(W), _vec(W)],
        out_shape=[jax.ShapeDtypeStruct((S, W), F32), vec_out, vec_out, vec_out],
        compiler_params=_cp("arbitrary"), name=name)(x, dh, g, sc, dres)


def _rmsnorm_fwd(p, width, col, g, name):
    S = p.shape[0]
    ts = _ts(S)

    def body(x_ref, g_ref, y_ref):
        xv = x_ref[...]
        y_ref[...] = ((xv * _rms(xv)) * g_ref[...]).astype(BF16)

    return pl.pallas_call(
        body, grid=(S // ts,), in_specs=[_rows(ts, width, col), _vec(width)],
        out_specs=_rows(ts, width), out_shape=jax.ShapeDtypeStruct((S, width), BF16),
        compiler_params=_cp("parallel"), name=name)(p, g)


def _rmsnorm_bwd(p, width, col, dn, g, name):
    S = p.shape[0]
    ts = _ts(S)

    def body(x_ref, dn_ref, g_ref, dx_ref, dg_ref):
        @pl.when(pl.program_id(0) == 0)
        def _():
            dg_ref[...] = jnp.zeros_like(dg_ref)

        xv, dn_v = x_ref[...], dn_ref[...]
        r = _rms(xv)
        y = xv * r
        dy = dn_v * g_ref[...]
        dx_ref[...] = r * (dy - y * jnp.mean(dy * y, axis=-1, keepdims=True))
        dg_ref[...] += _colsum(dn_v * y)

    return pl.pallas_call(
        body, grid=(S // ts,), in_specs=[_rows(ts, width, col), _rows(ts, width), _vec(width)],
        out_specs=[_rows(ts, width), _vec(width)],
        out_shape=[jax.ShapeDtypeStruct((S, width), F32), jax.ShapeDtypeStruct((1, width), F32)],
        compiler_params=_cp("arbitrary"), name=name)(p, dn, g)


def _rope_rot(t, c, s1, s2):
    return t * c + pltpu.roll(t, LANE - 16, 1) * s1 + pltpu.roll(t, 16, 1) * s2


def _rope_rot_t(d, c, s1, s2):
    return d * c + pltpu.roll(d * s1, 16, 1) + pltpu.roll(d * s2, LANE - 16, 1)


def _rope_fwd(qp, p, tabs, name):
    S = qp.shape[0]
    ts = _ts(S)
    W = N_HEADS * LANE

    def body(q_ref, kr_ref, c_ref, s1_ref, s2_ref, qr_ref, kpe_ref):
        c, s1, s2 = c_ref[...], s1_ref[...], s2_ref[...]
        for h in range(N_HEADS):
            sl = slice(h * LANE, (h + 1) * LANE)
            qr_ref[:, sl] = _rope_rot(q_ref[:, sl], c, s1, s2).astype(BF16)
        kpe_ref[...] = _rope_rot(kr_ref[...], c, s1, s2).astype(BF16)

    tab = _rows(ts, LANE)
    return pl.pallas_call(
        body, grid=(S // ts,), in_specs=[_rows(ts, W), _rows(ts, LANE, COL_KROPE // LANE), tab, tab, tab],
        out_specs=[_rows(ts, W), _rows(ts, LANE)],
        out_shape=[jax.ShapeDtypeStruct((S, W), BF16), jax.ShapeDtypeStruct((S, LANE), BF16)],
        compiler_params=_cp("parallel"), name=name)(qp, p, *tabs)


def _rope_bwd(dqr, dkpe_heads, tabs, name):
    S = dqr.shape[0]
    ts = _ts(S)
    W = N_HEADS * LANE

    def body(dq_ref, dk_ref, c_ref, s1_ref, s2_ref, dqp_ref, dkr_ref):
        c, s1, s2 = c_ref[...], s1_ref[...], s2_ref[...]
        dk = dk_ref[:, 0:LANE]
        for h in range(N_HEADS):
            sl = slice(h * LANE, (h + 1) * LANE)
            dqp_ref[:, sl] = _rope_rot_t(dq_ref[:, sl], c, s1, s2).astype(BF16)
            if h:
                dk = dk + dk_ref[:, sl]
        dkr_ref[...] = _rope_rot_t(dk, c, s1, s2)

    tab = _rows(ts, LANE)
    return pl.pallas_call(
        body, grid=(S // ts,), in_specs=[_rows(ts, W), _rows(ts, W), tab, tab, tab],
        out_specs=[_rows(ts, W), _rows(ts, LANE)],
        out_shape=[jax.ShapeDtypeStruct((S, W), BF16), jax.ShapeDtypeStruct((S, LANE), F32)],
        compiler_params=_cp("parallel"), name=name)(dqr, dkpe_heads, *tabs)


def _merge_fwd(p, o_sb, o_mla, name):
    S, W = o_sb.shape
    ts = _ts(S)

    def body(gs_ref, gm_ref, a_ref, b_ref, m_ref):
        m_ref[...] = (jax.nn.sigmoid(gs_ref[...]) * a_ref[...]
                      + jax.nn.sigmoid(gm_ref[...]) * b_ref[...]).astype(BF16)

    return pl.pallas_call(
        body, grid=(S // ts,),
        in_specs=[_rows(ts, W, COL_GATE_SB // W), _rows(ts, W, COL_GATE_MLA // W), _rows(ts, W), _rows(ts, W)],
        out_specs=_rows(ts, W), out_shape=jax.ShapeDtypeStruct((S, W), BF16),
        compiler_params=_cp("parallel"), name=name)(p, p, o_sb, o_mla)


def _merge_bwd(p, o_sb, o_mla, dm, name):
    S, W = o_sb.shape
    ts = _ts(S)

    def body(gs_ref, gm_ref, a_ref, b_ref, dm_ref, da_ref, db_ref, dgs_ref, dgm_ref):
        dmv = dm_ref[...]
        sa, sb = jax.nn.sigmoid(gs_ref[...]), jax.nn.sigmoid(gm_ref[...])
        da_ref[...] = (dmv * sa).astype(BF16)
        db_ref[...] = (dmv * sb).astype(BF16)
        dgs_ref[...] = dmv * a_ref[...] * sa * (1.0 - sa)
        dgm_ref[...] = dmv * b_ref[...] * sb * (1.0 - sb)

    row = _rows(ts, W)
    return pl.pallas_call(
        body, grid=(S // ts,),
        in_specs=[_rows(ts, W, COL_GATE_SB // W), _rows(ts, W, COL_GATE_MLA // W), row, row, row],
        out_specs=[row, row, row, row],
        out_shape=[jax.ShapeDtypeStruct((S, W), BF16), jax.ShapeDtypeStruct((S, W), BF16),
                   jax.ShapeDtypeStruct((S, W), F32), jax.ShapeDtypeStruct((S, W), F32)],
        compiler_params=_cp("parallel"), name=name)(p, p, o_sb, o_mla, dm)


def _res_fwd(x, y, gate, name):
    S, W = x.shape
    ts = _ts(S)

    def body(x_ref, y_ref, g_ref, o_ref):
        o_ref[...] = x_ref[...] + g_ref[...] * y_ref[...]

    return pl.pallas_call(
        body, grid=(S // ts,), in_specs=[_rows(ts, W), _rows(ts, W), _vec(W)], out_specs=_rows(ts, W),
        out_shape=jax.ShapeDtypeStruct((S, W), F32), compiler_params=_cp("parallel"), name=name)(x, y, gate)


def _res_bwd(dx, y, gate, name):
    S, W = dx.shape
    ts = _ts(S)

    def body(dx_ref, y_ref, g_ref, dy_ref, dg_ref):
        @pl.when(pl.program_id(0) == 0)
        def _():
            dg_ref[...] = jnp.zeros_like(dg_ref)

        dxv = dx_ref[...]
        dy_ref[...] = (g_ref[...] * dxv).astype(BF16)
        dg_ref[...] += _colsum(dxv * y_ref[...])

    return pl.pallas_call(
        body, grid=(S // ts,), in_specs=[_rows(ts, W), _rows(ts, W), _vec(W)],
        out_specs=[_rows(ts, W), _vec(W)],
        out_shape=[jax.ShapeDtypeStruct((S, W), BF16), jax.ShapeDtypeStruct((1, W), F32)],
        compiler_params=_cp("arbitrary"), name=name)(dx, y, gate)


def _final_loss(x, target, g, name):
    S, W = x.shape
    ts = _ts(S)

    def body(x_ref, t_ref, g_ref, dx_ref, dg_ref, loss_ref):
        @pl.when(pl.program_id(0) == 0)
        def _():
            dg_ref[...] = jnp.zeros_like(dg_ref)
            loss_ref[...] = jnp.zeros_like(loss_ref)

        xv, gv = x_ref[...], g_ref[...]
        r = _rms(xv)
        y = xv * r
        err = y * gv - t_ref[...]
        loss_ref[...] += jnp.full((1, LANE), 0.5 * jnp.sum(jnp.mean(err * err, axis=-1)), F32)
        dout = err * (1.0 / W)
        dy = dout * gv
        dx_ref[...] = r * (dy - y * jnp.mean(dy * y, axis=-1, keepdims=True))
        dg_ref[...] += _colsum(dout * y)

    return pl.pallas_call(
        body, grid=(S // ts,), in_specs=[_rows(ts, W), _rows(ts, W), _vec(W)],
        out_specs=[_rows(ts, W), _vec(W), _vec(LANE)],
        out_shape=[jax.ShapeDtypeStruct((S, W), F32), jax.ShapeDtypeStruct((1, W), F32),
                   jax.ShapeDtypeStruct((1, LANE), F32)],
        compiler_params=_cp("arbitrary"), name=name)(x, target, g)


def _silu(c, name):
    def body(c_ref, o_ref):
        cv = c_ref[...]
        o_ref[...] = cv * jax.nn.sigmoid(cv)

    return pl.pallas_call(body, out_shape=jax.ShapeDtypeStruct(c.shape, F32), name=name)(c)


def _bias_add(a, b, name):
    def body(a_ref, b_ref, o_ref):
        o_ref[...] = a_ref[...] + b_ref[...]

    return pl.pallas_call(body, out_shape=jax.ShapeDtypeStruct(a.shape, F32), name=name)(a, b)


def _sum_blocks(xs, name):
    n = xs.shape[0]

    def body(x_ref, o_ref):
        acc = x_ref[0]
        for d in range(1, n):
            acc = acc + x_ref[d]
        o_ref[...] = acc

    return pl.pallas_call(body, out_shape=jax.ShapeDtypeStruct(xs.shape[1:], F32), name=name)(xs)


def _adamw(w, g, m, v, name):
    shape = w.shape
    cols = shape[-1]
    w2, g2, m2, v2 = (t.reshape(-1, cols) for t in (w, g, m, v))
    rows = w2.shape[0]
    tr = _tile(rows, (128,))
    c1 = 1.0 - ADAM_B1 ** ADAM_STEP
    c2 = 1.0 - ADAM_B2 ** ADAM_STEP

    def body(w_ref, g_ref, m_ref, v_ref, d_ref, nm_ref, nv_ref):
        gv = g_ref[...]
        nm = ADAM_B1 * m_ref[...] + (1.0 - ADAM_B1) * gv
        nv = ADAM_B2 * v_ref[...] + (1.0 - ADAM_B2) * (gv * gv)
        d_ref[...] = -ADAM_LR * ((nm / c1) / (jnp.sqrt(nv / c2) + ADAM_EPS) + ADAM_WD * w_ref[...])
        nm_ref[...] = nm
        nv_ref[...] = nv

    spec = pl.BlockSpec((tr, cols), lambda i: (i, 0))
    out = jax.ShapeDtypeStruct((rows, cols), F32)
    d, nm, nv = pl.pallas_call(
        body, grid=(rows // tr,), in_specs=[spec] * 4, out_specs=[spec] * 3, out_shape=[out] * 3,
        compiler_params=_cp("parallel"), name=name)(w2, g2, m2, v2)
    return d.reshape(shape), nm.reshape(shape), nv.reshape(shape)


def _split_dot(x, tri):
    hi = x.astype(BF16)
    lo = (x - hi.astype(F32)).astype(BF16)
    return _nn(hi, tri) + _nn(lo, tri)


def _tri_cumsum(x, tri, later):
    h = x.shape[1] // 2
    first, second = x[:, :h], x[:, h:]
    sum_first = jnp.sum(first, axis=1, keepdims=True)
    sum_second = jnp.sum(second, axis=1, keepdims=True)
    run_first, run_second = _split_dot(first, tri), _split_dot(second, tri)
    if later:
        run_first = run_first + sum_second
    else:
        run_second = run_second + sum_first
    return jnp.concatenate([run_first, run_second], axis=1), sum_first + sum_second


def _sb_logs(z):
    soft = jnp.log(1.0 + jnp.exp(-jnp.abs(z)))
    return jnp.minimum(z, 0.0) - soft, -jnp.maximum(z, 0.0) - soft


def _attn_call(body, grid, ins, in_specs, out_specs, out_shape, scratch, exs, name):
    body, extra, split = _carry(exs, body, len(ins), len(out_shape), len(scratch), grid)
    return split(pl.pallas_call(
        body, grid=grid, in_specs=in_specs + extra["in_specs"], out_specs=out_specs + extra["out_specs"],
        out_shape=out_shape + extra["out_shape"], scratch_shapes=scratch + extra["scratch"],
        input_output_aliases=extra["aliases"], compiler_params=_cp("arbitrary", "arbitrary"),
        name=name)(*ins, *extra["ins"]))


def _sb_fwd(p, name, exs=None):
    S = p.shape[0]
    t = _attn_tile(S)
    qb, kb, vb = COL_QSB // LANE, COL_KSB // LANE, COL_VSB // LANE

    def body(q_ref, k_ref, v_ref, o_ref, cb_ref, acc_ref):
        i = pl.program_id(1)
        lane = lax.broadcasted_iota(jnp.int32, (t, LANE), 1)
        rows = lax.broadcasted_iota(jnp.int32, (t, t), 0)
        cols = lax.broadcasted_iota(jnp.int32, (t, t), 1)
        half_r = lax.broadcasted_iota(jnp.int32, (t // 2, t // 2), 0)
        half_c = lax.broadcasted_iota(jnp.int32, (t // 2, t // 2), 1)
        after = jnp.where(half_r > half_c, 1.0, 0.0).astype(BF16)
        diag = cols < rows
        q = q_ref[...] * SB_SCALE
        acc_ref[...] = jnp.zeros_like(acc_ref)
        cb_ref[...] = jnp.zeros_like(cb_ref)
        hms = [(lane >= SB_DIM * h) & (lane < SB_DIM * (h + 1)) for h in range(2)]
        qhs = [jnp.where(hm, q, 0.0).astype(BF16) for hm in hms]

        def step(j, cs, masked):
            rows_j = pl.ds(pl.multiple_of(j * t, t), t)
            kj = k_ref[rows_j, :].astype(BF16)
            vf = v_ref[rows_j, :]
            out, pv = [], None
            for h in range(2):
                ls, lf = _sb_logs(_nt(qhs[h], kj))
                if masked:
                    lf = jnp.where(diag, lf, 0.0)
                survive, total = _tri_cumsum(lf, after, True)
                a = jnp.exp(ls + survive + cs[h])
                if masked:
                    a = jnp.where(diag, a, 0.0)
                term = _nn(a.astype(BF16), jnp.where(hms[h], vf, 0.0).astype(BF16))
                pv = term if pv is None else pv + term
                cb_ref[h] = jnp.where(lane == j, cs[h], cb_ref[h])
                out.append(cs[h] + total)
            acc_ref[...] += pv
            return tuple(out)

        zero = jnp.zeros((t, 1), F32)
        cs = step(i, (zero, zero), True)
        lax.fori_loop(0, i, lambda it, cs: step(i - 1 - it, cs, False), cs)
        o_ref[...] = acc_ref[...].astype(BF16)

    return _attn_call(
        body, (SB_WIDTH // LANE, S // t), [p, p, p],
        [pl.BlockSpec((t, LANE), lambda hp, i: (i, qb + hp)),
         pl.BlockSpec((S, LANE), lambda hp, i: (0, kb + hp)),
         pl.BlockSpec((S, LANE), lambda hp, i: (0, vb + hp))],
        [pl.BlockSpec((t, LANE), lambda hp, i: (i, hp)),
         pl.BlockSpec((2, t, LANE), lambda hp, i: (hp, i, 0))],
        [jax.ShapeDtypeStruct((S, SB_WIDTH), BF16), jax.ShapeDtypeStruct((N_HEADS, S, LANE), F32)],
        [pltpu.VMEM((t, LANE), F32)], exs, name)


def _sb_bwd(p, do, cb, name, exs=None):
    S = p.shape[0]
    t = _attn_tile(S)
    qb, kb, vb = COL_QSB // LANE, COL_KSB // LANE, COL_VSB // LANE

    def body(q_ref, k_ref, v_ref, do_ref, cb_ref, dq_ref, dk_ref, dv_ref, acc_ref):
        i = pl.program_id(1)

        @pl.when(i == 0)
        def _():
            dk_ref[...] = jnp.zeros_like(dk_ref)
            dv_ref[...] = jnp.zeros_like(dv_ref)

        lane = lax.broadcasted_iota(jnp.int32, (t, LANE), 1)
        rows = lax.broadcasted_iota(jnp.int32, (t, t), 0)
        cols = lax.broadcasted_iota(jnp.int32, (t, t), 1)
        half_r = lax.broadcasted_iota(jnp.int32, (t // 2, t // 2), 0)
        half_c = lax.broadcasted_iota(jnp.int32, (t // 2, t // 2), 1)
        after = jnp.where(half_r > half_c, 1.0, 0.0).astype(BF16)
        before = jnp.where(half_r < half_c, 1.0, 0.0).astype(BF16)
        diag = cols < rows
        q = q_ref[...] * SB_SCALE
        dov = do_ref[...]
        acc_ref[...] = jnp.zeros_like(acc_ref)
        hms = [(lane >= SB_DIM * h) & (lane < SB_DIM * (h + 1)) for h in range(2)]
        qhs = [jnp.where(hm, q, 0.0).astype(BF16) for hm in hms]
        dohs = [jnp.where(hm, dov, 0.0).astype(BF16) for hm in hms]

        def step(j, fs, masked):
            rows_j = pl.ds(pl.multiple_of(j * t, t), t)
            kf = k_ref[rows_j, :]
            kj = kf.astype(BF16)
            vj = v_ref[rows_j, :].astype(BF16)
            out, dq_t, dk_t, dv_t = [], None, None, None
            for h in range(2):
                ls, lf = _sb_logs(_nt(qhs[h], kj))
                if masked:
                    lf = jnp.where(diag, lf, 0.0)
                c = jnp.sum(jnp.where(lane == j, cb_ref[h], 0.0), axis=1, keepdims=True)
                a = jnp.exp(ls + _tri_cumsum(lf, after, True)[0] + c)
                if masked:
                    a = jnp.where(diag, a, 0.0)
                dl = _nt(dohs[h], vj) * a
                sg = jnp.exp(ls)
                earlier, total = _tri_cumsum(dl, before, False)
                dz = dl * (1.0 - sg) - sg * (earlier + fs[h])
                if masked:
                    dz = jnp.where(diag, dz, 0.0)
                dzb = dz.astype(BF16)
                terms = (_nn(dzb, jnp.where(hms[h], kf, 0.0).astype(BF16)), _tn(dzb, qhs[h]),
                         _tn(a.astype(BF16), dohs[h]))
                dq_t, dk_t, dv_t = terms if dq_t is None else (dq_t + terms[0], dk_t + terms[1], dv_t + terms[2])
                out.append(fs[h] + total)
            acc_ref[...] += dq_t
            dk_ref[rows_j, :] += dk_t
            dv_ref[rows_j, :] += dv_t
            return tuple(out)

        zero = jnp.zeros((t, 1), F32)
        fs = lax.fori_loop(0, i, lambda j, fs: step(j, fs, False), (zero, zero))
        step(i, fs, True)
        dq_ref[...] = acc_ref[...] * SB_SCALE

    col = lambda hp, i: (0, hp)
    out = jax.ShapeDtypeStruct((S, SB_WIDTH), F32)
    return _attn_call(
        body, (SB_WIDTH // LANE, S // t), [p, p, p, do, cb],
        [pl.BlockSpec((t, LANE), lambda hp, i: (i, qb + hp)),
         pl.BlockSpec((S, LANE), lambda hp, i: (0, kb + hp)),
         pl.BlockSpec((S, LANE), lambda hp, i: (0, vb + hp)),
         pl.BlockSpec((t, LANE), lambda hp, i: (i, hp)),
         pl.BlockSpec((2, t, LANE), lambda hp, i: (hp, i, 0))],
        [pl.BlockSpec((t, LANE), lambda hp, i: (i, hp)), pl.BlockSpec((S, LANE), col), pl.BlockSpec((S, LANE), col)],
        [out, out, out], [pltpu.VMEM((t, LANE), F32)], exs, name)


def _mla_fwd(qr, kv, kpe, name, exs=None):
    S = qr.shape[0]
    t = _attn_tile(S)

    def body(q_ref, kv_ref, kpe_ref, o_ref, lse_ref, acc_ref, m_ref):
        i = pl.program_id(1)
        low = lax.broadcasted_iota(jnp.int32, (t, LANE), 1) < NOPE_DIM
        rows = lax.broadcasted_iota(jnp.int32, (t, t), 0)
        cols = lax.broadcasted_iota(jnp.int32, (t, t), 1)
        causal = cols <= rows
        one = jnp.ones((t, LANE), BF16)
        heads = [slice(h * LANE, (h + 1) * LANE) for h in range(2)]
        qs = [q_ref[:, sl] for sl in heads]
        acc_ref[...] = jnp.zeros_like(acc_ref)
        m_ref[...] = jnp.full_like(m_ref, NEG_BIG)

        def step(j, masked):
            rows_j = pl.ds(pl.multiple_of(j * t, t), t)
            kpe_j = kpe_ref[rows_j, :]
            for h, sl in enumerate(heads):
                kvj = kv_ref[rows_j, sl]
                z = _nt(qs[h], jnp.where(low, kvj, kpe_j)) * MLA_SCALE
                if masked:
                    z = jnp.where(causal, z, NEG_BIG)
                m_old = m_ref[h]
                m_new = jnp.maximum(m_old, jnp.max(z, axis=1, keepdims=True))
                pr = jnp.exp(z - m_new)
                acc_ref[:, sl] = jnp.exp(m_old - m_new) * acc_ref[:, sl] + _nn(
                    pr.astype(BF16), jnp.where(low, one, kvj))
                m_ref[h] = m_new

        def loop(j, carry):
            step(j, False)
            return carry

        lax.fori_loop(0, i, loop, 0)
        step(i, True)
        for h, sl in enumerate(heads):
            acc = acc_ref[:, sl]
            den = acc[:, 0:1]
            o_ref[:, sl] = jnp.where(low, 0.0, acc / den).astype(BF16)
            lse_ref[h] = jnp.broadcast_to(m_ref[h] + jnp.log(den), (t, LANE))

    pair = 2 * LANE
    return _attn_call(
        body, (N_HEADS // 2, S // t), [qr, kv, kpe],
        [pl.BlockSpec((t, pair), lambda hp, i: (i, hp)),
         pl.BlockSpec((S, pair), lambda hp, i: (0, hp)),
         pl.BlockSpec((S, LANE), lambda hp, i: (0, 0))],
        [pl.BlockSpec((t, pair), lambda hp, i: (i, hp)), pl.BlockSpec((2, t, LANE), lambda hp, i: (hp, i, 0))],
        [jax.ShapeDtypeStruct((S, N_HEADS * LANE), BF16), jax.ShapeDtypeStruct((N_HEADS, S, LANE), F32)],
        [pltpu.VMEM((t, pair), F32), pltpu.VMEM((2, t, 1), F32)], exs, name)


def _mla_bwd(qr, kv, kpe, do, o, lse, name, exs=None):
    S = qr.shape[0]
    t = _attn_tile(S)

    def body(q_ref, kv_ref, kpe_ref, do_ref, o_ref, lse_ref, dq_ref, dkv_ref, dkpe_ref, acc_ref):
        i = pl.program_id(1)

        @pl.when(i == 0)
        def _():
            dkv_ref[...] = jnp.zeros_like(dkv_ref)
            dkpe_ref[...] = jnp.zeros_like(dkpe_ref)

        low = lax.broadcasted_iota(jnp.int32, (t, LANE), 1) < NOPE_DIM
        rows = lax.broadcasted_iota(jnp.int32, (t, t), 0)
        cols = lax.broadcasted_iota(jnp.int32, (t, t), 1)
        causal = cols <= rows
        heads = [slice(h * LANE, (h + 1) * LANE) for h in range(2)]
        qs = [q_ref[:, sl] for sl in heads]
        dovs = [do_ref[:, sl] for sl in heads]
        dobs = [d.astype(BF16) for d in dovs]
        deltas = [jnp.sum(dovs[h] * o_ref[:, sl].astype(F32), axis=1, keepdims=True) for h, sl in enumerate(heads)]
        lses = [lse_ref[h][:, 0:1] for h in range(2)]
        acc_ref[...] = jnp.zeros_like(acc_ref)

        def step(j, masked):
            rows_j = pl.ds(pl.multiple_of(j * t, t), t)
            kpe_j = kpe_ref[rows_j, :]
            for h, sl in enumerate(heads):
                kvj = kv_ref[rows_j, sl]
                kcat = jnp.where(low, kvj, kpe_j)
                z = _nt(qs[h], kcat) * MLA_SCALE
                if masked:
                    z = jnp.where(causal, z, NEG_BIG)
                pr = jnp.exp(z - lses[h])
                ds = (pr * (_nt(dobs[h], kvj) - deltas[h])).astype(BF16)
                acc_ref[:, sl] += _nn(ds, kcat)
                dkc = _tn(ds, qs[h]) * MLA_SCALE
                dkv_ref[rows_j, sl] += jnp.where(low, dkc, _tn(pr.astype(BF16), dobs[h]))
                dkpe_ref[rows_j, sl] += jnp.where(low, 0.0, dkc)

        def loop(j, carry):
            step(j, False)
            return carry

        lax.fori_loop(0, i, loop, 0)
        step(i, True)
        dq_ref[...] = acc_ref[...] * MLA_SCALE

    pair = 2 * LANE
    blk = pl.BlockSpec((t, pair), lambda hp, i: (i, hp))
    col = pl.BlockSpec((S, pair), lambda hp, i: (0, hp))
    out = jax.ShapeDtypeStruct((S, N_HEADS * LANE), F32)
    return _attn_call(
        body, (N_HEADS // 2, S // t), [qr, kv, kpe, do, o, lse],
        [blk, col, pl.BlockSpec((S, LANE), lambda hp, i: (0, 0)), blk, blk,
         pl.BlockSpec((2, t, LANE), lambda hp, i: (hp, i, 0))],
        [blk, col, col], [out, out, out], [pltpu.VMEM((t, pair), F32)], exs, name)


_ANY = pl.BlockSpec(memory_space=pl.ANY)


def _place():
    return lax.axis_index("x"), lax.axis_index("y"), lax.axis_index("c")


class _Exchange(NamedTuple):
    ins: Sequence[Any]
    outs: Sequence[Any]
    aliases: Mapping[int, int]
    n_remote: int
    n_local: int
    start: Callable
    finish: Callable


def _exchange_scratch(ex):
    return [pltpu.SemaphoreType.DMA((ex.n_remote,)), pltpu.SemaphoreType.DMA((ex.n_remote,)),
            pltpu.SemaphoreType.DMA((ex.n_local,))]


def _run_exchange(ex, name):
    n_in, n_out = len(ex.ins), len(ex.outs)

    def body(*refs):
        args = (refs[:n_in], refs[n_in:n_in + n_out], *refs[n_in + n_out:])
        ex.start(*args)
        ex.finish(*args)

    return pl.pallas_call(
        body, out_shape=list(ex.outs), in_specs=[_ANY] * n_in, out_specs=[_ANY] * n_out,
        scratch_shapes=_exchange_scratch(ex), input_output_aliases=dict(ex.aliases), name=name)(*ex.ins)


def _carry(exs, body, n_in, n_out, n_scratch, grid):
    exs = [ex for ex in (exs or []) if ex is not None]
    e_ins, e_outs = [len(ex.ins) for ex in exs], [len(ex.outs) for ex in exs]

    def take(refs, counts):
        groups = []
        for n in counts:
            groups.append(refs[:n])
            refs = refs[n:]
        return groups, refs

    def carried(*refs):
        own_in, refs = refs[:n_in], refs[n_in:]
        ex_in, refs = take(refs, e_ins)
        own_out, refs = refs[:n_out], refs[n_out:]
        ex_out, refs = take(refs, e_outs)
        own_scratch, refs = refs[:n_scratch], refs[n_scratch:]
        sems, _ = take(refs, [3] * len(exs))
        at = [pl.program_id(d) for d in range(len(grid))]
        first, last = at[0] == 0, at[0] == grid[0] - 1
        for d in range(1, len(grid)):
            first, last = first & (at[d] == 0), last & (at[d] == grid[d] - 1)

        @pl.when(first)
        def _():
            for e, ex in enumerate(exs):
                ex.start(ex_in[e], ex_out[e], *sems[e])

        body(*own_in, *own_out, *own_scratch)

        @pl.when(last)
        def _():
            for e, ex in enumerate(exs):
                ex.finish(ex_in[e], ex_out[e], *sems[e])

    aliases, i0, o0 = {}, n_in, n_out
    for ex in exs:
        aliases.update({i0 + i: o0 + o for i, o in ex.aliases.items()})
        i0, o0 = i0 + len(ex.ins), o0 + len(ex.outs)

    def split(res):
        groups, _ = take(list(res[n_out:]), e_outs)
        return list(res[:n_out]), groups

    extra = dict(
        ins=[a for ex in exs for a in ex.ins], in_specs=[_ANY] * sum(e_ins), out_specs=[_ANY] * sum(e_outs),
        out_shape=[o for ex in exs for o in ex.outs], scratch=[s for ex in exs for s in _exchange_scratch(ex)],
        aliases=aliases)
    return (carried if exs else body), extra, split


def _gather_exchange(arrs, phase="all"):
    n_t = len(arrs)
    ms = [a.shape[0] // (8 if phase == "b" else 1) for a in arrs]

    def plan(in_refs, out_refs, send_sems, recv_sems, local_sems):
        x, y, c = _place()
        me, sibling = (x, y, c), (x, y, 1 - c)
        chips = [(1 - x, y), (x, 1 - y), (1 - x, 1 - y)]

        def rows(ref, t, px, py, pc):
            return ref.at[pl.ds((4 * px + 2 * py + pc) * ms[t], ms[t]), :]

        def copy(t, k, block, to, src):
            return pltpu.make_async_remote_copy(
                src_ref=src, dst_ref=rows(out_refs[t], t, *block), send_sem=send_sems.at[7 * t + k],
                recv_sem=recv_sems.at[7 * t + k], device_id=to, device_id_type=MESH)

        mine, first, first_in, passed, passed_in = [], [], [], [], []
        for t in range(n_t):
            if phase != "b":
                mine.append(pltpu.make_async_copy(in_refs[t], rows(out_refs[t], t, *me), local_sems.at[t]))
                first.append(copy(t, 0, me, sibling, in_refs[t]))
                first_in.append(copy(t, 0, sibling, me, in_refs[t]))
                for j, chip in enumerate(chips):
                    first.append(copy(t, 1 + j, me, (*chip, c), in_refs[t]))
                    first_in.append(copy(t, 1 + j, (*chip, c), me, in_refs[t]))
            if phase != "a":
                held = in_refs[t] if phase == "b" else out_refs[t]
                for j, chip in enumerate(chips):
                    passed.append(copy(t, 4 + j, (*chip, c), sibling, rows(held, t, *chip, c)))
                    passed_in.append(copy(t, 4 + j, (*chip, 1 - c), me, rows(held, t, *chip, c)))
        return mine, first, first_in, passed, passed_in

    def start(*refs):
        mine, first, _, passed, _ = plan(*refs)
        for cp in mine + first + (passed if phase == "b" else []):
            cp.start()

    def finish(*refs):
        mine, first, first_in, passed, passed_in = plan(*refs)
        for cp in first_in:
            cp.wait_recv()
        if phase == "all":
            for cp in passed:
                cp.start()
        for cp in passed_in:
            cp.wait_recv()
        for cp in first + passed:
            cp.wait_send()
        for cp in mine:
            cp.wait()

    if phase == "b":
        outs = [jax.ShapeDtypeStruct(a.shape, a.dtype) for a in arrs]
        aliases = {t: t for t in range(n_t)}
    else:
        outs = [jax.ShapeDtypeStruct((8 * a.shape[0], a.shape[1]), a.dtype) for a in arrs]
        aliases = {}
    return _Exchange(list(arrs), outs, aliases, 7 * n_t, n_t, start, finish)


def _all_gather8(blks, name):
    return _run_exchange(_gather_exchange(blks), name)


def _swap_halves_exchange(gs):
    n_t = len(gs)

    def plan(g_refs, out_refs, send_sems, recv_sems, local_sems):
        x, y, c = _place()
        copies = []
        for t in range(n_t):
            m = gs[t].shape[1] // 2
            copies += [pltpu.make_async_remote_copy(
                src_ref=g_refs[t].at[s, pl.ds((1 - c) * m, m), :], dst_ref=out_refs[t].at[s],
                send_sem=send_sems.at[4 * t + s], recv_sem=recv_sems.at[4 * t + s], device_id=(x, y, 1 - c),
                device_id_type=MESH) for s in range(4)]
        return copies

    def start(*refs):
        for cp in plan(*refs):
            cp.start()

    def finish(*refs):
        for cp in plan(*refs):
            cp.wait()

    outs = [jax.ShapeDtypeStruct((4, g.shape[1] // 2, g.shape[2]), g.dtype) for g in gs]
    return _Exchange(list(gs), outs, {}, 4 * n_t, 1, start, finish)


def _chip_scatter_exchange(parts):
    n_t = len(parts)

    def plan(p_refs, out_refs, send_sems, recv_sems, local_sems):
        x, y, c = _place()
        mine = 2 * x + y
        chips = [(1 - x, y), (x, 1 - y), (1 - x, 1 - y)]

        def copy(t, j, src_slot, dst_slot):
            px, py = chips[j]
            return pltpu.make_async_remote_copy(
                src_ref=p_refs[t].at[src_slot], dst_ref=out_refs[t].at[dst_slot],
                send_sem=send_sems.at[3 * t + j], recv_sem=recv_sems.at[3 * t + j], device_id=(px, py, c),
                device_id_type=MESH)

        own = [pltpu.make_async_copy(p_refs[t].at[mine], out_refs[t].at[mine], local_sems.at[t])
               for t in range(n_t)]
        sends = [copy(t, j, 2 * px + py, mine) for t in range(n_t) for j, (px, py) in enumerate(chips)]
        arrivals = [copy(t, j, mine, 2 * px + py) for t in range(n_t) for j, (px, py) in enumerate(chips)]
        return own, sends, arrivals

    def start(*refs):
        own, sends, _ = plan(*refs)
        for cp in own + sends:
            cp.start()

    def finish(*refs):
        own, sends, arrivals = plan(*refs)
        for cp in arrivals:
            cp.wait_recv()
        for cp in sends:
            cp.wait_send()
        for cp in own:
            cp.wait()

    outs = [jax.ShapeDtypeStruct(p.shape, p.dtype) for p in parts]
    return _Exchange(list(parts), outs, {}, 3 * n_t, n_t, start, finish)


def _sibling_gather_exchange(bufs):
    n_t = len(bufs)

    def plan(b_refs, out_refs, send_sems, recv_sems, local_sems):
        x, y, c = _place()

        def copy(t, pc):
            m = bufs[t].shape[0] // 2
            half = pl.ds(pc * m, m)
            return pltpu.make_async_remote_copy(
                src_ref=b_refs[t].at[half, :], dst_ref=out_refs[t].at[half, :], send_sem=send_sems.at[t],
                recv_sem=recv_sems.at[t], device_id=(x, y, 1 - c), device_id_type=MESH)

        return [copy(t, c) for t in range(n_t)], [copy(t, 1 - c) for t in range(n_t)]

    def start(*refs):
        for cp in plan(*refs)[0]:
            cp.start()

    def finish(*refs):
        sends, arrivals = plan(*refs)
        for cp in arrivals:
            cp.wait_recv()
        for cp in sends:
            cp.wait_send()

    outs = [jax.ShapeDtypeStruct(b.shape, b.dtype) for b in bufs]
    return _Exchange(list(bufs), outs, {t: t for t in range(n_t)}, n_t, 1, start, finish)


def _add_halves(g, recv, c, name):
    n_slot, m2, n = g.shape
    m = m2 // 2
    tr = _tile(m, (512, 256, 192, 128, 16))

    def body(c_ref, g_ref, r_ref, o_ref):
        o_ref[...] = (g_ref[...] + r_ref[...]).astype(BF16)

    nb = m // tr
    return pl.pallas_call(
        body,
        grid_spec=pltpu.PrefetchScalarGridSpec(
            num_scalar_prefetch=1, grid=(n_slot, nb),
            in_specs=[pl.BlockSpec((1, tr, n), lambda s, i, c_ref: (s, c_ref[0] * nb + i, 0)),
                      pl.BlockSpec((1, tr, n), lambda s, i, c_ref: (s, i, 0))],
            out_specs=pl.BlockSpec((1, tr, n), lambda s, i, c_ref: (s, i, 0))),
        out_shape=jax.ShapeDtypeStruct((n_slot, m, n), BF16),
        compiler_params=_cp("parallel", "parallel"), name=name)(c, g, recv)


def _sum_slots(parts, c, name):
    n_slot, m, n = parts.shape
    tr = _tile(m, (512, 256, 192, 128, 16))
    nb = m // tr

    def body(c_ref, p_ref, o_ref):
        acc = p_ref[0].astype(F32)
        for s in range(1, n_slot):
            acc = acc + p_ref[s].astype(F32)
        o_ref[...] = acc

    return pl.pallas_call(
        body,
        grid_spec=pltpu.PrefetchScalarGridSpec(
            num_scalar_prefetch=1, grid=(nb,),
            in_specs=[pl.BlockSpec((n_slot, tr, n), lambda i, c_ref: (0, i, 0))],
            out_specs=pl.BlockSpec((tr, n), lambda i, c_ref: (c_ref[0] * nb + i, 0))),
        out_shape=jax.ShapeDtypeStruct((2 * m, n), F32),
        compiler_params=_cp("parallel"), name=name)(c, parts)


_SHARDED = ("w_in", "w_q_up", "w_kv_up", "w_sb_out", "w_mla_out", "w_mix_out", "w_up", "w_down")
_ROW_SHARDED = ("w_mix_out", "w_down")
_BY_CHIP = ("w_up", "w_down")


def _unshard(parts, name):
    n, r, cs = parts.shape
    if name in _ROW_SHARDED:
        return parts.reshape(n * r, cs)
    return parts.transpose(1, 0, 2).reshape(r, n * cs)


def _reshard(full, name, n=4):
    R, C = full.shape
    if name in _ROW_SHARDED:
        return full.reshape(n, R // n, C)
    return full.reshape(R, n, C // n).transpose(1, 0, 2)


def _pad_w_in(w):
    z = lambda k: jnp.zeros(w.shape[:-1] + (k,), w.dtype)
    return jnp.concatenate([
        w[..., 2208:3232], w[..., 3232:4256], w[..., 0:1536], w[..., 1920:2176], w[..., 1536:1920],
        z(ROPE_LANE0), w[..., 2176:2208], z(LANE - ROPE_LANE0 - ROPE_DIM)], axis=-1)


def _unpad_w_in(g):
    k0 = COL_KROPE + ROPE_LANE0
    return jnp.concatenate([
        g[..., COL_QSB:COL_KVLAT], g[..., COL_QLAT:COL_KROPE], g[..., COL_KVLAT:COL_QLAT],
        g[..., k0:k0 + ROPE_DIM], g[..., 0:COL_QSB]], axis=-1)


def _pad_w_q(w):
    r = w.shape[0]
    return jnp.pad(w.reshape(r, N_HEADS, QK_DIM), ((0, 0), (0, 0), (0, LANE - QK_DIM))).reshape(r, N_HEADS * LANE)


def _unpad_w_q(g):
    r = g.shape[0]
    return g.reshape(r, N_HEADS, LANE)[..., :QK_DIM].reshape(r, N_HEADS * QK_DIM)


def _pad_w_mla(w):
    n = w.shape[1]
    return jnp.pad(w.reshape(N_HEADS, NOPE_DIM, n), ((0, 0), (LANE - NOPE_DIM, 0), (0, 0))).reshape(
        N_HEADS * LANE, n)


def _unpad_w_mla(g):
    n = g.shape[1]
    return g.reshape(N_HEADS, LANE, n)[:, LANE - NOPE_DIM:, :].reshape(N_HEADS * NOPE_DIM, n)


def _rope_tables(positions):
    half = ROPE_DIM // 2
    inv_freq = 1.0 / (ROPE_THETA ** (jnp.arange(0, ROPE_DIM, 2, dtype=F32) / ROPE_DIM))
    ang = positions.astype(F32)[:, None] * inv_freq
    cos, sin = jnp.cos(ang), jnp.sin(ang)
    S = positions.shape[0]
    one = jnp.ones((S, ROPE_LANE0), F32)
    zero = lambda k: jnp.zeros((S, k), F32)
    tail = LANE - ROPE_LANE0 - ROPE_DIM
    c = jnp.concatenate([one, cos, cos, zero(tail)], axis=1)
    s1 = jnp.concatenate([zero(ROPE_LANE0), -sin, zero(half + tail)], axis=1)
    s2 = jnp.concatenate([zero(ROPE_LANE0 + half), sin, zero(tail)], axis=1)
    return c, s1, s2


def _layer_fwd(x, W, mod, tabs, next_blocks=None):
    sh1, sc1, gt1, sh2, sc2, gt2 = (mod[i] for i in range(N_MOD))
    n_small = len(_SHARDED) - len(_BY_CHIP)
    small, big = (next_blocks[:n_small], next_blocks[n_small:]) if next_blocks else (None, None)
    h1 = _normmod_fwd(x, W["g_mix"], sc1, sh1, "mix_norm_fwd")
    p = _matmul(h1, W["w_in"], name="in_proj")
    (osbh, cb), carried = _sb_fwd(p, "sb_attn_fwd", [_gather_exchange(small, "a")] if small else None)
    o_sb = _matmul(osbh, W["w_sb_out"], name="sb_out")
    qn = _rmsnorm_fwd(p, Q_RANK, COL_QLAT // Q_RANK, W["g_q"], "q_lat_norm_fwd")
    kvn = _rmsnorm_fwd(p, KV_RANK, COL_KVLAT // KV_RANK, W["g_kv"], "kv_lat_norm_fwd")
    qp = _matmul(qn, W["w_q_up"], name="q_up")
    kv = _matmul(kvn, W["w_kv_up"], out_dtype=BF16, name="kv_up")
    qr, kpe = _rope_fwd(qp, p, tabs, "rope_fwd")
    (omh, lse), carried = _mla_fwd(
        qr, kv, kpe, "mla_attn_fwd",
        [_gather_exchange(carried[0], "b"), _gather_exchange(big, "a")] if small else None)
    o_mla = _matmul(omh, W["w_mla_out"], name="mla_out")
    merged = _merge_fwd(p, o_sb, o_mla, "merge_fwd")
    y1 = _matmul(merged, W["w_mix_out"], name="mix_out")
    x1 = _res_fwd(x, y1, gt1, "mix_residual")
    h2 = _normmod_fwd(x1, W["g_mlp"], sc2, sh2, "mlp_norm_fwd")

    def sqrelu(t):
        r = jnp.maximum(t, 0.0)
        return t, r * r

    if small:
        (u, a), big_done = _matmul(h2, W["w_up"], b_sharded="col", out_dtype=(F32, BF16), epilogue=sqrelu,
                                   exs=[_gather_exchange(carried[1], "b")], name="mlp_up")
        gathered = list(carried[0]) + list(big_done[0])
    else:
        u, a = _matmul(h2, W["w_up"], b_sharded="col", out_dtype=(F32, BF16), epilogue=sqrelu, name="mlp_up")
        gathered = []
    y2 = _matmul(a, W["w_down"], b_sharded="row", name="mlp_down")
    x2 = _res_fwd(x1, y2, gt2, "mlp_residual")
    saved = dict(x=x, h1=h1, p=p, osbh=osbh, cb=cb, o_sb=o_sb, qn=qn, kvn=kvn, qr=qr, kv=kv, kpe=kpe, omh=omh,
                 lse=lse, o_mla=o_mla, merged=merged, y1=y1, x1=x1, h2=h2, u=u, a=a, y2=y2)
    return x2, saved, gathered


def _layer_bwd(dx2, W, mod, tabs, sv, core, above=None):
    sh1, sc1, gt1, sh2, sc2, gt2 = (mod[i] for i in range(N_MOD))
    dy2, dgt2 = _res_bwd(dx2, sv["y2"], gt2, "mlp_residual_bwd")

    def sqrelu_bwd(da, u):
        return (da * (2.0 * jnp.maximum(u, 0.0)),)

    du = _matmul(dy2, W["w_down"], tb=True, b_sharded="row", out_dtype=BF16, epilogue=sqrelu_bwd, extra=(sv["u"],),
                 exs=[_swap_halves_exchange(above)] if above else None, name="mlp_down_dx")
    pending = None
    if above:
        du, (from_sibling,) = du
        pending = [_add_halves(d, r, core, "grads_add_halves") for d, r in zip(above, from_sibling)]
    g_down = _matmul(sv["a"], dy2, ta=True, out_sharded=("row", W["w_down"].shape), name="mlp_down_dw")
    dh2 = _matmul(du, W["w_up"], tb=True, b_sharded="col", name="mlp_up_dx")
    g_up = _matmul(sv["h2"], du, ta=True, out_sharded=("col", W["w_up"].shape), name="mlp_up_dw")
    dx1, dsh2, dsc2, dg_mlp = _normmod_bwd(sv["x1"], dh2, W["g_mlp"], sc2, dx2, "mlp_norm_bwd")
    dy1, dgt1 = _res_bwd(dx1, sv["y1"], gt1, "mix_residual_bwd")
    dm = _matmul(dy1, W["w_mix_out"], tb=True, name="mix_out_dx")
    g_mix_out = _matmul(sv["merged"], dy1, ta=True, name="mix_out_dw")
    do_sb, do_mla, dgs, dgm = _merge_bwd(sv["p"], sv["o_sb"], sv["o_mla"], dm, "merge_bwd")
    do_sbh = _matmul(do_sb, W["w_sb_out"], tb=True, name="sb_out_dx")
    g_sb_out = _matmul(sv["osbh"], do_sb, ta=True, name="sb_out_dw")
    (dqs, dks, dvs), carried = _sb_bwd(
        sv["p"], do_sbh, sv["cb"], "sb_attn_bwd", [_chip_scatter_exchange(pending)] if above else None)
    my_sum = [_sum_slots(part, core, "grads_sum_chips") for part in carried[0]] if above else []
    do_mh = _matmul(do_mla, W["w_mla_out"], tb=True, name="mla_out_dx")
    g_mla_out = _matmul(sv["omh"], do_mla, ta=True, name="mla_out_dw")
    (dqr, dkv, dkpe), carried = _mla_bwd(
        sv["qr"], sv["kv"], sv["kpe"], do_mh, sv["omh"], sv["lse"], "mla_attn_bwd",
        [_sibling_gather_exchange(my_sum)] if above else None)
    reduced_above = list(carried[0]) if above else []
    dqp, dkr = _rope_bwd(dqr, dkpe, tabs, "rope_bwd")
    dqn = _matmul(dqp, W["w_q_up"], tb=True, name="q_up_dx")
    g_q_up = _matmul(sv["qn"], dqp, ta=True, name="q_up_dw")
    dkvn = _matmul(dkv, W["w_kv_up"], tb=True, name="kv_up_dx")
    g_kv_up = _matmul(sv["kvn"], dkv, ta=True, name="kv_up_dw")
    dqlat, dg_q = _rmsnorm_bwd(sv["p"], Q_RANK, COL_QLAT // Q_RANK, dqn, W["g_q"], "q_lat_norm_bwd")
    dkvlat, dg_kv = _rmsnorm_bwd(sv["p"], KV_RANK, COL_KVLAT // KV_RANK, dkvn, W["g_kv"], "kv_lat_norm_bwd")
    dp = jnp.concatenate([dgs, dgm, dqs, dks, dvs, dkvlat, dqlat, dkr], axis=1)
    dh1 = _matmul(dp, W["w_in"], tb=True, name="in_proj_dx")
    g_in = _matmul(sv["h1"], dp, ta=True, name="in_proj_dw")
    dx, dsh1, dsc1, dg_mix = _normmod_bwd(sv["x"], dh1, W["g_mix"], sc1, dx1, "mix_norm_bwd")
    grads = dict(w_in=g_in, w_q_up=g_q_up, w_kv_up=g_kv_up, w_sb_out=g_sb_out, w_mla_out=g_mla_out,
                 w_mix_out=g_mix_out, w_up=g_up, w_down=g_down,
                 dmod=jnp.concatenate([dsh1, dsc1, dgt1, dsh2, dsc2, dgt2], axis=0),
                 g_mix=dg_mix, g_mlp=dg_mlp, g_q=dg_q, g_kv=dg_kv)
    return dx, grads, reduced_above


def kernel(x, c, positions, w_ada, b_ada, g_mix_norm, w_in, g_q_lat, w_q_up, g_kv_lat, w_kv_up, w_sb_out, w_mla_out, w_mix_out, g_mlp_norm, w_up, w_down, g_final, loss_target, m_w_ada, m_b_ada, m_g_mix_norm, m_w_in, m_g_q_lat, m_w_q_up, m_g_kv_lat, m_w_kv_up, m_w_sb_out, m_w_mla_out, m_w_mix_out, m_g_mlp_norm, m_w_up, m_w_down, m_g_final, v_w_ada, v_b_ada, v_g_mix_norm, v_w_in, v_g_q_lat, v_w_q_up, v_g_kv_lat, v_w_kv_up, v_w_sb_out, v_w_mla_out, v_w_mix_out, v_g_mlp_norm, v_w_up, v_w_down, v_g_final):
    xi, yi, ci = _place()
    chip = 2 * xi + yi
    batch = 2 * chip + ci
    L = w_ada.shape[0]
    S = x.shape[1]
    shards = dict(w_in=w_in, w_q_up=w_q_up, w_kv_up=w_kv_up, w_sb_out=w_sb_out, w_mla_out=w_mla_out,
                  w_mix_out=w_mix_out, w_up=w_up, w_down=w_down)

    def my_halves(l):
        def half_of(w):
            half = w.shape[1] // 2
            return lax.dynamic_slice_in_dim(w[l].astype(BF16), ci * half, half, 0)

        return [half_of(shards[n]) for n in _SHARDED]

    def layer_weights(l, gathered):
        W = {}
        for n, g in zip(_SHARDED, gathered):
            by_chip = g.reshape((4,) + shards[n].shape[1:])
            W[n] = by_chip if n in _BY_CHIP else _unshard(by_chip, n)
        W["w_in"] = _pad_w_in(W["w_in"])
        W["w_q_up"] = _pad_w_q(W["w_q_up"])
        W["w_mla_out"] = _pad_w_mla(W["w_mla_out"])
        return dict(W, g_mix=g_mix_norm[l:l + 1], g_mlp=g_mlp_norm[l:l + 1], g_q=g_q_lat[l:l + 1],
                    g_kv=g_kv_lat[l:l + 1])

    gathered0 = _all_gather8(my_halves(0), "gather_weights")

    c_act = _silu(c, "silu_c")
    c_all = _all_gather8([jnp.broadcast_to(c_act, (8, D_MODEL))], "gather_c")[0].reshape(8, 8, D_MODEL)[:, 0]
    c16 = jnp.concatenate([c_all, jnp.zeros_like(c_all)], axis=0)
    ada_cols = w_ada.shape[2]
    b_shard = lax.dynamic_slice_in_dim(b_ada, chip * ada_cols, ada_cols, 1)
    mod_part = jnp.stack([_matmul(c16, w_ada[l], name="ada_mod") for l in range(L)])
    mod_part = _bias_add(mod_part, jnp.broadcast_to(b_shard[:, None, :], mod_part.shape), "ada_bias")
    mod_all = _all_gather8([mod_part.reshape(L * 16, ada_cols)], "gather_mod")[0].reshape(4, 2, L, 16, ada_cols)
    mod_mine = lax.dynamic_index_in_dim(mod_all[:, 0], batch, axis=2, keepdims=False)
    mods = mod_mine.transpose(1, 0, 2).reshape(L, N_MOD, 1, D_MODEL)

    tabs = _rope_tables(positions[0])

    xc, saved, layer_w = x[0], [], [layer_weights(0, gathered0)]
    for l in range(L):
        xc, sv, gathered = _layer_fwd(xc, layer_w[l], mods[l], tabs, my_halves(l + 1) if l + 1 < L else None)
        saved.append(sv)
        if l + 1 < L:
            layer_w.append(layer_weights(l + 1, gathered))
    dxc, dg_final, loss_part = _final_loss(xc, loss_target[0], g_final[None, :], "final_norm_loss")
    loss = lax.psum(loss_part[0, 0], ("x", "y", "c"))
    core = jnp.reshape(ci, (1,)).astype(jnp.int32)

    grads, reduced, above = [None] * L, [None] * L, None
    for l in reversed(range(L)):
        dxc, grads[l], reduced_above = _layer_bwd(dxc, layer_w[l], mods[l], tabs, saved[l], core, above)
        if above:
            reduced[l + 1] = reduced_above
        grads[l]["w_in"] = _unpad_w_in(grads[l]["w_in"])
        grads[l]["w_q_up"] = _unpad_w_q(grads[l]["w_q_up"])
        grads[l]["w_mla_out"] = _unpad_w_mla(grads[l]["w_mla_out"])
        above = [grads[l][n] if n in _BY_CHIP else _reshard(grads[l][n], n) for n in _SHARDED]
    from_sibling = _run_exchange(_swap_halves_exchange(above), "grads_swap_halves")
    pending = [_add_halves(d, r, core, "grads_add_halves") for d, r in zip(above, from_sibling)]
    from_chips = _run_exchange(_chip_scatter_exchange(pending), "grads_chip_scatter")
    my_sum = [_sum_slots(part, core, "grads_sum_chips") for part in from_chips]
    reduced[0] = _run_exchange(_sibling_gather_exchange(my_sum), "grads_sibling_gather")
    grad_x = dxc
    gw = {n: jnp.stack([reduced[l][i] for l in range(L)]) for i, n in enumerate(_SHARDED)}

    def row(v):
        return jnp.pad(v, ((0, 0), (0, D_MODEL - v.shape[1])))

    per_layer_rows = N_MOD + 4
    small = jnp.concatenate(
        [jnp.concatenate([grads[l]["dmod"], row(grads[l]["g_mix"]), row(grads[l]["g_mlp"]),
                          row(grads[l]["g_q"]), row(grads[l]["g_kv"])], axis=0) for l in range(L)]
        + [dg_final], axis=0)
    n_small = -(-small.shape[0] // 8) * 8
    small = jnp.pad(small, ((0, n_small - small.shape[0]), (0, 0)))
    small_all = _all_gather8([small], "gather_vector_grads")[0].reshape(8, n_small, D_MODEL)
    small_sum = _sum_blocks(small_all, "sum_vector_grads")
    lay = small_sum[:L * per_layer_rows].reshape(L, per_layer_rows, D_MODEL)
    g_b_ada = lay[:, :N_MOD].reshape(L, N_MOD * D_MODEL)
    g_g_mix, g_g_mlp = lay[:, N_MOD], lay[:, N_MOD + 1]
    g_g_q, g_g_kv = lay[:, N_MOD + 2, :Q_RANK], lay[:, N_MOD + 3, :KV_RANK]
    g_g_final = small_sum[L * per_layer_rows]
    dmod_all = small_all[:, :L * per_layer_rows].reshape(8, L, per_layer_rows, D_MODEL)[:, :, :N_MOD]
    dmod_all = dmod_all.reshape(8, L, N_MOD * D_MODEL)
    dmod_cols = lax.dynamic_slice_in_dim(dmod_all, chip * ada_cols, ada_cols, 2)
    dmod16 = jnp.concatenate([dmod_cols, jnp.zeros_like(dmod_cols)], axis=0)
    g_w_ada = jnp.stack([_matmul(c16, dmod16[:, l], ta=True, name="ada_dw") for l in range(L)])

    weights = dict(w_ada=w_ada, b_ada=b_ada, g_mix_norm=g_mix_norm, w_in=w_in, g_q_lat=g_q_lat, w_q_up=w_q_up,
                   g_kv_lat=g_kv_lat, w_kv_up=w_kv_up, w_sb_out=w_sb_out, w_mla_out=w_mla_out,
                   w_mix_out=w_mix_out, g_mlp_norm=g_mlp_norm, w_up=w_up, w_down=w_down, g_final=g_final)
    mom = dict(w_ada=(m_w_ada, v_w_ada), b_ada=(m_b_ada, v_b_ada), g_mix_norm=(m_g_mix_norm, v_g_mix_norm),
               w_in=(m_w_in, v_w_in), g_q_lat=(m_g_q_lat, v_g_q_lat), w_q_up=(m_w_q_up, v_w_q_up),
               g_kv_lat=(m_g_kv_lat, v_g_kv_lat), w_kv_up=(m_w_kv_up, v_w_kv_up),
               w_sb_out=(m_w_sb_out, v_w_sb_out), w_mla_out=(m_w_mla_out, v_w_mla_out),
               w_mix_out=(m_w_mix_out, v_w_mix_out), g_mlp_norm=(m_g_mlp_norm, v_g_mlp_norm),
               w_up=(m_w_up, v_w_up), w_down=(m_w_down, v_w_down), g_final=(m_g_final, v_g_final))
    gr = dict(gw, w_ada=g_w_ada, b_ada=g_b_ada, g_mix_norm=g_g_mix, g_q_lat=g_g_q, g_kv_lat=g_g_kv,
              g_mlp_norm=g_g_mlp, g_final=g_g_final)
    order = list(weights)
    deltas, new_m, new_v = [], [], []
    for n in order:
        wv, gv, (mv, vv) = weights[n], gr[n], mom[n]
        if wv.ndim == 1:
            d, nm, nv = (t[0] for t in _adamw(wv[None], gv[None], mv[None], vv[None], "adamw_" + n))
        else:
            d, nm, nv = _adamw(wv, gv, mv, vv, "adamw_" + n)
        deltas.append(d)
        new_m.append(nm)
        new_v.append(nv)
    return (loss, grad_x[None], *[gr[n] for n in order], *deltas, *new_m, *new_v)
```

```python
from typing import Any, Callable, Mapping, NamedTuple, Sequence

import jax
import jax.numpy as jnp
from jax import lax
from jax.experimental import pallas as pl
from jax.experimental.pallas import tpu as pltpu

F32 = jnp.float32
BF16 = jnp.bfloat16
MESH = pl.DeviceIdType.MESH

D_MODEL = 1024
N_HEADS = 8
SB_DIM = 64
SB_WIDTH = 512
Q_RANK = 384
KV_RANK = 256
ROPE_DIM = 32
NOPE_DIM = 64
QK_DIM = 96
D_FF = 4096
N_MOD = 6
EPS = 1e-6
ROPE_THETA = 10000.0
SB_SCALE = SB_DIM ** -0.5
MLA_SCALE = QK_DIM ** -0.5
ADAM_LR, ADAM_B1, ADAM_B2, ADAM_EPS, ADAM_WD, ADAM_STEP = 0.001, 0.9, 0.999, 1e-08, 0.01, 10

LANE = 128
IN_PAD = 4352
COL_GATE_SB, COL_GATE_MLA, COL_QSB, COL_KSB, COL_VSB, COL_KVLAT, COL_QLAT, COL_KROPE = (
    0, 1024, 2048, 2560, 3072, 3584, 3840, 4224)
ROPE_LANE0 = 64
VMEM_LIMIT = 48 * 1024 * 1024
NEG_BIG = -1e30


def _cp(*sem):
    return pltpu.CompilerParams(dimension_semantics=sem, vmem_limit_bytes=VMEM_LIMIT)


def _tile(n, prefs):
    for t in prefs:
        if t <= n and n % t == 0:
            return t
    return n


def _dot(a, b, dims):
    return lax.dot_general(a, b, (dims, ((), ())), preferred_element_type=F32)


def _nn(a, b):
    return _dot(a, b, ((1,), (0,)))


def _nt(a, b):
    return _dot(a, b, ((1,), (1,)))


def _tn(a, b):
    return _dot(a, b, ((0,), (0,)))


def _sharded_dims(shape, kind):
    n, r, cs = shape
    return (n * r, cs) if kind == "row" else (r, n * cs)


def _sharded_spec(shape, kind, t_rows, t_cols, tile_of):
    _, r, cs = shape
    if kind == "row":
        assert r % t_rows == 0, (shape, t_rows)
        per = r // t_rows

        def index(i, j, k):
            tr, tc = tile_of(i, j, k)
            return tr // per, tr % per, tc
    else:
        assert cs % t_cols == 0, (shape, t_cols)
        per = cs // t_cols

        def index(i, j, k):
            tr, tc = tile_of(i, j, k)
            return tc // per, tr, tc % per
    return pl.BlockSpec((None, t_rows, t_cols), index)


def _matmul(a, b, *, ta=False, tb=False, out_dtype=F32, b_sharded=None, out_sharded=None, epilogue=None,
            extra=(), exs=None, name):
    (K, M) = a.shape if ta else a.shape[::-1]
    b_dims = _sharded_dims(b.shape, b_sharded) if b_sharded else b.shape
    (N, Kb) = b_dims if tb else b_dims[::-1]
    assert K == Kb, (a.shape, b.shape, ta, tb)
    tm = _tile(M, (512, 384, 256, 128))
    tn = _tile(N, (1024, 2176, 768, 512, 384, 256, 128))
    tk = _tile(K, (1024, 2176, 768, 512, 384, 256, 128))
    nk = K // tk
    dims = ((0 if ta else 1,), (1 if tb else 0,))

    out_dtypes = out_dtype if isinstance(out_dtype, tuple) else (out_dtype,)
    n_extra, n_o = len(extra), len(out_dtypes)

    def body(a_ref, b_ref, *rest):
        extra_refs, o_refs, acc = rest[:n_extra], rest[n_extra:n_extra + n_o], rest[n_extra + n_o:]
        prod = _dot(a_ref[...].astype(BF16), b_ref[...].astype(BF16), dims)

        def write(total):
            vals = epilogue(total, *[r[...] for r in extra_refs]) if epilogue else (total,)
            for o_ref, val, dt in zip(o_refs, vals, out_dtypes):
                o_ref[...] = val.astype(dt)

        if nk == 1:
            write(prod)
            return
        acc_ref, = acc
        k = pl.program_id(2)

        @pl.when(k == 0)
        def _():
            acc_ref[...] = prod

        @pl.when(k > 0)
        def _():
            acc_ref[...] += prod

        @pl.when(k == nk - 1)
        def _():
            write(acc_ref[...])

    a_spec = (pl.BlockSpec((tk, tm), lambda i, j, k: (k, i)) if ta
              else pl.BlockSpec((tm, tk), lambda i, j, k: (i, k)))
    if b_sharded:
        b_spec = (_sharded_spec(b.shape, b_sharded, tn, tk, lambda i, j, k: (j, k)) if tb
                  else _sharded_spec(b.shape, b_sharded, tk, tn, lambda i, j, k: (k, j)))
    else:
        b_spec = (pl.BlockSpec((tn, tk), lambda i, j, k: (j, k)) if tb
                  else pl.BlockSpec((tk, tn), lambda i, j, k: (k, j)))
    tile = pl.BlockSpec((tm, tn), lambda i, j, k: (i, j))
    if out_sharded:
        kind, shape = out_sharded
        assert _sharded_dims(shape, kind) == (M, N), (shape, kind, M, N)
        out_specs = [_sharded_spec(shape, kind, tm, tn, lambda i, j, k: (i, j)) for _ in out_dtypes]
        out_shape = [jax.ShapeDtypeStruct(shape, dt) for dt in out_dtypes]
    else:
        out_specs = [tile] * n_o
        out_shape = [jax.ShapeDtypeStruct((M, N), dt) for dt in out_dtypes]
    grid = (M // tm, N // tn, nk)
    scratch = [pltpu.VMEM((tm, tn), F32)] if nk > 1 else []
    ins = [a, b, *extra]
    body, more, split = _carry(exs, body, len(ins), n_o, len(scratch), grid)
    own, carried = split(pl.pallas_call(
        body, grid=grid, in_specs=[a_spec, b_spec] + [tile] * n_extra + more["in_specs"],
        out_specs=out_specs + more["out_specs"], out_shape=out_shape + more["out_shape"],
        scratch_shapes=scratch + more["scratch"], input_output_aliases=more["aliases"],
        compiler_params=_cp(*(("arbitrary",) * 3 if exs else ("parallel", "parallel", "arbitrary"))),
        name=name)(*ins, *more["ins"]))
    result = own[0] if n_o == 1 else tuple(own)
    return (result, carried) if exs else result


def _rows(ts, w, col=0):
    return pl.BlockSpec((ts, w), lambda i: (i, col))


def _vec(w):
    return pl.BlockSpec((1, w), lambda i: (0, 0))


def _ts(S):
    return _tile(S, (256, 128))


def _attn_tile(S):
    return _tile(S, (512, 256, 128))


def _rms(x):
    return lax.rsqrt(jnp.mean(x * x, axis=-1, keepdims=True) + EPS)


def _colsum(x):
    return jnp.sum(x, axis=0, keepdims=True)


def _normmod_fwd(x, g, sc, sh, name):
    S, W = x.shape
    ts = _ts(S)

    def body(x_ref, g_ref, sc_ref, sh_ref, h_ref):
        xv = x_ref[...]
        h_ref[...] = ((xv * _rms(xv)) * g_ref[...] * (1.0 + sc_ref[...]) + sh_ref[...]).astype(BF16)

    return pl.pallas_call(
        body, grid=(S // ts,), in_specs=[_rows(ts, W), _vec(W), _vec(W), _vec(W)],
        out_specs=_rows(ts, W), out_shape=jax.ShapeDtypeStruct((S, W), BF16),
        compiler_params=_cp("parallel"), name=name)(x, g, sc, sh)


def _normmod_bwd(x, dh, g, sc, dres, name):
    S, W = x.shape
    ts = _ts(S)

    def body(x_ref, dh_ref, g_ref, sc_ref, dres_ref, dx_ref, dsh_ref, dsc_ref, dg_ref):
        @pl.when(pl.program_id(0) == 0)
        def _():
            dsh_ref[...] = jnp.zeros_like(dsh_ref)
            dsc_ref[...] = jnp.zeros_like(dsc_ref)
            dg_ref[...] = jnp.zeros_like(dg_ref)

        xv, dh_v, gv = x_ref[...], dh_ref[...], g_ref[...]
        r = _rms(xv)
        y = xv * r
        dn = dh_v * (1.0 + sc_ref[...])
        dy = dn * gv
        dx_ref[...] = dres_ref[...] + r * (dy - y * jnp.mean(dy * y, axis=-1, keepdims=True))
        dsh_ref[...] += _colsum(dh_v)
        dsc_ref[...] += _colsum(dh_v * y * gv)
        dg_ref[...] += _colsum(dn * y)

    vec_out = jax.ShapeDtypeStruct((1, W), F32)
    return pl.pallas_call(
        body, grid=(S // ts,),
        in_specs=[_rows(ts, W), _rows(ts, W), _vec(W), _vec(W), _rows(ts, W)],
        out_specs=[_rows(ts, W), _vec(W), _vec(W), _vec(W)],
        out_shape=[jax.ShapeDtypeStruct((S, W), F32), vec_out, vec_out, vec_out],
        compiler_params=_cp("arbitrary"), name=name)(x, dh, g, sc, dres)


def _rmsnorm_fwd(p, width, col, g, name):
    S = p.shape[0]
    ts = _ts(S)

    def body(x_ref, g_ref, y_ref):
        xv = x_ref[...]
        y_ref[...] = ((xv * _rms(xv)) * g_ref[...]).astype(BF16)

    return pl.pallas_call(
        body, grid=(S // ts,), in_specs=[_rows(ts, width, col), _vec(width)],
        out_specs=_rows(ts, width), out_shape=jax.ShapeDtypeStruct((S, width), BF16),
        compiler_params=_cp("parallel"), name=name)(p, g)


def _rmsnorm_bwd(p, width, col, dn, g, name):
    S = p.shape[0]
    ts = _ts(S)

    def body(x_ref, dn_ref, g_ref, dx_ref, dg_ref):
        @pl.when(pl.program_id(0) == 0)
        def _():
            dg_ref[...] = jnp.zeros_like(dg_ref)

        xv, dn_v = x_ref[...], dn_ref[...]
        r = _rms(xv)
        y = xv * r
        dy = dn_v * g_ref[...]
        dx_ref[...] = r * (dy - y * jnp.mean(dy * y, axis=-1, keepdims=True))
        dg_ref[...] += _colsum(dn_v * y)

    return pl.pallas_call(
        body, grid=(S // ts,), in_specs=[_rows(ts, width, col), _rows(ts, width), _vec(width)],
        out_specs=[_rows(ts, width), _vec(width)],
        out_shape=[jax.ShapeDtypeStruct((S, width), F32), jax.ShapeDtypeStruct((1, width), F32)],
        compiler_params=_cp("arbitrary"), name=name)(p, dn, g)


def _rope_rot(t, c, s1, s2):
    return t * c + pltpu.roll(t, LANE - 16, 1) * s1 + pltpu.roll(t, 16, 1) * s2


def _rope_rot_t(d, c, s1, s2):
    return d * c + pltpu.roll(d * s1, 16, 1) + pltpu.roll(d * s2, LANE - 16, 1)


def _rope_fwd(qp, p, tabs, name):
    S = qp.shape[0]
    ts = _ts(S)
    W = N_HEADS * LANE

    def body(q_ref, kr_ref, c_ref, s1_ref, s2_ref, qr_ref, kpe_ref):
        c, s1, s2 = c_ref[...], s1_ref[...], s2_ref[...]
        for h in range(N_HEADS):
            sl = slice(h * LANE, (h + 1) * LANE)
            qr_ref[:, sl] = _rope_rot(q_ref[:, sl], c, s1, s2).astype(BF16)
        kpe_ref[...] = _rope_rot(kr_ref[...], c, s1, s2).astype(BF16)

    tab = _rows(ts, LANE)
    return pl.pallas_call(
        body, grid=(S // ts,), in_specs=[_rows(ts, W), _rows(ts, LANE, COL_KROPE // LANE), tab, tab, tab],
        out_specs=[_rows(ts, W), _rows(ts, LANE)],
        out_shape=[jax.ShapeDtypeStruct((S, W), BF16), jax.ShapeDtypeStruct((S, LANE), BF16)],
        compiler_params=_cp("parallel"), name=name)(qp, p, *tabs)


def _rope_bwd(dqr, dkpe_heads, tabs, name):
    S = dqr.shape[0]
    ts = _ts(S)
    W = N_HEADS * LANE

    def body(dq_ref, dk_ref, c_ref, s1_ref, s2_ref, dqp_ref, dkr_ref):
        c, s1, s2 = c_ref[...], s1_ref[...], s2_ref[...]
        dk = dk_ref[:, 0:LANE]
        for h in range(N_HEADS):
            sl = slice(h * LANE, (h + 1) * LANE)
            dqp_ref[:, sl] = _rope_rot_t(dq_ref[:, sl], c, s1, s2).astype(BF16)
            if h:
                dk = dk + dk_ref[:, sl]
        dkr_ref[...] = _rope_rot_t(dk, c, s1, s2)

    tab = _rows(ts, LANE)
    return pl.pallas_call(
        body, grid=(S // ts,), in_specs=[_rows(ts, W), _rows(ts, W), tab, tab, tab],
        out_specs=[_rows(ts, W), _rows(ts, LANE)],
        out_shape=[jax.ShapeDtypeStruct((S, W), BF16), jax.ShapeDtypeStruct((S, LANE), F32)],
        compiler_params=_cp("parallel"), name=name)(dqr, dkpe_heads, *tabs)


def _merge_fwd(p, o_sb, o_mla, name):
    S, W = o_sb.shape
    ts = _ts(S)

    def body(gs_ref, gm_ref, a_ref, b_ref, m_ref):
        m_ref[...] = (jax.nn.sigmoid(gs_ref[...]) * a_ref[...]
                      + jax.nn.sigmoid(gm_ref[...]) * b_ref[...]).astype(BF16)

    return pl.pallas_call(
        body, grid=(S // ts,),
        in_specs=[_rows(ts, W, COL_GATE_SB // W), _rows(ts, W, COL_GATE_MLA // W), _rows(ts, W), _rows(ts, W)],
        out_specs=_rows(ts, W), out_shape=jax.ShapeDtypeStruct((S, W), BF16),
        compiler_params=_cp("parallel"), name=name)(p, p, o_sb, o_mla)


def _merge_bwd(p, o_sb, o_mla, dm, name):
    S, W = o_sb.shape
    ts = _ts(S)

    def body(gs_ref, gm_ref, a_ref, b_ref, dm_ref, da_ref, db_ref, dgs_ref, dgm_ref):
        dmv = dm_ref[...]
        sa, sb = jax.nn.sigmoid(gs_ref[...]), jax.nn.sigmoid(gm_ref[...])
        da_ref[...] = (dmv * sa).astype(BF16)
        db_ref[...] = (dmv * sb).astype(BF16)
        dgs_ref[...] = dmv * a_ref[...] * sa * (1.0 - sa)
        dgm_ref[...] = dmv * b_ref[...] * sb * (1.0 - sb)

    row = _rows(ts, W)
    return pl.pallas_call(
        body, grid=(S // ts,),
        in_specs=[_rows(ts, W, COL_GATE_SB // W), _rows(ts, W, COL_GATE_MLA // W), row, row, row],
        out_specs=[row, row, row, row],
        out_shape=[jax.ShapeDtypeStruct((S, W), BF16), jax.ShapeDtypeStruct((S, W), BF16),
                   jax.ShapeDtypeStruct((S, W), F32), jax.ShapeDtypeStruct((S, W), F32)],
        compiler_params=_cp("parallel"), name=name)(p, p, o_sb, o_mla, dm)


def _res_fwd(x, y, gate, name):
    S, W = x.shape
    ts = _ts(S)

    def body(x_ref, y_ref, g_ref, o_ref):
        o_ref[...] = x_ref[...] + g_ref[...] * y_ref[...]

    return pl.pallas_call(
        body, grid=(S // ts,), in_specs=[_rows(ts, W), _rows(ts, W), _vec(W)], out_specs=_rows(ts, W),
        out_shape=jax.ShapeDtypeStruct((S, W), F32), compiler_params=_cp("parallel"), name=name)(x, y, gate)


def _res_bwd(dx, y, gate, name):
    S, W = dx.shape
    ts = _ts(S)

    def body(dx_ref, y_ref, g_ref, dy_ref, dg_ref):
        @pl.when(pl.program_id(0) == 0)
        def _():
            dg_ref[...] = jnp.zeros_like(dg_ref)

        dxv = dx_ref[...]
        dy_ref[...] = (g_ref[...] * dxv).astype(BF16)
        dg_ref[...] += _colsum(dxv * y_ref[...])

    return pl.pallas_call(
        body, grid=(S // ts,), in_specs=[_rows(ts, W), _rows(ts, W), _vec(W)],
        out_specs=[_rows(ts, W), _vec(W)],
        out_shape=[jax.ShapeDtypeStruct((S, W), BF16), jax.ShapeDtypeStruct((1, W), F32)],
        compiler_params=_cp("arbitrary"), name=name)(dx, y, gate)


def _final_loss(x, target, g, name):
    S, W = x.shape
    ts = _ts(S)

    def body(x_ref, t_ref, g_ref, dx_ref, dg_ref, loss_ref):
        @pl.when(pl.program_id(0) == 0)
        def _():
            dg_ref[...] = jnp.zeros_like(dg_ref)
            loss_ref[...] = jnp.zeros_like(loss_ref)

        xv, gv = x_ref[...], g_ref[...]
        r = _rms(xv)
        y = xv * r
        err = y * gv - t_ref[...]
        loss_ref[...] += jnp.full((1, LANE), 0.5 * jnp.sum(jnp.mean(err * err, axis=-1)), F32)
        dout = err * (1.0 / W)
        dy = dout * gv
        dx_ref[...] = r * (dy - y * jnp.mean(dy * y, axis=-1, keepdims=True))
        dg_ref[...] += _colsum(dout * y)

    return pl.pallas_call(
        body, grid=(S // ts,), in_specs=[_rows(ts, W), _rows(ts, W), _vec(W)],
        out_specs=[_rows(ts, W), _vec(W), _vec(LANE)],
        out_shape=[jax.ShapeDtypeStruct((S, W), F32), jax.ShapeDtypeStruct((1, W), F32),
                   jax.ShapeDtypeStruct((1, LANE), F32)],
        compiler_params=_cp("arbitrary"), name=name)(x, target, g)


def _silu(c, name):
    def body(c_ref, o_ref):
        cv = c_ref[...]
        o_ref[...] = cv * jax.nn.sigmoid(cv)

    return pl.pallas_call(body, out_shape=jax.ShapeDtypeStruct(c.shape, F32), name=name)(c)


def _bias_add(a, b, name):
    def body(a_ref, b_ref, o_ref):
        o_ref[...] = a_ref[...] + b_ref[...]

    return pl.pallas_call(body, out_shape=jax.ShapeDtypeStruct(a.shape, F32), name=name)(a, b)


def _sum_blocks(xs, name):
    n = xs.shape[0]

    def body(x_ref, o_ref):
        acc = x_ref[0]
        for d in range(1, n):
            acc = acc + x_ref[d]
        o_ref[...] = acc

    return pl.pallas_call(body, out_shape=jax.ShapeDtypeStruct(xs.shape[1:], F32), name=name)(xs)


def _adamw(w, g, m, v, name):
    shape = w.shape
    cols = shape[-1]
    w2, g2, m2, v2 = (t.reshape(-1, cols) for t in (w, g, m, v))
    rows = w2.shape[0]
    tr = _tile(rows, (128,))
    c1 = 1.0 - ADAM_B1 ** ADAM_STEP
    c2 = 1.0 - ADAM_B2 ** ADAM_STEP

    def body(w_ref, g_ref, m_ref, v_ref, d_ref, nm_ref, nv_ref):
        gv = g_ref[...]
        nm = ADAM_B1 * m_ref[...] + (1.0 - ADAM_B1) * gv
        nv = ADAM_B2 * v_ref[...] + (1.0 - ADAM_B2) * (gv * gv)
        d_ref[...] = -ADAM_LR * ((nm / c1) / (jnp.sqrt(nv / c2) + ADAM_EPS) + ADAM_WD * w_ref[...])
        nm_ref[...] = nm
        nv_ref[...] = nv

    spec = pl.BlockSpec((tr, cols), lambda i: (i, 0))
    out = jax.ShapeDtypeStruct((rows, cols), F32)
    d, nm, nv = pl.pallas_call(
        body, grid=(rows // tr,), in_specs=[spec] * 4, out_specs=[spec] * 3, out_shape=[out] * 3,
        compiler_params=_cp("parallel"), name=name)(w2, g2, m2, v2)
    return d.reshape(shape), nm.reshape(shape), nv.reshape(shape)


def _split_dot(x, tri):
    hi = x.astype(BF16)
    lo = (x - hi.astype(F32)).astype(BF16)
    return _nn(hi, tri) + _nn(lo, tri)


def _tri_cumsum(x, tri, later):
    h = x.shape[1] // 2
    first, second = x[:, :h], x[:, h:]
    sum_first = jnp.sum(first, axis=1, keepdims=True)
    sum_second = jnp.sum(second, axis=1, keepdims=True)
    run_first, run_second = _split_dot(first, tri), _split_dot(second, tri)
    if later:
        run_first = run_first + sum_second
    else:
        run_second = run_second + sum_first
    return jnp.concatenate([run_first, run_second], axis=1), sum_first + sum_second


def _sb_logs(z):
    soft = jnp.log(1.0 + jnp.exp(-jnp.abs(z)))
    return jnp.minimum(z, 0.0) - soft, -jnp.maximum(z, 0.0) - soft


def _attn_call(body, grid, ins, in_specs, out_specs, out_shape, scratch, exs, name):
    body, extra, split = _carry(exs, body, len(ins), len(out_shape), len(scratch), grid)
    return split(pl.pallas_call(
        body, grid=grid, in_specs=in_specs + extra["in_specs"], out_specs=out_specs + extra["out_specs"],
        out_shape=out_shape + extra["out_shape"], scratch_shapes=scratch + extra["scratch"],
        input_output_aliases=extra["aliases"], compiler_params=_cp("arbitrary", "arbitrary"),
        name=name)(*ins, *extra["ins"]))


def _sb_fwd(p, name, exs=None):
    S = p.shape[0]
    t = _attn_tile(S)
    qb, kb, vb = COL_QSB // LANE, COL_KSB // LANE, COL_VSB // LANE

    def body(q_ref, k_ref, v_ref, o_ref, cb_ref, acc_ref):
        i = pl.program_id(1)
        lane = lax.broadcasted_iota(jnp.int32, (t, LANE), 1)
        rows = lax.broadcasted_iota(jnp.int32, (t, t), 0)
        cols = lax.broadcasted_iota(jnp.int32, (t, t), 1)
        half_r = lax.broadcasted_iota(jnp.int32, (t // 2, t // 2), 0)
        half_c = lax.broadcasted_iota(jnp.int32, (t // 2, t // 2), 1)
        after = jnp.where(half_r > half_c, 1.0, 0.0).astype(BF16)
        diag = cols < rows
        q = q_ref[...] * SB_SCALE
        acc_ref[...] = jnp.zeros_like(acc_ref)
        cb_ref[...] = jnp.zeros_like(cb_ref)
        hms = [(lane >= SB_DIM * h) & (lane < SB_DIM * (h + 1)) for h in range(2)]
        qhs = [jnp.where(hm, q, 0.0).astype(BF16) for hm in hms]

        def step(j, cs, masked):
            rows_j = pl.ds(pl.multiple_of(j * t, t), t)
            kj = k_ref[rows_j, :].astype(BF16)
            vf = v_ref[rows_j, :]
            out, pv = [], None
            for h in range(2):
                ls, lf = _sb_logs(_nt(qhs[h], kj))
                if masked:
                    lf = jnp.where(diag, lf, 0.0)
                survive, total = _tri_cumsum(lf, after, True)
                a = jnp.exp(ls + survive + cs[h])
                if masked:
                    a = jnp.where(diag, a, 0.0)
                term = _nn(a.astype(BF16), jnp.where(hms[h], vf, 0.0).astype(BF16))
                pv = term if pv is None else pv + term
                cb_ref[h] = jnp.where(lane == j, cs[h], cb_ref[h])
                out.append(cs[h] + total)
            acc_ref[...] += pv
            return tuple(out)

        zero = jnp.zeros((t, 1), F32)
        cs = step(i, (zero, zero), True)
        lax.fori_loop(0, i, lambda it, cs: step(i - 1 - it, cs, False), cs)
        o_ref[...] = acc_ref[...].astype(BF16)

    return _attn_call(
        body, (SB_WIDTH // LANE, S // t), [p, p, p],
        [pl.BlockSpec((t, LANE), lambda hp, i: (i, qb + hp)),
         pl.BlockSpec((S, LANE), lambda hp, i: (0, kb + hp)),
         pl.BlockSpec((S, LANE), lambda hp, i: (0, vb + hp))],
        [pl.BlockSpec((t, LANE), lambda hp, i: (i, hp)),
         pl.BlockSpec((2, t, LANE), lambda hp, i: (hp, i, 0))],
        [jax.ShapeDtypeStruct((S, SB_WIDTH), BF16), jax.ShapeDtypeStruct((N_HEADS, S, LANE), F32)],
        [pltpu.VMEM((t, LANE), F32)], exs, name)


def _sb_bwd(p, do, cb, name, exs=None):
    S = p.shape[0]
    t = _attn_tile(S)
    qb, kb, vb = COL_QSB // LANE, COL_KSB // LANE, COL_VSB // LANE

    def body(q_ref, k_ref, v_ref, do_ref, cb_ref, dq_ref, dk_ref, dv_ref, acc_ref):
        i = pl.program_id(1)

        @pl.when(i == 0)
        def _():
            dk_ref[...] = jnp.zeros_like(dk_ref)
            dv_ref[...] = jnp.zeros_like(dv_ref)

        lane = lax.broadcasted_iota(jnp.int32, (t, LANE), 1)
        rows = lax.broadcasted_iota(jnp.int32, (t, t), 0)
        cols = lax.broadcasted_iota(jnp.int32, (t, t), 1)
        half_r = lax.broadcasted_iota(jnp.int32, (t // 2, t // 2), 0)
        half_c = lax.broadcasted_iota(jnp.int32, (t // 2, t // 2), 1)
        after = jnp.where(half_r > half_c, 1.0, 0.0).astype(BF16)
        before = jnp.where(half_r < half_c, 1.0, 0.0).astype(BF16)
        diag = cols < rows
        q = q_ref[...] * SB_SCALE
        dov = do_ref[...]
        acc_ref[...] = jnp.zeros_like(acc_ref)
        hms = [(lane >= SB_DIM * h) & (lane < SB_DIM * (h + 1)) for h in range(2)]
        qhs = [jnp.where(hm, q, 0.0).astype(BF16) for hm in hms]
        dohs = [jnp.where(hm, dov, 0.0).astype(BF16) for hm in hms]

        def step(j, fs, masked):
            rows_j = pl.ds(pl.multiple_of(j * t, t), t)
            kf = k_ref[rows_j, :]
            kj = kf.astype(BF16)
            vj = v_ref[rows_j, :].astype(BF16)
            out, dq_t, dk_t, dv_t = [], None, None, None
            for h in range(2):
                ls, lf = _sb_logs(_nt(qhs[h], kj))
                if masked:
                    lf = jnp.where(diag, lf, 0.0)
                c = jnp.sum(jnp.where(lane == j, cb_ref[h], 0.0), axis=1, keepdims=True)
                a = jnp.exp(ls + _tri_cumsum(lf, after, True)[0] + c)
                if masked:
                    a = jnp.where(diag, a, 0.0)
                dl = _nt(dohs[h], vj) * a
                sg = jnp.exp(ls)
                earlier, total = _tri_cumsum(dl, before, False)
                dz = dl * (1.0 - sg) - sg * (earlier + fs[h])
                if masked:
                    dz = jnp.where(diag, dz, 0.0)
                dzb = dz.astype(BF16)
                terms = (_nn(dzb, jnp.where(hms[h], kf, 0.0).astype(BF16)), _tn(dzb, qhs[h]),
                         _tn(a.astype(BF16), dohs[h]))
                dq_t, dk_t, dv_t = terms if dq_t is None else (dq_t + terms[0], dk_t + terms[1], dv_t + terms[2])
                out.append(fs[h] + total)
            acc_ref[...] += dq_t
            dk_ref[rows_j, :] += dk_t
            dv_ref[rows_j, :] += dv_t
            return tuple(out)

        zero = jnp.zeros((t, 1), F32)
        fs = lax.fori_loop(0, i, lambda j, fs: step(j, fs, False), (zero, zero))
        step(i, fs, True)
        dq_ref[...] = acc_ref[...] * SB_SCALE

    col = lambda hp, i: (0, hp)
    out = jax.ShapeDtypeStruct((S, SB_WIDTH), F32)
    return _attn_call(
        body, (SB_WIDTH // LANE, S // t), [p, p, p, do, cb],
        [pl.BlockSpec((t, LANE), lambda hp, i: (i, qb + hp)),
         pl.BlockSpec((S, LANE), lambda hp, i: (0, kb + hp)),
         pl.BlockSpec((S, LANE), lambda hp, i: (0, vb + hp)),
         pl.BlockSpec((t, LANE), lambda hp, i: (i, hp)),
         pl.BlockSpec((2, t, LANE), lambda hp, i: (hp, i, 0))],
        [pl.BlockSpec((t, LANE), lambda hp, i: (i, hp)), pl.BlockSpec((S, LANE), col), pl.BlockSpec((S, LANE), col)],
        [out, out, out], [pltpu.VMEM((t, LANE), F32)], exs, name)


def _mla_fwd(qr, kv, kpe, name, exs=None):
    S = qr.shape[0]
    t = _attn_tile(S)

    def body(q_ref, kv_ref, kpe_ref, o_ref, lse_ref, acc_ref, m_ref):
        i = pl.program_id(1)
        low = lax.broadcasted_iota(jnp.int32, (t, LANE), 1) < NOPE_DIM
        rows = lax.broadcasted_iota(jnp.int32, (t, t), 0)
        cols = lax.broadcasted_iota(jnp.int32, (t, t), 1)
        causal = cols <= rows
        one = jnp.ones((t, LANE), BF16)
        heads = [slice(h * LANE, (h + 1) * LANE) for h in range(2)]
        qs = [q_ref[:, sl] for sl in heads]
        acc_ref[...] = jnp.zeros_like(acc_ref)
        m_ref[...] = jnp.full_like(m_ref, NEG_BIG)

        def step(j, masked):
            rows_j = pl.ds(pl.multiple_of(j * t, t), t)
            kpe_j = kpe_ref[rows_j, :]
            for h, sl in enumerate(heads):
                kvj = kv_ref[rows_j, sl]
                z = _nt(qs[h], jnp.where(low, kvj, kpe_j)) * MLA_SCALE
                if masked:
                    z = jnp.where(causal, z, NEG_BIG)
                m_old = m_ref[h]
                m_new = jnp.maximum(m_old, jnp.max(z, axis=1, keepdims=True))
                pr = jnp.exp(z - m_new)
                acc_ref[:, sl] = jnp.exp(m_old - m_new) * acc_ref[:, sl] + _nn(
                    pr.astype(BF16), jnp.where(low, one, kvj))
                m_ref[h] = m_new

        def loop(j, carry):
            step(j, False)
            return carry

        lax.fori_loop(0, i, loop, 0)
        step(i, True)
        for h, sl in enumerate(heads):
            acc = acc_ref[:, sl]
            den = acc[:, 0:1]
            o_ref[:, sl] = jnp.where(low, 0.0, acc / den).astype(BF16)
            lse_ref[h] = jnp.broadcast_to(m_ref[h] + jnp.log(den), (t, LANE))

    pair = 2 * LANE
    return _attn_call(
        body, (N_HEADS // 2, S // t), [qr, kv, kpe],
        [pl.BlockSpec((t, pair), lambda hp, i: (i, hp)),
         pl.BlockSpec((S, pair), lambda hp, i: (0, hp)),
         pl.BlockSpec((S, LANE), lambda hp, i: (0, 0))],
        [pl.BlockSpec((t, pair), lambda hp, i: (i, hp)), pl.BlockSpec((2, t, LANE), lambda hp, i: (hp, i, 0))],
        [jax.ShapeDtypeStruct((S, N_HEADS * LANE), BF16), jax.ShapeDtypeStruct((N_HEADS, S, LANE), F32)],
        [pltpu.VMEM((t, pair), F32), pltpu.VMEM((2, t, 1), F32)], exs, name)


def _mla_bwd(qr, kv, kpe, do, o, lse, name, exs=None):
    S = qr.shape[0]
    t = _attn_tile(S)

    def body(q_ref, kv_ref, kpe_ref, do_ref, o_ref, lse_ref, dq_ref, dkv_ref, dkpe_ref, acc_ref):
        i = pl.program_id(1)

        @pl.when(i == 0)
        def _():
            dkv_ref[...] = jnp.zeros_like(dkv_ref)
            dkpe_ref[...] = jnp.zeros_like(dkpe_ref)

        low = lax.broadcasted_iota(jnp.int32, (t, LANE), 1) < NOPE_DIM
        rows = lax.broadcasted_iota(jnp.int32, (t, t), 0)
        cols = lax.broadcasted_iota(jnp.int32, (t, t), 1)
        causal = cols <= rows
        heads = [slice(h * LANE, (h + 1) * LANE) for h in range(2)]
        qs = [q_ref[:, sl] for sl in heads]
        dovs = [do_ref[:, sl] for sl in heads]
        dobs = [d.astype(BF16) for d in dovs]
        deltas = [jnp.sum(dovs[h] * o_ref[:, sl].astype(F32), axis=1, keepdims=True) for h, sl in enumerate(heads)]
        lses = [lse_ref[h][:, 0:1] for h in range(2)]
        acc_ref[...] = jnp.zeros_like(acc_ref)

        def step(j, masked):
            rows_j = pl.ds(pl.multiple_of(j * t, t), t)
            kpe_j = kpe_ref[rows_j, :]
            for h, sl in enumerate(heads):
                kvj = kv_ref[rows_j, sl]
                kcat = jnp.where(low, kvj, kpe_j)
                z = _nt(qs[h], kcat) * MLA_SCALE
                if masked:
                    z = jnp.where(causal, z, NEG_BIG)
                pr = jnp.exp(z - lses[h])
                ds = (pr * (_nt(dobs[h], kvj) - deltas[h])).astype(BF16)
                acc_ref[:, sl] += _nn(ds, kcat)
                dkc = _tn(ds, qs[h]) * MLA_SCALE
                dkv_ref[rows_j, sl] += jnp.where(low, dkc, _tn(pr.astype(BF16), dobs[h]))
                dkpe_ref[rows_j, sl] += jnp.where(low, 0.0, dkc)

        def loop(j, carry):
            step(j, False)
            return carry

        lax.fori_loop(0, i, loop, 0)
        step(i, True)
        dq_ref[...] = acc_ref[...] * MLA_SCALE

    pair = 2 * LANE
    blk = pl.BlockSpec((t, pair), lambda hp, i: (i, hp))
    col = pl.BlockSpec((S, pair), lambda hp, i: (0, hp))
    out = jax.ShapeDtypeStruct((S, N_HEADS * LANE), F32)
    return _attn_call(
        body, (N_HEADS // 2, S // t), [qr, kv, kpe, do, o, lse],
        [blk, col, pl.BlockSpec((S, LANE), lambda hp, i: (0, 0)), blk, blk,
         pl.BlockSpec((2, t, LANE), lambda hp, i: (hp, i, 0))],
        [blk, col, col], [out, out, out], [pltpu.VMEM((t, pair), F32)], exs, name)


_ANY = pl.BlockSpec(memory_space=pl.ANY)


def _place():
    return lax.axis_index("x"), lax.axis_index("y"), lax.axis_index("c")


class _Exchange(NamedTuple):
    ins: Sequence[Any]
    outs: Sequence[Any]
    aliases: Mapping[int, int]
    n_remote: int
    n_local: int
    start: Callable
    finish: Callable


def _exchange_scratch(ex):
    return [pltpu.SemaphoreType.DMA((ex.n_remote,)), pltpu.SemaphoreType.DMA((ex.n_remote,)),
            pltpu.SemaphoreType.DMA((ex.n_local,))]


def _run_exchange(ex, name):
    n_in, n_out = len(ex.ins), len(ex.outs)

    def body(*refs):
        args = (refs[:n_in], refs[n_in:n_in + n_out], *refs[n_in + n_out:])
        ex.start(*args)
        ex.finish(*args)

    return pl.pallas_call(
        body, out_shape=list(ex.outs), in_specs=[_ANY] * n_in, out_specs=[_ANY] * n_out,
        scratch_shapes=_exchange_scratch(ex), input_output_aliases=dict(ex.aliases), name=name)(*ex.ins)


def _carry(exs, body, n_in, n_out, n_scratch, grid):
    exs = [ex for ex in (exs or []) if ex is not None]
    e_ins, e_outs = [len(ex.ins) for ex in exs], [len(ex.outs) for ex in exs]

    def take(refs, counts):
        groups = []
        for n in counts:
            groups.append(refs[:n])
            refs = refs[n:]
        return groups, refs

    def carried(*refs):
        own_in, refs = refs[:n_in], refs[n_in:]
        ex_in, refs = take(refs, e_ins)
        own_out, refs = refs[:n_out], refs[n_out:]
        ex_out, refs = take(refs, e_outs)
        own_scratch, refs = refs[:n_scratch], refs[n_scratch:]
        sems, _ = take(refs, [3] * len(exs))
        at = [pl.program_id(d) for d in range(len(grid))]
        first, last = at[0] == 0, at[0] == grid[0] - 1
        for d in range(1, len(grid)):
            first, last = first & (at[d] == 0), last & (at[d] == grid[d] - 1)

        @pl.when(first)
        def _():
            for e, ex in enumerate(exs):
                ex.start(ex_in[e], ex_out[e], *sems[e])

        body(*own_in, *own_out, *own_scratch)

        @pl.when(last)
        def _():
            for e, ex in enumerate(exs):
                ex.finish(ex_in[e], ex_out[e], *sems[e])

    aliases, i0, o0 = {}, n_in, n_out
    for ex in exs:
        aliases.update({i0 + i: o0 + o for i, o in ex.aliases.items()})
        i0, o0 = i0 + len(ex.ins), o0 + len(ex.outs)

    def split(res):
        groups, _ = take(list(res[n_out:]), e_outs)
        return list(res[:n_out]), groups

    extra = dict(
        ins=[a for ex in exs for a in ex.ins], in_specs=[_ANY] * sum(e_ins), out_specs=[_ANY] * sum(e_outs),
        out_shape=[o for ex in exs for o in ex.outs], scratch=[s for ex in exs for s in _exchange_scratch(ex)],
        aliases=aliases)
    return (carried if exs else body), extra, split


def _gather_exchange(arrs, phase="all"):
    n_t = len(arrs)
    ms = [a.shape[0] // (8 if phase == "b" else 1) for a in arrs]

    def plan(in_refs, out_refs, send_sems, recv_sems, local_sems):
        x, y, c = _place()
        me, sibling = (x, y, c), (x, y, 1 - c)
        chips = [(1 - x, y), (x, 1 - y), (1 - x, 1 - y)]

        def rows(ref, t, px, py, pc):
            return ref.at[pl.ds((4 * px + 2 * py + pc) * ms[t], ms[t]), :]

        def copy(t, k, block, to, src):
            return pltpu.make_async_remote_copy(
                src_ref=src, dst_ref=rows(out_refs[t], t, *block), send_sem=send_sems.at[7 * t + k],
                recv_sem=recv_sems.at[7 * t + k], device_id=to, device_id_type=MESH)

        mine, first, first_in, passed, passed_in = [], [], [], [], []
        for t in range(n_t):
            if phase != "b":
                mine.append(pltpu.make_async_copy(in_refs[t], rows(out_refs[t], t, *me), local_sems.at[t]))
                first.append(copy(t, 0, me, sibling, in_refs[t]))
                first_in.append(copy(t, 0, sibling, me, in_refs[t]))
                for j, chip in enumerate(chips):
                    first.append(copy(t, 1 + j, me, (*chip, c), in_refs[t]))
                    first_in.append(copy(t, 1 + j, (*chip, c), me, in_refs[t]))
            if phase != "a":
                held = in_refs[t] if phase == "b" else out_refs[t]
                for j, chip in enumerate(chips):
                    passed.append(copy(t, 4 + j, (*chip, c), sibling, rows(held, t, *chip, c)))
                    passed_in.append(copy(t, 4 + j, (*chip, 1 - c), me, rows(held, t, *chip, c)))
        return mine, first, first_in, passed, passed_in

    def start(*refs):
        mine, first, _, passed, _ = plan(*refs)
        for cp in mine + first + (passed if phase == "b" else []):
            cp.start()

    def finish(*refs):
        mine, first, first_in, passed, passed_in = plan(*refs)
        for cp in first_in:
            cp.wait_recv()
        if phase == "all":
            for cp in passed:
                cp.start()
        for cp in passed_in:
            cp.wait_recv()
        for cp in first + passed:
            cp.wait_send()
        for cp in mine:
            cp.wait()

    if phase == "b":
        outs = [jax.ShapeDtypeStruct(a.shape, a.dtype) for a in arrs]
        aliases = {t: t for t in range(n_t)}
    else:
        outs = [jax.ShapeDtypeStruct((8 * a.shape[0], a.shape[1]), a.dtype) for a in arrs]
        aliases = {}
    return _Exchange(list(arrs), outs, aliases, 7 * n_t, n_t, start, finish)


def _all_gather8(blks, name):
    return _run_exchange(_gather_exchange(blks), name)


def _swap_halves_exchange(gs):
    n_t = len(gs)

    def plan(g_refs, out_refs, send_sems, recv_sems, local_sems):
        x, y, c = _place()
        copies = []
        for t in range(n_t):
            m = gs[t].shape[1] // 2
            copies += [pltpu.make_async_remote_copy(
                src_ref=g_refs[t].at[s, pl.ds((1 - c) * m, m), :], dst_ref=out_refs[t].at[s],
                send_sem=send_sems.at[4 * t + s], recv_sem=recv_sems.at[4 * t + s], device_id=(x, y, 1 - c),
                device_id_type=MESH) for s in range(4)]
        return copies

    def start(*refs):
        for cp in plan(*refs):
            cp.start()

    def finish(*refs):
        for cp in plan(*refs):
            cp.wait()

    outs = [jax.ShapeDtypeStruct((4, g.shape[1] // 2, g.shape[2]), g.dtype) for g in gs]
    return _Exchange(list(gs), outs, {}, 4 * n_t, 1, start, finish)


def _chip_scatter_exchange(parts):
    n_t = len(parts)

    def plan(p_refs, out_refs, send_sems, recv_sems, local_sems):
        x, y, c = _place()
        mine = 2 * x + y
        chips = [(1 - x, y), (x, 1 - y), (1 - x, 1 - y)]

        def copy(t, j, src_slot, dst_slot):
            px, py = chips[j]
            return pltpu.make_async_remote_copy(
                src_ref=p_refs[t].at[src_slot], dst_ref=out_refs[t].at[dst_slot],
                send_sem=send_sems.at[3 * t + j], recv_sem=recv_sems.at[3 * t + j], device_id=(px, py, c),
                device_id_type=MESH)

        own = [pltpu.make_async_copy(p_refs[t].at[mine], out_refs[t].at[mine], local_sems.at[t])
               for t in range(n_t)]
        sends = [copy(t, j, 2 * px + py, mine) for t in range(n_t) for j, (px, py) in enumerate(chips)]
        arrivals = [copy(t, j, mine, 2 * px + py) for t in range(n_t) for j, (px, py) in enumerate(chips)]
        return own, sends, arrivals

    def start(*refs):
        own, sends, _ = plan(*refs)
        for cp in own + sends:
            cp.start()

    def finish(*refs):
        own, sends, arrivals = plan(*refs)
        for cp in arrivals:
            cp.wait_recv()
        for cp in sends:
            cp.wait_send()
        for cp in own:
            cp.wait()

    outs = [jax.ShapeDtypeStruct(p.shape, p.dtype) for p in parts]
    return _Exchange(list(parts), outs, {}, 3 * n_t, n_t, start, finish)


def _sibling_gather_exchange(bufs):
    n_t = len(bufs)

    def plan(b_refs, out_refs, send_sems, recv_sems, local_sems):
        x, y, c = _place()

        def copy(t, pc):
            m = bufs[t].shape[0] // 2
            half = pl.ds(pc * m, m)
            return pltpu.make_async_remote_copy(
                src_ref=b_refs[t].at[half, :], dst_ref=out_refs[t].at[half, :], send_sem=send_sems.at[t],
                recv_sem=recv_sems.at[t], device_id=(x, y, 1 - c), device_id_type=MESH)

        return [copy(t, c) for t in range(n_t)], [copy(t, 1 - c) for t in range(n_t)]

    def start(*refs):
        for cp in plan(*refs)[0]:
            cp.start()

    def finish(*refs):
        sends, arrivals = plan(*refs)
        for cp in arrivals:
            cp.wait_recv()
        for cp in sends:
            cp.wait_send()

    outs = [jax.ShapeDtypeStruct(b.shape, b.dtype) for b in bufs]
    return _Exchange(list(bufs), outs, {t: t for t in range(n_t)}, n_t, 1, start, finish)


def _add_halves(g, recv, c, name):
    n_slot, m2, n = g.shape
    m = m2 // 2
    tr = _tile(m, (512, 256, 192, 128, 16))

    def body(c_ref, g_ref, r_ref, o_ref):
        o_ref[...] = (g_ref[...] + r_ref[...].astype(F32)).astype(BF16)

    nb = m // tr
    return pl.pallas_call(
        body,
        grid_spec=pltpu.PrefetchScalarGridSpec(
            num_scalar_prefetch=1, grid=(n_slot, nb),
            in_specs=[pl.BlockSpec((1, tr, n), lambda s, i, c_ref: (s, c_ref[0] * nb + i, 0)),
                      pl.BlockSpec((1, tr, n), lambda s, i, c_ref: (s, i, 0))],
            out_specs=pl.BlockSpec((1, tr, n), lambda s, i, c_ref: (s, i, 0))),
        out_shape=jax.ShapeDtypeStruct((n_slot, m, n), BF16),
        compiler_params=_cp("parallel", "parallel"), name=name)(c, g, recv)


def _sum_slots(parts, c, name):
    n_slot, m, n = parts.shape
    tr = _tile(m, (512, 256, 192, 128, 16))
    nb = m // tr

    def body(c_ref, p_ref, o_ref):
        acc = p_ref[0].astype(F32)
        for s in range(1, n_slot):
            acc = acc + p_ref[s].astype(F32)
        o_ref[...] = acc

    return pl.pallas_call(
        body,
        grid_spec=pltpu.PrefetchScalarGridSpec(
            num_scalar_prefetch=1, grid=(nb,),
            in_specs=[pl.BlockSpec((n_slot, tr, n), lambda i, c_ref: (0, i, 0))],
            out_specs=pl.BlockSpec((tr, n), lambda i, c_ref: (c_ref[0] * nb + i, 0))),
        out_shape=jax.ShapeDtypeStruct((2 * m, n), F32),
        compiler_params=_cp("parallel"), name=name)(c, parts)


_SHARDED = ("w_in", "w_q_up", "w_kv_up", "w_sb_out", "w_mla_out", "w_mix_out", "w_up", "w_down")
_ROW_SHARDED = ("w_mix_out", "w_down")
_BY_CHIP = ("w_up", "w_down")


def _unshard(parts, name):
    n, r, cs = parts.shape
    if name in _ROW_SHARDED:
        return parts.reshape(n * r, cs)
    return parts.transpose(1, 0, 2).reshape(r, n * cs)


def _reshard(full, name, n=4):
    R, C = full.shape
    if name in _ROW_SHARDED:
        return full.reshape(n, R // n, C)
    return full.reshape(R, n, C // n).transpose(1, 0, 2)


def _pad_w_in(w):
    z = lambda k: jnp.zeros(w.shape[:-1] + (k,), w.dtype)
    return jnp.concatenate([
        w[..., 2208:3232], w[..., 3232:4256], w[..., 0:1536], w[..., 1920:2176], w[..., 1536:1920],
        z(ROPE_LANE0), w[..., 2176:2208], z(LANE - ROPE_LANE0 - ROPE_DIM)], axis=-1)


def _unpad_w_in(g):
    k0 = COL_KROPE + ROPE_LANE0
    return jnp.concatenate([
        g[..., COL_QSB:COL_KVLAT], g[..., COL_QLAT:COL_KROPE], g[..., COL_KVLAT:COL_QLAT],
        g[..., k0:k0 + ROPE_DIM], g[..., 0:COL_QSB]], axis=-1)


def _pad_w_q(w):
    r = w.shape[0]
    return jnp.pad(w.reshape(r, N_HEADS, QK_DIM), ((0, 0), (0, 0), (0, LANE - QK_DIM))).reshape(r, N_HEADS * LANE)


def _unpad_w_q(g):
    r = g.shape[0]
    return g.reshape(r, N_HEADS, LANE)[..., :QK_DIM].reshape(r, N_HEADS * QK_DIM)


def _pad_w_mla(w):
    n = w.shape[1]
    return jnp.pad(w.reshape(N_HEADS, NOPE_DIM, n), ((0, 0), (LANE - NOPE_DIM, 0), (0, 0))).reshape(
        N_HEADS * LANE, n)


def _unpad_w_mla(g):
    n = g.shape[1]
    return g.reshape(N_HEADS, LANE, n)[:, LANE - NOPE_DIM:, :].reshape(N_HEADS * NOPE_DIM, n)


def _rope_tables(positions):
    half = ROPE_DIM // 2
    inv_freq = 1.0 / (ROPE_THETA ** (jnp.arange(0, ROPE_DIM, 2, dtype=F32) / ROPE_DIM))
    ang = positions.astype(F32)[:, None] * inv_freq
    cos, sin = jnp.cos(ang), jnp.sin(ang)
    S = positions.shape[0]
    one = jnp.ones((S, ROPE_LANE0), F32)
    zero = lambda k: jnp.zeros((S, k), F32)
    tail = LANE - ROPE_LANE0 - ROPE_DIM
    c = jnp.concatenate([one, cos, cos, zero(tail)], axis=1)
    s1 = jnp.concatenate([zero(ROPE_LANE0), -sin, zero(half + tail)], axis=1)
    s2 = jnp.concatenate([zero(ROPE_LANE0 + half), sin, zero(tail)], axis=1)
    return c, s1, s2


def _layer_fwd(x, W, mod, tabs, next_blocks=None):
    sh1, sc1, gt1, sh2, sc2, gt2 = (mod[i] for i in range(N_MOD))
    n_small = len(_SHARDED) - len(_BY_CHIP)
    small, big = (next_blocks[:n_small], next_blocks[n_small:]) if next_blocks else (None, None)
    h1 = _normmod_fwd(x, W["g_mix"], sc1, sh1, "mix_norm_fwd")
    p = _matmul(h1, W["w_in"], name="in_proj")
    (osbh, cb), carried = _sb_fwd(p, "sb_attn_fwd", [_gather_exchange(small, "a")] if small else None)
    o_sb = _matmul(osbh, W["w_sb_out"], name="sb_out")
    qn = _rmsnorm_fwd(p, Q_RANK, COL_QLAT // Q_RANK, W["g_q"], "q_lat_norm_fwd")
    kvn = _rmsnorm_fwd(p, KV_RANK, COL_KVLAT // KV_RANK, W["g_kv"], "kv_lat_norm_fwd")
    qp = _matmul(qn, W["w_q_up"], name="q_up")
    kv = _matmul(kvn, W["w_kv_up"], out_dtype=BF16, name="kv_up")
    qr, kpe = _rope_fwd(qp, p, tabs, "rope_fwd")
    (omh, lse), carried = _mla_fwd(
        qr, kv, kpe, "mla_attn_fwd",
        [_gather_exchange(carried[0], "b"), _gather_exchange(big, "a")] if small else None)
    o_mla = _matmul(omh, W["w_mla_out"], name="mla_out")
    merged = _merge_fwd(p, o_sb, o_mla, "merge_fwd")
    y1 = _matmul(merged, W["w_mix_out"], name="mix_out")
    x1 = _res_fwd(x, y1, gt1, "mix_residual")
    h2 = _normmod_fwd(x1, W["g_mlp"], sc2, sh2, "mlp_norm_fwd")

    def sqrelu(t):
        r = jnp.maximum(t, 0.0)
        return t, r * r

    if small:
        (u, a), big_done = _matmul(h2, W["w_up"], b_sharded="col", out_dtype=(F32, BF16), epilogue=sqrelu,
                                   exs=[_gather_exchange(carried[1], "b")], name="mlp_up")
        gathered = list(carried[0]) + list(big_done[0])
    else:
        u, a = _matmul(h2, W["w_up"], b_sharded="col", out_dtype=(F32, BF16), epilogue=sqrelu, name="mlp_up")
        gathered = []
    y2 = _matmul(a, W["w_down"], b_sharded="row", name="mlp_down")
    x2 = _res_fwd(x1, y2, gt2, "mlp_residual")
    saved = dict(x=x, h1=h1, p=p, osbh=osbh, cb=cb, o_sb=o_sb, qn=qn, kvn=kvn, qr=qr, kv=kv, kpe=kpe, omh=omh,
                 lse=lse, o_mla=o_mla, merged=merged, y1=y1, x1=x1, h2=h2, u=u, a=a, y2=y2)
    return x2, saved, gathered


def _layer_bwd(dx2, W, mod, tabs, sv, core, above=None, above_send=None):
    sh1, sc1, gt1, sh2, sc2, gt2 = (mod[i] for i in range(N_MOD))
    dy2, dgt2 = _res_bwd(dx2, sv["y2"], gt2, "mlp_residual_bwd")

    def sqrelu_bwd(da, u):
        return (da * (2.0 * jnp.maximum(u, 0.0)),)

    du = _matmul(dy2, W["w_down"], tb=True, b_sharded="row", out_dtype=BF16, epilogue=sqrelu_bwd, extra=(sv["u"],),
                 exs=[_swap_halves_exchange(above_send)] if above else None, name="mlp_down_dx")
    pending = None
    if above:
        du, (from_sibling,) = du
        pending = [_add_halves(d, r, core, "grads_add_halves") for d, r in zip(above, from_sibling)]

    def with_bf16(t):
        return t, t

    g_down, g_down_send = _matmul(sv["a"], dy2, ta=True, out_dtype=(F32, BF16), epilogue=with_bf16,
                                  out_sharded=("row", W["w_down"].shape), name="mlp_down_dw")
    dh2 = _matmul(du, W["w_up"], tb=True, b_sharded="col", name="mlp_up_dx")
    g_up, g_up_send = _matmul(sv["h2"], du, ta=True, out_dtype=(F32, BF16), epilogue=with_bf16,
                              out_sharded=("col", W["w_up"].shape), name="mlp_up_dw")
    dx1, dsh2, dsc2, dg_mlp = _normmod_bwd(sv["x1"], dh2, W["g_mlp"], sc2, dx2, "mlp_norm_bwd")
    dy1, dgt1 = _res_bwd(dx1, sv["y1"], gt1, "mix_residual_bwd")
    dm = _matmul(dy1, W["w_mix_out"], tb=True, name="mix_out_dx")
    g_mix_out = _matmul(sv["merged"], dy1, ta=True, name="mix_out_dw")
    do_sb, do_mla, dgs, dgm = _merge_bwd(sv["p"], sv["o_sb"], sv["o_mla"], dm, "merge_bwd")
    do_sbh = _matmul(do_sb, W["w_sb_out"], tb=True, name="sb_out_dx")
    g_sb_out = _matmul(sv["osbh"], do_sb, ta=True, name="sb_out_dw")
    (dqs, dks, dvs), carried = _sb_bwd(
        sv["p"], do_sbh, sv["cb"], "sb_attn_bwd", [_chip_scatter_exchange(pending)] if above else None)
    my_sum = [_sum_slots(part, core, "grads_sum_chips") for part in carried[0]] if above else []
    do_mh = _matmul(do_mla, W["w_mla_out"], tb=True, name="mla_out_dx")
    g_mla_out = _matmul(sv["omh"], do_mla, ta=True, name="mla_out_dw")
    (dqr, dkv, dkpe), carried = _mla_bwd(
        sv["qr"], sv["kv"], sv["kpe"], do_mh, sv["omh"], sv["lse"], "mla_attn_bwd",
        [_sibling_gather_exchange(my_sum)] if above else None)
    reduced_above = list(carried[0]) if above else []
    dqp, dkr = _rope_bwd(dqr, dkpe, tabs, "rope_bwd")
    dqn = _matmul(dqp, W["w_q_up"], tb=True, name="q_up_dx")
    g_q_up = _matmul(sv["qn"], dqp, ta=True, name="q_up_dw")
    dkvn = _matmul(dkv, W["w_kv_up"], tb=True, name="kv_up_dx")
    g_kv_up = _matmul(sv["kvn"], dkv, ta=True, name="kv_up_dw")
    dqlat, dg_q = _rmsnorm_bwd(sv["p"], Q_RANK, COL_QLAT // Q_RANK, dqn, W["g_q"], "q_lat_norm_bwd")
    dkvlat, dg_kv = _rmsnorm_bwd(sv["p"], KV_RANK, COL_KVLAT // KV_RANK, dkvn, W["g_kv"], "kv_lat_norm_bwd")
    dp = jnp.concatenate([dgs, dgm, dqs, dks, dvs, dkvlat, dqlat, dkr], axis=1)
    dh1 = _matmul(dp, W["w_in"], tb=True, name="in_proj_dx")
    g_in = _matmul(sv["h1"], dp, ta=True, name="in_proj_dw")
    dx, dsh1, dsc1, dg_mix = _normmod_bwd(sv["x"], dh1, W["g_mix"], sc1, dx1, "mix_norm_bwd")
    grads = dict(w_in=g_in, w_q_up=g_q_up, w_kv_up=g_kv_up, w_sb_out=g_sb_out, w_mla_out=g_mla_out,
                 w_mix_out=g_mix_out, w_up=g_up, w_down=g_down, w_up_send=g_up_send, w_down_send=g_down_send,
                 dmod=jnp.concatenate([dsh1, dsc1, dgt1, dsh2, dsc2, dgt2], axis=0),
                 g_mix=dg_mix, g_mlp=dg_mlp, g_q=dg_q, g_kv=dg_kv)
    return dx, grads, reduced_above


def kernel(x, c, positions, w_ada, b_ada, g_mix_norm, w_in, g_q_lat, w_q_up, g_kv_lat, w_kv_up, w_sb_out, w_mla_out, w_mix_out, g_mlp_norm, w_up, w_down, g_final, loss_target, m_w_ada, m_b_ada, m_g_mix_norm, m_w_in, m_g_q_lat, m_w_q_up, m_g_kv_lat, m_w_kv_up, m_w_sb_out, m_w_mla_out, m_w_mix_out, m_g_mlp_norm, m_w_up, m_w_down, m_g_final, v_w_ada, v_b_ada, v_g_mix_norm, v_w_in, v_g_q_lat, v_w_q_up, v_g_kv_lat, v_w_kv_up, v_w_sb_out, v_w_mla_out, v_w_mix_out, v_g_mlp_norm, v_w_up, v_w_down, v_g_final):
    xi, yi, ci = _place()
    chip = 2 * xi + yi
    batch = 2 * chip + ci
    L = w_ada.shape[0]
    S = x.shape[1]
    shards = dict(w_in=w_in, w_q_up=w_q_up, w_kv_up=w_kv_up, w_sb_out=w_sb_out, w_mla_out=w_mla_out,
                  w_mix_out=w_mix_out, w_up=w_up, w_down=w_down)

    def my_halves(l):
        def half_of(w):
            half = w.shape[1] // 2
            return lax.dynamic_slice_in_dim(w[l].astype(BF16), ci * half, half, 0)

        return [half_of(shards[n]) for n in _SHARDED]

    def layer_weights(l, gathered):
        W = {}
        for n, g in zip(_SHARDED, gathered):
            by_chip = g.reshape((4,) + shards[n].shape[1:])
            W[n] = by_chip if n in _BY_CHIP else _unshard(by_chip, n)
        W["w_in"] = _pad_w_in(W["w_in"])
        W["w_q_up"] = _pad_w_q(W["w_q_up"])
        W["w_mla_out"] = _pad_w_mla(W["w_mla_out"])
        return dict(W, g_mix=g_mix_norm[l:l + 1], g_mlp=g_mlp_norm[l:l + 1], g_q=g_q_lat[l:l + 1],
                    g_kv=g_kv_lat[l:l + 1])

    gathered0 = _all_gather8(my_halves(0), "gather_weights")

    c_act = _silu(c, "silu_c")
    c_all = _all_gather8([jnp.broadcast_to(c_act, (8, D_MODEL))], "gather_c")[0].reshape(8, 8, D_MODEL)[:, 0]
    c16 = jnp.concatenate([c_all, jnp.zeros_like(c_all)], axis=0)
    ada_cols = w_ada.shape[2]
    b_shard = lax.dynamic_slice_in_dim(b_ada, chip * ada_cols, ada_cols, 1)
    mod_part = jnp.stack([_matmul(c16, w_ada[l], name="ada_mod") for l in range(L)])
    mod_part = _bias_add(mod_part, jnp.broadcast_to(b_shard[:, None, :], mod_part.shape), "ada_bias")
    mod_all = _all_gather8([mod_part.reshape(L * 16, ada_cols)], "gather_mod")[0].reshape(4, 2, L, 16, ada_cols)
    mod_mine = lax.dynamic_index_in_dim(mod_all[:, 0], batch, axis=2, keepdims=False)
    mods = mod_mine.transpose(1, 0, 2).reshape(L, N_MOD, 1, D_MODEL)

    tabs = _rope_tables(positions[0])

    xc, saved, layer_w = x[0], [], [layer_weights(0, gathered0)]
    for l in range(L):
        xc, sv, gathered = _layer_fwd(xc, layer_w[l], mods[l], tabs, my_halves(l + 1) if l + 1 < L else None)
        saved.append(sv)
        if l + 1 < L:
            layer_w.append(layer_weights(l + 1, gathered))
    dxc, dg_final, loss_part = _final_loss(xc, loss_target[0], g_final[None, :], "final_norm_loss")
    loss = lax.psum(loss_part[0, 0], ("x", "y", "c"))
    core = jnp.reshape(ci, (1,)).astype(jnp.int32)

    grads, reduced, above, above_send = [None] * L, [None] * L, None, None
    for l in reversed(range(L)):
        dxc, grads[l], reduced_above = _layer_bwd(
            dxc, layer_w[l], mods[l], tabs, saved[l], core, above, above_send)
        if above:
            reduced[l + 1] = reduced_above
        grads[l]["w_in"] = _unpad_w_in(grads[l]["w_in"])
        grads[l]["w_q_up"] = _unpad_w_q(grads[l]["w_q_up"])
        grads[l]["w_mla_out"] = _unpad_w_mla(grads[l]["w_mla_out"])
        above = [grads[l][n] if n in _BY_CHIP else _reshard(grads[l][n], n) for n in _SHARDED]
        above_send = [grads[l][n + "_send"] if n in _BY_CHIP else a for n, a in zip(_SHARDED, above)]
    from_sibling = _run_exchange(_swap_halves_exchange(above_send), "grads_swap_halves")
    pending = [_add_halves(d, r, core, "grads_add_halves") for d, r in zip(above, from_sibling)]
    from_chips = _run_exchange(_chip_scatter_exchange(pending), "grads_chip_scatter")
    my_sum = [_sum_slots(part, core, "grads_sum_chips") for part in from_chips]
    reduced[0] = _run_exchange(_sibling_gather_exchange(my_sum), "grads_sibling_gather")
    grad_x = dxc
    gw = {n: jnp.stack([reduced[l][i] for l in range(L)]) for i, n in enumerate(_SHARDED)}

    def row(v):
        return jnp.pad(v, ((0, 0), (0, D_MODEL - v.shape[1])))

    per_layer_rows = N_MOD + 4
    small = jnp.concatenate(
        [jnp.concatenate([grads[l]["dmod"], row(grads[l]["g_mix"]), row(grads[l]["g_mlp"]),
                          row(grads[l]["g_q"]), row(grads[l]["g_kv"])], axis=0) for l in range(L)]
        + [dg_final], axis=0)
    n_small = -(-small.shape[0] // 8) * 8
    small = jnp.pad(small, ((0, n_small - small.shape[0]), (0, 0)))
    small_all = _all_gather8([small], "gather_vector_grads")[0].reshape(8, n_small, D_MODEL)
    small_sum = _sum_blocks(small_all, "sum_vector_grads")
    lay = small_sum[:L * per_layer_rows].reshape(L, per_layer_rows, D_MODEL)
    g_b_ada = lay[:, :N_MOD].reshape(L, N_MOD * D_MODEL)
    g_g_mix, g_g_mlp = lay[:, N_MOD], lay[:, N_MOD + 1]
    g_g_q, g_g_kv = lay[:, N_MOD + 2, :Q_RANK], lay[:, N_MOD + 3, :KV_RANK]
    g_g_final = small_sum[L * per_layer_rows]
    dmod_all = small_all[:, :L * per_layer_rows].reshape(8, L, per_layer_rows, D_MODEL)[:, :, :N_MOD]
    dmod_all = dmod_all.reshape(8, L, N_MOD * D_MODEL)
    dmod_cols = lax.dynamic_slice_in_dim(dmod_all, chip * ada_cols, ada_cols, 2)
    dmod16 = jnp.concatenate([dmod_cols, jnp.zeros_like(dmod_cols)], axis=0)
    g_w_ada = jnp.stack([_matmul(c16, dmod16[:, l], ta=True, name="ada_dw") for l in range(L)])

    weights = dict(w_ada=w_ada, b_ada=b_ada, g_mix_norm=g_mix_norm, w_in=w_in, g_q_lat=g_q_lat, w_q_up=w_q_up,
                   g_kv_lat=g_kv_lat, w_kv_up=w_kv_up, w_sb_out=w_sb_out, w_mla_out=w_mla_out,
                   w_mix_out=w_mix_out, g_mlp_norm=g_mlp_norm, w_up=w_up, w_down=w_down, g_final=g_final)
    mom = dict(w_ada=(m_w_ada, v_w_ada), b_ada=(m_b_ada, v_b_ada), g_mix_norm=(m_g_mix_norm, v_g_mix_norm),
               w_in=(m_w_in, v_w_in), g_q_lat=(m_g_q_lat, v_g_q_lat), w_q_up=(m_w_q_up, v_w_q_up),
               g_kv_lat=(m_g_kv_lat, v_g_kv_lat), w_kv_up=(m_w_kv_up, v_w_kv_up),
               w_sb_out=(m_w_sb_out, v_w_sb_out), w_mla_out=(m_w_mla_out, v_w_mla_out),
               w_mix_out=(m_w_mix_out, v_w_mix_out), g_mlp_norm=(m_g_mlp_norm, v_g_mlp_norm),
               w_up=(m_w_up, v_w_up), w_down=(m_w_down, v_w_down), g_final=(m_g_final, v_g_final))
    gr = dict(gw, w_ada=g_w_ada, b_ada=g_b_ada, g_mix_norm=g_g_mix, g_q_lat=g_g_q, g_kv_lat=g_g_kv,
              g_mlp_norm=g_g_mlp, g_final=g_g_final)
    order = list(weights)
    deltas, new_m, new_v = [], [], []
    for n in order:
        wv, gv, (mv, vv) = weights[n], gr[n], mom[n]
        if wv.ndim == 1:
            d, nm, nv = (t[0] for t in _adamw(wv[None], gv[None], mv[None], vv[None], "adamw_" + n))
        else:
            d, nm, nv = _adamw(wv, gv, mv, vv, "adamw_" + n)
        deltas.append(d)
        new_m.append(nm)
        new_v.append(nv)
    return (loss, grad_x[None], *[gr[n] for n in order], *deltas, *new_m, *new_v)
```

```python
from typing import Any, Callable, Mapping, NamedTuple, Sequence

import jax
import jax.numpy as jnp
from jax import lax
from jax.experimental import pallas as pl
from jax.experimental.pallas import tpu as pltpu

F32 = jnp.float32
BF16 = jnp.bfloat16
MESH = pl.DeviceIdType.MESH

D_MODEL = 1024
N_HEADS = 8
SB_DIM = 64
SB_WIDTH = 512
Q_RANK = 384
KV_RANK = 256
ROPE_DIM = 32
NOPE_DIM = 64
QK_DIM = 96
D_FF = 4096
N_MOD = 6
EPS = 1e-6
ROPE_THETA = 10000.0
SB_SCALE = SB_DIM ** -0.5
MLA_SCALE = QK_DIM ** -0.5
ADAM_LR, ADAM_B1, ADAM_B2, ADAM_EPS, ADAM_WD, ADAM_STEP = 0.001, 0.9, 0.999, 1e-08, 0.01, 10

LANE = 128
IN_PAD = 4352
COL_GATE_SB, COL_GATE_MLA, COL_QSB, COL_KSB, COL_VSB, COL_KVLAT, COL_QLAT, COL_KROPE = (
    0, 1024, 2048, 2560, 3072, 3584, 3840, 4224)
ROPE_LANE0 = 64
VMEM_LIMIT = 48 * 1024 * 1024
NEG_BIG = -1e30


def _cp(*sem):
    return pltpu.CompilerParams(dimension_semantics=sem, vmem_limit_bytes=VMEM_LIMIT)


def _tile(n, prefs):
    for t in prefs:
        if t <= n and n % t == 0:
            return t
    return n


def _dot(a, b, dims):
    return lax.dot_general(a, b, (dims, ((), ())), preferred_element_type=F32)


def _nn(a, b):
    return _dot(a, b, ((1,), (0,)))


def _nt(a, b):
    return _dot(a, b, ((1,), (1,)))


def _tn(a, b):
    return _dot(a, b, ((0,), (0,)))


def _sharded_dims(shape, kind):
    n, r, cs = shape
    return (n * r, cs) if kind == "row" else (r, n * cs)


def _sharded_spec(shape, kind, t_rows, t_cols, tile_of):
    _, r, cs = shape
    if kind == "row":
        assert r % t_rows == 0, (shape, t_rows)
        per = r // t_rows

        def index(i, j, k):
            tr, tc = tile_of(i, j, k)
            return tr // per, tr % per, tc
    else:
        assert cs % t_cols == 0, (shape, t_cols)
        per = cs // t_cols

        def index(i, j, k):
            tr, tc = tile_of(i, j, k)
            return tc // per, tr, tc % per
    return pl.BlockSpec((None, t_rows, t_cols), index)


def _matmul(a, b, *, ta=False, tb=False, out_dtype=F32, b_sharded=None, out_sharded=None, epilogue=None,
            extra=(), exs=None, name):
    (K, M) = a.shape if ta else a.shape[::-1]
    b_dims = _sharded_dims(b.shape, b_sharded) if b_sharded else b.shape
    (N, Kb) = b_dims if tb else b_dims[::-1]
    assert K == Kb, (a.shape, b.shape, ta, tb)
    tm = _tile(M, (512, 384, 256, 128))
    tn = _tile(N, (1024, 2176, 768, 512, 384, 256, 128))
    tk = _tile(K, (1024, 2176, 768, 512, 384, 256, 128))
    nk = K // tk
    dims = ((0 if ta else 1,), (1 if tb else 0,))

    out_dtypes = out_dtype if isinstance(out_dtype, tuple) else (out_dtype,)
    n_extra, n_o = len(extra), len(out_dtypes)

    def body(a_ref, b_ref, *rest):
        extra_refs, o_refs, acc = rest[:n_extra], rest[n_extra:n_extra + n_o], rest[n_extra + n_o:]
        prod = _dot(a_ref[...].astype(BF16), b_ref[...].astype(BF16), dims)

        def write(total):
            vals = epilogue(total, *[r[...] for r in extra_refs]) if epilogue else (total,)
            for o_ref, val, dt in zip(o_refs, vals, out_dtypes):
                o_ref[...] = val.astype(dt)

        if nk == 1:
            write(prod)
            return
        acc_ref, = acc
        k = pl.program_id(2)

        @pl.when(k == 0)
        def _():
            acc_ref[...] = prod

        @pl.when(k > 0)
        def _():
            acc_ref[...] += prod

        @pl.when(k == nk - 1)
        def _():
            write(acc_ref[...])

    a_spec = (pl.BlockSpec((tk, tm), lambda i, j, k: (k, i)) if ta
              else pl.BlockSpec((tm, tk), lambda i, j, k: (i, k)))
    if b_sharded:
        b_spec = (_sharded_spec(b.shape, b_sharded, tn, tk, lambda i, j, k: (j, k)) if tb
                  else _sharded_spec(b.shape, b_sharded, tk, tn, lambda i, j, k: (k, j)))
    else:
        b_spec = (pl.BlockSpec((tn, tk), lambda i, j, k: (j, k)) if tb
                  else pl.BlockSpec((tk, tn), lambda i, j, k: (k, j)))
    tile = pl.BlockSpec((tm, tn), lambda i, j, k: (i, j))
    if out_sharded:
        kind, shape = out_sharded
        assert _sharded_dims(shape, kind) == (M, N), (shape, kind, M, N)
        out_specs = [_sharded_spec(shape, kind, tm, tn, lambda i, j, k: (i, j)) for _ in out_dtypes]
        out_shape = [jax.ShapeDtypeStruct(shape, dt) for dt in out_dtypes]
    else:
        out_specs = [tile] * n_o
        out_shape = [jax.ShapeDtypeStruct((M, N), dt) for dt in out_dtypes]
    grid = (M // tm, N // tn, nk)
    scratch = [pltpu.VMEM((tm, tn), F32)] if nk > 1 else []
    ins = [a, b, *extra]
    body, more, split = _carry(exs, body, len(ins), n_o, len(scratch), grid)
    own, carried = split(pl.pallas_call(
        body, grid=grid, in_specs=[a_spec, b_spec] + [tile] * n_extra + more["in_specs"],
        out_specs=out_specs + more["out_specs"], out_shape=out_shape + more["out_shape"],
        scratch_shapes=scratch + more["scratch"], input_output_aliases=more["aliases"],
        compiler_params=_cp(*(("arbitrary",) * 3 if exs else ("parallel", "parallel", "arbitrary"))),
        name=name)(*ins, *more["ins"]))
    result = own[0] if n_o == 1 else tuple(own)
    return (result, carried) if exs else result


def _rows(ts, w, col=0):
    return pl.BlockSpec((ts, w), lambda i: (i, col))


def _vec(w):
    return pl.BlockSpec((1, w), lambda i: (0, 0))


def _ts(S):
    return _tile(S, (256, 128))


def _attn_tile(S):
    return _tile(S, (512, 256, 128))


def _rms(x):
    return lax.rsqrt(jnp.mean(x * x, axis=-1, keepdims=True) + EPS)


def _colsum(x):
    return jnp.sum(x, axis=0, keepdims=True)


def _normmod_fwd(x, g, sc, sh, name):
    S, W = x.shape
    ts = _ts(S)

    def body(x_ref, g_ref, sc_ref, sh_ref, h_ref):
        xv = x_ref[...]
        h_ref[...] = ((xv * _rms(xv)) * g_ref[...] * (1.0 + sc_ref[...]) + sh_ref[...]).astype(BF16)

    return pl.pallas_call(
        body, grid=(S // ts,), in_specs=[_rows(ts, W), _vec(W), _vec(W), _vec(W)],
        out_specs=_rows(ts, W), out_shape=jax.ShapeDtypeStruct((S, W), BF16),
        compiler_params=_cp("parallel"), name=name)(x, g, sc, sh)


def _normmod_bwd(x, dh, g, sc, dres, name):
    S, W = x.shape
    ts = _ts(S)

    def body(x_ref, dh_ref, g_ref, sc_ref, dres_ref, dx_ref, dsh_ref, dsc_ref, dg_ref):
        @pl.when(pl.program_id(0) == 0)
        def _():
            dsh_ref[...] = jnp.zeros_like(dsh_ref)
            dsc_ref[...] = jnp.zeros_like(dsc_ref)
            dg_ref[...] = jnp.zeros_like(dg_ref)

        xv, dh_v, gv = x_ref[...], dh_ref[...], g_ref[...]
        r = _rms(xv)
        y = xv * r
        dn = dh_v * (1.0 + sc_ref[...])
        dy = dn * gv
        dx_ref[...] = dres_ref[...] + r * (dy - y * jnp.mean(dy * y, axis=-1, keepdims=True))
        dsh_ref[...] += _colsum(dh_v)
        dsc_ref[...] += _colsum(dh_v * y * gv)
        dg_ref[...] += _colsum(dn * y)

    vec_out = jax.ShapeDtypeStruct((1, W), F32)
    return pl.pallas_call(
        body, grid=(S // ts,),
        in_specs=[_rows(ts, W), _rows(ts, W), _vec(W), _vec(W), _rows(ts, W)],
        out_specs=[_rows(ts, W), _vec(W), _vec(W), _vec(W)],
        out_shape=[jax.ShapeDtypeStruct((S, W), F32), vec_out, vec_out, vec_out],
        compiler_params=_cp("arbitrary"), name=name)(x, dh, g, sc, dres)


def _rmsnorm_fwd(p, width, col, g, name):
    S = p.shape[0]
    ts = _ts(S)

    def body(x_ref, g_ref, y_ref):
        xv = x_ref[...]
        y_ref[...] = ((xv * _rms(xv)) * g_ref[...]).astype(BF16)

    return pl.pallas_call(
        body, grid=(S // ts,), in_specs=[_rows(ts, width, col), _vec(width)],
        out_specs=_rows(ts, width), out_shape=jax.ShapeDtypeStruct((S, width), BF16),
        compiler_params=_cp("parallel"), name=name)(p, g)


def _rmsnorm_bwd(p, width, col, dn, g, name):
    S = p.shape[0]
    ts = _ts(S)

    def body(x_ref, dn_ref, g_ref, dx_ref, dg_ref):
        @pl.when(pl.program_id(0) == 0)
        def _():
            dg_ref[...] = jnp.zeros_like(dg_ref)

        xv, dn_v = x_ref[...], dn_ref[...]
        r = _rms(xv)
        y = xv * r
        dy = dn_v * g_ref[...]
        dx_ref[...] = r * (dy - y * jnp.mean(dy * y, axis=-1, keepdims=True))
        dg_ref[...] += _colsum(dn_v * y)

    return pl.pallas_call(
        body, grid=(S // ts,), in_specs=[_rows(ts, width, col), _rows(ts, width), _vec(width)],
        out_specs=[_rows(ts, width), _vec(width)],
        out_shape=[jax.ShapeDtypeStruct((S, width), F32), jax.ShapeDtypeStruct((1, width), F32)],
        compiler_params=_cp("arbitrary"), name=name)(p, dn, g)


def _rope_rot(t, c, s1, s2):
    return t * c + pltpu.roll(t, LANE - 16, 1) * s1 + pltpu.roll(t, 16, 1) * s2


def _rope_rot_t(d, c, s1, s2):
    return d * c + pltpu.roll(d * s1, 16, 1) + pltpu.roll(d * s2, LANE - 16, 1)


def _rope_fwd(qp, p, tabs, name):
    S = qp.shape[0]
    ts = _ts(S)
    W = N_HEADS * LANE

    def body(q_ref, kr_ref, c_ref, s1_ref, s2_ref, qr_ref, kpe_ref):
        c, s1, s2 = c_ref[...], s1_ref[...], s2_ref[...]
        for h in range(N_HEADS):
            sl = slice(h * LANE, (h + 1) * LANE)
            qr_ref[:, sl] = _rope_rot(q_ref[:, sl], c, s1, s2).astype(BF16)
        kpe_ref[...] = _rope_rot(kr_ref[...], c, s1, s2).astype(BF16)

    tab = _rows(ts, LANE)
    return pl.pallas_call(
        body, grid=(S // ts,), in_specs=[_rows(ts, W), _rows(ts, LANE, COL_KROPE // LANE), tab, tab, tab],
        out_specs=[_rows(ts, W), _rows(ts, LANE)],
        out_shape=[jax.ShapeDtypeStruct((S, W), BF16), jax.ShapeDtypeStruct((S, LANE), BF16)],
        compiler_params=_cp("parallel"), name=name)(qp, p, *tabs)


def _rope_bwd(dqr, dkpe_heads, tabs, name):
    S = dqr.shape[0]
    ts = _ts(S)
    W = N_HEADS * LANE

    def body(dq_ref, dk_ref, c_ref, s1_ref, s2_ref, dqp_ref, dkr_ref):
        c, s1, s2 = c_ref[...], s1_ref[...], s2_ref[...]
        dk = dk_ref[:, 0:LANE]
        for h in range(N_HEADS):
            sl = slice(h * LANE, (h + 1) * LANE)
            dqp_ref[:, sl] = _rope_rot_t(dq_ref[:, sl], c, s1, s2).astype(BF16)
            if h:
                dk = dk + dk_ref[:, sl]
        dkr_ref[...] = _rope_rot_t(dk, c, s1, s2)

    tab = _rows(ts, LANE)
    return pl.pallas_call(
        body, grid=(S // ts,), in_specs=[_rows(ts, W), _rows(ts, W), tab, tab, tab],
        out_specs=[_rows(ts, W), _rows(ts, LANE)],
        out_shape=[jax.ShapeDtypeStruct((S, W), BF16), jax.ShapeDtypeStruct((S, LANE), F32)],
        compiler_params=_cp("parallel"), name=name)(dqr, dkpe_heads, *tabs)


def _merge_fwd(p, o_sb, o_mla, name):
    S, W = o_sb.shape
    ts = _ts(S)

    def body(gs_ref, gm_ref, a_ref, b_ref, m_ref):
        m_ref[...] = (jax.nn.sigmoid(gs_ref[...]) * a_ref[...]
                      + jax.nn.sigmoid(gm_ref[...]) * b_ref[...]).astype(BF16)

    return pl.pallas_call(
        body, grid=(S // ts,),
        in_specs=[_rows(ts, W, COL_GATE_SB // W), _rows(ts, W, COL_GATE_MLA // W), _rows(ts, W), _rows(ts, W)],
        out_specs=_rows(ts, W), out_shape=jax.ShapeDtypeStruct((S, W), BF16),
        compiler_params=_cp("parallel"), name=name)(p, p, o_sb, o_mla)


def _merge_bwd(p, o_sb, o_mla, dm, name):
    S, W = o_sb.shape
    ts = _ts(S)

    def body(gs_ref, gm_ref, a_ref, b_ref, dm_ref, da_ref, db_ref, dgs_ref, dgm_ref):
        dmv = dm_ref[...]
        sa, sb = jax.nn.sigmoid(gs_ref[...]), jax.nn.sigmoid(gm_ref[...])
        da_ref[...] = (dmv * sa).astype(BF16)
        db_ref[...] = (dmv * sb).astype(BF16)
        dgs_ref[...] = dmv * a_ref[...] * sa * (1.0 - sa)
        dgm_ref[...] = dmv * b_ref[...] * sb * (1.0 - sb)

    row = _rows(ts, W)
    return pl.pallas_call(
        body, grid=(S // ts,),
        in_specs=[_rows(ts, W, COL_GATE_SB // W), _rows(ts, W, COL_GATE_MLA // W), row, row, row],
        out_specs=[row, row, row, row],
        out_shape=[jax.ShapeDtypeStruct((S, W), BF16), jax.ShapeDtypeStruct((S, W), BF16),
                   jax.ShapeDtypeStruct((S, W), F32), jax.ShapeDtypeStruct((S, W), F32)],
        compiler_params=_cp("parallel"), name=name)(p, p, o_sb, o_mla, dm)


def _res_fwd(x, y, gate, name):
    S, W = x.shape
    ts = _ts(S)

    def body(x_ref, y_ref, g_ref, o_ref):
        o_ref[...] = x_ref[...] + g_ref[...] * y_ref[...]

    return pl.pallas_call(
        body, grid=(S // ts,), in_specs=[_rows(ts, W), _rows(ts, W), _vec(W)], out_specs=_rows(ts, W),
        out_shape=jax.ShapeDtypeStruct((S, W), F32), compiler_params=_cp("parallel"), name=name)(x, y, gate)


def _res_bwd(dx, y, gate, name):
    S, W = dx.shape
    ts = _ts(S)

    def body(dx_ref, y_ref, g_ref, dy_ref, dg_ref):
        @pl.when(pl.program_id(0) == 0)
        def _():
            dg_ref[...] = jnp.zeros_like(dg_ref)

        dxv = dx_ref[...]
        dy_ref[...] = (g_ref[...] * dxv).astype(BF16)
        dg_ref[...] += _colsum(dxv * y_ref[...])

    return pl.pallas_call(
        body, grid=(S // ts,), in_specs=[_rows(ts, W), _rows(ts, W), _vec(W)],
        out_specs=[_rows(ts, W), _vec(W)],
        out_shape=[jax.ShapeDtypeStruct((S, W), BF16), jax.ShapeDtypeStruct((1, W), F32)],
        compiler_params=_cp("arbitrary"), name=name)(dx, y, gate)


def _final_loss(x, target, g, name):
    S, W = x.shape
    ts = _ts(S)

    def body(x_ref, t_ref, g_ref, dx_ref, dg_ref, loss_ref):
        @pl.when(pl.program_id(0) == 0)
        def _():
            dg_ref[...] = jnp.zeros_like(dg_ref)
            loss_ref[...] = jnp.zeros_like(loss_ref)

        xv, gv = x_ref[...], g_ref[...]
        r = _rms(xv)
        y = xv * r
        err = y * gv - t_ref[...]
        loss_ref[...] += jnp.full((1, LANE), 0.5 * jnp.sum(jnp.mean(err * err, axis=-1)), F32)
        dout = err * (1.0 / W)
        dy = dout * gv
        dx_ref[...] = r * (dy - y * jnp.mean(dy * y, axis=-1, keepdims=True))
        dg_ref[...] += _colsum(dout * y)

    return pl.pallas_call(
        body, grid=(S // ts,), in_specs=[_rows(ts, W), _rows(ts, W), _vec(W)],
        out_specs=[_rows(ts, W), _vec(W), _vec(LANE)],
        out_shape=[jax.ShapeDtypeStruct((S, W), F32), jax.ShapeDtypeStruct((1, W), F32),
                   jax.ShapeDtypeStruct((1, LANE), F32)],
        compiler_params=_cp("arbitrary"), name=name)(x, target, g)


def _silu(c, name):
    def body(c_ref, o_ref):
        cv = c_ref[...]
        o_ref[...] = cv * jax.nn.sigmoid(cv)

    return pl.pallas_call(body, out_shape=jax.ShapeDtypeStruct(c.shape, F32), name=name)(c)


def _bias_add(a, b, name):
    def body(a_ref, b_ref, o_ref):
        o_ref[...] = a_ref[...] + b_ref[...]

    return pl.pallas_call(body, out_shape=jax.ShapeDtypeStruct(a.shape, F32), name=name)(a, b)


def _sum_blocks(xs, name):
    n = xs.shape[0]

    def body(x_ref, o_ref):
        acc = x_ref[0]
        for d in range(1, n):
            acc = acc + x_ref[d]
        o_ref[...] = acc

    return pl.pallas_call(body, out_shape=jax.ShapeDtypeStruct(xs.shape[1:], F32), name=name)(xs)


def _adamw(w, g, m, v, name):
    shape = w.shape
    cols = shape[-1]
    w2, g2, m2, v2 = (t.reshape(-1, cols) for t in (w, g, m, v))
    rows = w2.shape[0]
    tr = _tile(rows, (128,))
    c1 = 1.0 - ADAM_B1 ** ADAM_STEP
    c2 = 1.0 - ADAM_B2 ** ADAM_STEP

    def body(w_ref, g_ref, m_ref, v_ref, d_ref, nm_ref, nv_ref):
        gv = g_ref[...]
        nm = ADAM_B1 * m_ref[...] + (1.0 - ADAM_B1) * gv
        nv = ADAM_B2 * v_ref[...] + (1.0 - ADAM_B2) * (gv * gv)
        d_ref[...] = -ADAM_LR * ((nm / c1) / (jnp.sqrt(nv / c2) + ADAM_EPS) + ADAM_WD * w_ref[...])
        nm_ref[...] = nm
        nv_ref[...] = nv

    spec = pl.BlockSpec((tr, cols), lambda i: (i, 0))
    out = jax.ShapeDtypeStruct((rows, cols), F32)
    d, nm, nv = pl.pallas_call(
        body, grid=(rows // tr,), in_specs=[spec] * 4, out_specs=[spec] * 3, out_shape=[out] * 3,
        compiler_params=_cp("parallel"), name=name)(w2, g2, m2, v2)
    return d.reshape(shape), nm.reshape(shape), nv.reshape(shape)


def _split_dot(x, tri):
    hi = x.astype(BF16)
    lo = (x - hi.astype(F32)).astype(BF16)
    return _nn(hi, tri) + _nn(lo, tri)


def _tri_cumsum(x, tri, later, split=True):
    h = x.shape[1] // 2
    first, second = x[:, :h], x[:, h:]
    sum_first = jnp.sum(first, axis=1, keepdims=True)
    sum_second = jnp.sum(second, axis=1, keepdims=True)
    dot = _split_dot if split else (lambda v, m: _nn(v.astype(BF16), m))
    run_first, run_second = dot(first, tri), dot(second, tri)
    if later:
        run_first = run_first + sum_second
    else:
        run_second = run_second + sum_first
    return jnp.concatenate([run_first, run_second], axis=1), sum_first + sum_second


def _sb_logs(z):
    soft = jnp.log(1.0 + jnp.exp(-jnp.abs(z)))
    return jnp.minimum(z, 0.0) - soft, -jnp.maximum(z, 0.0) - soft


def _attn_call(body, grid, ins, in_specs, out_specs, out_shape, scratch, exs, name):
    body, extra, split = _carry(exs, body, len(ins), len(out_shape), len(scratch), grid)
    return split(pl.pallas_call(
        body, grid=grid, in_specs=in_specs + extra["in_specs"], out_specs=out_specs + extra["out_specs"],
        out_shape=out_shape + extra["out_shape"], scratch_shapes=scratch + extra["scratch"],
        input_output_aliases=extra["aliases"], compiler_params=_cp("arbitrary", "arbitrary"),
        name=name)(*ins, *extra["ins"]))


def _sb_fwd(p, name, exs=None):
    S = p.shape[0]
    t = _attn_tile(S)
    qb, kb, vb = COL_QSB // LANE, COL_KSB // LANE, COL_VSB // LANE

    def body(q_ref, k_ref, v_ref, o_ref, cb_ref, acc_ref):
        i = pl.program_id(1)
        lane = lax.broadcasted_iota(jnp.int32, (t, LANE), 1)
        rows = lax.broadcasted_iota(jnp.int32, (t, t), 0)
        cols = lax.broadcasted_iota(jnp.int32, (t, t), 1)
        half_r = lax.broadcasted_iota(jnp.int32, (t // 2, t // 2), 0)
        half_c = lax.broadcasted_iota(jnp.int32, (t // 2, t // 2), 1)
        after = jnp.where(half_r > half_c, 1.0, 0.0).astype(BF16)
        diag = cols < rows
        q = q_ref[...] * SB_SCALE
        acc_ref[...] = jnp.zeros_like(acc_ref)
        cb_ref[...] = jnp.zeros_like(cb_ref)
        hms = [(lane >= SB_DIM * h) & (lane < SB_DIM * (h + 1)) for h in range(2)]
        qhs = [jnp.where(hm, q, 0.0).astype(BF16) for hm in hms]

        def step(j, cs, masked):
            rows_j = pl.ds(pl.multiple_of(j * t, t), t)
            kj = k_ref[rows_j, :].astype(BF16)
            vf = v_ref[rows_j, :]
            out, pv = [], None
            for h in range(2):
                ls, lf = _sb_logs(_nt(qhs[h], kj))
                if masked:
                    lf = jnp.where(diag, lf, 0.0)
                survive, total = _tri_cumsum(lf, after, True)
                a = jnp.exp(ls + survive + cs[h])
                if masked:
                    a = jnp.where(diag, a, 0.0)
                term = _nn(a.astype(BF16), jnp.where(hms[h], vf, 0.0).astype(BF16))
                pv = term if pv is None else pv + term
                cb_ref[h] = jnp.where(lane == j, cs[h], cb_ref[h])
                out.append(cs[h] + total)
            acc_ref[...] += pv
            return tuple(out)

        zero = jnp.zeros((t, 1), F32)
        cs = step(i, (zero, zero), True)
        lax.fori_loop(0, i, lambda it, cs: step(i - 1 - it, cs, False), cs)
        o_ref[...] = acc_ref[...].astype(BF16)

    return _attn_call(
        body, (SB_WIDTH // LANE, S // t), [p, p, p],
        [pl.BlockSpec((t, LANE), lambda hp, i: (i, qb + hp)),
         pl.BlockSpec((S, LANE), lambda hp, i: (0, kb + hp)),
         pl.BlockSpec((S, LANE), lambda hp, i: (0, vb + hp))],
        [pl.BlockSpec((t, LANE), lambda hp, i: (i, hp)),
         pl.BlockSpec((2, t, LANE), lambda hp, i: (hp, i, 0))],
        [jax.ShapeDtypeStruct((S, SB_WIDTH), BF16), jax.ShapeDtypeStruct((N_HEADS, S, LANE), F32)],
        [pltpu.VMEM((t, LANE), F32)], exs, name)


def _sb_bwd(p, do, cb, name, exs=None):
    S = p.shape[0]
    t = _attn_tile(S)
    qb, kb, vb = COL_QSB // LANE, COL_KSB // LANE, COL_VSB // LANE

    def body(q_ref, k_ref, v_ref, do_ref, cb_ref, dq_ref, dk_ref, dv_ref, acc_ref):
        i = pl.program_id(1)

        @pl.when(i == 0)
        def _():
            dk_ref[...] = jnp.zeros_like(dk_ref)
            dv_ref[...] = jnp.zeros_like(dv_ref)

        lane = lax.broadcasted_iota(jnp.int32, (t, LANE), 1)
        rows = lax.broadcasted_iota(jnp.int32, (t, t), 0)
        cols = lax.broadcasted_iota(jnp.int32, (t, t), 1)
        half_r = lax.broadcasted_iota(jnp.int32, (t // 2, t // 2), 0)
        half_c = lax.broadcasted_iota(jnp.int32, (t // 2, t // 2), 1)
        after = jnp.where(half_r > half_c, 1.0, 0.0).astype(BF16)
        before = jnp.where(half_r < half_c, 1.0, 0.0).astype(BF16)
        diag = cols < rows
        q = q_ref[...] * SB_SCALE
        dov = do_ref[...]
        acc_ref[...] = jnp.zeros_like(acc_ref)
        hms = [(lane >= SB_DIM * h) & (lane < SB_DIM * (h + 1)) for h in range(2)]
        qhs = [jnp.where(hm, q, 0.0).astype(BF16) for hm in hms]
        dohs = [jnp.where(hm, dov, 0.0).astype(BF16) for hm in hms]

        def step(j, fs, masked):
            rows_j = pl.ds(pl.multiple_of(j * t, t), t)
            kf = k_ref[rows_j, :]
            kj = kf.astype(BF16)
            vj = v_ref[rows_j, :].astype(BF16)
            out, dq_t, dk_t, dv_t = [], None, None, None
            for h in range(2):
                ls, lf = _sb_logs(_nt(qhs[h], kj))
                if masked:
                    lf = jnp.where(diag, lf, 0.0)
                c = jnp.sum(jnp.where(lane == j, cb_ref[h], 0.0), axis=1, keepdims=True)
                a = jnp.exp(ls + _tri_cumsum(lf, after, True)[0] + c)
                if masked:
                    a = jnp.where(diag, a, 0.0)
                dl = _nt(dohs[h], vj) * a
                sg = jnp.exp(ls)
                earlier, total = _tri_cumsum(dl, before, False, split=False)
                dz = dl * (1.0 - sg) - sg * (earlier + fs[h])
                if masked:
                    dz = jnp.where(diag, dz, 0.0)
                dzb = dz.astype(BF16)
                terms = (_nn(dzb, jnp.where(hms[h], kf, 0.0).astype(BF16)), _tn(dzb, qhs[h]),
                         _tn(a.astype(BF16), dohs[h]))
                dq_t, dk_t, dv_t = terms if dq_t is None else (dq_t + terms[0], dk_t + terms[1], dv_t + terms[2])
                out.append(fs[h] + total)
            acc_ref[...] += dq_t
            dk_ref[rows_j, :] += dk_t
            dv_ref[rows_j, :] += dv_t
            return tuple(out)

        zero = jnp.zeros((t, 1), F32)
        fs = lax.fori_loop(0, i, lambda j, fs: step(j, fs, False), (zero, zero))
        step(i, fs, True)
        dq_ref[...] = acc_ref[...] * SB_SCALE

    col = lambda hp, i: (0, hp)
    out = jax.ShapeDtypeStruct((S, SB_WIDTH), F32)
    return _attn_call(
        body, (SB_WIDTH // LANE, S // t), [p, p, p, do, cb],
        [pl.BlockSpec((t, LANE), lambda hp, i: (i, qb + hp)),
         pl.BlockSpec((S, LANE), lambda hp, i: (0, kb + hp)),
         pl.BlockSpec((S, LANE), lambda hp, i: (0, vb + hp)),
         pl.BlockSpec((t, LANE), lambda hp, i: (i, hp)),
         pl.BlockSpec((2, t, LANE), lambda hp, i: (hp, i, 0))],
        [pl.BlockSpec((t, LANE), lambda hp, i: (i, hp)), pl.BlockSpec((S, LANE), col), pl.BlockSpec((S, LANE), col)],
        [out, out, out], [pltpu.VMEM((t, LANE), F32)], exs, name)


def _mla_fwd(qr, kv, kpe, name, exs=None):
    S = qr.shape[0]
    t = _attn_tile(S)

    def body(q_ref, kv_ref, kpe_ref, o_ref, lse_ref, acc_ref, m_ref):
        i = pl.program_id(1)
        low = lax.broadcasted_iota(jnp.int32, (t, LANE), 1) < NOPE_DIM
        rows = lax.broadcasted_iota(jnp.int32, (t, t), 0)
        cols = lax.broadcasted_iota(jnp.int32, (t, t), 1)
        causal = cols <= rows
        one = jnp.ones((t, LANE), BF16)
        heads = [slice(h * LANE, (h + 1) * LANE) for h in range(2)]
        qs = [q_ref[:, sl] for sl in heads]
        acc_ref[...] = jnp.zeros_like(acc_ref)
        m_ref[...] = jnp.full_like(m_ref, NEG_BIG)

        def step(j, masked):
            rows_j = pl.ds(pl.multiple_of(j * t, t), t)
            kpe_j = kpe_ref[rows_j, :]
            for h, sl in enumerate(heads):
                kvj = kv_ref[rows_j, sl]
                z = _nt(qs[h], jnp.where(low, kvj, kpe_j)) * MLA_SCALE
                if masked:
                    z = jnp.where(causal, z, NEG_BIG)
                m_old = m_ref[h]
                m_new = jnp.maximum(m_old, jnp.max(z, axis=1, keepdims=True))
                pr = jnp.exp(z - m_new)
                acc_ref[:, sl] = jnp.exp(m_old - m_new) * acc_ref[:, sl] + _nn(
                    pr.astype(BF16), jnp.where(low, one, kvj))
                m_ref[h] = m_new

        def loop(j, carry):
            step(j, False)
            return carry

        lax.fori_loop(0, i, loop, 0)
        step(i, True)
        for h, sl in enumerate(heads):
            acc = acc_ref[:, sl]
            den = acc[:, 0:1]
            o_ref[:, sl] = jnp.where(low, 0.0, acc / den).astype(BF16)
            lse_ref[h] = jnp.broadcast_to(m_ref[h] + jnp.log(den), (t, LANE))

    pair = 2 * LANE
    return _attn_call(
        body, (N_HEADS // 2, S // t), [qr, kv, kpe],
        [pl.BlockSpec((t, pair), lambda hp, i: (i, hp)),
         pl.BlockSpec((S, pair), lambda hp, i: (0, hp)),
         pl.BlockSpec((S, LANE), lambda hp, i: (0, 0))],
        [pl.BlockSpec((t, pair), lambda hp, i: (i, hp)), pl.BlockSpec((2, t, LANE), lambda hp, i: (hp, i, 0))],
        [jax.ShapeDtypeStruct((S, N_HEADS * LANE), BF16), jax.ShapeDtypeStruct((N_HEADS, S, LANE), F32)],
        [pltpu.VMEM((t, pair), F32), pltpu.VMEM((2, t, 1), F32)], exs, name)


def _mla_bwd(qr, kv, kpe, do, o, lse, name, exs=None):
    S = qr.shape[0]
    t = _attn_tile(S)

    def body(q_ref, kv_ref, kpe_ref, do_ref, o_ref, lse_ref, dq_ref, dkv_ref, dkpe_ref, acc_ref):
        i = pl.program_id(1)

        @pl.when(i == 0)
        def _():
            dkv_ref[...] = jnp.zeros_like(dkv_ref)
            dkpe_ref[...] = jnp.zeros_like(dkpe_ref)

        low = lax.broadcasted_iota(jnp.int32, (t, LANE), 1) < NOPE_DIM
        rows = lax.broadcasted_iota(jnp.int32, (t, t), 0)
        cols = lax.broadcasted_iota(jnp.int32, (t, t), 1)
        causal = cols <= rows
        heads = [slice(h * LANE, (h + 1) * LANE) for h in range(2)]
        qs = [q_ref[:, sl] for sl in heads]
        dovs = [do_ref[:, sl] for sl in heads]
        dobs = [d.astype(BF16) for d in dovs]
        deltas = [jnp.sum(dovs[h] * o_ref[:, sl].astype(F32), axis=1, keepdims=True) for h, sl in enumerate(heads)]
        lses = [lse_ref[h][:, 0:1] for h in range(2)]
        acc_ref[...] = jnp.zeros_like(acc_ref)

        def step(j, masked):
            rows_j = pl.ds(pl.multiple_of(j * t, t), t)
            kpe_j = kpe_ref[rows_j, :]
            for h, sl in enumerate(heads):
                kvj = kv_ref[rows_j, sl]
                kcat = jnp.where(low, kvj, kpe_j)
                z = _nt(qs[h], kcat) * MLA_SCALE
                if masked:
                    z = jnp.where(causal, z, NEG_BIG)
                pr = jnp.exp(z - lses[h])
                ds = (pr * (_nt(dobs[h], kvj) - deltas[h])).astype(BF16)
                acc_ref[:, sl] += _nn(ds, kcat)
                dkc = _tn(ds, qs[h]) * MLA_SCALE
                dkv_ref[rows_j, sl] += jnp.where(low, dkc, _tn(pr.astype(BF16), dobs[h]))
                dkpe_ref[rows_j, sl] += jnp.where(low, 0.0, dkc)

        def loop(j, carry):
            step(j, False)
            return carry

        lax.fori_loop(0, i, loop, 0)
        step(i, True)
        dq_ref[...] = acc_ref[...] * MLA_SCALE

    pair = 2 * LANE
    blk = pl.BlockSpec((t, pair), lambda hp, i: (i, hp))
    col = pl.BlockSpec((S, pair), lambda hp, i: (0, hp))
    out = jax.ShapeDtypeStruct((S, N_HEADS * LANE), F32)
    return _attn_call(
        body, (N_HEADS // 2, S // t), [qr, kv, kpe, do, o, lse],
        [blk, col, pl.BlockSpec((S, LANE), lambda hp, i: (0, 0)), blk, blk,
         pl.BlockSpec((2, t, LANE), lambda hp, i: (hp, i, 0))],
        [blk, col, col], [out, out, out], [pltpu.VMEM((t, pair), F32)], exs, name)


_ANY = pl.BlockSpec(memory_space=pl.ANY)


def _place():
    return lax.axis_index("x"), lax.axis_index("y"), lax.axis_index("c")


class _Exchange(NamedTuple):
    ins: Sequence[Any]
    outs: Sequence[Any]
    aliases: Mapping[int, int]
    n_remote: int
    n_local: int
    start: Callable
    finish: Callable


def _exchange_scratch(ex):
    return [pltpu.SemaphoreType.DMA((ex.n_remote,)), pltpu.SemaphoreType.DMA((ex.n_remote,)),
            pltpu.SemaphoreType.DMA((ex.n_local,))]


def _run_exchange(ex, name):
    n_in, n_out = len(ex.ins), len(ex.outs)

    def body(*refs):
        args = (refs[:n_in], refs[n_in:n_in + n_out], *refs[n_in + n_out:])
        ex.start(*args)
        ex.finish(*args)

    return pl.pallas_call(
        body, out_shape=list(ex.outs), in_specs=[_ANY] * n_in, out_specs=[_ANY] * n_out,
        scratch_shapes=_exchange_scratch(ex), input_output_aliases=dict(ex.aliases), name=name)(*ex.ins)


def _carry(exs, body, n_in, n_out, n_scratch, grid):
    exs = [ex for ex in (exs or []) if ex is not None]
    e_ins, e_outs = [len(ex.ins) for ex in exs], [len(ex.outs) for ex in exs]

    def take(refs, counts):
        groups = []
        for n in counts:
            groups.append(refs[:n])
            refs = refs[n:]
        return groups, refs

    def carried(*refs):
        own_in, refs = refs[:n_in], refs[n_in:]
        ex_in, refs = take(refs, e_ins)
        own_out, refs = refs[:n_out], refs[n_out:]
        ex_out, refs = take(refs, e_outs)
        own_scratch, refs = refs[:n_scratch], refs[n_scratch:]
        sems, _ = take(refs, [3] * len(exs))
        at = [pl.program_id(d) for d in range(len(grid))]
        first, last = at[0] == 0, at[0] == grid[0] - 1
        for d in range(1, len(grid)):
            first, last = first & (at[d] == 0), last & (at[d] == grid[d] - 1)

        @pl.when(first)
        def _():
            for e, ex in enumerate(exs):
                ex.start(ex_in[e], ex_out[e], *sems[e])

        body(*own_in, *own_out, *own_scratch)

        @pl.when(last)
        def _():
            for e, ex in enumerate(exs):
                ex.finish(ex_in[e], ex_out[e], *sems[e])

    aliases, i0, o0 = {}, n_in, n_out
    for ex in exs:
        aliases.update({i0 + i: o0 + o for i, o in ex.aliases.items()})
        i0, o0 = i0 + len(ex.ins), o0 + len(ex.outs)

    def split(res):
        groups, _ = take(list(res[n_out:]), e_outs)
        return list(res[:n_out]), groups

    extra = dict(
        ins=[a for ex in exs for a in ex.ins], in_specs=[_ANY] * sum(e_ins), out_specs=[_ANY] * sum(e_outs),
        out_shape=[o for ex in exs for o in ex.outs], scratch=[s for ex in exs for s in _exchange_scratch(ex)],
        aliases=aliases)
    return (carried if exs else body), extra, split


def _gather_exchange(arrs, phase="all"):
    n_t = len(arrs)
    ms = [a.shape[0] // (8 if phase == "b" else 1) for a in arrs]

    def plan(in_refs, out_refs, send_sems, recv_sems, local_sems):
        x, y, c = _place()
        me, sibling = (x, y, c), (x, y, 1 - c)
        chips = [(1 - x, y), (x, 1 - y), (1 - x, 1 - y)]

        def rows(ref, t, px, py, pc):
            return ref.at[pl.ds((4 * px + 2 * py + pc) * ms[t], ms[t]), :]

        def copy(t, k, block, to, src):
            return pltpu.make_async_remote_copy(
                src_ref=src, dst_ref=rows(out_refs[t], t, *block), send_sem=send_sems.at[7 * t + k],
                recv_sem=recv_sems.at[7 * t + k], device_id=to, device_id_type=MESH)

        mine, first, first_in, passed, passed_in = [], [], [], [], []
        for t in range(n_t):
            if phase != "b":
                mine.append(pltpu.make_async_copy(in_refs[t], rows(out_refs[t], t, *me), local_sems.at[t]))
                first.append(copy(t, 0, me, sibling, in_refs[t]))
                first_in.append(copy(t, 0, sibling, me, in_refs[t]))
                for j, chip in enumerate(chips):
                    first.append(copy(t, 1 + j, me, (*chip, c), in_refs[t]))
                    first_in.append(copy(t, 1 + j, (*chip, c), me, in_refs[t]))
            if phase != "a":
                held = in_refs[t] if phase == "b" else out_refs[t]
                for j, chip in enumerate(chips):
                    passed.append(copy(t, 4 + j, (*chip, c), sibling, rows(held, t, *chip, c)))
                    passed_in.append(copy(t, 4 + j, (*chip, 1 - c), me, rows(held, t, *chip, c)))
        return mine, first, first_in, passed, passed_in

    def start(*refs):
        mine, first, _, passed, _ = plan(*refs)
        for cp in mine + first + (passed if phase == "b" else []):
            cp.start()

    def finish(*refs):
        mine, first, first_in, passed, passed_in = plan(*refs)
        for cp in first_in:
            cp.wait_recv()
        if phase == "all":
            for cp in passed:
                cp.start()
        for cp in passed_in:
            cp.wait_recv()
        for cp in first + passed:
            cp.wait_send()
        for cp in mine:
            cp.wait()

    if phase == "b":
        outs = [jax.ShapeDtypeStruct(a.shape, a.dtype) for a in arrs]
        aliases = {t: t for t in range(n_t)}
    else:
        outs = [jax.ShapeDtypeStruct((8 * a.shape[0], a.shape[1]), a.dtype) for a in arrs]
        aliases = {}
    return _Exchange(list(arrs), outs, aliases, 7 * n_t, n_t, start, finish)


def _all_gather8(blks, name):
    return _run_exchange(_gather_exchange(blks), name)


def _swap_halves_exchange(gs):
    n_t = len(gs)

    def plan(g_refs, out_refs, send_sems, recv_sems, local_sems):
        x, y, c = _place()
        copies = []
        for t in range(n_t):
            m = gs[t].shape[1] // 2
            copies += [pltpu.make_async_remote_copy(
                src_ref=g_refs[t].at[s, pl.ds((1 - c) * m, m), :], dst_ref=out_refs[t].at[s],
                send_sem=send_sems.at[4 * t + s], recv_sem=recv_sems.at[4 * t + s], device_id=(x, y, 1 - c),
                device_id_type=MESH) for s in range(4)]
        return copies

    def start(*refs):
        for cp in plan(*refs):
            cp.start()

    def finish(*refs):
        for cp in plan(*refs):
            cp.wait()

    outs = [jax.ShapeDtypeStruct((4, g.shape[1] // 2, g.shape[2]), g.dtype) for g in gs]
    return _Exchange(list(gs), outs, {}, 4 * n_t, 1, start, finish)


def _chip_scatter_exchange(parts):
    n_t = len(parts)

    def plan(p_refs, out_refs, send_sems, recv_sems, local_sems):
        x, y, c = _place()
        mine = 2 * x + y
        chips = [(1 - x, y), (x, 1 - y), (1 - x, 1 - y)]

        def copy(t, j, src_slot, dst_slot):
            px, py = chips[j]
            return pltpu.make_async_remote_copy(
                src_ref=p_refs[t].at[src_slot], dst_ref=out_refs[t].at[dst_slot],
                send_sem=send_sems.at[3 * t + j], recv_sem=recv_sems.at[3 * t + j], device_id=(px, py, c),
                device_id_type=MESH)

        own = [pltpu.make_async_copy(p_refs[t].at[mine], out_refs[t].at[mine], local_sems.at[t])
               for t in range(n_t)]
        sends = [copy(t, j, 2 * px + py, mine) for t in range(n_t) for j, (px, py) in enumerate(chips)]
        arrivals = [copy(t, j, mine, 2 * px + py) for t in range(n_t) for j, (px, py) in enumerate(chips)]
        return own, sends, arrivals

    def start(*refs):
        own, sends, _ = plan(*refs)
        for cp in own + sends:
            cp.start()

    def finish(*refs):
        own, sends, arrivals = plan(*refs)
        for cp in arrivals:
            cp.wait_recv()
        for cp in sends:
            cp.wait_send()
        for cp in own:
            cp.wait()

    outs = [jax.ShapeDtypeStruct(p.shape, p.dtype) for p in parts]
    return _Exchange(list(parts), outs, {}, 3 * n_t, n_t, start, finish)


def _sibling_gather_exchange(bufs):
    n_t = len(bufs)

    def plan(b_refs, out_refs, send_sems, recv_sems, local_sems):
        x, y, c = _place()

        def copy(t, pc):
            m = bufs[t].shape[0] // 2
            half = pl.ds(pc * m, m)
            return pltpu.make_async_remote_copy(
                src_ref=b_refs[t].at[half, :], dst_ref=out_refs[t].at[half, :], send_sem=send_sems.at[t],
                recv_sem=recv_sems.at[t], device_id=(x, y, 1 - c), device_id_type=MESH)

        return [copy(t, c) for t in range(n_t)], [copy(t, 1 - c) for t in range(n_t)]

    def start(*refs):
        for cp in plan(*refs)[0]:
            cp.start()

    def finish(*refs):
        sends, arrivals = plan(*refs)
        for cp in arrivals:
            cp.wait_recv()
        for cp in sends:
            cp.wait_send()

    outs = [jax.ShapeDtypeStruct(b.shape, b.dtype) for b in bufs]
    return _Exchange(list(bufs), outs, {t: t for t in range(n_t)}, n_t, 1, start, finish)


def _add_halves(g, recv, c, name):
    n_slot, m2, n = g.shape
    m = m2 // 2
    tr = _tile(m, (512, 256, 192, 128, 16))

    def body(c_ref, g_ref, r_ref, o_ref):
        o_ref[...] = (g_ref[...] + r_ref[...].astype(F32)).astype(BF16)

    nb = m // tr
    return pl.pallas_call(
        body,
        grid_spec=pltpu.PrefetchScalarGridSpec(
            num_scalar_prefetch=1, grid=(n_slot, nb),
            in_specs=[pl.BlockSpec((1, tr, n), lambda s, i, c_ref: (s, c_ref[0] * nb + i, 0)),
                      pl.BlockSpec((1, tr, n), lambda s, i, c_ref: (s, i, 0))],
            out_specs=pl.BlockSpec((1, tr, n), lambda s, i, c_ref: (s, i, 0))),
        out_shape=jax.ShapeDtypeStruct((n_slot, m, n), BF16),
        compiler_params=_cp("parallel", "parallel"), name=name)(c, g, recv)


def _sum_slots(parts, c, name):
    n_slot, m, n = parts.shape
    tr = _tile(m, (512, 256, 192, 128, 16))
    nb = m // tr

    def body(c_ref, p_ref, o_ref):
        acc = p_ref[0].astype(F32)
        for s in range(1, n_slot):
            acc = acc + p_ref[s].astype(F32)
        o_ref[...] = acc

    return pl.pallas_call(
        body,
        grid_spec=pltpu.PrefetchScalarGridSpec(
            num_scalar_prefetch=1, grid=(nb,),
            in_specs=[pl.BlockSpec((n_slot, tr, n), lambda i, c_ref: (0, i, 0))],
            out_specs=pl.BlockSpec((tr, n), lambda i, c_ref: (c_ref[0] * nb + i, 0))),
        out_shape=jax.ShapeDtypeStruct((2 * m, n), F32),
        compiler_params=_cp("parallel"), name=name)(c, parts)


_SHARDED = ("w_in", "w_q_up", "w_kv_up", "w_sb_out", "w_mla_out", "w_mix_out", "w_up", "w_down")
_ROW_SHARDED = ("w_mix_out", "w_down")
_BY_CHIP = ("w_up", "w_down")


def _unshard(parts, name):
    n, r, cs = parts.shape
    if name in _ROW_SHARDED:
        return parts.reshape(n * r, cs)
    return parts.transpose(1, 0, 2).reshape(r, n * cs)


def _reshard(full, name, n=4):
    R, C = full.shape
    if name in _ROW_SHARDED:
        return full.reshape(n, R // n, C)
    return full.reshape(R, n, C // n).transpose(1, 0, 2)


def _pad_w_in(w):
    z = lambda k: jnp.zeros(w.shape[:-1] + (k,), w.dtype)
    return jnp.concatenate([
        w[..., 2208:3232], w[..., 3232:4256], w[..., 0:1536], w[..., 1920:2176], w[..., 1536:1920],
        z(ROPE_LANE0), w[..., 2176:2208], z(LANE - ROPE_LANE0 - ROPE_DIM)], axis=-1)


def _unpad_w_in(g):
    k0 = COL_KROPE + ROPE_LANE0
    return jnp.concatenate([
        g[..., COL_QSB:COL_KVLAT], g[..., COL_QLAT:COL_KROPE], g[..., COL_KVLAT:COL_QLAT],
        g[..., k0:k0 + ROPE_DIM], g[..., 0:COL_QSB]], axis=-1)


def _pad_w_q(w):
    r = w.shape[0]
    return jnp.pad(w.reshape(r, N_HEADS, QK_DIM), ((0, 0), (0, 0), (0, LANE - QK_DIM))).reshape(r, N_HEADS * LANE)


def _unpad_w_q(g):
    r = g.shape[0]
    return g.reshape(r, N_HEADS, LANE)[..., :QK_DIM].reshape(r, N_HEADS * QK_DIM)


def _pad_w_mla(w):
    n = w.shape[1]
    return jnp.pad(w.reshape(N_HEADS, NOPE_DIM, n), ((0, 0), (LANE - NOPE_DIM, 0), (0, 0))).reshape(
        N_HEADS * LANE, n)


def _unpad_w_mla(g):
    n = g.shape[1]
    return g.reshape(N_HEADS, LANE, n)[:, LANE - NOPE_DIM:, :].reshape(N_HEADS * NOPE_DIM, n)


def _rope_tables(positions):
    half = ROPE_DIM // 2
    inv_freq = 1.0 / (ROPE_THETA ** (jnp.arange(0, ROPE_DIM, 2, dtype=F32) / ROPE_DIM))
    ang = positions.astype(F32)[:, None] * inv_freq
    cos, sin = jnp.cos(ang), jnp.sin(ang)
    S = positions.shape[0]
    one = jnp.ones((S, ROPE_LANE0), F32)
    zero = lambda k: jnp.zeros((S, k), F32)
    tail = LANE - ROPE_LANE0 - ROPE_DIM
    c = jnp.concatenate([one, cos, cos, zero(tail)], axis=1)
    s1 = jnp.concatenate([zero(ROPE_LANE0), -sin, zero(half + tail)], axis=1)
    s2 = jnp.concatenate([zero(ROPE_LANE0 + half), sin, zero(tail)], axis=1)
    return c, s1, s2


def _layer_fwd(x, W, mod, tabs, next_blocks=None):
    sh1, sc1, gt1, sh2, sc2, gt2 = (mod[i] for i in range(N_MOD))
    ride = bool(next_blocks)
    groups = ([1, 2, 3, 4, 5], [0, 7], [6])
    first, second, third = ([[next_blocks[i] for i in g] for g in groups] if ride else (None, None, None))
    h1 = _normmod_fwd(x, W["g_mix"], sc1, sh1, "mix_norm_fwd")
    p = _matmul(h1, W["w_in"], exs=[_gather_exchange(first, "a")] if ride else None, name="in_proj")
    if ride:
        p, (first,) = p
    (osbh, cb), carried = _sb_fwd(
        p, "sb_attn_fwd", [_gather_exchange(first, "b"), _gather_exchange(second, "a")] if ride else None)
    o_sb = _matmul(osbh, W["w_sb_out"], name="sb_out")
    qn = _rmsnorm_fwd(p, Q_RANK, COL_QLAT // Q_RANK, W["g_q"], "q_lat_norm_fwd")
    kvn = _rmsnorm_fwd(p, KV_RANK, COL_KVLAT // KV_RANK, W["g_kv"], "kv_lat_norm_fwd")
    qp = _matmul(qn, W["w_q_up"], name="q_up")
    kv = _matmul(kvn, W["w_kv_up"], out_dtype=BF16, name="kv_up")
    qr, kpe = _rope_fwd(qp, p, tabs, "rope_fwd")
    if ride:
        first, second = carried
    (omh, lse), carried = _mla_fwd(
        qr, kv, kpe, "mla_attn_fwd",
        [_gather_exchange(second, "b"), _gather_exchange(third, "a")] if ride else None)
    o_mla = _matmul(omh, W["w_mla_out"], name="mla_out")
    merged = _merge_fwd(p, o_sb, o_mla, "merge_fwd")
    y1 = _matmul(merged, W["w_mix_out"], name="mix_out")
    x1 = _res_fwd(x, y1, gt1, "mix_residual")
    h2 = _normmod_fwd(x1, W["g_mlp"], sc2, sh2, "mlp_norm_fwd")

    def sqrelu(t):
        r = jnp.maximum(t, 0.0)
        return t, r * r

    if ride:
        second, third = carried
        (u, a), (third,) = _matmul(h2, W["w_up"], b_sharded="col", out_dtype=(F32, BF16), epilogue=sqrelu,
                                   exs=[_gather_exchange(third, "b")], name="mlp_up")
        gathered = [None] * len(_SHARDED)
        for g, got in zip(groups, (first, second, third)):
            for i, arr in zip(g, got):
                gathered[i] = arr
    else:
        u, a = _matmul(h2, W["w_up"], b_sharded="col", out_dtype=(F32, BF16), epilogue=sqrelu, name="mlp_up")
        gathered = []
    y2 = _matmul(a, W["w_down"], b_sharded="row", name="mlp_down")
    x2 = _res_fwd(x1, y2, gt2, "mlp_residual")
    saved = dict(x=x, h1=h1, p=p, osbh=osbh, cb=cb, o_sb=o_sb, qn=qn, kvn=kvn, qr=qr, kv=kv, kpe=kpe, omh=omh,
                 lse=lse, o_mla=o_mla, merged=merged, y1=y1, x1=x1, h2=h2, u=u, a=a, y2=y2)
    return x2, saved, gathered


def _layer_bwd(dx2, W, mod, tabs, sv, core, above=None, above_send=None):
    sh1, sc1, gt1, sh2, sc2, gt2 = (mod[i] for i in range(N_MOD))
    dy2, dgt2 = _res_bwd(dx2, sv["y2"], gt2, "mlp_residual_bwd")

    def sqrelu_bwd(da, u):
        return (da * (2.0 * jnp.maximum(u, 0.0)),)

    du = _matmul(dy2, W["w_down"], tb=True, b_sharded="row", out_dtype=BF16, epilogue=sqrelu_bwd, extra=(sv["u"],),
                 exs=[_swap_halves_exchange(above_send)] if above else None, name="mlp_down_dx")
    pending = None
    if above:
        du, (from_sibling,) = du
        pending = [_add_halves(d, r, core, "grads_add_halves") for d, r in zip(above, from_sibling)]

    def with_bf16(t):
        return t, t

    g_down, g_down_send = _matmul(sv["a"], dy2, ta=True, out_dtype=(F32, BF16), epilogue=with_bf16,
                                  out_sharded=("row", W["w_down"].shape), name="mlp_down_dw")
    dh2 = _matmul(du, W["w_up"], tb=True, b_sharded="col", name="mlp_up_dx")
    g_up, g_up_send = _matmul(sv["h2"], du, ta=True, out_dtype=(F32, BF16), epilogue=with_bf16,
                              out_sharded=("col", W["w_up"].shape), name="mlp_up_dw")
    dx1, dsh2, dsc2, dg_mlp = _normmod_bwd(sv["x1"], dh2, W["g_mlp"], sc2, dx2, "mlp_norm_bwd")
    dy1, dgt1 = _res_bwd(dx1, sv["y1"], gt1, "mix_residual_bwd")
    dm = _matmul(dy1, W["w_mix_out"], tb=True, name="mix_out_dx")
    g_mix_out = _matmul(sv["merged"], dy1, ta=True, name="mix_out_dw")
    do_sb, do_mla, dgs, dgm = _merge_bwd(sv["p"], sv["o_sb"], sv["o_mla"], dm, "merge_bwd")
    do_sbh = _matmul(do_sb, W["w_sb_out"], tb=True, name="sb_out_dx")
    g_sb_out = _matmul(sv["osbh"], do_sb, ta=True, name="sb_out_dw")
    (dqs, dks, dvs), carried = _sb_bwd(
        sv["p"], do_sbh, sv["cb"], "sb_attn_bwd", [_chip_scatter_exchange(pending)] if above else None)
    my_sum = [_sum_slots(part, core, "grads_sum_chips") for part in carried[0]] if above else []
    do_mh = _matmul(do_mla, W["w_mla_out"], tb=True, name="mla_out_dx")
    g_mla_out = _matmul(sv["omh"], do_mla, ta=True, name="mla_out_dw")
    (dqr, dkv, dkpe), carried = _mla_bwd(
        sv["qr"], sv["kv"], sv["kpe"], do_mh, sv["omh"], sv["lse"], "mla_attn_bwd",
        [_sibling_gather_exchange(my_sum)] if above else None)
    reduced_above = list(carried[0]) if above else []
    dqp, dkr = _rope_bwd(dqr, dkpe, tabs, "rope_bwd")
    dqn = _matmul(dqp, W["w_q_up"], tb=True, name="q_up_dx")
    g_q_up = _matmul(sv["qn"], dqp, ta=True, name="q_up_dw")
    dkvn = _matmul(dkv, W["w_kv_up"], tb=True, name="kv_up_dx")
    g_kv_up = _matmul(sv["kvn"], dkv, ta=True, name="kv_up_dw")
    dqlat, dg_q = _rmsnorm_bwd(sv["p"], Q_RANK, COL_QLAT // Q_RANK, dqn, W["g_q"], "q_lat_norm_bwd")
    dkvlat, dg_kv = _rmsnorm_bwd(sv["p"], KV_RANK, COL_KVLAT // KV_RANK, dkvn, W["g_kv"], "kv_lat_norm_bwd")
    dp = jnp.concatenate([dgs, dgm, dqs, dks, dvs, dkvlat, dqlat, dkr], axis=1)
    dh1 = _matmul(dp, W["w_in"], tb=True, name="in_proj_dx")
    g_in = _matmul(sv["h1"], dp, ta=True, name="in_proj_dw")
    dx, dsh1, dsc1, dg_mix = _normmod_bwd(sv["x"], dh1, W["g_mix"], sc1, dx1, "mix_norm_bwd")
    grads = dict(w_in=g_in, w_q_up=g_q_up, w_kv_up=g_kv_up, w_sb_out=g_sb_out, w_mla_out=g_mla_out,
                 w_mix_out=g_mix_out, w_up=g_up, w_down=g_down, w_up_send=g_up_send, w_down_send=g_down_send,
                 dmod=jnp.concatenate([dsh1, dsc1, dgt1, dsh2, dsc2, dgt2], axis=0),
                 g_mix=dg_mix, g_mlp=dg_mlp, g_q=dg_q, g_kv=dg_kv)
    return dx, grads, reduced_above


def kernel(x, c, positions, w_ada, b_ada, g_mix_norm, w_in, g_q_lat, w_q_up, g_kv_lat, w_kv_up, w_sb_out, w_mla_out, w_mix_out, g_mlp_norm, w_up, w_down, g_final, loss_target, m_w_ada, m_b_ada, m_g_mix_norm, m_w_in, m_g_q_lat, m_w_q_up, m_g_kv_lat, m_w_kv_up, m_w_sb_out, m_w_mla_out, m_w_mix_out, m_g_mlp_norm, m_w_up, m_w_down, m_g_final, v_w_ada, v_b_ada, v_g_mix_norm, v_w_in, v_g_q_lat, v_w_q_up, v_g_kv_lat, v_w_kv_up, v_w_sb_out, v_w_mla_out, v_w_mix_out, v_g_mlp_norm, v_w_up, v_w_down, v_g_final):
    xi, yi, ci = _place()
    chip = 2 * xi + yi
    batch = 2 * chip + ci
    L = w_ada.shape[0]
    S = x.shape[1]
    shards = dict(w_in=w_in, w_q_up=w_q_up, w_kv_up=w_kv_up, w_sb_out=w_sb_out, w_mla_out=w_mla_out,
                  w_mix_out=w_mix_out, w_up=w_up, w_down=w_down)

    def my_halves(l):
        def half_of(w):
            half = w.shape[1] // 2
            return lax.dynamic_slice_in_dim(w[l].astype(BF16), ci * half, half, 0)

        return [half_of(shards[n]) for n in _SHARDED]

    def layer_weights(l, gathered):
        W = {}
        for n, g in zip(_SHARDED, gathered):
            by_chip = g.reshape((4,) + shards[n].shape[1:])
            W[n] = by_chip if n in _BY_CHIP else _unshard(by_chip, n)
        W["w_in"] = _pad_w_in(W["w_in"])
        W["w_q_up"] = _pad_w_q(W["w_q_up"])
        W["w_mla_out"] = _pad_w_mla(W["w_mla_out"])
        return dict(W, g_mix=g_mix_norm[l:l + 1], g_mlp=g_mlp_norm[l:l + 1], g_q=g_q_lat[l:l + 1],
                    g_kv=g_kv_lat[l:l + 1])

    gathered0 = _all_gather8(my_halves(0), "gather_weights")

    c_act = _silu(c, "silu_c")
    c_all = _all_gather8([jnp.broadcast_to(c_act, (8, D_MODEL))], "gather_c")[0].reshape(8, 8, D_MODEL)[:, 0]
    c16 = jnp.concatenate([c_all, jnp.zeros_like(c_all)], axis=0)
    ada_cols = w_ada.shape[2]
    b_shard = lax.dynamic_slice_in_dim(b_ada, chip * ada_cols, ada_cols, 1)
    mod_part = jnp.stack([_matmul(c16, w_ada[l], name="ada_mod") for l in range(L)])
    mod_part = _bias_add(mod_part, jnp.broadcast_to(b_shard[:, None, :], mod_part.shape), "ada_bias")
    mod_all = _all_gather8([mod_part.reshape(L * 16, ada_cols)], "gather_mod")[0].reshape(4, 2, L, 16, ada_cols)
    mod_mine = lax.dynamic_index_in_dim(mod_all[:, 0], batch, axis=2, keepdims=False)
    mods = mod_mine.transpose(1, 0, 2).reshape(L, N_MOD, 1, D_MODEL)

    tabs = _rope_tables(positions[0])

    xc, saved, layer_w = x[0], [], [layer_weights(0, gathered0)]
    for l in range(L):
        xc, sv, gathered = _layer_fwd(xc, layer_w[l], mods[l], tabs, my_halves(l + 1) if l + 1 < L else None)
        saved.append(sv)
        if l + 1 < L:
            layer_w.append(layer_weights(l + 1, gathered))
    dxc, dg_final, loss_part = _final_loss(xc, loss_target[0], g_final[None, :], "final_norm_loss")
    loss = lax.psum(loss_part[0, 0], ("x", "y", "c"))
    core = jnp.reshape(ci, (1,)).astype(jnp.int32)

    grads, reduced, above, above_send = [None] * L, [None] * L, None, None
    for l in reversed(range(L)):
        dxc, grads[l], reduced_above = _layer_bwd(
            dxc, layer_w[l], mods[l], tabs, saved[l], core, above, above_send)
        if above:
            reduced[l + 1] = reduced_above
        grads[l]["w_in"] = _unpad_w_in(grads[l]["w_in"])
        grads[l]["w_q_up"] = _unpad_w_q(grads[l]["w_q_up"])
        grads[l]["w_mla_out"] = _unpad_w_mla(grads[l]["w_mla_out"])
        above = [grads[l][n] if n in _BY_CHIP else _reshard(grads[l][n], n) for n in _SHARDED]
        above_send = [grads[l][n + "_send"] if n in _BY_CHIP else a for n, a in zip(_SHARDED, above)]
    from_sibling = _run_exchange(_swap_halves_exchange(above_send), "grads_swap_halves")
    pending = [_add_halves(d, r, core, "grads_add_halves") for d, r in zip(above, from_sibling)]
    from_chips = _run_exchange(_chip_scatter_exchange(pending), "grads_chip_scatter")
    my_sum = [_sum_slots(part, core, "grads_sum_chips") for part in from_chips]
    reduced[0] = _run_exchange(_sibling_gather_exchange(my_sum), "grads_sibling_gather")
    grad_x = dxc
    gw = {n: jnp.stack([reduced[l][i] for l in range(L)]) for i, n in enumerate(_SHARDED)}

    def row(v):
        return jnp.pad(v, ((0, 0), (0, D_MODEL - v.shape[1])))

    per_layer_rows = N_MOD + 4
    small = jnp.concatenate(
        [jnp.concatenate([grads[l]["dmod"], row(grads[l]["g_mix"]), row(grads[l]["g_mlp"]),
                          row(grads[l]["g_q"]), row(grads[l]["g_kv"])], axis=0) for l in range(L)]
        + [dg_final], axis=0)
    n_small = -(-small.shape[0] // 8) * 8
    small = jnp.pad(small, ((0, n_small - small.shape[0]), (0, 0)))
    small_all = _all_gather8([small], "gather_vector_grads")[0].reshape(8, n_small, D_MODEL)
    small_sum = _sum_blocks(small_all, "sum_vector_grads")
    lay = small_sum[:L * per_layer_rows].reshape(L, per_layer_rows, D_MODEL)
    g_b_ada = lay[:, :N_MOD].reshape(L, N_MOD * D_MODEL)
    g_g_mix, g_g_mlp = lay[:, N_MOD], lay[:, N_MOD + 1]
    g_g_q, g_g_kv = lay[:, N_MOD + 2, :Q_RANK], lay[:, N_MOD + 3, :KV_RANK]
    g_g_final = small_sum[L * per_layer_rows]
    dmod_all = small_all[:, :L * per_layer_rows].reshape(8, L, per_layer_rows, D_MODEL)[:, :, :N_MOD]
    dmod_all = dmod_all.reshape(8, L, N_MOD * D_MODEL)
    dmod_cols = lax.dynamic_slice_in_dim(dmod_all, chip * ada_cols, ada_cols, 2)
    dmod16 = jnp.concatenate([dmod_cols, jnp.zeros_like(dmod_cols)], axis=0)
    g_w_ada = jnp.stack([_matmul(c16, dmod16[:, l], ta=True, name="ada_dw") for l in range(L)])

    weights = dict(w_ada=w_ada, b_ada=b_ada, g_mix_norm=g_mix_norm, w_in=w_in, g_q_lat=g_q_lat, w_q_up=w_q_up,
                   g_kv_lat=g_kv_lat, w_kv_up=w_kv_up, w_sb_out=w_sb_out, w_mla_out=w_mla_out,
                   w_mix_out=w_mix_out, g_mlp_norm=g_mlp_norm, w_up=w_up, w_down=w_down, g_final=g_final)
    mom = dict(w_ada=(m_w_ada, v_w_ada), b_ada=(m_b_ada, v_b_ada), g_mix_norm=(m_g_mix_norm, v_g_mix_norm),
               w_in=(m_w_in, v_w_in), g_q_lat=(m_g_q_lat, v_g_q_lat), w_q_up=(m_w_q_up, v_w_q_up),
               g_kv_lat=(m_g_kv_lat, v_g_kv_lat), w_kv_up=(m_w_kv_up, v_w_kv_up),
               w_sb_out=(m_w_sb_out, v_w_sb_out), w_mla_out=(m_w_mla_out, v_w_mla_out),
               w_mix_out=(m_w_mix_out, v_w_mix_out), g_mlp_norm=(m_g_mlp_norm, v_g_mlp_norm),
               w_up=(m_w_up, v_w_up), w_down=(m_w_down, v_w_down), g_final=(m_g_final, v_g_final))
    gr = dict(gw, w_ada=g_w_ada, b_ada=g_b_ada, g_mix_norm=g_g_mix, g_q_lat=g_g_q, g_kv_lat=g_g_kv,
              g_mlp_norm=g_g_mlp, g_final=g_g_final)
    order = list(weights)
    deltas, new_m, new_v = [], [], []
    for n in order:
        wv, gv, (mv, vv) = weights[n], gr[n], mom[n]
        if wv.ndim == 1:
            d, nm, nv = (t[0] for t in _adamw(wv[None], gv[None], mv[None], vv[None], "adamw_" + n))
        else:
            d, nm, nv = _adamw(wv, gv, mv, vv, "adamw_" + n)
        deltas.append(d)
        new_m.append(nm)
        new_v.append(nv)
    return (loss, grad_x[None], *[gr[n] for n in order], *deltas, *new_m, *new_v)
```

```python
from typing import Any, Callable, Mapping, NamedTuple, Sequence

import jax
import jax.numpy as jnp
from jax import lax
from jax.experimental import pallas as pl
from jax.experimental.pallas import tpu as pltpu

F32 = jnp.float32
BF16 = jnp.bfloat16
MESH = pl.DeviceIdType.MESH

D_MODEL = 1024
N_HEADS = 8
SB_DIM = 64
SB_WIDTH = 512
Q_RANK = 384
KV_RANK = 256
ROPE_DIM = 32
NOPE_DIM = 64
QK_DIM = 96
D_FF = 4096
N_MOD = 6
EPS = 1e-6
ROPE_THETA = 10000.0
SB_SCALE = SB_DIM ** -0.5
MLA_SCALE = QK_DIM ** -0.5
ADAM_LR, ADAM_B1, ADAM_B2, ADAM_EPS, ADAM_WD, ADAM_STEP = 0.001, 0.9, 0.999, 1e-08, 0.01, 10

LANE = 128
IN_PAD = 4352
COL_GATE_SB, COL_GATE_MLA, COL_QSB, COL_KSB, COL_VSB, COL_KVLAT, COL_QLAT, COL_KROPE = (
    0, 1024, 2048, 2560, 3072, 3584, 3840, 4224)
ROPE_LANE0 = 64
VMEM_LIMIT = 48 * 1024 * 1024
NEG_BIG = -1e30


def _cp(*sem):
    return pltpu.CompilerParams(dimension_semantics=sem, vmem_limit_bytes=VMEM_LIMIT)


def _tile(n, prefs):
    for t in prefs:
        if t <= n and n % t == 0:
            return t
    return n


def _dot(a, b, dims):
    return lax.dot_general(a, b, (dims, ((), ())), preferred_element_type=F32)


def _nn(a, b):
    return _dot(a, b, ((1,), (0,)))


def _nt(a, b):
    return _dot(a, b, ((1,), (1,)))


def _tn(a, b):
    return _dot(a, b, ((0,), (0,)))


def _sharded_dims(shape, kind):
    n, r, cs = shape
    return (n * r, cs) if kind == "row" else (r, n * cs)


def _sharded_spec(shape, kind, t_rows, t_cols, tile_of):
    _, r, cs = shape
    if kind == "row":
        assert r % t_rows == 0, (shape, t_rows)
        per = r // t_rows

        def index(i, j, k):
            tr, tc = tile_of(i, j, k)
            return tr // per, tr % per, tc
    else:
        assert cs % t_cols == 0, (shape, t_cols)
        per = cs // t_cols

        def index(i, j, k):
            tr, tc = tile_of(i, j, k)
            return tc // per, tr, tc % per
    return pl.BlockSpec((None, t_rows, t_cols), index)


def _matmul(a, b, *, ta=False, tb=False, out_dtype=F32, b_sharded=None, out_sharded=None, epilogue=None,
            extra=(), exs=None, name):
    (K, M) = a.shape if ta else a.shape[::-1]
    b_dims = _sharded_dims(b.shape, b_sharded) if b_sharded else b.shape
    (N, Kb) = b_dims if tb else b_dims[::-1]
    assert K == Kb, (a.shape, b.shape, ta, tb)
    tm = _tile(M, (512, 384, 256, 128))
    tn = _tile(N, (1024, 2176, 768, 512, 384, 256, 128))
    tk = _tile(K, (1024, 2176, 768, 512, 384, 256, 128))
    nk = K // tk
    dims = ((0 if ta else 1,), (1 if tb else 0,))

    out_dtypes = out_dtype if isinstance(out_dtype, tuple) else (out_dtype,)
    n_extra, n_o = len(extra), len(out_dtypes)

    def body(a_ref, b_ref, *rest):
        extra_refs, o_refs, acc = rest[:n_extra], rest[n_extra:n_extra + n_o], rest[n_extra + n_o:]
        prod = _dot(a_ref[...].astype(BF16), b_ref[...].astype(BF16), dims)

        def write(total):
            vals = epilogue(total, *[r[...] for r in extra_refs]) if epilogue else (total,)
            for o_ref, val, dt in zip(o_refs, vals, out_dtypes):
                o_ref[...] = val.astype(dt)

        if nk == 1:
            write(prod)
            return
        acc_ref, = acc
        k = pl.program_id(2)

        @pl.when(k == 0)
        def _():
            acc_ref[...] = prod

        @pl.when(k > 0)
        def _():
            acc_ref[...] += prod

        @pl.when(k == nk - 1)
        def _():
            write(acc_ref[...])

    a_spec = (pl.BlockSpec((tk, tm), lambda i, j, k: (k, i)) if ta
              else pl.BlockSpec((tm, tk), lambda i, j, k: (i, k)))
    if b_sharded:
        b_spec = (_sharded_spec(b.shape, b_sharded, tn, tk, lambda i, j, k: (j, k)) if tb
                  else _sharded_spec(b.shape, b_sharded, tk, tn, lambda i, j, k: (k, j)))
    else:
        b_spec = (pl.BlockSpec((tn, tk), lambda i, j, k: (j, k)) if tb
                  else pl.BlockSpec((tk, tn), lambda i, j, k: (k, j)))
    tile = pl.BlockSpec((tm, tn), lambda i, j, k: (i, j))
    if out_sharded:
        kind, shape = out_sharded
        assert _sharded_dims(shape, kind) == (M, N), (shape, kind, M, N)
        out_specs = [_sharded_spec(shape, kind, tm, tn, lambda i, j, k: (i, j)) for _ in out_dtypes]
        out_shape = [jax.ShapeDtypeStruct(shape, dt) for dt in out_dtypes]
    else:
        out_specs = [tile] * n_o
        out_shape = [jax.ShapeDtypeStruct((M, N), dt) for dt in out_dtypes]
    grid = (M // tm, N // tn, nk)
    scratch = [pltpu.VMEM((tm, tn), F32)] if nk > 1 else []
    ins = [a, b, *extra]
    body, more, split = _carry(exs, body, len(ins), n_o, len(scratch), grid)
    own, carried = split(pl.pallas_call(
        body, grid=grid, in_specs=[a_spec, b_spec] + [tile] * n_extra + more["in_specs"],
        out_specs=out_specs + more["out_specs"], out_shape=out_shape + more["out_shape"],
        scratch_shapes=scratch + more["scratch"], input_output_aliases=more["aliases"],
        compiler_params=_cp(*(("arbitrary",) * 3 if exs else ("parallel", "parallel", "arbitrary"))),
        name=name)(*ins, *more["ins"]))
    result = own[0] if n_o == 1 else tuple(own)
    return (result, carried) if exs else result


def _rows(ts, w, col=0):
    return pl.BlockSpec((ts, w), lambda i: (i, col))


def _vec(w):
    return pl.BlockSpec((1, w), lambda i: (0, 0))


def _ts(S):
    return _tile(S, (512, 256, 128))


def _attn_tile(S):
    return _tile(S, (512, 256, 128))


def _rms(x):
    return lax.rsqrt(jnp.mean(x * x, axis=-1, keepdims=True) + EPS)


def _colsum(x):
    return jnp.sum(x, axis=0, keepdims=True)


def _normmod_fwd(x, g, sc, sh, name):
    S, W = x.shape
    ts = _ts(S)

    def body(x_ref, g_ref, sc_ref, sh_ref, h_ref):
        xv = x_ref[...]
        h_ref[...] = ((xv * _rms(xv)) * g_ref[...] * (1.0 + sc_ref[...]) + sh_ref[...]).astype(BF16)

    return pl.pallas_call(
        body, grid=(S // ts,), in_specs=[_rows(ts, W), _vec(W), _vec(W), _vec(W)],
        out_specs=_rows(ts, W), out_shape=jax.ShapeDtypeStruct((S, W), BF16),
        compiler_params=_cp("parallel"), name=name)(x, g, sc, sh)


def _normmod_bwd(x, dh, g, sc, dres, name):
    S, W = x.shape
    ts = _ts(S)

    def body(x_ref, dh_ref, g_ref, sc_ref, dres_ref, dx_ref, dsh_ref, dsc_ref, dg_ref):
        @pl.when(pl.program_id(0) == 0)
        def _():
            dsh_ref[...] = jnp.zeros_like(dsh_ref)
            dsc_ref[...] = jnp.zeros_like(dsc_ref)
            dg_ref[...] = jnp.zeros_like(dg_ref)

        xv, dh_v, gv = x_ref[...], dh_ref[...], g_ref[...]
        r = _rms(xv)
        y = xv * r
        dn = dh_v * (1.0 + sc_ref[...])
        dy = dn * gv
        dx_ref[...] = dres_ref[...] + r * (dy - y * jnp.mean(dy * y, axis=-1, keepdims=True))
        dsh_ref[...] += _colsum(dh_v)
        dsc_ref[...] += _colsum(dh_v * y * gv)
        dg_ref[...] += _colsum(dn * y)

    vec_out = jax.ShapeDtypeStruct((1, W), F32)
    return pl.pallas_call(
        body, grid=(S // ts,),
        in_specs=[_rows(ts, W), _rows(ts, W), _vec(W), _vec(W), _rows(ts, W)],
        out_specs=[_rows(ts, W), _vec(W), _vec(W), _vec(W)],
        out_shape=[jax.ShapeDtypeStruct((S, W), F32), vec_out, vec_out, vec_out],
        compiler_params=_cp("arbitrary"), name=name)(x, dh, g, sc, dres)


def _rmsnorm_fwd(p, width, col, g, name):
    S = p.shape[0]
    ts = _ts(S)

    def body(x_ref, g_ref, y_ref):
        xv = x_ref[...]
        y_ref[...] = ((xv * _rms(xv)) * g_ref[...]).astype(BF16)

    return pl.pallas_call(
        body, grid=(S // ts,), in_specs=[_rows(ts, width, col), _vec(width)],
        out_specs=_rows(ts, width), out_shape=jax.ShapeDtypeStruct((S, width), BF16),
        compiler_params=_cp("parallel"), name=name)(p, g)


def _rmsnorm_bwd(p, width, col, dn, g, name):
    S = p.shape[0]
    ts = _ts(S)

    def body(x_ref, dn_ref, g_ref, dx_ref, dg_ref):
        @pl.when(pl.program_id(0) == 0)
        def _():
            dg_ref[...] = jnp.zeros_like(dg_ref)

        xv, dn_v = x_ref[...], dn_ref[...]
        r = _rms(xv)
        y = xv * r
        dy = dn_v * g_ref[...]
        dx_ref[...] = r * (dy - y * jnp.mean(dy * y, axis=-1, keepdims=True))
        dg_ref[...] += _colsum(dn_v * y)

    return pl.pallas_call(
        body, grid=(S // ts,), in_specs=[_rows(ts, width, col), _rows(ts, width), _vec(width)],
        out_specs=[_rows(ts, width), _vec(width)],
        out_shape=[jax.ShapeDtypeStruct((S, width), F32), jax.ShapeDtypeStruct((1, width), F32)],
        compiler_params=_cp("arbitrary"), name=name)(p, dn, g)


def _rope_rot(t, c, s1, s2):
    return t * c + pltpu.roll(t, LANE - 16, 1) * s1 + pltpu.roll(t, 16, 1) * s2


def _rope_rot_t(d, c, s1, s2):
    return d * c + pltpu.roll(d * s1, 16, 1) + pltpu.roll(d * s2, LANE - 16, 1)


def _rope_fwd(qp, p, tabs, name):
    S = qp.shape[0]
    ts = _ts(S)
    W = N_HEADS * LANE

    def body(q_ref, kr_ref, c_ref, s1_ref, s2_ref, qr_ref, kpe_ref):
        c, s1, s2 = c_ref[...], s1_ref[...], s2_ref[...]
        for h in range(N_HEADS):
            sl = slice(h * LANE, (h + 1) * LANE)
            qr_ref[:, sl] = _rope_rot(q_ref[:, sl], c, s1, s2).astype(BF16)
        kpe_ref[...] = _rope_rot(kr_ref[...], c, s1, s2).astype(BF16)

    tab = _rows(ts, LANE)
    return pl.pallas_call(
        body, grid=(S // ts,), in_specs=[_rows(ts, W), _rows(ts, LANE, COL_KROPE // LANE), tab, tab, tab],
        out_specs=[_rows(ts, W), _rows(ts, LANE)],
        out_shape=[jax.ShapeDtypeStruct((S, W), BF16), jax.ShapeDtypeStruct((S, LANE), BF16)],
        compiler_params=_cp("parallel"), name=name)(qp, p, *tabs)


def _rope_bwd(dqr, dkpe_heads, tabs, name):
    S = dqr.shape[0]
    ts = _ts(S)
    W = N_HEADS * LANE

    def body(dq_ref, dk_ref, c_ref, s1_ref, s2_ref, dqp_ref, dkr_ref):
        c, s1, s2 = c_ref[...], s1_ref[...], s2_ref[...]
        dk = dk_ref[:, 0:LANE]
        for h in range(N_HEADS):
            sl = slice(h * LANE, (h + 1) * LANE)
            dqp_ref[:, sl] = _rope_rot_t(dq_ref[:, sl], c, s1, s2).astype(BF16)
            if h:
                dk = dk + dk_ref[:, sl]
        dkr_ref[...] = _rope_rot_t(dk, c, s1, s2)

    tab = _rows(ts, LANE)
    return pl.pallas_call(
        body, grid=(S // ts,), in_specs=[_rows(ts, W), _rows(ts, W), tab, tab, tab],
        out_specs=[_rows(ts, W), _rows(ts, LANE)],
        out_shape=[jax.ShapeDtypeStruct((S, W), BF16), jax.ShapeDtypeStruct((S, LANE), F32)],
        compiler_params=_cp("parallel"), name=name)(dqr, dkpe_heads, *tabs)


def _merge_fwd(p, o_sb, o_mla, name):
    S, W = o_sb.shape
    ts = _ts(S)

    def body(gs_ref, gm_ref, a_ref, b_ref, m_ref):
        m_ref[...] = (jax.nn.sigmoid(gs_ref[...]) * a_ref[...]
                      + jax.nn.sigmoid(gm_ref[...]) * b_ref[...]).astype(BF16)

    return pl.pallas_call(
        body, grid=(S // ts,),
        in_specs=[_rows(ts, W, COL_GATE_SB // W), _rows(ts, W, COL_GATE_MLA // W), _rows(ts, W), _rows(ts, W)],
        out_specs=_rows(ts, W), out_shape=jax.ShapeDtypeStruct((S, W), BF16),
        compiler_params=_cp("parallel"), name=name)(p, p, o_sb, o_mla)


def _merge_bwd(p, o_sb, o_mla, dm, name):
    S, W = o_sb.shape
    ts = _ts(S)

    def body(gs_ref, gm_ref, a_ref, b_ref, dm_ref, da_ref, db_ref, dgs_ref, dgm_ref):
        dmv = dm_ref[...]
        sa, sb = jax.nn.sigmoid(gs_ref[...]), jax.nn.sigmoid(gm_ref[...])
        da_ref[...] = (dmv * sa).astype(BF16)
        db_ref[...] = (dmv * sb).astype(BF16)
        dgs_ref[...] = dmv * a_ref[...] * sa * (1.0 - sa)
        dgm_ref[...] = dmv * b_ref[...] * sb * (1.0 - sb)

    row = _rows(ts, W)
    return pl.pallas_call(
        body, grid=(S // ts,),
        in_specs=[_rows(ts, W, COL_GATE_SB // W), _rows(ts, W, COL_GATE_MLA // W), row, row, row],
        out_specs=[row, row, row, row],
        out_shape=[jax.ShapeDtypeStruct((S, W), BF16), jax.ShapeDtypeStruct((S, W), BF16),
                   jax.ShapeDtypeStruct((S, W), F32), jax.ShapeDtypeStruct((S, W), F32)],
        compiler_params=_cp("parallel"), name=name)(p, p, o_sb, o_mla, dm)


def _res_fwd(x, y, gate, name):
    S, W = x.shape
    ts = _ts(S)

    def body(x_ref, y_ref, g_ref, o_ref):
        o_ref[...] = x_ref[...] + g_ref[...] * y_ref[...]

    return pl.pallas_call(
        body, grid=(S // ts,), in_specs=[_rows(ts, W), _rows(ts, W), _vec(W)], out_specs=_rows(ts, W),
        out_shape=jax.ShapeDtypeStruct((S, W), F32), compiler_params=_cp("parallel"), name=name)(x, y, gate)


def _res_bwd(dx, y, gate, name):
    S, W = dx.shape
    ts = _ts(S)

    def body(dx_ref, y_ref, g_ref, dy_ref, dg_ref):
        @pl.when(pl.program_id(0) == 0)
        def _():
            dg_ref[...] = jnp.zeros_like(dg_ref)

        dxv = dx_ref[...]
        dy_ref[...] = (g_ref[...] * dxv).astype(BF16)
        dg_ref[...] += _colsum(dxv * y_ref[...])

    return pl.pallas_call(
        body, grid=(S // ts,), in_specs=[_rows(ts, W), _rows(ts, W), _vec(W)],
        out_specs=[_rows(ts, W), _vec(W)],
        out_shape=[jax.ShapeDtypeStruct((S, W), BF16), jax.ShapeDtypeStruct((1, W), F32)],
        compiler_params=_cp("arbitrary"), name=name)(dx, y, gate)


def _final_loss(x, target, g, name):
    S, W = x.shape
    ts = _ts(S)

    def body(x_ref, t_ref, g_ref, dx_ref, dg_ref, loss_ref):
        @pl.when(pl.program_id(0) == 0)
        def _():
            dg_ref[...] = jnp.zeros_like(dg_ref)
            loss_ref[...] = jnp.zeros_like(loss_ref)

        xv, gv = x_ref[...], g_ref[...]
        r = _rms(xv)
        y = xv * r
        err = y * gv - t_ref[...]
        loss_ref[...] += jnp.full((1, LANE), 0.5 * jnp.sum(jnp.mean(err * err, axis=-1)), F32)
        dout = err * (1.0 / W)
        dy = dout * gv
        dx_ref[...] = r * (dy - y * jnp.mean(dy * y, axis=-1, keepdims=True))
        dg_ref[...] += _colsum(dout * y)

    return pl.pallas_call(
        body, grid=(S // ts,), in_specs=[_rows(ts, W), _rows(ts, W), _vec(W)],
        out_specs=[_rows(ts, W), _vec(W), _vec(LANE)],
        out_shape=[jax.ShapeDtypeStruct((S, W), F32), jax.ShapeDtypeStruct((1, W), F32),
                   jax.ShapeDtypeStruct((1, LANE), F32)],
        compiler_params=_cp("arbitrary"), name=name)(x, target, g)


def _silu(c, name):
    def body(c_ref, o_ref):
        cv = c_ref[...]
        o_ref[...] = cv * jax.nn.sigmoid(cv)

    return pl.pallas_call(body, out_shape=jax.ShapeDtypeStruct(c.shape, F32), name=name)(c)


def _bias_add(a, b, name):
    def body(a_ref, b_ref, o_ref):
        o_ref[...] = a_ref[...] + b_ref[...]

    return pl.pallas_call(body, out_shape=jax.ShapeDtypeStruct(a.shape, F32), name=name)(a, b)


def _sum_blocks(xs, name):
    n = xs.shape[0]

    def body(x_ref, o_ref):
        acc = x_ref[0]
        for d in range(1, n):
            acc = acc + x_ref[d]
        o_ref[...] = acc

    return pl.pallas_call(body, out_shape=jax.ShapeDtypeStruct(xs.shape[1:], F32), name=name)(xs)


def _adamw(w, g, m, v, name):
    shape = w.shape
    cols = shape[-1]
    w2, g2, m2, v2 = (t.reshape(-1, cols) for t in (w, g, m, v))
    rows = w2.shape[0]
    tr = _tile(rows, (128,))
    c1 = 1.0 - ADAM_B1 ** ADAM_STEP
    c2 = 1.0 - ADAM_B2 ** ADAM_STEP

    def body(w_ref, g_ref, m_ref, v_ref, d_ref, nm_ref, nv_ref):
        gv = g_ref[...]
        nm = ADAM_B1 * m_ref[...] + (1.0 - ADAM_B1) * gv
        nv = ADAM_B2 * v_ref[...] + (1.0 - ADAM_B2) * (gv * gv)
        d_ref[...] = -ADAM_LR * ((nm / c1) / (jnp.sqrt(nv / c2) + ADAM_EPS) + ADAM_WD * w_ref[...])
        nm_ref[...] = nm
        nv_ref[...] = nv

    spec = pl.BlockSpec((tr, cols), lambda i: (i, 0))
    out = jax.ShapeDtypeStruct((rows, cols), F32)
    d, nm, nv = pl.pallas_call(
        body, grid=(rows // tr,), in_specs=[spec] * 4, out_specs=[spec] * 3, out_shape=[out] * 3,
        compiler_params=_cp("parallel"), name=name)(w2, g2, m2, v2)
    return d.reshape(shape), nm.reshape(shape), nv.reshape(shape)


def _split_dot(x, tri):
    hi = x.astype(BF16)
    lo = (x - hi.astype(F32)).astype(BF16)
    return _nn(hi, tri) + _nn(lo, tri)


def _tri_cumsum(x, tri, later):
    h = x.shape[1] // 2
    first, second = x[:, :h], x[:, h:]
    sum_first = jnp.sum(first, axis=1, keepdims=True)
    sum_second = jnp.sum(second, axis=1, keepdims=True)
    run_first, run_second = _split_dot(first, tri), _split_dot(second, tri)
    if later:
        run_first = run_first + sum_second
    else:
        run_second = run_second + sum_first
    return jnp.concatenate([run_first, run_second], axis=1), sum_first + sum_second


def _sb_logs(z):
    soft = jnp.log(1.0 + jnp.exp(-jnp.abs(z)))
    return jnp.minimum(z, 0.0) - soft, -jnp.maximum(z, 0.0) - soft


def _attn_call(body, grid, ins, in_specs, out_specs, out_shape, scratch, exs, name):
    body, extra, split = _carry(exs, body, len(ins), len(out_shape), len(scratch), grid)
    return split(pl.pallas_call(
        body, grid=grid, in_specs=in_specs + extra["in_specs"], out_specs=out_specs + extra["out_specs"],
        out_shape=out_shape + extra["out_shape"], scratch_shapes=scratch + extra["scratch"],
        input_output_aliases=extra["aliases"], compiler_params=_cp("arbitrary", "arbitrary"),
        name=name)(*ins, *extra["ins"]))


def _sb_fwd(p, name, exs=None):
    S = p.shape[0]
    t = _attn_tile(S)
    qb, kb, vb = COL_QSB // LANE, COL_KSB // LANE, COL_VSB // LANE

    def body(q_ref, k_ref, v_ref, o_ref, cb_ref, acc_ref):
        i = pl.program_id(1)
        lane = lax.broadcasted_iota(jnp.int32, (t, LANE), 1)
        rows = lax.broadcasted_iota(jnp.int32, (t, t), 0)
        cols = lax.broadcasted_iota(jnp.int32, (t, t), 1)
        half_r = lax.broadcasted_iota(jnp.int32, (t // 2, t // 2), 0)
        half_c = lax.broadcasted_iota(jnp.int32, (t // 2, t // 2), 1)
        after = jnp.where(half_r > half_c, 1.0, 0.0).astype(BF16)
        diag = cols < rows
        q = q_ref[...] * SB_SCALE
        acc_ref[...] = jnp.zeros_like(acc_ref)
        cb_ref[...] = jnp.zeros_like(cb_ref)
        hms = [(lane >= SB_DIM * h) & (lane < SB_DIM * (h + 1)) for h in range(2)]
        qhs = [jnp.where(hm, q, 0.0).astype(BF16) for hm in hms]

        def step(j, cs, masked):
            rows_j = pl.ds(pl.multiple_of(j * t, t), t)
            kj = k_ref[rows_j, :].astype(BF16)
            vf = v_ref[rows_j, :]
            out, pv = [], None
            for h in range(2):
                ls, lf = _sb_logs(_nt(qhs[h], kj))
                if masked:
                    lf = jnp.where(diag, lf, 0.0)
                survive, total = _tri_cumsum(lf, after, True)
                a = jnp.exp(ls + survive + cs[h])
                if masked:
                    a = jnp.where(diag, a, 0.0)
                term = _nn(a.astype(BF16), jnp.where(hms[h], vf, 0.0).astype(BF16))
                pv = term if pv is None else pv + term
                cb_ref[h] = jnp.where(lane == j, cs[h], cb_ref[h])
                out.append(cs[h] + total)
            acc_ref[...] += pv
            return tuple(out)

        zero = jnp.zeros((t, 1), F32)
        cs = step(i, (zero, zero), True)
        lax.fori_loop(0, i, lambda it, cs: step(i - 1 - it, cs, False), cs)
        o_ref[...] = acc_ref[...].astype(BF16)

    return _attn_call(
        body, (SB_WIDTH // LANE, S // t), [p, p, p],
        [pl.BlockSpec((t, LANE), lambda hp, i: (i, qb + hp)),
         pl.BlockSpec((S, LANE), lambda hp, i: (0, kb + hp)),
         pl.BlockSpec((S, LANE), lambda hp, i: (0, vb + hp))],
        [pl.BlockSpec((t, LANE), lambda hp, i: (i, hp)),
         pl.BlockSpec((2, t, LANE), lambda hp, i: (hp, i, 0))],
        [jax.ShapeDtypeStruct((S, SB_WIDTH), BF16), jax.ShapeDtypeStruct((N_HEADS, S, LANE), F32)],
        [pltpu.VMEM((t, LANE), F32)], exs, name)


def _sb_bwd(p, do, cb, name, exs=None):
    S = p.shape[0]
    t = _attn_tile(S)
    qb, kb, vb = COL_QSB // LANE, COL_KSB // LANE, COL_VSB // LANE

    def body(q_ref, k_ref, v_ref, do_ref, cb_ref, dq_ref, dk_ref, dv_ref, acc_ref):
        i = pl.program_id(1)

        @pl.when(i == 0)
        def _():
            dk_ref[...] = jnp.zeros_like(dk_ref)
            dv_ref[...] = jnp.zeros_like(dv_ref)

        lane = lax.broadcasted_iota(jnp.int32, (t, LANE), 1)
        rows = lax.broadcasted_iota(jnp.int32, (t, t), 0)
        cols = lax.broadcasted_iota(jnp.int32, (t, t), 1)
        half_r = lax.broadcasted_iota(jnp.int32, (t // 2, t // 2), 0)
        half_c = lax.broadcasted_iota(jnp.int32, (t // 2, t // 2), 1)
        after = jnp.where(half_r > half_c, 1.0, 0.0).astype(BF16)
        before = jnp.where(half_r < half_c, 1.0, 0.0).astype(BF16)
        diag = cols < rows
        q = q_ref[...] * SB_SCALE
        dov = do_ref[...]
        acc_ref[...] = jnp.zeros_like(acc_ref)
        hms = [(lane >= SB_DIM * h) & (lane < SB_DIM * (h + 1)) for h in range(2)]
        qhs = [jnp.where(hm, q, 0.0).astype(BF16) for hm in hms]
        dohs = [jnp.where(hm, dov, 0.0).astype(BF16) for hm in hms]

        def step(j, fs, masked):
            rows_j = pl.ds(pl.multiple_of(j * t, t), t)
            kf = k_ref[rows_j, :]
            kj = kf.astype(BF16)
            vj = v_ref[rows_j, :].astype(BF16)
            out, dq_t, dk_t, dv_t = [], None, None, None
            for h in range(2):
                ls, lf = _sb_logs(_nt(qhs[h], kj))
                if masked:
                    lf = jnp.where(diag, lf, 0.0)
                c = jnp.sum(jnp.where(lane == j, cb_ref[h], 0.0), axis=1, keepdims=True)
                a = jnp.exp(ls + _tri_cumsum(lf, after, True)[0] + c)
                if masked:
                    a = jnp.where(diag, a, 0.0)
                dl = _nt(dohs[h], vj) * a
                sg = jnp.exp(ls)
                earlier, total = _tri_cumsum(dl, before, False)
                dz = dl * (1.0 - sg) - sg * (earlier + fs[h])
                if masked:
                    dz = jnp.where(diag, dz, 0.0)
                dzb = dz.astype(BF16)
                terms = (_nn(dzb, jnp.where(hms[h], kf, 0.0).astype(BF16)), _tn(dzb, qhs[h]),
                         _tn(a.astype(BF16), dohs[h]))
                dq_t, dk_t, dv_t = terms if dq_t is None else (dq_t + terms[0], dk_t + terms[1], dv_t + terms[2])
                out.append(fs[h] + total)
            acc_ref[...] += dq_t
            dk_ref[rows_j, :] += dk_t
            dv_ref[rows_j, :] += dv_t
            return tuple(out)

        zero = jnp.zeros((t, 1), F32)
        fs = lax.fori_loop(0, i, lambda j, fs: step(j, fs, False), (zero, zero))
        step(i, fs, True)
        dq_ref[...] = acc_ref[...] * SB_SCALE

    col = lambda hp, i: (0, hp)
    out = jax.ShapeDtypeStruct((S, SB_WIDTH), F32)
    return _attn_call(
        body, (SB_WIDTH // LANE, S // t), [p, p, p, do, cb],
        [pl.BlockSpec((t, LANE), lambda hp, i: (i, qb + hp)),
         pl.BlockSpec((S, LANE), lambda hp, i: (0, kb + hp)),
         pl.BlockSpec((S, LANE), lambda hp, i: (0, vb + hp)),
         pl.BlockSpec((t, LANE), lambda hp, i: (i, hp)),
         pl.BlockSpec((2, t, LANE), lambda hp, i: (hp, i, 0))],
        [pl.BlockSpec((t, LANE), lambda hp, i: (i, hp)), pl.BlockSpec((S, LANE), col), pl.BlockSpec((S, LANE), col)],
        [out, out, out], [pltpu.VMEM((t, LANE), F32)], exs, name)


def _mla_fwd(qr, kv, kpe, name, exs=None):
    S = qr.shape[0]
    t = _attn_tile(S)

    def body(q_ref, kv_ref, kpe_ref, o_ref, lse_ref, acc_ref, m_ref):
        i = pl.program_id(1)
        low = lax.broadcasted_iota(jnp.int32, (t, LANE), 1) < NOPE_DIM
        rows = lax.broadcasted_iota(jnp.int32, (t, t), 0)
        cols = lax.broadcasted_iota(jnp.int32, (t, t), 1)
        causal = cols <= rows
        one = jnp.ones((t, LANE), BF16)
        heads = [slice(h * LANE, (h + 1) * LANE) for h in range(2)]
        qs = [q_ref[:, sl] for sl in heads]
        acc_ref[...] = jnp.zeros_like(acc_ref)
        m_ref[...] = jnp.full_like(m_ref, NEG_BIG)

        def step(j, masked):
            rows_j = pl.ds(pl.multiple_of(j * t, t), t)
            kpe_j = kpe_ref[rows_j, :]
            for h, sl in enumerate(heads):
                kvj = kv_ref[rows_j, sl]
                z = _nt(qs[h], jnp.where(low, kvj, kpe_j)) * MLA_SCALE
                if masked:
                    z = jnp.where(causal, z, NEG_BIG)
                m_old = m_ref[h]
                m_new = jnp.maximum(m_old, jnp.max(z, axis=1, keepdims=True))
                pr = jnp.exp(z - m_new)
                acc_ref[:, sl] = jnp.exp(m_old - m_new) * acc_ref[:, sl] + _nn(
                    pr.astype(BF16), jnp.where(low, one, kvj))
                m_ref[h] = m_new

        def loop(j, carry):
            step(j, False)
            return carry

        lax.fori_loop(0, i, loop, 0)
        step(i, True)
        for h, sl in enumerate(heads):
            acc = acc_ref[:, sl]
            den = acc[:, 0:1]
            o_ref[:, sl] = jnp.where(low, 0.0, acc / den).astype(BF16)
            lse_ref[h] = jnp.broadcast_to(m_ref[h] + jnp.log(den), (t, LANE))

    pair = 2 * LANE
    return _attn_call(
        body, (N_HEADS // 2, S // t), [qr, kv, kpe],
        [pl.BlockSpec((t, pair), lambda hp, i: (i, hp)),
         pl.BlockSpec((S, pair), lambda hp, i: (0, hp)),
         pl.BlockSpec((S, LANE), lambda hp, i: (0, 0))],
        [pl.BlockSpec((t, pair), lambda hp, i: (i, hp)), pl.BlockSpec((2, t, LANE), lambda hp, i: (hp, i, 0))],
        [jax.ShapeDtypeStruct((S, N_HEADS * LANE), BF16), jax.ShapeDtypeStruct((N_HEADS, S, LANE), F32)],
        [pltpu.VMEM((t, pair), F32), pltpu.VMEM((2, t, 1), F32)], exs, name)


def _mla_bwd(qr, kv, kpe, do, o, lse, name, exs=None):
    S = qr.shape[0]
    t = _attn_tile(S)

    def body(q_ref, kv_ref, kpe_ref, do_ref, o_ref, lse_ref, dq_ref, dkv_ref, dkpe_ref, acc_ref):
        i = pl.program_id(1)

        @pl.when(i == 0)
        def _():
            dkv_ref[...] = jnp.zeros_like(dkv_ref)
            dkpe_ref[...] = jnp.zeros_like(dkpe_ref)

        low = lax.broadcasted_iota(jnp.int32, (t, LANE), 1) < NOPE_DIM
        rows = lax.broadcasted_iota(jnp.int32, (t, t), 0)
        cols = lax.broadcasted_iota(jnp.int32, (t, t), 1)
        causal = cols <= rows
        heads = [slice(h * LANE, (h + 1) * LANE) for h in range(2)]
        qs = [q_ref[:, sl] for sl in heads]
        dovs = [do_ref[:, sl] for sl in heads]
        dobs = [d.astype(BF16) for d in dovs]
        deltas = [jnp.sum(dovs[h] * o_ref[:, sl].astype(F32), axis=1, keepdims=True) for h, sl in enumerate(heads)]
        lses = [lse_ref[h][:, 0:1] for h in range(2)]
        acc_ref[...] = jnp.zeros_like(acc_ref)

        def step(j, masked):
            rows_j = pl.ds(pl.multiple_of(j * t, t), t)
            kpe_j = kpe_ref[rows_j, :]
            for h, sl in enumerate(heads):
                kvj = kv_ref[rows_j, sl]
                kcat = jnp.where(low, kvj, kpe_j)
                z = _nt(qs[h], kcat) * MLA_SCALE
                if masked:
                    z = jnp.where(causal, z, NEG_BIG)
                pr = jnp.exp(z - lses[h])
                ds = (pr * (_nt(dobs[h], kvj) - deltas[h])).astype(BF16)
                acc_ref[:, sl] += _nn(ds, kcat)
                dkc = _tn(ds, qs[h]) * MLA_SCALE
                dkv_ref[rows_j, sl] += jnp.where(low, dkc, _tn(pr.astype(BF16), dobs[h]))
                dkpe_ref[rows_j, sl] += jnp.where(low, 0.0, dkc)

        def loop(j, carry):
            step(j, False)
            return carry

        lax.fori_loop(0, i, loop, 0)
        step(i, True)
        dq_ref[...] = acc_ref[...] * MLA_SCALE

    pair = 2 * LANE
    blk = pl.BlockSpec((t, pair), lambda hp, i: (i, hp))
    col = pl.BlockSpec((S, pair), lambda hp, i: (0, hp))
    out = jax.ShapeDtypeStruct((S, N_HEADS * LANE), F32)
    return _attn_call(
        body, (N_HEADS // 2, S // t), [qr, kv, kpe, do, o, lse],
        [blk, col, pl.BlockSpec((S, LANE), lambda hp, i: (0, 0)), blk, blk,
         pl.BlockSpec((2, t, LANE), lambda hp, i: (hp, i, 0))],
        [blk, col, col], [out, out, out], [pltpu.VMEM((t, pair), F32)], exs, name)


_ANY = pl.BlockSpec(memory_space=pl.ANY)


def _place():
    return lax.axis_index("x"), lax.axis_index("y"), lax.axis_index("c")


class _Exchange(NamedTuple):
    ins: Sequence[Any]
    outs: Sequence[Any]
    aliases: Mapping[int, int]
    n_remote: int
    n_local: int
    start: Callable
    finish: Callable


def _exchange_scratch(ex):
    return [pltpu.SemaphoreType.DMA((ex.n_remote,)), pltpu.SemaphoreType.DMA((ex.n_remote,)),
            pltpu.SemaphoreType.DMA((ex.n_local,))]


def _run_exchange(ex, name):
    n_in, n_out = len(ex.ins), len(ex.outs)

    def body(*refs):
        args = (refs[:n_in], refs[n_in:n_in + n_out], *refs[n_in + n_out:])
        ex.start(*args)
        ex.finish(*args)

    return pl.pallas_call(
        body, out_shape=list(ex.outs), in_specs=[_ANY] * n_in, out_specs=[_ANY] * n_out,
        scratch_shapes=_exchange_scratch(ex), input_output_aliases=dict(ex.aliases), name=name)(*ex.ins)


def _carry(exs, body, n_in, n_out, n_scratch, grid):
    exs = [ex for ex in (exs or []) if ex is not None]
    e_ins, e_outs = [len(ex.ins) for ex in exs], [len(ex.outs) for ex in exs]

    def take(refs, counts):
        groups = []
        for n in counts:
            groups.append(refs[:n])
            refs = refs[n:]
        return groups, refs

    def carried(*refs):
        own_in, refs = refs[:n_in], refs[n_in:]
        ex_in, refs = take(refs, e_ins)
        own_out, refs = refs[:n_out], refs[n_out:]
        ex_out, refs = take(refs, e_outs)
        own_scratch, refs = refs[:n_scratch], refs[n_scratch:]
        sems, _ = take(refs, [3] * len(exs))
        at = [pl.program_id(d) for d in range(len(grid))]
        first, last = at[0] == 0, at[0] == grid[0] - 1
        for d in range(1, len(grid)):
            first, last = first & (at[d] == 0), last & (at[d] == grid[d] - 1)

        @pl.when(first)
        def _():
            for e, ex in enumerate(exs):
                ex.start(ex_in[e], ex_out[e], *sems[e])

        body(*own_in, *own_out, *own_scratch)

        @pl.when(last)
        def _():
            for e, ex in enumerate(exs):
                ex.finish(ex_in[e], ex_out[e], *sems[e])

    aliases, i0, o0 = {}, n_in, n_out
    for ex in exs:
        aliases.update({i0 + i: o0 + o for i, o in ex.aliases.items()})
        i0, o0 = i0 + len(ex.ins), o0 + len(ex.outs)

    def split(res):
        groups, _ = take(list(res[n_out:]), e_outs)
        return list(res[:n_out]), groups

    extra = dict(
        ins=[a for ex in exs for a in ex.ins], in_specs=[_ANY] * sum(e_ins), out_specs=[_ANY] * sum(e_outs),
        out_shape=[o for ex in exs for o in ex.outs], scratch=[s for ex in exs for s in _exchange_scratch(ex)],
        aliases=aliases)
    return (carried if exs else body), extra, split


def _gather_exchange(arrs, phase="all"):
    n_t = len(arrs)
    ms = [a.shape[0] // (8 if phase == "b" else 1) for a in arrs]

    def plan(in_refs, out_refs, send_sems, recv_sems, local_sems):
        x, y, c = _place()
        me, sibling = (x, y, c), (x, y, 1 - c)
        chips = [(1 - x, y), (x, 1 - y), (1 - x, 1 - y)]

        def rows(ref, t, px, py, pc):
            return ref.at[pl.ds((4 * px + 2 * py + pc) * ms[t], ms[t]), :]

        def copy(t, k, block, to, src):
            return pltpu.make_async_remote_copy(
                src_ref=src, dst_ref=rows(out_refs[t], t, *block), send_sem=send_sems.at[7 * t + k],
                recv_sem=recv_sems.at[7 * t + k], device_id=to, device_id_type=MESH)

        mine, first, first_in, passed, passed_in = [], [], [], [], []
        for t in range(n_t):
            if phase != "b":
                mine.append(pltpu.make_async_copy(in_refs[t], rows(out_refs[t], t, *me), local_sems.at[t]))
                first.append(copy(t, 0, me, sibling, in_refs[t]))
                first_in.append(copy(t, 0, sibling, me, in_refs[t]))
                for j, chip in enumerate(chips):
                    first.append(copy(t, 1 + j, me, (*chip, c), in_refs[t]))
                    first_in.append(copy(t, 1 + j, (*chip, c), me, in_refs[t]))
            if phase != "a":
                held = in_refs[t] if phase == "b" else out_refs[t]
                for j, chip in enumerate(chips):
                    passed.append(copy(t, 4 + j, (*chip, c), sibling, rows(held, t, *chip, c)))
                    passed_in.append(copy(t, 4 + j, (*chip, 1 - c), me, rows(held, t, *chip, c)))
        return mine, first, first_in, passed, passed_in

    def start(*refs):
        mine, first, _, passed, _ = plan(*refs)
        for cp in mine + first + (passed if phase == "b" else []):
            cp.start()

    def finish(*refs):
        mine, first, first_in, passed, passed_in = plan(*refs)
        for cp in first_in:
            cp.wait_recv()
        if phase == "all":
            for cp in passed:
                cp.start()
        for cp in passed_in:
            cp.wait_recv()
        for cp in first + passed:
            cp.wait_send()
        for cp in mine:
            cp.wait()

    if phase == "b":
        outs = [jax.ShapeDtypeStruct(a.shape, a.dtype) for a in arrs]
        aliases = {t: t for t in range(n_t)}
    else:
        outs = [jax.ShapeDtypeStruct((8 * a.shape[0], a.shape[1]), a.dtype) for a in arrs]
        aliases = {}
    return _Exchange(list(arrs), outs, aliases, 7 * n_t, n_t, start, finish)


def _all_gather8(blks, name):
    return _run_exchange(_gather_exchange(blks), name)


def _swap_halves_exchange(gs):
    n_t = len(gs)

    def plan(g_refs, out_refs, send_sems, recv_sems, local_sems):
        x, y, c = _place()
        copies = []
        for t in range(n_t):
            m = gs[t].shape[1] // 2
            copies += [pltpu.make_async_remote_copy(
                src_ref=g_refs[t].at[s, pl.ds((1 - c) * m, m), :], dst_ref=out_refs[t].at[s],
                send_sem=send_sems.at[4 * t + s], recv_sem=recv_sems.at[4 * t + s], device_id=(x, y, 1 - c),
                device_id_type=MESH) for s in range(4)]
        return copies

    def start(*refs):
        for cp in plan(*refs):
            cp.start()

    def finish(*refs):
        for cp in plan(*refs):
            cp.wait()

    outs = [jax.ShapeDtypeStruct((4, g.shape[1] // 2, g.shape[2]), g.dtype) for g in gs]
    return _Exchange(list(gs), outs, {}, 4 * n_t, 1, start, finish)


def _chip_scatter_exchange(parts):
    n_t = len(parts)

    def plan(p_refs, out_refs, send_sems, recv_sems, local_sems):
        x, y, c = _place()
        mine = 2 * x + y
        chips = [(1 - x, y), (x, 1 - y), (1 - x, 1 - y)]

        def copy(t, j, src_slot, dst_slot):
            px, py = chips[j]
            return pltpu.make_async_remote_copy(
                src_ref=p_refs[t].at[src_slot], dst_ref=out_refs[t].at[dst_slot],
                send_sem=send_sems.at[3 * t + j], recv_sem=recv_sems.at[3 * t + j], device_id=(px, py, c),
                device_id_type=MESH)

        own = [pltpu.make_async_copy(p_refs[t].at[mine], out_refs[t].at[mine], local_sems.at[t])
               for t in range(n_t)]
        sends = [copy(t, j, 2 * px + py, mine) for t in range(n_t) for j, (px, py) in enumerate(chips)]
        arrivals = [copy(t, j, mine, 2 * px + py) for t in range(n_t) for j, (px, py) in enumerate(chips)]
        return own, sends, arrivals

    def start(*refs):
        own, sends, _ = plan(*refs)
        for cp in own + sends:
            cp.start()

    def finish(*refs):
        own, sends, arrivals = plan(*refs)
        for cp in arrivals:
            cp.wait_recv()
        for cp in sends:
            cp.wait_send()
        for cp in own:
            cp.wait()

    outs = [jax.ShapeDtypeStruct(p.shape, p.dtype) for p in parts]
    return _Exchange(list(parts), outs, {}, 3 * n_t, n_t, start, finish)


def _sibling_gather_exchange(bufs):
    n_t = len(bufs)

    def plan(b_refs, out_refs, send_sems, recv_sems, local_sems):
        x, y, c = _place()

        def copy(t, pc):
            m = bufs[t].shape[0] // 2
            half = pl.ds(pc * m, m)
            return pltpu.make_async_remote_copy(
                src_ref=b_refs[t].at[half, :], dst_ref=out_refs[t].at[half, :], send_sem=send_sems.at[t],
                recv_sem=recv_sems.at[t], device_id=(x, y, 1 - c), device_id_type=MESH)

        return [copy(t, c) for t in range(n_t)], [copy(t, 1 - c) for t in range(n_t)]

    def start(*refs):
        for cp in plan(*refs)[0]:
            cp.start()

    def finish(*refs):
        sends, arrivals = plan(*refs)
        for cp in arrivals:
            cp.wait_recv()
        for cp in sends:
            cp.wait_send()

    outs = [jax.ShapeDtypeStruct(b.shape, b.dtype) for b in bufs]
    return _Exchange(list(bufs), outs, {t: t for t in range(n_t)}, n_t, 1, start, finish)


def _add_halves(g, recv, c, name):
    n_slot, m2, n = g.shape
    m = m2 // 2
    tr = _tile(m, (512, 256, 192, 128, 16))

    def body(c_ref, g_ref, r_ref, o_ref):
        o_ref[...] = (g_ref[...] + r_ref[...].astype(F32)).astype(BF16)

    nb = m // tr
    return pl.pallas_call(
        body,
        grid_spec=pltpu.PrefetchScalarGridSpec(
            num_scalar_prefetch=1, grid=(n_slot, nb),
            in_specs=[pl.BlockSpec((1, tr, n), lambda s, i, c_ref: (s, c_ref[0] * nb + i, 0)),
                      pl.BlockSpec((1, tr, n), lambda s, i, c_ref: (s, i, 0))],
            out_specs=pl.BlockSpec((1, tr, n), lambda s, i, c_ref: (s, i, 0))),
        out_shape=jax.ShapeDtypeStruct((n_slot, m, n), BF16),
        compiler_params=_cp("parallel", "parallel"), name=name)(c, g, recv)


def _sum_slots(parts, c, name):
    n_slot, m, n = parts.shape
    tr = _tile(m, (512, 256, 192, 128, 16))
    nb = m // tr

    def body(c_ref, p_ref, o_ref):
        acc = p_ref[0].astype(F32)
        for s in range(1, n_slot):
            acc = acc + p_ref[s].astype(F32)
        o_ref[...] = acc

    return pl.pallas_call(
        body,
        grid_spec=pltpu.PrefetchScalarGridSpec(
            num_scalar_prefetch=1, grid=(nb,),
            in_specs=[pl.BlockSpec((n_slot, tr, n), lambda i, c_ref: (0, i, 0))],
            out_specs=pl.BlockSpec((tr, n), lambda i, c_ref: (c_ref[0] * nb + i, 0))),
        out_shape=jax.ShapeDtypeStruct((2 * m, n), F32),
        compiler_params=_cp("parallel"), name=name)(c, parts)


_SHARDED = ("w_in", "w_q_up", "w_kv_up", "w_sb_out", "w_mla_out", "w_mix_out", "w_up", "w_down")
_ROW_SHARDED = ("w_mix_out", "w_down")
_BY_CHIP = ("w_up", "w_down")


def _unshard(parts, name):
    n, r, cs = parts.shape
    if name in _ROW_SHARDED:
        return parts.reshape(n * r, cs)
    return parts.transpose(1, 0, 2).reshape(r, n * cs)


def _reshard(full, name, n=4):
    R, C = full.shape
    if name in _ROW_SHARDED:
        return full.reshape(n, R // n, C)
    return full.reshape(R, n, C // n).transpose(1, 0, 2)


def _pad_w_in(w):
    z = lambda k: jnp.zeros(w.shape[:-1] + (k,), w.dtype)
    return jnp.concatenate([
        w[..., 2208:3232], w[..., 3232:4256], w[..., 0:1536], w[..., 1920:2176], w[..., 1536:1920],
        z(ROPE_LANE0), w[..., 2176:2208], z(LANE - ROPE_LANE0 - ROPE_DIM)], axis=-1)


def _unpad_w_in(g):
    k0 = COL_KROPE + ROPE_LANE0
    return jnp.concatenate([
        g[..., COL_QSB:COL_KVLAT], g[..., COL_QLAT:COL_KROPE], g[..., COL_KVLAT:COL_QLAT],
        g[..., k0:k0 + ROPE_DIM], g[..., 0:COL_QSB]], axis=-1)


def _pad_w_q(w):
    r = w.shape[0]
    return jnp.pad(w.reshape(r, N_HEADS, QK_DIM), ((0, 0), (0, 0), (0, LANE - QK_DIM))).reshape(r, N_HEADS * LANE)


def _unpad_w_q(g):
    r = g.shape[0]
    return g.reshape(r, N_HEADS, LANE)[..., :QK_DIM].reshape(r, N_HEADS * QK_DIM)


def _pad_w_mla(w):
    n = w.shape[1]
    return jnp.pad(w.reshape(N_HEADS, NOPE_DIM, n), ((0, 0), (LANE - NOPE_DIM, 0), (0, 0))).reshape(
        N_HEADS * LANE, n)


def _unpad_w_mla(g):
    n = g.shape[1]
    return g.reshape(N_HEADS, LANE, n)[:, LANE - NOPE_DIM:, :].reshape(N_HEADS * NOPE_DIM, n)


def _rope_tables(positions):
    half = ROPE_DIM // 2
    inv_freq = 1.0 / (ROPE_THETA ** (jnp.arange(0, ROPE_DIM, 2, dtype=F32) / ROPE_DIM))
    ang = positions.astype(F32)[:, None] * inv_freq
    cos, sin = jnp.cos(ang), jnp.sin(ang)
    S = positions.shape[0]
    one = jnp.ones((S, ROPE_LANE0), F32)
    zero = lambda k: jnp.zeros((S, k), F32)
    tail = LANE - ROPE_LANE0 - ROPE_DIM
    c = jnp.concatenate([one, cos, cos, zero(tail)], axis=1)
    s1 = jnp.concatenate([zero(ROPE_LANE0), -sin, zero(half + tail)], axis=1)
    s2 = jnp.concatenate([zero(ROPE_LANE0 + half), sin, zero(tail)], axis=1)
    return c, s1, s2


def _layer_fwd(x, W, mod, tabs, next_blocks=None):
    sh1, sc1, gt1, sh2, sc2, gt2 = (mod[i] for i in range(N_MOD))
    n_small = len(_SHARDED) - len(_BY_CHIP)
    small, big = (next_blocks[:n_small], next_blocks[n_small:]) if next_blocks else (None, None)
    h1 = _normmod_fwd(x, W["g_mix"], sc1, sh1, "mix_norm_fwd")
    p = _matmul(h1, W["w_in"], name="in_proj")
    (osbh, cb), carried = _sb_fwd(p, "sb_attn_fwd", [_gather_exchange(small, "a")] if small else None)
    o_sb = _matmul(osbh, W["w_sb_out"], name="sb_out")
    qn = _rmsnorm_fwd(p, Q_RANK, COL_QLAT // Q_RANK, W["g_q"], "q_lat_norm_fwd")
    kvn = _rmsnorm_fwd(p, KV_RANK, COL_KVLAT // KV_RANK, W["g_kv"], "kv_lat_norm_fwd")
    qp = _matmul(qn, W["w_q_up"], name="q_up")
    kv = _matmul(kvn, W["w_kv_up"], out_dtype=BF16, name="kv_up")
    qr, kpe = _rope_fwd(qp, p, tabs, "rope_fwd")
    (omh, lse), carried = _mla_fwd(
        qr, kv, kpe, "mla_attn_fwd",
        [_gather_exchange(carried[0], "b"), _gather_exchange(big, "a")] if small else None)
    o_mla = _matmul(omh, W["w_mla_out"], name="mla_out")
    merged = _merge_fwd(p, o_sb, o_mla, "merge_fwd")
    y1 = _matmul(merged, W["w_mix_out"], name="mix_out")
    x1 = _res_fwd(x, y1, gt1, "mix_residual")
    h2 = _normmod_fwd(x1, W["g_mlp"], sc2, sh2, "mlp_norm_fwd")

    def sqrelu(t):
        r = jnp.maximum(t, 0.0)
        return t, r * r

    if small:
        (u, a), big_done = _matmul(h2, W["w_up"], b_sharded="col", out_dtype=(F32, BF16), epilogue=sqrelu,
                                   exs=[_gather_exchange(carried[1], "b")], name="mlp_up")
        gathered = list(carried[0]) + list(big_done[0])
    else:
        u, a = _matmul(h2, W["w_up"], b_sharded="col", out_dtype=(F32, BF16), epilogue=sqrelu, name="mlp_up")
        gathered = []
    y2 = _matmul(a, W["w_down"], b_sharded="row", name="mlp_down")
    x2 = _res_fwd(x1, y2, gt2, "mlp_residual")
    saved = dict(x=x, h1=h1, p=p, osbh=osbh, cb=cb, o_sb=o_sb, qn=qn, kvn=kvn, qr=qr, kv=kv, kpe=kpe, omh=omh,
                 lse=lse, o_mla=o_mla, merged=merged, y1=y1, x1=x1, h2=h2, u=u, a=a, y2=y2)
    return x2, saved, gathered


def _layer_bwd(dx2, W, mod, tabs, sv, core, above=None, above_send=None):
    sh1, sc1, gt1, sh2, sc2, gt2 = (mod[i] for i in range(N_MOD))
    dy2, dgt2 = _res_bwd(dx2, sv["y2"], gt2, "mlp_residual_bwd")

    def sqrelu_bwd(da, u):
        return (da * (2.0 * jnp.maximum(u, 0.0)),)

    du = _matmul(dy2, W["w_down"], tb=True, b_sharded="row", out_dtype=BF16, epilogue=sqrelu_bwd, extra=(sv["u"],),
                 exs=[_swap_halves_exchange(above_send)] if above else None, name="mlp_down_dx")
    pending = None
    if above:
        du, (from_sibling,) = du
        pending = [_add_halves(d, r, core, "grads_add_halves") for d, r in zip(above, from_sibling)]

    def with_bf16(t):
        return t, t

    g_down, g_down_send = _matmul(sv["a"], dy2, ta=True, out_dtype=(F32, BF16), epilogue=with_bf16,
                                  out_sharded=("row", W["w_down"].shape), name="mlp_down_dw")
    dh2 = _matmul(du, W["w_up"], tb=True, b_sharded="col", name="mlp_up_dx")
    g_up, g_up_send = _matmul(sv["h2"], du, ta=True, out_dtype=(F32, BF16), epilogue=with_bf16,
                              out_sharded=("col", W["w_up"].shape), name="mlp_up_dw")
    dx1, dsh2, dsc2, dg_mlp = _normmod_bwd(sv["x1"], dh2, W["g_mlp"], sc2, dx2, "mlp_norm_bwd")
    dy1, dgt1 = _res_bwd(dx1, sv["y1"], gt1, "mix_residual_bwd")
    dm = _matmul(dy1, W["w_mix_out"], tb=True, name="mix_out_dx")
    g_mix_out = _matmul(sv["merged"], dy1, ta=True, name="mix_out_dw")
    do_sb, do_mla, dgs, dgm = _merge_bwd(sv["p"], sv["o_sb"], sv["o_mla"], dm, "merge_bwd")
    do_sbh = _matmul(do_sb, W["w_sb_out"], tb=True, name="sb_out_dx")
    g_sb_out = _matmul(sv["osbh"], do_sb, ta=True, name="sb_out_dw")
    (dqs, dks, dvs), carried = _sb_bwd(
        sv["p"], do_sbh, sv["cb"], "sb_attn_bwd", [_chip_scatter_exchange(pending)] if above else None)
    my_sum = [_sum_slots(part, core, "grads_sum_chips") for part in carried[0]] if above else []
    do_mh = _matmul(do_mla, W["w_mla_out"], tb=True, name="mla_out_dx")
    g_mla_out = _matmul(sv["omh"], do_mla, ta=True, name="mla_out_dw")
    (dqr, dkv, dkpe), carried = _mla_bwd(
        sv["qr"], sv["kv"], sv["kpe"], do_mh, sv["omh"], sv["lse"], "mla_attn_bwd",
        [_sibling_gather_exchange(my_sum)] if above else None)
    reduced_above = list(carried[0]) if above else []
    dqp, dkr = _rope_bwd(dqr, dkpe, tabs, "rope_bwd")
    dqn = _matmul(dqp, W["w_q_up"], tb=True, name="q_up_dx")
    g_q_up = _matmul(sv["qn"], dqp, ta=True, name="q_up_dw")
    dkvn = _matmul(dkv, W["w_kv_up"], tb=True, name="kv_up_dx")
    g_kv_up = _matmul(sv["kvn"], dkv, ta=True, name="kv_up_dw")
    dqlat, dg_q = _rmsnorm_bwd(sv["p"], Q_RANK, COL_QLAT // Q_RANK, dqn, W["g_q"], "q_lat_norm_bwd")
    dkvlat, dg_kv = _rmsnorm_bwd(sv["p"], KV_RANK, COL_KVLAT // KV_RANK, dkvn, W["g_kv"], "kv_lat_norm_bwd")
    dp = jnp.concatenate([dgs, dgm, dqs, dks, dvs, dkvlat, dqlat, dkr], axis=1)
    dh1 = _matmul(dp, W["w_in"], tb=True, name="in_proj_dx")
    g_in = _matmul(sv["h1"], dp, ta=True, name="in_proj_dw")
    dx, dsh1, dsc1, dg_mix = _normmod_bwd(sv["x"], dh1, W["g_mix"], sc1, dx1, "mix_norm_bwd")
    grads = dict(w_in=g_in, w_q_up=g_q_up, w_kv_up=g_kv_up, w_sb_out=g_sb_out, w_mla_out=g_mla_out,
                 w_mix_out=g_mix_out, w_up=g_up, w_down=g_down, w_up_send=g_up_send, w_down_send=g_down_send,
                 dmod=jnp.concatenate([dsh1, dsc1, dgt1, dsh2, dsc2, dgt2], axis=0),
                 g_mix=dg_mix, g_mlp=dg_mlp, g_q=dg_q, g_kv=dg_kv)
    return dx, grads, reduced_above


def kernel(x, c, positions, w_ada, b_ada, g_mix_norm, w_in, g_q_lat, w_q_up, g_kv_lat, w_kv_up, w_sb_out, w_mla_out, w_mix_out, g_mlp_norm, w_up, w_down, g_final, loss_target, m_w_ada, m_b_ada, m_g_mix_norm, m_w_in, m_g_q_lat, m_w_q_up, m_g_kv_lat, m_w_kv_up, m_w_sb_out, m_w_mla_out, m_w_mix_out, m_g_mlp_norm, m_w_up, m_w_down, m_g_final, v_w_ada, v_b_ada, v_g_mix_norm, v_w_in, v_g_q_lat, v_w_q_up, v_g_kv_lat, v_w_kv_up, v_w_sb_out, v_w_mla_out, v_w_mix_out, v_g_mlp_norm, v_w_up, v_w_down, v_g_final):
    xi, yi, ci = _place()
    chip = 2 * xi + yi
    batch = 2 * chip + ci
    L = w_ada.shape[0]
    S = x.shape[1]
    shards = dict(w_in=w_in, w_q_up=w_q_up, w_kv_up=w_kv_up, w_sb_out=w_sb_out, w_mla_out=w_mla_out,
                  w_mix_out=w_mix_out, w_up=w_up, w_down=w_down)

    def my_halves(l):
        def half_of(w):
            half = w.shape[1] // 2
            return lax.dynamic_slice_in_dim(w[l].astype(BF16), ci * half, half, 0)

        return [half_of(shards[n]) for n in _SHARDED]

    def layer_weights(l, gathered):
        W = {}
        for n, g in zip(_SHARDED, gathered):
            by_chip = g.reshape((4,) + shards[n].shape[1:])
            W[n] = by_chip if n in _BY_CHIP else _unshard(by_chip, n)
        W["w_in"] = _pad_w_in(W["w_in"])
        W["w_q_up"] = _pad_w_q(W["w_q_up"])
        W["w_mla_out"] = _pad_w_mla(W["w_mla_out"])
        return dict(W, g_mix=g_mix_norm[l:l + 1], g_mlp=g_mlp_norm[l:l + 1], g_q=g_q_lat[l:l + 1],
                    g_kv=g_kv_lat[l:l + 1])

    gathered0 = _all_gather8(my_halves(0), "gather_weights")

    c_act = _silu(c, "silu_c")
    c_all = _all_gather8([jnp.broadcast_to(c_act, (8, D_MODEL))], "gather_c")[0].reshape(8, 8, D_MODEL)[:, 0]
    c16 = jnp.concatenate([c_all, jnp.zeros_like(c_all)], axis=0)
    ada_cols = w_ada.shape[2]
    b_shard = lax.dynamic_slice_in_dim(b_ada, chip * ada_cols, ada_cols, 1)
    mod_part = jnp.stack([_matmul(c16, w_ada[l], name="ada_mod") for l in range(L)])
    mod_part = _bias_add(mod_part, jnp.broadcast_to(b_shard[:, None, :], mod_part.shape), "ada_bias")
    mod_all = _all_gather8([mod_part.reshape(L * 16, ada_cols)], "gather_mod")[0].reshape(4, 2, L, 16, ada_cols)
    mod_mine = lax.dynamic_index_in_dim(mod_all[:, 0], batch, axis=2, keepdims=False)
    mods = mod_mine.transpose(1, 0, 2).reshape(L, N_MOD, 1, D_MODEL)

    tabs = _rope_tables(positions[0])

    xc, saved, layer_w = x[0], [], [layer_weights(0, gathered0)]
    for l in range(L):
        xc, sv, gathered = _layer_fwd(xc, layer_w[l], mods[l], tabs, my_halves(l + 1) if l + 1 < L else None)
        saved.append(sv)
        if l + 1 < L:
            layer_w.append(layer_weights(l + 1, gathered))
    dxc, dg_final, loss_part = _final_loss(xc, loss_target[0], g_final[None, :], "final_norm_loss")
    loss = lax.psum(loss_part[0, 0], ("x", "y", "c"))
    core = jnp.reshape(ci, (1,)).astype(jnp.int32)

    grads, reduced, above, above_send = [None] * L, [None] * L, None, None
    for l in reversed(range(L)):
        dxc, grads[l], reduced_above = _layer_bwd(
            dxc, layer_w[l], mods[l], tabs, saved[l], core, above, above_send)
        if above:
            reduced[l + 1] = reduced_above
        grads[l]["w_in"] = _unpad_w_in(grads[l]["w_in"])
        grads[l]["w_q_up"] = _unpad_w_q(grads[l]["w_q_up"])
        grads[l]["w_mla_out"] = _unpad_w_mla(grads[l]["w_mla_out"])
        above = [grads[l][n] if n in _BY_CHIP else _reshard(grads[l][n], n) for n in _SHARDED]
        above_send = [grads[l][n + "_send"] if n in _BY_CHIP else a for n, a in zip(_SHARDED, above)]
    from_sibling = _run_exchange(_swap_halves_exchange(above_send), "grads_swap_halves")
    pending = [_add_halves(d, r, core, "grads_add_halves") for d, r in zip(above, from_sibling)]
    from_chips = _run_exchange(_chip_scatter_exchange(pending), "grads_chip_scatter")
    my_sum = [_sum_slots(part, core, "grads_sum_chips") for part in from_chips]
    reduced[0] = _run_exchange(_sibling_gather_exchange(my_sum), "grads_sibling_gather")
    grad_x = dxc
    gw = {n: jnp.stack([reduced[l][i] for l in range(L)]) for i, n in enumerate(_SHARDED)}

    def row(v):
        return jnp.pad(v, ((0, 0), (0, D_MODEL - v.shape[1])))

    per_layer_rows = N_MOD + 4
    small = jnp.concatenate(
        [jnp.concatenate([grads[l]["dmod"], row(grads[l]["g_mix"]), row(grads[l]["g_mlp"]),
                          row(grads[l]["g_q"]), row(grads[l]["g_kv"])], axis=0) for l in range(L)]
        + [dg_final], axis=0)
    n_small = -(-small.shape[0] // 8) * 8
    small = jnp.pad(small, ((0, n_small - small.shape[0]), (0, 0)))
    small_all = _all_gather8([small], "gather_vector_grads")[0].reshape(8, n_small, D_MODEL)
    small_sum = _sum_blocks(small_all, "sum_vector_grads")
    lay = small_sum[:L * per_layer_rows].reshape(L, per_layer_rows, D_MODEL)
    g_b_ada = lay[:, :N_MOD].reshape(L, N_MOD * D_MODEL)
    g_g_mix, g_g_mlp = lay[:, N_MOD], lay[:, N_MOD + 1]
    g_g_q, g_g_kv = lay[:, N_MOD + 2, :Q_RANK], lay[:, N_MOD + 3, :KV_RANK]
    g_g_final = small_sum[L * per_layer_rows]
    dmod_all = small_all[:, :L * per_layer_rows].reshape(8, L, per_layer_rows, D_MODEL)[:, :, :N_MOD]
    dmod_all = dmod_all.reshape(8, L, N_MOD * D_MODEL)
    dmod_cols = lax.dynamic_slice_in_dim(dmod_all, chip * ada_cols, ada_cols, 2)
    dmod16 = jnp.concatenate([dmod_cols, jnp.zeros_like(dmod_cols)], axis=0)
    g_w_ada = jnp.stack([_matmul(c16, dmod16[:, l], ta=True, name="ada_dw") for l in range(L)])

    weights = dict(w_ada=w_ada, b_ada=b_ada, g_mix_norm=g_mix_norm, w_in=w_in, g_q_lat=g_q_lat, w_q_up=w_q_up,
                   g_kv_lat=g_kv_lat, w_kv_up=w_kv_up, w_sb_out=w_sb_out, w_mla_out=w_mla_out,
                   w_mix_out=w_mix_out, g_mlp_norm=g_mlp_norm, w_up=w_up, w_down=w_down, g_final=g_final)
    mom = dict(w_ada=(m_w_ada, v_w_ada), b_ada=(m_b_ada, v_b_ada), g_mix_norm=(m_g_mix_norm, v_g_mix_norm),
               w_in=(m_w_in, v_w_in), g_q_lat=(m_g_q_lat, v_g_q_lat), w_q_up=(m_w_q_up, v_w_q_up),
               g_kv_lat=(m_g_kv_lat, v_g_kv_lat), w_kv_up=(m_w_kv_up, v_w_kv_up),
               w_sb_out=(m_w_sb_out, v_w_sb_out), w_mla_out=(m_w_mla_out, v_w_mla_out),
               w_mix_out=(m_w_mix_out, v_w_mix_out), g_mlp_norm=(m_g_mlp_norm, v_g_mlp_norm),
               w_up=(m_w_up, v_w_up), w_down=(m_w_down, v_w_down), g_final=(m_g_final, v_g_final))
    gr = dict(gw, w_ada=g_w_ada, b_ada=g_b_ada, g_mix_norm=g_g_mix, g_q_lat=g_g_q, g_kv_lat=g_g_kv,
              g_mlp_norm=g_g_mlp, g_final=g_g_final)
    order = list(weights)
    deltas, new_m, new_v = [], [], []
    for n in order:
        wv, gv, (mv, vv) = weights[n], gr[n], mom[n]
        if wv.ndim == 1:
            d, nm, nv = (t[0] for t in _adamw(wv[None], gv[None], mv[None], vv[None], "adamw_" + n))
        else:
            d, nm, nv = _adamw(wv, gv, mv, vv, "adamw_" + n)
        deltas.append(d)
        new_m.append(nm)
        new_v.append(nv)
    return (loss, grad_x[None], *[gr[n] for n in order], *deltas, *new_m, *new_v)
```

```python
from typing import Any, Callable, Mapping, NamedTuple, Sequence

import jax
import jax.numpy as jnp
from jax import lax
from jax.experimental import pallas as pl
from jax.experimental.pallas import tpu as pltpu

F32 = jnp.float32
BF16 = jnp.bfloat16
MESH = pl.DeviceIdType.MESH

D_MODEL = 1024
N_HEADS = 8
SB_DIM = 64
SB_WIDTH = 512
Q_RANK = 384
KV_RANK = 256
ROPE_DIM = 32
NOPE_DIM = 64
QK_DIM = 96
D_FF = 4096
N_MOD = 6
EPS = 1e-6
ROPE_THETA = 10000.0
SB_SCALE = SB_DIM ** -0.5
MLA_SCALE = QK_DIM ** -0.5
ADAM_LR, ADAM_B1, ADAM_B2, ADAM_EPS, ADAM_WD, ADAM_STEP = 0.001, 0.9, 0.999, 1e-08, 0.01, 10

LANE = 128
IN_PAD = 4352
COL_GATE_SB, COL_GATE_MLA, COL_QSB, COL_KSB, COL_VSB, COL_KVLAT, COL_QLAT, COL_KROPE = (
    0, 1024, 2048, 2560, 3072, 3584, 3840, 4224)
ROPE_LANE0 = 64
VMEM_LIMIT = 48 * 1024 * 1024
NEG_BIG = -1e30


def _cp(*sem):
    return pltpu.CompilerParams(dimension_semantics=sem, vmem_limit_bytes=VMEM_LIMIT)


def _tile(n, prefs):
    for t in prefs:
        if t <= n and n % t == 0:
            return t
    return n


def _dot(a, b, dims):
    return lax.dot_general(a, b, (dims, ((), ())), preferred_element_type=F32)


def _nn(a, b):
    return _dot(a, b, ((1,), (0,)))


def _nt(a, b):
    return _dot(a, b, ((1,), (1,)))


def _tn(a, b):
    return _dot(a, b, ((0,), (0,)))


def _sharded_dims(shape, kind):
    n, r, cs = shape
    return (n * r, cs) if kind == "row" else (r, n * cs)


def _sharded_spec(shape, kind, t_rows, t_cols, tile_of):
    _, r, cs = shape
    if kind == "row":
        assert r % t_rows == 0, (shape, t_rows)
        per = r // t_rows

        def index(i, j, k):
            tr, tc = tile_of(i, j, k)
            return tr // per, tr % per, tc
    else:
        assert cs % t_cols == 0, (shape, t_cols)
        per = cs // t_cols

        def index(i, j, k):
            tr, tc = tile_of(i, j, k)
            return tc // per, tr, tc % per
    return pl.BlockSpec((None, t_rows, t_cols), index)


def _matmul(a, b, *, ta=False, tb=False, out_dtype=F32, b_sharded=None, out_sharded=None, epilogue=None,
            extra=(), exs=None, name):
    (K, M) = a.shape if ta else a.shape[::-1]
    b_dims = _sharded_dims(b.shape, b_sharded) if b_sharded else b.shape
    (N, Kb) = b_dims if tb else b_dims[::-1]
    assert K == Kb, (a.shape, b.shape, ta, tb)
    tm = _tile(M, (512, 384, 256, 128))
    tn = _tile(N, (1024, 2176, 768, 512, 384, 256, 128))
    tk = _tile(K, (1024, 2176, 768, 512, 384, 256, 128))
    nk = K // tk
    dims = ((0 if ta else 1,), (1 if tb else 0,))

    out_dtypes = out_dtype if isinstance(out_dtype, tuple) else (out_dtype,)
    n_extra, n_o = len(extra), len(out_dtypes)

    def body(a_ref, b_ref, *rest):
        extra_refs, o_refs, acc = rest[:n_extra], rest[n_extra:n_extra + n_o], rest[n_extra + n_o:]
        prod = _dot(a_ref[...].astype(BF16), b_ref[...].astype(BF16), dims)

        def write(total):
            vals = epilogue(total, *[r[...] for r in extra_refs]) if epilogue else (total,)
            for o_ref, val, dt in zip(o_refs, vals, out_dtypes):
                o_ref[...] = val.astype(dt)

        if nk == 1:
            write(prod)
            return
        acc_ref, = acc
        k = pl.program_id(2)

        @pl.when(k == 0)
        def _():
            acc_ref[...] = prod

        @pl.when(k > 0)
        def _():
            acc_ref[...] += prod

        @pl.when(k == nk - 1)
        def _():
            write(acc_ref[...])

    a_spec = (pl.BlockSpec((tk, tm), lambda i, j, k: (k, i)) if ta
              else pl.BlockSpec((tm, tk), lambda i, j, k: (i, k)))
    if b_sharded:
        b_spec = (_sharded_spec(b.shape, b_sharded, tn, tk, lambda i, j, k: (j, k)) if tb
                  else _sharded_spec(b.shape, b_sharded, tk, tn, lambda i, j, k: (k, j)))
    else:
        b_spec = (pl.BlockSpec((tn, tk), lambda i, j, k: (j, k)) if tb
                  else pl.BlockSpec((tk, tn), lambda i, j, k: (k, j)))
    tile = pl.BlockSpec((tm, tn), lambda i, j, k: (i, j))
    if out_sharded:
        kind, shape = out_sharded
        assert _sharded_dims(shape, kind) == (M, N), (shape, kind, M, N)
        out_specs = [_sharded_spec(shape, kind, tm, tn, lambda i, j, k: (i, j)) for _ in out_dtypes]
        out_shape = [jax.ShapeDtypeStruct(shape, dt) for dt in out_dtypes]
    else:
        out_specs = [tile] * n_o
        out_shape = [jax.ShapeDtypeStruct((M, N), dt) for dt in out_dtypes]
    grid = (M // tm, N // tn, nk)
    scratch = [pltpu.VMEM((tm, tn), F32)] if nk > 1 else []
    ins = [a, b, *extra]
    body, more, split = _carry(exs, body, len(ins), n_o, len(scratch), grid)
    own, carried = split(pl.pallas_call(
        body, grid=grid, in_specs=[a_spec, b_spec] + [tile] * n_extra + more["in_specs"],
        out_specs=out_specs + more["out_specs"], out_shape=out_shape + more["out_shape"],
        scratch_shapes=scratch + more["scratch"], input_output_aliases=more["aliases"],
        compiler_params=_cp(*(("arbitrary",) * 3 if exs else ("parallel", "parallel", "arbitrary"))),
        name=name)(*ins, *more["ins"]))
    result = own[0] if n_o == 1 else tuple(own)
    return (result, carried) if exs else result


def _rows(ts, w, col=0):
    return pl.BlockSpec((ts, w), lambda i: (i, col))


def _vec(w):
    return pl.BlockSpec((1, w), lambda i: (0, 0))


def _ts(S):
    return _tile(S, (512, 256, 128))


def _attn_tile(S):
    return _tile(S, (512, 256, 128))


def _rms(x):
    return lax.rsqrt(jnp.mean(x * x, axis=-1, keepdims=True) + EPS)


def _colsum(x):
    return jnp.sum(x, axis=0, keepdims=True)


def _normmod_fwd(x, g, sc, sh, name):
    S, W = x.shape
    ts = _ts(S)

    def body(x_ref, g_ref, sc_ref, sh_ref, h_ref):
        xv = x_ref[...]
        h_ref[...] = ((xv * _rms(xv)) * g_ref[...] * (1.0 + sc_ref[...]) + sh_ref[...]).astype(BF16)

    return pl.pallas_call(
        body, grid=(S // ts,), in_specs=[_rows(ts, W), _vec(W), _vec(W), _vec(W)],
        out_specs=_rows(ts, W), out_shape=jax.ShapeDtypeStruct((S, W), BF16),
        compiler_params=_cp("parallel"), name=name)(x, g, sc, sh)


def _normmod_bwd(x, dh, g, sc, dres, name):
    S, W = x.shape
    ts = _ts(S)

    def body(x_ref, dh_ref, g_ref, sc_ref, dres_ref, dx_ref, dsh_ref, dsc_ref, dg_ref):
        @pl.when(pl.program_id(0) == 0)
        def _():
            dsh_ref[...] = jnp.zeros_like(dsh_ref)
            dsc_ref[...] = jnp.zeros_like(dsc_ref)
            dg_ref[...] = jnp.zeros_like(dg_ref)

        xv, dh_v, gv = x_ref[...], dh_ref[...], g_ref[...]
        r = _rms(xv)
        y = xv * r
        dn = dh_v * (1.0 + sc_ref[...])
        dy = dn * gv
        dx_ref[...] = dres_ref[...] + r * (dy - y * jnp.mean(dy * y, axis=-1, keepdims=True))
        dsh_ref[...] += _colsum(dh_v)
        dsc_ref[...] += _colsum(dh_v * y * gv)
        dg_ref[...] += _colsum(dn * y)

    vec_out = jax.ShapeDtypeStruct((1, W), F32)
    return pl.pallas_call(
        body, grid=(S // ts,),
        in_specs=[_rows(ts, W), _rows(ts, W), _vec(W), _vec(W), _rows(ts, W)],
        out_specs=[_rows(ts, W), _vec(W), _vec(W), _vec(W)],
        out_shape=[jax.ShapeDtypeStruct((S, W), F32), vec_out, vec_out, vec_out],
        compiler_params=_cp("arbitrary"), name=name)(x, dh, g, sc, dres)


def _rmsnorm_fwd(p, width, col, g, name):
    S = p.shape[0]
    ts = _ts(S)

    def body(x_ref, g_ref, y_ref):
        xv = x_ref[...]
        y_ref[...] = ((xv * _rms(xv)) * g_ref[...]).astype(BF16)

    return pl.pallas_call(
        body, grid=(S // ts,), in_specs=[_rows(ts, width, col), _vec(width)],
        out_specs=_rows(ts, width), out_shape=jax.ShapeDtypeStruct((S, width), BF16),
        compiler_params=_cp("parallel"), name=name)(p, g)


def _rmsnorm_bwd(p, width, col, dn, g, name):
    S = p.shape[0]
    ts = _ts(S)

    def body(x_ref, dn_ref, g_ref, dx_ref, dg_ref):
        @pl.when(pl.program_id(0) == 0)
        def _():
            dg_ref[...] = jnp.zeros_like(dg_ref)

        xv, dn_v = x_ref[...], dn_ref[...]
        r = _rms(xv)
        y = xv * r
        dy = dn_v * g_ref[...]
        dx_ref[...] = r * (dy - y * jnp.mean(dy * y, axis=-1, keepdims=True))
        dg_ref[...] += _colsum(dn_v * y)

    return pl.pallas_call(
        body, grid=(S // ts,), in_specs=[_rows(ts, width, col), _rows(ts, width), _vec(width)],
        out_specs=[_rows(ts, width), _vec(width)],
        out_shape=[jax.ShapeDtypeStruct((S, width), F32), jax.ShapeDtypeStruct((1, width), F32)],
        compiler_params=_cp("arbitrary"), name=name)(p, dn, g)


def _rope_rot(t, c, s1, s2):
    return t * c + pltpu.roll(t, LANE - 16, 1) * s1 + pltpu.roll(t, 16, 1) * s2


def _rope_rot_t(d, c, s1, s2):
    return d * c + pltpu.roll(d * s1, 16, 1) + pltpu.roll(d * s2, LANE - 16, 1)


def _rope_fwd(qp, p, tabs, name):
    S = qp.shape[0]
    ts = _ts(S)
    W = N_HEADS * LANE

    def body(q_ref, kr_ref, c_ref, s1_ref, s2_ref, qr_ref, kpe_ref):
        c, s1, s2 = c_ref[...], s1_ref[...], s2_ref[...]
        for h in range(N_HEADS):
            sl = slice(h * LANE, (h + 1) * LANE)
            qr_ref[:, sl] = _rope_rot(q_ref[:, sl], c, s1, s2).astype(BF16)
        kpe_ref[...] = _rope_rot(kr_ref[...], c, s1, s2).astype(BF16)

    tab = _rows(ts, LANE)
    return pl.pallas_call(
        body, grid=(S // ts,), in_specs=[_rows(ts, W), _rows(ts, LANE, COL_KROPE // LANE), tab, tab, tab],
        out_specs=[_rows(ts, W), _rows(ts, LANE)],
        out_shape=[jax.ShapeDtypeStruct((S, W), BF16), jax.ShapeDtypeStruct((S, LANE), BF16)],
        compiler_params=_cp("parallel"), name=name)(qp, p, *tabs)


def _rope_bwd(dqr, dkpe_heads, tabs, name):
    S = dqr.shape[0]
    ts = _ts(S)
    W = N_HEADS * LANE

    def body(dq_ref, dk_ref, c_ref, s1_ref, s2_ref, dqp_ref, dkr_ref):
        c, s1, s2 = c_ref[...], s1_ref[...], s2_ref[...]
        dk = dk_ref[:, 0:LANE]
        for h in range(N_HEADS):
            sl = slice(h * LANE, (h + 1) * LANE)
            dqp_ref[:, sl] = _rope_rot_t(dq_ref[:, sl], c, s1, s2).astype(BF16)
            if h:
                dk = dk + dk_ref[:, sl]
        dkr_ref[...] = _rope_rot_t(dk, c, s1, s2)

    tab = _rows(ts, LANE)
    return pl.pallas_call(
        body, grid=(S // ts,), in_specs=[_rows(ts, W), _rows(ts, W), tab, tab, tab],
        out_specs=[_rows(ts, W), _rows(ts, LANE)],
        out_shape=[jax.ShapeDtypeStruct((S, W), BF16), jax.ShapeDtypeStruct((S, LANE), F32)],
        compiler_params=_cp("parallel"), name=name)(dqr, dkpe_heads, *tabs)


def _merge_fwd(p, o_sb, o_mla, name):
    S, W = o_sb.shape
    ts = _ts(S)

    def body(gs_ref, gm_ref, a_ref, b_ref, m_ref):
        m_ref[...] = (jax.nn.sigmoid(gs_ref[...]) * a_ref[...]
                      + jax.nn.sigmoid(gm_ref[...]) * b_ref[...]).astype(BF16)

    return pl.pallas_call(
        body, grid=(S // ts,),
        in_specs=[_rows(ts, W, COL_GATE_SB // W), _rows(ts, W, COL_GATE_MLA // W), _rows(ts, W), _rows(ts, W)],
        out_specs=_rows(ts, W), out_shape=jax.ShapeDtypeStruct((S, W), BF16),
        compiler_params=_cp("parallel"), name=name)(p, p, o_sb, o_mla)


def _merge_bwd(p, o_sb, o_mla, dm, name):
    S, W = o_sb.shape
    ts = _ts(S)

    def body(gs_ref, gm_ref, a_ref, b_ref, dm_ref, da_ref, db_ref, dgs_ref, dgm_ref):
        dmv = dm_ref[...]
        sa, sb = jax.nn.sigmoid(gs_ref[...]), jax.nn.sigmoid(gm_ref[...])
        da_ref[...] = (dmv * sa).astype(BF16)
        db_ref[...] = (dmv * sb).astype(BF16)
        dgs_ref[...] = dmv * a_ref[...] * sa * (1.0 - sa)
        dgm_ref[...] = dmv * b_ref[...] * sb * (1.0 - sb)

    row = _rows(ts, W)
    return pl.pallas_call(
        body, grid=(S // ts,),
        in_specs=[_rows(ts, W, COL_GATE_SB // W), _rows(ts, W, COL_GATE_MLA // W), row, row, row],
        out_specs=[row, row, row, row],
        out_shape=[jax.ShapeDtypeStruct((S, W), BF16), jax.ShapeDtypeStruct((S, W), BF16),
                   jax.ShapeDtypeStruct((S, W), F32), jax.ShapeDtypeStruct((S, W), F32)],
        compiler_params=_cp("parallel"), name=name)(p, p, o_sb, o_mla, dm)


def _res_fwd(x, y, gate, name):
    S, W = x.shape
    ts = _ts(S)

    def body(x_ref, y_ref, g_ref, o_ref):
        o_ref[...] = x_ref[...] + g_ref[...] * y_ref[...]

    return pl.pallas_call(
        body, grid=(S // ts,), in_specs=[_rows(ts, W), _rows(ts, W), _vec(W)], out_specs=_rows(ts, W),
        out_shape=jax.ShapeDtypeStruct((S, W), F32), compiler_params=_cp("parallel"), name=name)(x, y, gate)


def _res_bwd(dx, y, gate, name):
    S, W = dx.shape
    ts = _ts(S)

    def body(dx_ref, y_ref, g_ref, dy_ref, dg_ref):
        @pl.when(pl.program_id(0) == 0)
        def _():
            dg_ref[...] = jnp.zeros_like(dg_ref)

        dxv = dx_ref[...]
        dy_ref[...] = (g_ref[...] * dxv).astype(BF16)
        dg_ref[...] += _colsum(dxv * y_ref[...])

    return pl.pallas_call(
        body, grid=(S // ts,), in_specs=[_rows(ts, W), _rows(ts, W), _vec(W)],
        out_specs=[_rows(ts, W), _vec(W)],
        out_shape=[jax.ShapeDtypeStruct((S, W), BF16), jax.ShapeDtypeStruct((1, W), F32)],
        compiler_params=_cp("arbitrary"), name=name)(dx, y, gate)


def _final_loss(x, target, g, name):
    S, W = x.shape
    ts = _ts(S)

    def body(x_ref, t_ref, g_ref, dx_ref, dg_ref, loss_ref):
        @pl.when(pl.program_id(0) == 0)
        def _():
            dg_ref[...] = jnp.zeros_like(dg_ref)
            loss_ref[...] = jnp.zeros_like(loss_ref)

        xv, gv = x_ref[...], g_ref[...]
        r = _rms(xv)
        y = xv * r
        err = y * gv - t_ref[...]
        loss_ref[...] += jnp.full((1, LANE), 0.5 * jnp.sum(jnp.mean(err * err, axis=-1)), F32)
        dout = err * (1.0 / W)
        dy = dout * gv
        dx_ref[...] = r * (dy - y * jnp.mean(dy * y, axis=-1, keepdims=True))
        dg_ref[...] += _colsum(dout * y)

    return pl.pallas_call(
        body, grid=(S // ts,), in_specs=[_rows(ts, W), _rows(ts, W), _vec(W)],
        out_specs=[_rows(ts, W), _vec(W), _vec(LANE)],
        out_shape=[jax.ShapeDtypeStruct((S, W), F32), jax.ShapeDtypeStruct((1, W), F32),
                   jax.ShapeDtypeStruct((1, LANE), F32)],
        compiler_params=_cp("arbitrary"), name=name)(x, target, g)


def _silu(c, name):
    def body(c_ref, o_ref):
        cv = c_ref[...]
        o_ref[...] = cv * jax.nn.sigmoid(cv)

    return pl.pallas_call(body, out_shape=jax.ShapeDtypeStruct(c.shape, F32), name=name)(c)


def _bias_add(a, b, name):
    def body(a_ref, b_ref, o_ref):
        o_ref[...] = a_ref[...] + b_ref[...]

    return pl.pallas_call(body, out_shape=jax.ShapeDtypeStruct(a.shape, F32), name=name)(a, b)


def _sum_blocks(xs, name):
    n = xs.shape[0]

    def body(x_ref, o_ref):
        acc = x_ref[0]
        for d in range(1, n):
            acc = acc + x_ref[d]
        o_ref[...] = acc

    return pl.pallas_call(body, out_shape=jax.ShapeDtypeStruct(xs.shape[1:], F32), name=name)(xs)


def _adamw(w, g, m, v, name):
    shape = w.shape
    cols = shape[-1]
    w2, g2, m2, v2 = (t.reshape(-1, cols) for t in (w, g, m, v))
    rows = w2.shape[0]
    tr = _tile(rows, (256, 128))
    c1 = 1.0 - ADAM_B1 ** ADAM_STEP
    c2 = 1.0 - ADAM_B2 ** ADAM_STEP

    def body(w_ref, g_ref, m_ref, v_ref, d_ref, nm_ref, nv_ref):
        gv = g_ref[...]
        nm = ADAM_B1 * m_ref[...] + (1.0 - ADAM_B1) * gv
        nv = ADAM_B2 * v_ref[...] + (1.0 - ADAM_B2) * (gv * gv)
        d_ref[...] = -ADAM_LR * ((nm / c1) / (jnp.sqrt(nv / c2) + ADAM_EPS) + ADAM_WD * w_ref[...])
        nm_ref[...] = nm
        nv_ref[...] = nv

    spec = pl.BlockSpec((tr, cols), lambda i: (i, 0))
    out = jax.ShapeDtypeStruct((rows, cols), F32)
    d, nm, nv = pl.pallas_call(
        body, grid=(rows // tr,), in_specs=[spec] * 4, out_specs=[spec] * 3, out_shape=[out] * 3,
        compiler_params=_cp("parallel"), name=name)(w2, g2, m2, v2)
    return d.reshape(shape), nm.reshape(shape), nv.reshape(shape)


def _split_dot(x, tri):
    hi = x.astype(BF16)
    lo = (x - hi.astype(F32)).astype(BF16)
    return _nn(hi, tri) + _nn(lo, tri)


def _tri_cumsum(x, tri, later):
    h = x.shape[1] // 2
    first, second = x[:, :h], x[:, h:]
    sum_first = jnp.sum(first, axis=1, keepdims=True)
    sum_second = jnp.sum(second, axis=1, keepdims=True)
    run_first, run_second = _split_dot(first, tri), _split_dot(second, tri)
    if later:
        run_first = run_first + sum_second
    else:
        run_second = run_second + sum_first
    return jnp.concatenate([run_first, run_second], axis=1), sum_first + sum_second


def _sb_logs(z):
    soft = jnp.log(1.0 + jnp.exp(-jnp.abs(z)))
    return jnp.minimum(z, 0.0) - soft, -jnp.maximum(z, 0.0) - soft


def _attn_call(body, grid, ins, in_specs, out_specs, out_shape, scratch, exs, name):
    body, extra, split = _carry(exs, body, len(ins), len(out_shape), len(scratch), grid)
    return split(pl.pallas_call(
        body, grid=grid, in_specs=in_specs + extra["in_specs"], out_specs=out_specs + extra["out_specs"],
        out_shape=out_shape + extra["out_shape"], scratch_shapes=scratch + extra["scratch"],
        input_output_aliases=extra["aliases"], compiler_params=_cp("arbitrary", "arbitrary"),
        name=name)(*ins, *extra["ins"]))


def _sb_fwd(p, name, exs=None):
    S = p.shape[0]
    t = _attn_tile(S)
    qb, kb, vb = COL_QSB // LANE, COL_KSB // LANE, COL_VSB // LANE

    def body(q_ref, k_ref, v_ref, o_ref, cb_ref, acc_ref):
        i = pl.program_id(1)
        lane = lax.broadcasted_iota(jnp.int32, (t, LANE), 1)
        rows = lax.broadcasted_iota(jnp.int32, (t, t), 0)
        cols = lax.broadcasted_iota(jnp.int32, (t, t), 1)
        half_r = lax.broadcasted_iota(jnp.int32, (t // 2, t // 2), 0)
        half_c = lax.broadcasted_iota(jnp.int32, (t // 2, t // 2), 1)
        after = jnp.where(half_r > half_c, 1.0, 0.0).astype(BF16)
        diag = cols < rows
        q = q_ref[...] * SB_SCALE
        acc_ref[...] = jnp.zeros_like(acc_ref)
        cb_ref[...] = jnp.zeros_like(cb_ref)
        hms = [(lane >= SB_DIM * h) & (lane < SB_DIM * (h + 1)) for h in range(2)]
        qhs = [jnp.where(hm, q, 0.0).astype(BF16) for hm in hms]

        def step(j, cs, masked):
            rows_j = pl.ds(pl.multiple_of(j * t, t), t)
            kj = k_ref[rows_j, :].astype(BF16)
            vf = v_ref[rows_j, :]
            out, pv = [], None
            for h in range(2):
                ls, lf = _sb_logs(_nt(qhs[h], kj))
                if masked:
                    lf = jnp.where(diag, lf, 0.0)
                survive, total = _tri_cumsum(lf, after, True)
                a = jnp.exp(ls + survive + cs[h])
                if masked:
                    a = jnp.where(diag, a, 0.0)
                term = _nn(a.astype(BF16), jnp.where(hms[h], vf, 0.0).astype(BF16))
                pv = term if pv is None else pv + term
                cb_ref[h] = jnp.where(lane == j, cs[h], cb_ref[h])
                out.append(cs[h] + total)
            acc_ref[...] += pv
            return tuple(out)

        zero = jnp.zeros((t, 1), F32)
        cs = step(i, (zero, zero), True)
        lax.fori_loop(0, i, lambda it, cs: step(i - 1 - it, cs, False), cs)
        o_ref[...] = acc_ref[...].astype(BF16)

    return _attn_call(
        body, (SB_WIDTH // LANE, S // t), [p, p, p],
        [pl.BlockSpec((t, LANE), lambda hp, i: (i, qb + hp)),
         pl.BlockSpec((S, LANE), lambda hp, i: (0, kb + hp)),
         pl.BlockSpec((S, LANE), lambda hp, i: (0, vb + hp))],
        [pl.BlockSpec((t, LANE), lambda hp, i: (i, hp)),
         pl.BlockSpec((2, t, LANE), lambda hp, i: (hp, i, 0))],
        [jax.ShapeDtypeStruct((S, SB_WIDTH), BF16), jax.ShapeDtypeStruct((N_HEADS, S, LANE), F32)],
        [pltpu.VMEM((t, LANE), F32)], exs, name)


def _sb_bwd(p, do, cb, name, exs=None):
    S = p.shape[0]
    t = _attn_tile(S)
    qb, kb, vb = COL_QSB // LANE, COL_KSB // LANE, COL_VSB // LANE

    def body(q_ref, k_ref, v_ref, do_ref, cb_ref, dq_ref, dk_ref, dv_ref, acc_ref):
        i = pl.program_id(1)

        @pl.when(i == 0)
        def _():
            dk_ref[...] = jnp.zeros_like(dk_ref)
            dv_ref[...] = jnp.zeros_like(dv_ref)

        lane = lax.broadcasted_iota(jnp.int32, (t, LANE), 1)
        rows = lax.broadcasted_iota(jnp.int32, (t, t), 0)
        cols = lax.broadcasted_iota(jnp.int32, (t, t), 1)
        half_r = lax.broadcasted_iota(jnp.int32, (t // 2, t // 2), 0)
        half_c = lax.broadcasted_iota(jnp.int32, (t // 2, t // 2), 1)
        after = jnp.where(half_r > half_c, 1.0, 0.0).astype(BF16)
        before = jnp.where(half_r < half_c, 1.0, 0.0).astype(BF16)
        diag = cols < rows
        q = q_ref[...] * SB_SCALE
        dov = do_ref[...]
        acc_ref[...] = jnp.zeros_like(acc_ref)
        hms = [(lane >= SB_DIM * h) & (lane < SB_DIM * (h + 1)) for h in range(2)]
        qhs = [jnp.where(hm, q, 0.0).astype(BF16) for hm in hms]
        dohs = [jnp.where(hm, dov, 0.0).astype(BF16) for hm in hms]

        def step(j, fs, masked):
            rows_j = pl.ds(pl.multiple_of(j * t, t), t)
            kf = k_ref[rows_j, :]
            kj = kf.astype(BF16)
            vj = v_ref[rows_j, :].astype(BF16)
            out, dq_t, dk_t, dv_t = [], None, None, None
            for h in range(2):
                ls, lf = _sb_logs(_nt(qhs[h], kj))
                if masked:
                    lf = jnp.where(diag, lf, 0.0)
                c = jnp.sum(jnp.where(lane == j, cb_ref[h], 0.0), axis=1, keepdims=True)
                a = jnp.exp(ls + _tri_cumsum(lf, after, True)[0] + c)
                if masked:
                    a = jnp.where(diag, a, 0.0)
                dl = _nt(dohs[h], vj) * a
                sg = jnp.exp(ls)
                earlier, total = _tri_cumsum(dl, before, False)
                dz = dl * (1.0 - sg) - sg * (earlier + fs[h])
                if masked:
                    dz = jnp.where(diag, dz, 0.0)
                dzb = dz.astype(BF16)
                terms = (_nn(dzb, jnp.where(hms[h], kf, 0.0).astype(BF16)), _tn(dzb, qhs[h]),
                         _tn(a.astype(BF16), dohs[h]))
                dq_t, dk_t, dv_t = terms if dq_t is None else (dq_t + terms[0], dk_t + terms[1], dv_t + terms[2])
                out.append(fs[h] + total)
            acc_ref[...] += dq_t
            dk_ref[rows_j, :] += dk_t
            dv_ref[rows_j, :] += dv_t
            return tuple(out)

        zero = jnp.zeros((t, 1), F32)
        fs = lax.fori_loop(0, i, lambda j, fs: step(j, fs, False), (zero, zero))
        step(i, fs, True)
        dq_ref[...] = acc_ref[...] * SB_SCALE

    col = lambda hp, i: (0, hp)
    out = jax.ShapeDtypeStruct((S, SB_WIDTH), F32)
    return _attn_call(
        body, (SB_WIDTH // LANE, S // t), [p, p, p, do, cb],
        [pl.BlockSpec((t, LANE), lambda hp, i: (i, qb + hp)),
         pl.BlockSpec((S, LANE), lambda hp, i: (0, kb + hp)),
         pl.BlockSpec((S, LANE), lambda hp, i: (0, vb + hp)),
         pl.BlockSpec((t, LANE), lambda hp, i: (i, hp)),
         pl.BlockSpec((2, t, LANE), lambda hp, i: (hp, i, 0))],
        [pl.BlockSpec((t, LANE), lambda hp, i: (i, hp)), pl.BlockSpec((S, LANE), col), pl.BlockSpec((S, LANE), col)],
        [out, out, out], [pltpu.VMEM((t, LANE), F32)], exs, name)


def _mla_fwd(qr, kv, kpe, name, exs=None):
    S = qr.shape[0]
    t = _attn_tile(S)

    def body(q_ref, kv_ref, kpe_ref, o_ref, lse_ref, acc_ref, m_ref):
        i = pl.program_id(1)
        low = lax.broadcasted_iota(jnp.int32, (t, LANE), 1) < NOPE_DIM
        rows = lax.broadcasted_iota(jnp.int32, (t, t), 0)
        cols = lax.broadcasted_iota(jnp.int32, (t, t), 1)
        causal = cols <= rows
        one = jnp.ones((t, LANE), BF16)
        heads = [slice(h * LANE, (h + 1) * LANE) for h in range(2)]
        qs = [q_ref[:, sl] for sl in heads]
        acc_ref[...] = jnp.zeros_like(acc_ref)
        m_ref[...] = jnp.full_like(m_ref, NEG_BIG)

        def step(j, masked):
            rows_j = pl.ds(pl.multiple_of(j * t, t), t)
            kpe_j = kpe_ref[rows_j, :]
            for h, sl in enumerate(heads):
                kvj = kv_ref[rows_j, sl]
                z = _nt(qs[h], jnp.where(low, kvj, kpe_j)) * MLA_SCALE
                if masked:
                    z = jnp.where(causal, z, NEG_BIG)
                m_old = m_ref[h]
                m_new = jnp.maximum(m_old, jnp.max(z, axis=1, keepdims=True))
                pr = jnp.exp(z - m_new)
                acc_ref[:, sl] = jnp.exp(m_old - m_new) * acc_ref[:, sl] + _nn(
                    pr.astype(BF16), jnp.where(low, one, kvj))
                m_ref[h] = m_new

        def loop(j, carry):
            step(j, False)
            return carry

        lax.fori_loop(0, i, loop, 0)
        step(i, True)
        for h, sl in enumerate(heads):
            acc = acc_ref[:, sl]
            den = acc[:, 0:1]
            o_ref[:, sl] = jnp.where(low, 0.0, acc / den).astype(BF16)
            lse_ref[h] = jnp.broadcast_to(m_ref[h] + jnp.log(den), (t, LANE))

    pair = 2 * LANE
    return _attn_call(
        body, (N_HEADS // 2, S // t), [qr, kv, kpe],
        [pl.BlockSpec((t, pair), lambda hp, i: (i, hp)),
         pl.BlockSpec((S, pair), lambda hp, i: (0, hp)),
         pl.BlockSpec((S, LANE), lambda hp, i: (0, 0))],
        [pl.BlockSpec((t, pair), lambda hp, i: (i, hp)), pl.BlockSpec((2, t, LANE), lambda hp, i: (hp, i, 0))],
        [jax.ShapeDtypeStruct((S, N_HEADS * LANE), BF16), jax.ShapeDtypeStruct((N_HEADS, S, LANE), F32)],
        [pltpu.VMEM((t, pair), F32), pltpu.VMEM((2, t, 1), F32)], exs, name)


def _mla_bwd(qr, kv, kpe, do, o, lse, name, exs=None):
    S = qr.shape[0]
    t = _attn_tile(S)

    def body(q_ref, kv_ref, kpe_ref, do_ref, o_ref, lse_ref, dq_ref, dkv_ref, dkpe_ref, acc_ref):
        i = pl.program_id(1)

        @pl.when(i == 0)
        def _():
            dkv_ref[...] = jnp.zeros_like(dkv_ref)
            dkpe_ref[...] = jnp.zeros_like(dkpe_ref)

        low = lax.broadcasted_iota(jnp.int32, (t, LANE), 1) < NOPE_DIM
        rows = lax.broadcasted_iota(jnp.int32, (t, t), 0)
        cols = lax.broadcasted_iota(jnp.int32, (t, t), 1)
        causal = cols <= rows
        heads = [slice(h * LANE, (h + 1) * LANE) for h in range(2)]
        qs = [q_ref[:, sl] for sl in heads]
        dovs = [do_ref[:, sl] for sl in heads]
        dobs = [d.astype(BF16) for d in dovs]
        deltas = [jnp.sum(dovs[h] * o_ref[:, sl].astype(F32), axis=1, keepdims=True) for h, sl in enumerate(heads)]
        lses = [lse_ref[h][:, 0:1] for h in range(2)]
        acc_ref[...] = jnp.zeros_like(acc_ref)

        def step(j, masked):
            rows_j = pl.ds(pl.multiple_of(j * t, t), t)
            kpe_j = kpe_ref[rows_j, :]
            for h, sl in enumerate(heads):
                kvj = kv_ref[rows_j, sl]
                kcat = jnp.where(low, kvj, kpe_j)
                z = _nt(qs[h], kcat) * MLA_SCALE
                if masked:
                    z = jnp.where(causal, z, NEG_BIG)
                pr = jnp.exp(z - lses[h])
                ds = (pr * (_nt(dobs[h], kvj) - deltas[h])).astype(BF16)
                acc_ref[:, sl] += _nn(ds, kcat)
                dkc = _tn(ds, qs[h]) * MLA_SCALE
                dkv_ref[rows_j, sl] += jnp.where(low, dkc, _tn(pr.astype(BF16), dobs[h]))
                dkpe_ref[rows_j, sl] += jnp.where(low, 0.0, dkc)

        def loop(j, carry):
            step(j, False)
            return carry

        lax.fori_loop(0, i, loop, 0)
        step(i, True)
        dq_ref[...] = acc_ref[...] * MLA_SCALE

    pair = 2 * LANE
    blk = pl.BlockSpec((t, pair), lambda hp, i: (i, hp))
    col = pl.BlockSpec((S, pair), lambda hp, i: (0, hp))
    out = jax.ShapeDtypeStruct((S, N_HEADS * LANE), F32)
    return _attn_call(
        body, (N_HEADS // 2, S // t), [qr, kv, kpe, do, o, lse],
        [blk, col, pl.BlockSpec((S, LANE), lambda hp, i: (0, 0)), blk, blk,
         pl.BlockSpec((2, t, LANE), lambda hp, i: (hp, i, 0))],
        [blk, col, col], [out, out, out], [pltpu.VMEM((t, pair), F32)], exs, name)


_ANY = pl.BlockSpec(memory_space=pl.ANY)


def _place():
    return lax.axis_index("x"), lax.axis_index("y"), lax.axis_index("c")


class _Exchange(NamedTuple):
    ins: Sequence[Any]
    outs: Sequence[Any]
    aliases: Mapping[int, int]
    n_remote: int
    n_local: int
    start: Callable
    finish: Callable


def _exchange_scratch(ex):
    return [pltpu.SemaphoreType.DMA((ex.n_remote,)), pltpu.SemaphoreType.DMA((ex.n_remote,)),
            pltpu.SemaphoreType.DMA((ex.n_local,))]


def _run_exchange(ex, name):
    n_in, n_out = len(ex.ins), len(ex.outs)

    def body(*refs):
        args = (refs[:n_in], refs[n_in:n_in + n_out], *refs[n_in + n_out:])
        ex.start(*args)
        ex.finish(*args)

    return pl.pallas_call(
        body, out_shape=list(ex.outs), in_specs=[_ANY] * n_in, out_specs=[_ANY] * n_out,
        scratch_shapes=_exchange_scratch(ex), input_output_aliases=dict(ex.aliases), name=name)(*ex.ins)


def _carry(exs, body, n_in, n_out, n_scratch, grid):
    exs = [ex for ex in (exs or []) if ex is not None]
    e_ins, e_outs = [len(ex.ins) for ex in exs], [len(ex.outs) for ex in exs]

    def take(refs, counts):
        groups = []
        for n in counts:
            groups.append(refs[:n])
            refs = refs[n:]
        return groups, refs

    def carried(*refs):
        own_in, refs = refs[:n_in], refs[n_in:]
        ex_in, refs = take(refs, e_ins)
        own_out, refs = refs[:n_out], refs[n_out:]
        ex_out, refs = take(refs, e_outs)
        own_scratch, refs = refs[:n_scratch], refs[n_scratch:]
        sems, _ = take(refs, [3] * len(exs))
        at = [pl.program_id(d) for d in range(len(grid))]
        first, last = at[0] == 0, at[0] == grid[0] - 1
        for d in range(1, len(grid)):
            first, last = first & (at[d] == 0), last & (at[d] == grid[d] - 1)

        @pl.when(first)
        def _():
            for e, ex in enumerate(exs):
                ex.start(ex_in[e], ex_out[e], *sems[e])

        body(*own_in, *own_out, *own_scratch)

        @pl.when(last)
        def _():
            for e, ex in enumerate(exs):
                ex.finish(ex_in[e], ex_out[e], *sems[e])

    aliases, i0, o0 = {}, n_in, n_out
    for ex in exs:
        aliases.update({i0 + i: o0 + o for i, o in ex.aliases.items()})
        i0, o0 = i0 + len(ex.ins), o0 + len(ex.outs)

    def split(res):
        groups, _ = take(list(res[n_out:]), e_outs)
        return list(res[:n_out]), groups

    extra = dict(
        ins=[a for ex in exs for a in ex.ins], in_specs=[_ANY] * sum(e_ins), out_specs=[_ANY] * sum(e_outs),
        out_shape=[o for ex in exs for o in ex.outs], scratch=[s for ex in exs for s in _exchange_scratch(ex)],
        aliases=aliases)
    return (carried if exs else body), extra, split


def _gather_exchange(arrs, phase="all"):
    n_t = len(arrs)
    ms = [a.shape[0] // (8 if phase == "b" else 1) for a in arrs]

    def plan(in_refs, out_refs, send_sems, recv_sems, local_sems):
        x, y, c = _place()
        me, sibling = (x, y, c), (x, y, 1 - c)
        chips = [(1 - x, y), (x, 1 - y), (1 - x, 1 - y)]

        def rows(ref, t, px, py, pc):
            return ref.at[pl.ds((4 * px + 2 * py + pc) * ms[t], ms[t]), :]

        def copy(t, k, block, to, src):
            return pltpu.make_async_remote_copy(
                src_ref=src, dst_ref=rows(out_refs[t], t, *block), send_sem=send_sems.at[7 * t + k],
                recv_sem=recv_sems.at[7 * t + k], device_id=to, device_id_type=MESH)

        mine, first, first_in, passed, passed_in = [], [], [], [], []
        for t in range(n_t):
            if phase != "b":
                mine.append(pltpu.make_async_copy(in_refs[t], rows(out_refs[t], t, *me), local_sems.at[t]))
                first.append(copy(t, 0, me, sibling, in_refs[t]))
                first_in.append(copy(t, 0, sibling, me, in_refs[t]))
                for j, chip in enumerate(chips):
                    first.append(copy(t, 1 + j, me, (*chip, c), in_refs[t]))
                    first_in.append(copy(t, 1 + j, (*chip, c), me, in_refs[t]))
            if phase != "a":
                held = in_refs[t] if phase == "b" else out_refs[t]
                for j, chip in enumerate(chips):
                    passed.append(copy(t, 4 + j, (*chip, c), sibling, rows(held, t, *chip, c)))
                    passed_in.append(copy(t, 4 + j, (*chip, 1 - c), me, rows(held, t, *chip, c)))
        return mine, first, first_in, passed, passed_in

    def start(*refs):
        mine, first, _, passed, _ = plan(*refs)
        for cp in mine + first + (passed if phase == "b" else []):
            cp.start()

    def finish(*refs):
        mine, first, first_in, passed, passed_in = plan(*refs)
        for cp in first_in:
            cp.wait_recv()
        if phase == "all":
            for cp in passed:
                cp.start()
        for cp in passed_in:
            cp.wait_recv()
        for cp in first + passed:
            cp.wait_send()
        for cp in mine:
            cp.wait()

    if phase == "b":
        outs = [jax.ShapeDtypeStruct(a.shape, a.dtype) for a in arrs]
        aliases = {t: t for t in range(n_t)}
    else:
        outs = [jax.ShapeDtypeStruct((8 * a.shape[0], a.shape[1]), a.dtype) for a in arrs]
        aliases = {}
    return _Exchange(list(arrs), outs, aliases, 7 * n_t, n_t, start, finish)


def _all_gather8(blks, name):
    return _run_exchange(_gather_exchange(blks), name)


def _swap_halves_exchange(gs):
    n_t = len(gs)

    def plan(g_refs, out_refs, send_sems, recv_sems, local_sems):
        x, y, c = _place()
        copies = []
        for t in range(n_t):
            m = gs[t].shape[1] // 2
            copies += [pltpu.make_async_remote_copy(
                src_ref=g_refs[t].at[s, pl.ds((1 - c) * m, m), :], dst_ref=out_refs[t].at[s],
                send_sem=send_sems.at[4 * t + s], recv_sem=recv_sems.at[4 * t + s], device_id=(x, y, 1 - c),
                device_id_type=MESH) for s in range(4)]
        return copies

    def start(*refs):
        for cp in plan(*refs):
            cp.start()

    def finish(*refs):
        for cp in plan(*refs):
            cp.wait()

    outs = [jax.ShapeDtypeStruct((4, g.shape[1] // 2, g.shape[2]), g.dtype) for g in gs]
    return _Exchange(list(gs), outs, {}, 4 * n_t, 1, start, finish)


def _chip_scatter_exchange(parts):
    n_t = len(parts)

    def plan(p_refs, out_refs, send_sems, recv_sems, local_sems):
        x, y, c = _place()
        mine = 2 * x + y
        chips = [(1 - x, y), (x, 1 - y), (1 - x, 1 - y)]

        def copy(t, j, src_slot, dst_slot):
            px, py = chips[j]
            return pltpu.make_async_remote_copy(
                src_ref=p_refs[t].at[src_slot], dst_ref=out_refs[t].at[dst_slot],
                send_sem=send_sems.at[3 * t + j], recv_sem=recv_sems.at[3 * t + j], device_id=(px, py, c),
                device_id_type=MESH)

        own = [pltpu.make_async_copy(p_refs[t].at[mine], out_refs[t].at[mine], local_sems.at[t])
               for t in range(n_t)]
        sends = [copy(t, j, 2 * px + py, mine) for t in range(n_t) for j, (px, py) in enumerate(chips)]
        arrivals = [copy(t, j, mine, 2 * px + py) for t in range(n_t) for j, (px, py) in enumerate(chips)]
        return own, sends, arrivals

    def start(*refs):
        own, sends, _ = plan(*refs)
        for cp in own + sends:
            cp.start()

    def finish(*refs):
        own, sends, arrivals = plan(*refs)
        for cp in arrivals:
            cp.wait_recv()
        for cp in sends:
            cp.wait_send()
        for cp in own:
            cp.wait()

    outs = [jax.ShapeDtypeStruct(p.shape, p.dtype) for p in parts]
    return _Exchange(list(parts), outs, {}, 3 * n_t, n_t, start, finish)


def _sibling_gather_exchange(bufs):
    n_t = len(bufs)

    def plan(b_refs, out_refs, send_sems, recv_sems, local_sems):
        x, y, c = _place()

        def copy(t, pc):
            m = bufs[t].shape[0] // 2
            half = pl.ds(pc * m, m)
            return pltpu.make_async_remote_copy(
                src_ref=b_refs[t].at[half, :], dst_ref=out_refs[t].at[half, :], send_sem=send_sems.at[t],
                recv_sem=recv_sems.at[t], device_id=(x, y, 1 - c), device_id_type=MESH)

        return [copy(t, c) for t in range(n_t)], [copy(t, 1 - c) for t in range(n_t)]

    def start(*refs):
        for cp in plan(*refs)[0]:
            cp.start()

    def finish(*refs):
        sends, arrivals = plan(*refs)
        for cp in arrivals:
            cp.wait_recv()
        for cp in sends:
            cp.wait_send()

    outs = [jax.ShapeDtypeStruct(b.shape, b.dtype) for b in bufs]
    return _Exchange(list(bufs), outs, {t: t for t in range(n_t)}, n_t, 1, start, finish)


def _add_halves(g, recv, c, name):
    n_slot, m2, n = g.shape
    m = m2 // 2
    tr = _tile(m, (512, 256, 192, 128, 16))

    def body(c_ref, g_ref, r_ref, o_ref):
        o_ref[...] = (g_ref[...] + r_ref[...].astype(F32)).astype(BF16)

    nb = m // tr
    return pl.pallas_call(
        body,
        grid_spec=pltpu.PrefetchScalarGridSpec(
            num_scalar_prefetch=1, grid=(n_slot, nb),
            in_specs=[pl.BlockSpec((1, tr, n), lambda s, i, c_ref: (s, c_ref[0] * nb + i, 0)),
                      pl.BlockSpec((1, tr, n), lambda s, i, c_ref: (s, i, 0))],
            out_specs=pl.BlockSpec((1, tr, n), lambda s, i, c_ref: (s, i, 0))),
        out_shape=jax.ShapeDtypeStruct((n_slot, m, n), BF16),
        compiler_params=_cp("parallel", "parallel"), name=name)(c, g, recv)


def _sum_slots(parts, c, name):
    n_slot, m, n = parts.shape
    tr = _tile(m, (512, 256, 192, 128, 16))
    nb = m // tr

    def body(c_ref, p_ref, o_ref):
        acc = p_ref[0].astype(F32)
        for s in range(1, n_slot):
            acc = acc + p_ref[s].astype(F32)
        o_ref[...] = acc

    return pl.pallas_call(
        body,
        grid_spec=pltpu.PrefetchScalarGridSpec(
            num_scalar_prefetch=1, grid=(nb,),
            in_specs=[pl.BlockSpec((n_slot, tr, n), lambda i, c_ref: (0, i, 0))],
            out_specs=pl.BlockSpec((tr, n), lambda i, c_ref: (c_ref[0] * nb + i, 0))),
        out_shape=jax.ShapeDtypeStruct((2 * m, n), F32),
        compiler_params=_cp("parallel"), name=name)(c, parts)


_SHARDED = ("w_in", "w_q_up", "w_kv_up", "w_sb_out", "w_mla_out", "w_mix_out", "w_up", "w_down")
_ROW_SHARDED = ("w_mix_out", "w_down")
_BY_CHIP = ("w_up", "w_down")


def _unshard(parts, name):
    n, r, cs = parts.shape
    if name in _ROW_SHARDED:
        return parts.reshape(n * r, cs)
    return parts.transpose(1, 0, 2).reshape(r, n * cs)


def _reshard(full, name, n=4):
    R, C = full.shape
    if name in _ROW_SHARDED:
        return full.reshape(n, R // n, C)
    return full.reshape(R, n, C // n).transpose(1, 0, 2)


def _pad_w_in(w):
    z = lambda k: jnp.zeros(w.shape[:-1] + (k,), w.dtype)
    return jnp.concatenate([
        w[..., 2208:3232], w[..., 3232:4256], w[..., 0:1536], w[..., 1920:2176], w[..., 1536:1920],
        z(ROPE_LANE0), w[..., 2176:2208], z(LANE - ROPE_LANE0 - ROPE_DIM)], axis=-1)


def _unpad_w_in(g):
    k0 = COL_KROPE + ROPE_LANE0
    return jnp.concatenate([
        g[..., COL_QSB:COL_KVLAT], g[..., COL_QLAT:COL_KROPE], g[..., COL_KVLAT:COL_QLAT],
        g[..., k0:k0 + ROPE_DIM], g[..., 0:COL_QSB]], axis=-1)


def _pad_w_q(w):
    r = w.shape[0]
    return jnp.pad(w.reshape(r, N_HEADS, QK_DIM), ((0, 0), (0, 0), (0, LANE - QK_DIM))).reshape(r, N_HEADS * LANE)


def _unpad_w_q(g):
    r = g.shape[0]
    return g.reshape(r, N_HEADS, LANE)[..., :QK_DIM].reshape(r, N_HEADS * QK_DIM)


def _pad_w_mla(w):
    n = w.shape[1]
    return jnp.pad(w.reshape(N_HEADS, NOPE_DIM, n), ((0, 0), (LANE - NOPE_DIM, 0), (0, 0))).reshape(
        N_HEADS * LANE, n)


def _unpad_w_mla(g):
    n = g.shape[1]
    return g.reshape(N_HEADS, LANE, n)[:, LANE - NOPE_DIM:, :].reshape(N_HEADS * NOPE_DIM, n)


def _rope_tables(positions):
    half = ROPE_DIM // 2
    inv_freq = 1.0 / (ROPE_THETA ** (jnp.arange(0, ROPE_DIM, 2, dtype=F32) / ROPE_DIM))
    ang = positions.astype(F32)[:, None] * inv_freq
    cos, sin = jnp.cos(ang), jnp.sin(ang)
    S = positions.shape[0]
    one = jnp.ones((S, ROPE_LANE0), F32)
    zero = lambda k: jnp.zeros((S, k), F32)
    tail = LANE - ROPE_LANE0 - ROPE_DIM
    c = jnp.concatenate([one, cos, cos, zero(tail)], axis=1)
    s1 = jnp.concatenate([zero(ROPE_LANE0), -sin, zero(half + tail)], axis=1)
    s2 = jnp.concatenate([zero(ROPE_LANE0 + half), sin, zero(tail)], axis=1)
    return c, s1, s2


def _layer_fwd(x, W, mod, tabs, next_blocks=None):
    sh1, sc1, gt1, sh2, sc2, gt2 = (mod[i] for i in range(N_MOD))
    n_small = len(_SHARDED) - len(_BY_CHIP)
    small, big = (next_blocks[:n_small], next_blocks[n_small:]) if next_blocks else (None, None)
    h1 = _normmod_fwd(x, W["g_mix"], sc1, sh1, "mix_norm_fwd")
    p = _matmul(h1, W["w_in"], name="in_proj")
    (osbh, cb), carried = _sb_fwd(p, "sb_attn_fwd", [_gather_exchange(small, "a")] if small else None)
    o_sb = _matmul(osbh, W["w_sb_out"], name="sb_out")
    qn = _rmsnorm_fwd(p, Q_RANK, COL_QLAT // Q_RANK, W["g_q"], "q_lat_norm_fwd")
    kvn = _rmsnorm_fwd(p, KV_RANK, COL_KVLAT // KV_RANK, W["g_kv"], "kv_lat_norm_fwd")
    qp = _matmul(qn, W["w_q_up"], name="q_up")
    kv = _matmul(kvn, W["w_kv_up"], out_dtype=BF16, name="kv_up")
    qr, kpe = _rope_fwd(qp, p, tabs, "rope_fwd")
    (omh, lse), carried = _mla_fwd(
        qr, kv, kpe, "mla_attn_fwd",
        [_gather_exchange(carried[0], "b"), _gather_exchange(big, "a")] if small else None)
    o_mla = _matmul(omh, W["w_mla_out"], name="mla_out")
    merged = _merge_fwd(p, o_sb, o_mla, "merge_fwd")
    y1 = _matmul(merged, W["w_mix_out"], name="mix_out")
    x1 = _res_fwd(x, y1, gt1, "mix_residual")
    h2 = _normmod_fwd(x1, W["g_mlp"], sc2, sh2, "mlp_norm_fwd")

    def sqrelu(t):
        r = jnp.maximum(t, 0.0)
        return t, r * r

    if small:
        (u, a), big_done = _matmul(h2, W["w_up"], b_sharded="col", out_dtype=(F32, BF16), epilogue=sqrelu,
                                   exs=[_gather_exchange(carried[1], "b")], name="mlp_up")
        gathered = list(carried[0]) + list(big_done[0])
    else:
        u, a = _matmul(h2, W["w_up"], b_sharded="col", out_dtype=(F32, BF16), epilogue=sqrelu, name="mlp_up")
        gathered = []
    y2 = _matmul(a, W["w_down"], b_sharded="row", name="mlp_down")
    x2 = _res_fwd(x1, y2, gt2, "mlp_residual")
    saved = dict(x=x, h1=h1, p=p, osbh=osbh, cb=cb, o_sb=o_sb, qn=qn, kvn=kvn, qr=qr, kv=kv, kpe=kpe, omh=omh,
                 lse=lse, o_mla=o_mla, merged=merged, y1=y1, x1=x1, h2=h2, u=u, a=a, y2=y2)
    return x2, saved, gathered


def _layer_bwd(dx2, W, mod, tabs, sv, core, above=None, above_send=None):
    sh1, sc1, gt1, sh2, sc2, gt2 = (mod[i] for i in range(N_MOD))
    dy2, dgt2 = _res_bwd(dx2, sv["y2"], gt2, "mlp_residual_bwd")

    def sqrelu_bwd(da, u):
        return (da * (2.0 * jnp.maximum(u, 0.0)),)

    du = _matmul(dy2, W["w_down"], tb=True, b_sharded="row", out_dtype=BF16, epilogue=sqrelu_bwd, extra=(sv["u"],),
                 exs=[_swap_halves_exchange(above_send)] if above else None, name="mlp_down_dx")
    pending = None
    if above:
        du, (from_sibling,) = du
        pending = [_add_halves(d, r, core, "grads_add_halves") for d, r in zip(above, from_sibling)]

    def with_bf16(t):
        return t, t

    g_down, g_down_send = _matmul(sv["a"], dy2, ta=True, out_dtype=(F32, BF16), epilogue=with_bf16,
                                  out_sharded=("row", W["w_down"].shape), name="mlp_down_dw")
    dh2 = _matmul(du, W["w_up"], tb=True, b_sharded="col", name="mlp_up_dx")
    g_up, g_up_send = _matmul(sv["h2"], du, ta=True, out_dtype=(F32, BF16), epilogue=with_bf16,
                              out_sharded=("col", W["w_up"].shape), name="mlp_up_dw")
    dx1, dsh2, dsc2, dg_mlp = _normmod_bwd(sv["x1"], dh2, W["g_mlp"], sc2, dx2, "mlp_norm_bwd")
    dy1, dgt1 = _res_bwd(dx1, sv["y1"], gt1, "mix_residual_bwd")
    dm = _matmul(dy1, W["w_mix_out"], tb=True, name="mix_out_dx")
    g_mix_out = _matmul(sv["merged"], dy1, ta=True, name="mix_out_dw")
    do_sb, do_mla, dgs, dgm = _merge_bwd(sv["p"], sv["o_sb"], sv["o_mla"], dm, "merge_bwd")
    do_sbh = _matmul(do_sb, W["w_sb_out"], tb=True, name="sb_out_dx")
    g_sb_out = _matmul(sv["osbh"], do_sb, ta=True, name="sb_out_dw")
    (dqs, dks, dvs), carried = _sb_bwd(
        sv["p"], do_sbh, sv["cb"], "sb_attn_bwd", [_chip_scatter_exchange(pending)] if above else None)
    my_sum = [_sum_slots(part, core, "grads_sum_chips") for part in carried[0]] if above else []
    do_mh = _matmul(do_mla, W["w_mla_out"], tb=True, name="mla_out_dx")
    g_mla_out = _matmul(sv["omh"], do_mla, ta=True, name="mla_out_dw")
    (dqr, dkv, dkpe), carried = _mla_bwd(
        sv["qr"], sv["kv"], sv["kpe"], do_mh, sv["omh"], sv["lse"], "mla_attn_bwd",
        [_sibling_gather_exchange(my_sum)] if above else None)
    reduced_above = list(carried[0]) if above else []
    dqp, dkr = _rope_bwd(dqr, dkpe, tabs, "rope_bwd")
    dqn = _matmul(dqp, W["w_q_up"], tb=True, name="q_up_dx")
    g_q_up = _matmul(sv["qn"], dqp, ta=True, name="q_up_dw")
    dkvn = _matmul(dkv, W["w_kv_up"], tb=True, name="kv_up_dx")
    g_kv_up = _matmul(sv["kvn"], dkv, ta=True, name="kv_up_dw")
    dqlat, dg_q = _rmsnorm_bwd(sv["p"], Q_RANK, COL_QLAT // Q_RANK, dqn, W["g_q"], "q_lat_norm_bwd")
    dkvlat, dg_kv = _rmsnorm_bwd(sv["p"], KV_RANK, COL_KVLAT // KV_RANK, dkvn, W["g_kv"], "kv_lat_norm_bwd")
    dp = jnp.concatenate([dgs, dgm, dqs, dks, dvs, dkvlat, dqlat, dkr], axis=1)
    dh1 = _matmul(dp, W["w_in"], tb=True, name="in_proj_dx")
    g_in = _matmul(sv["h1"], dp, ta=True, name="in_proj_dw")
    dx, dsh1, dsc1, dg_mix = _normmod_bwd(sv["x"], dh1, W["g_mix"], sc1, dx1, "mix_norm_bwd")
    grads = dict(w_in=g_in, w_q_up=g_q_up, w_kv_up=g_kv_up, w_sb_out=g_sb_out, w_mla_out=g_mla_out,
                 w_mix_out=g_mix_out, w_up=g_up, w_down=g_down, w_up_send=g_up_send, w_down_send=g_down_send,
                 dmod=jnp.concatenate([dsh1, dsc1, dgt1, dsh2, dsc2, dgt2], axis=0),
                 g_mix=dg_mix, g_mlp=dg_mlp, g_q=dg_q, g_kv=dg_kv)
    return dx, grads, reduced_above


def kernel(x, c, positions, w_ada, b_ada, g_mix_norm, w_in, g_q_lat, w_q_up, g_kv_lat, w_kv_up, w_sb_out, w_mla_out, w_mix_out, g_mlp_norm, w_up, w_down, g_final, loss_target, m_w_ada, m_b_ada, m_g_mix_norm, m_w_in, m_g_q_lat, m_w_q_up, m_g_kv_lat, m_w_kv_up, m_w_sb_out, m_w_mla_out, m_w_mix_out, m_g_mlp_norm, m_w_up, m_w_down, m_g_final, v_w_ada, v_b_ada, v_g_mix_norm, v_w_in, v_g_q_lat, v_w_q_up, v_g_kv_lat, v_w_kv_up, v_w_sb_out, v_w_mla_out, v_w_mix_out, v_g_mlp_norm, v_w_up, v_w_down, v_g_final):
    xi, yi, ci = _place()
    chip = 2 * xi + yi
    batch = 2 * chip + ci
    L = w_ada.shape[0]
    S = x.shape[1]
    shards = dict(w_in=w_in, w_q_up=w_q_up, w_kv_up=w_kv_up, w_sb_out=w_sb_out, w_mla_out=w_mla_out,
                  w_mix_out=w_mix_out, w_up=w_up, w_down=w_down)

    def my_halves(l):
        def half_of(w):
            half = w.shape[1] // 2
            return lax.dynamic_slice_in_dim(w[l].astype(BF16), ci * half, half, 0)

        return [half_of(shards[n]) for n in _SHARDED]

    def layer_weights(l, gathered):
        W = {}
        for n, g in zip(_SHARDED, gathered):
            by_chip = g.reshape((4,) + shards[n].shape[1:])
            W[n] = by_chip if n in _BY_CHIP else _unshard(by_chip, n)
        W["w_in"] = _pad_w_in(W["w_in"])
        W["w_q_up"] = _pad_w_q(W["w_q_up"])
        W["w_mla_out"] = _pad_w_mla(W["w_mla_out"])
        return dict(W, g_mix=g_mix_norm[l:l + 1], g_mlp=g_mlp_norm[l:l + 1], g_q=g_q_lat[l:l + 1],
                    g_kv=g_kv_lat[l:l + 1])

    gathered0 = _all_gather8(my_halves(0), "gather_weights")

    c_act = _silu(c, "silu_c")
    c_all = _all_gather8([jnp.broadcast_to(c_act, (8, D_MODEL))], "gather_c")[0].reshape(8, 8, D_MODEL)[:, 0]
    c16 = jnp.concatenate([c_all, jnp.zeros_like(c_all)], axis=0)
    ada_cols = w_ada.shape[2]
    b_shard = lax.dynamic_slice_in_dim(b_ada, chip * ada_cols, ada_cols, 1)
    mod_part = jnp.stack([_matmul(c16, w_ada[l], name="ada_mod") for l in range(L)])
    mod_part = _bias_add(mod_part, jnp.broadcast_to(b_shard[:, None, :], mod_part.shape), "ada_bias")
    mod_all = _all_gather8([mod_part.reshape(L * 16, ada_cols)], "gather_mod")[0].reshape(4, 2, L, 16, ada_cols)
    mod_mine = lax.dynamic_index_in_dim(mod_all[:, 0], batch, axis=2, keepdims=False)
    mods = mod_mine.transpose(1, 0, 2).reshape(L, N_MOD, 1, D_MODEL)

    tabs = _rope_tables(positions[0])

    xc, saved, layer_w = x[0], [], [layer_weights(0, gathered0)]
    for l in range(L):
        xc, sv, gathered = _layer_fwd(xc, layer_w[l], mods[l], tabs, my_halves(l + 1) if l + 1 < L else None)
        saved.append(sv)
        if l + 1 < L:
            layer_w.append(layer_weights(l + 1, gathered))
    dxc, dg_final, loss_part = _final_loss(xc, loss_target[0], g_final[None, :], "final_norm_loss")
    loss = lax.psum(loss_part[0, 0], ("x", "y", "c"))
    core = jnp.reshape(ci, (1,)).astype(jnp.int32)

    grads, reduced, above, above_send = [None] * L, [None] * L, None, None
    for l in reversed(range(L)):
        dxc, grads[l], reduced_above = _layer_bwd(
            dxc, layer_w[l], mods[l], tabs, saved[l], core, above, above_send)
        if above:
            reduced[l + 1] = reduced_above
        grads[l]["w_in"] = _unpad_w_in(grads[l]["w_in"])
        grads[l]["w_q_up"] = _unpad_w_q(grads[l]["w_q_up"])
        grads[l]["w_mla_out"] = _unpad_w_mla(grads[l]["w_mla_out"])
        above = [grads[l][n] if n in _BY_CHIP else _reshard(grads[l][n], n) for n in _SHARDED]
        above_send = [grads[l][n + "_send"] if n in _BY_CHIP else a for n, a in zip(_SHARDED, above)]
    from_sibling = _run_exchange(_swap_halves_exchange(above_send), "grads_swap_halves")
    pending = [_add_halves(d, r, core, "grads_add_halves") for d, r in zip(above, from_sibling)]
    from_chips = _run_exchange(_chip_scatter_exchange(pending), "grads_chip_scatter")
    my_sum = [_sum_slots(part, core, "grads_sum_chips") for part in from_chips]
    reduced[0] = _run_exchange(_sibling_gather_exchange(my_sum), "grads_sibling_gather")
    grad_x = dxc
    gw = {n: jnp.stack([reduced[l][i] for l in range(L)]) for i, n in enumerate(_SHARDED)}

    def row(v):
        return jnp.pad(v, ((0, 0), (0, D_MODEL - v.shape[1])))

    per_layer_rows = N_MOD + 4
    small = jnp.concatenate(
        [jnp.concatenate([grads[l]["dmod"], row(grads[l]["g_mix"]), row(grads[l]["g_mlp"]),
                          row(grads[l]["g_q"]), row(grads[l]["g_kv"])], axis=0) for l in range(L)]
        + [dg_final], axis=0)
    n_small = -(-small.shape[0] // 8) * 8
    small = jnp.pad(small, ((0, n_small - small.shape[0]), (0, 0)))
    small_all = _all_gather8([small], "gather_vector_grads")[0].reshape(8, n_small, D_MODEL)
    small_sum = _sum_blocks(small_all, "sum_vector_grads")
    lay = small_sum[:L * per_layer_rows].reshape(L, per_layer_rows, D_MODEL)
    g_b_ada = lay[:, :N_MOD].reshape(L, N_MOD * D_MODEL)
    g_g_mix, g_g_mlp = lay[:, N_MOD], lay[:, N_MOD + 1]
    g_g_q, g_g_kv = lay[:, N_MOD + 2, :Q_RANK], lay[:, N_MOD + 3, :KV_RANK]
    g_g_final = small_sum[L * per_layer_rows]
    dmod_all = small_all[:, :L * per_layer_rows].reshape(8, L, per_layer_rows, D_MODEL)[:, :, :N_MOD]
    dmod_all = dmod_all.reshape(8, L, N_MOD * D_MODEL)
    dmod_cols = lax.dynamic_slice_in_dim(dmod_all, chip * ada_cols, ada_cols, 2)
    dmod16 = jnp.concatenate([dmod_cols, jnp.zeros_like(dmod_cols)], axis=0)
    g_w_ada = jnp.stack([_matmul(c16, dmod16[:, l], ta=True, name="ada_dw") for l in range(L)])

    weights = dict(w_ada=w_ada, b_ada=b_ada, g_mix_norm=g_mix_norm, w_in=w_in, g_q_lat=g_q_lat, w_q_up=w_q_up,
                   g_kv_lat=g_kv_lat, w_kv_up=w_kv_up, w_sb_out=w_sb_out, w_mla_out=w_mla_out,
                   w_mix_out=w_mix_out, g_mlp_norm=g_mlp_norm, w_up=w_up, w_down=w_down, g_final=g_final)
    mom = dict(w_ada=(m_w_ada, v_w_ada), b_ada=(m_b_ada, v_b_ada), g_mix_norm=(m_g_mix_norm, v_g_mix_norm),
               w_in=(m_w_in, v_w_in), g_q_lat=(m_g_q_lat, v_g_q_lat), w_q_up=(m_w_q_up, v_w_q_up),
               g_kv_lat=(m_g_kv_lat, v_g_kv_lat), w_kv_up=(m_w_kv_up, v_w_kv_up),
               w_sb_out=(m_w_sb_out, v_w_sb_out), w_mla_out=(m_w_mla_out, v_w_mla_out),
               w_mix_out=(m_w_mix_out, v_w_mix_out), g_mlp_norm=(m_g_mlp_norm, v_g_mlp_norm),
               w_up=(m_w_up, v_w_up), w_down=(m_w_down, v_w_down), g_final=(m_g_final, v_g_final))
    gr = dict(gw, w_ada=g_w_ada, b_ada=g_b_ada, g_mix_norm=g_g_mix, g_q_lat=g_g_q, g_kv_lat=g_g_kv,
              g_mlp_norm=g_g_mlp, g_final=g_g_final)
    order = list(weights)
    deltas, new_m, new_v = [], [], []
    for n in order:
        wv, gv, (mv, vv) = weights[n], gr[n], mom[n]
        if wv.ndim == 1:
            d, nm, nv = (t[0] for t in _adamw(wv[None], gv[None], mv[None], vv[None], "adamw_" + n))
        else:
            d, nm, nv = _adamw(wv, gv, mv, vv, "adamw_" + n)
        deltas.append(d)
        new_m.append(nm)
        new_v.append(nv)
    return (loss, grad_x[None], *[gr[n] for n in order], *deltas, *new_m, *new_v)
```
